```python
import math
import jax, jax.numpy as jnp
from jax import lax
import numpy as np


D_MODEL = 1024
BATCH = 8
SEQ = 4096
DEPTH = 2
DEC_BATCH = 2
DEC_SEQ = 16384
PAST_LEN = 128

RET_HEADS = 4
RET_DK = 64
RET_DV = 64
RET_CHUNK = 128
ROPE_BASE = 10000.0
SWA_Q_HEADS = 8
SWA_KV_HEADS = 2
SWA_DH = 64
WINDOW = 128
SWA_BLOCK = 128
REL_BUCKETS = 32
REL_MAX_DIST = 128
MASK_VALUE = -1e30
HG_HEADS = 4
HG_DK = 128
HG_DV = 64
HG_CHUNK = 64
FFN_DIM = 2816
N_EXPERTS = 8
TOP_K = 2
EXPERT_DIM = 3584
N_DENSE = (DEPTH + 1) // 2
N_MOE = DEPTH // 2
LN_EPS = 1e-5
RMS_EPS = 1e-6
DEEPNORM_ALPHA = (2.0 * DEPTH) ** 0.25
DEEPNORM_BETA = (8.0 * DEPTH) ** -0.25

SPLIT_SIZES = (
    RET_HEADS * RET_DK, RET_HEADS * RET_DK, RET_HEADS * RET_DV, RET_HEADS * RET_DV,
    SWA_Q_HEADS * SWA_DH, SWA_KV_HEADS * SWA_DH, SWA_KV_HEADS * SWA_DH,
    HG_HEADS * HG_DK, HG_HEADS * HG_DK, HG_HEADS * HG_DK, HG_HEADS * HG_DV, HG_HEADS * HG_DV,
)
IN_COLS = sum(SPLIT_SIZES)
SPLIT_POINTS = tuple(int(v) for v in np.cumsum(SPLIT_SIZES)[:-1])
MIX_WIDTH = RET_HEADS * RET_DV + SWA_Q_HEADS * SWA_DH + HG_HEADS * HG_DV

kernel_name = "hybrid_bidir_encoder_ret_swa_hgrn2"


def layer_norm(x, g, b):
    xf = x.astype(jnp.float32)
    mu = jnp.mean(xf, -1, keepdims=True)
    var = jnp.mean(jnp.square(xf - mu), -1, keepdims=True)
    return ((xf - mu) * lax.rsqrt(var + LN_EPS) * g + b).astype(x.dtype)


def rms_norm_last(x):
    return x * lax.rsqrt(jnp.mean(jnp.square(x), -1, keepdims=True) + RMS_EPS)


def rotary(x, pos):
    half = x.shape[-1] // 2
    inv = ROPE_BASE ** (-jnp.arange(half, dtype=jnp.float32) / half)
    ang = pos.astype(jnp.float32)[:, None] * inv[None, :]
    cos = jnp.cos(ang)[None, :, None, :]
    sin = jnp.sin(ang)[None, :, None, :]
    x1, x2 = x[..., :half], x[..., half:]
    return jnp.concatenate([x1 * cos - x2 * sin, x1 * sin + x2 * cos], -1)


def retention_chunkwise(q, k, v, log_gamma):
    B, S, H, Dk = q.shape
    Dv = v.shape[-1]
    C = RET_CHUNK
    N = S // C
    qc = q.reshape(B, N, C, H, Dk)
    kc = k.reshape(B, N, C, H, Dk)
    vc = v.reshape(B, N, C, H, Dv)
    idx = jnp.arange(C, dtype=jnp.float32)
    rel = idx[:, None] - idx[None, :]
    intra_decay = jnp.where(rel[None] >= 0, jnp.exp(log_gamma[:, None, None] * jnp.maximum(rel, 0.0)[None]), 0.0)
    scores = jnp.einsum('bnihd,bnjhd->bnhij', qc, kc) * intra_decay
    intra = jnp.einsum('bnhij,bnjhe->bnihe', scores, vc)
    k_w = jnp.exp(log_gamma[None, :] * (C - 1.0 - idx)[:, None])
    chunk_kv = jnp.einsum('bnjhd,jh,bnjhe->nbhde', kc, k_w, vc)
    chunk_decay = jnp.exp(log_gamma * C)[None, :, None, None]

    def step(state, kv):
        return chunk_decay * state + kv, state

    _, prev = lax.scan(step, jnp.zeros((B, H, Dk, Dv), q.dtype), chunk_kv)
    q_w = jnp.exp(log_gamma[None, :] * (idx + 1.0)[:, None])
    inter = jnp.einsum('bnihd,ih,nbhde->bnihe', qc, q_w, prev)
    return (intra + inter).reshape(B, S, H, Dv)


def t5_bucket(rel):
    nb = REL_BUCKETS // 2
    ret = (rel > 0).astype(np.int32) * nb
    n = np.abs(rel)
    max_exact = nb // 2
    large = max_exact + (np.log(np.maximum(n, 1) / max_exact) / np.log(REL_MAX_DIST / max_exact) * (nb - max_exact)).astype(np.int32)
    large = np.minimum(large, nb - 1)
    return ret + np.where(n < max_exact, n, large)


def window_gqa(q, k, v, sink, rel_bias):
    B, S = q.shape[:2]
    W = SWA_BLOCK
    N = S // W
    G = SWA_Q_HEADS // SWA_KV_HEADS
    qb = q.reshape(B, N, W, SWA_KV_HEADS, G, SWA_DH)

    def band(t):
        tp = jnp.pad(t, ((0, 0), (W, W), (0, 0), (0, 0))).reshape(B, N + 2, W, SWA_KV_HEADS, SWA_DH)
        return jnp.concatenate([tp[:, :-2], tp[:, 1:-1], tp[:, 2:]], axis=2)

    kb, vb = band(k), band(v)
    rel = np.arange(3 * W)[None, :] - W - np.arange(W)[:, None]
    bias = jnp.transpose(rel_bias[t5_bucket(rel)], (2, 0, 1)).reshape(SWA_KV_HEADS, G, W, 3 * W)
    kpos = (np.arange(N)[:, None] - 1) * W + np.arange(3 * W)[None, :]
    valid = (np.abs(rel) <= WINDOW)[None] & ((kpos >= 0) & (kpos < S))[:, None, :]
    s = jnp.einsum('bnqhgd,bnkhd->bnhgqk', qb, kb) * (SWA_DH ** -0.5) + bias
    s = jnp.where(valid[None, :, None, None], s, MASK_VALUE)
    sink_l = sink.reshape(SWA_KV_HEADS, G)[None, None, :, :, None, None]
    m = jnp.maximum(jnp.max(s, -1, keepdims=True), sink_l)
    p = jnp.exp(s - m)
    denom = jnp.sum(p, -1, keepdims=True) + jnp.exp(sink_l - m)
    o = jnp.einsum('bnhgqk,bnkhd->bnqhgd', p / denom, vb)
    return o.reshape(B, S, SWA_Q_HEADS * SWA_DH)


def hgrn2_chunkwise(q, k, v, log_f):
    B, S, H, Dk = q.shape
    Dv = v.shape[-1]
    C = HG_CHUNK
    N = S // C

    def chunks(t):
        return jnp.moveaxis(t.reshape(B, N, C, H, t.shape[-1]), 1, 0)

    lower_tri = jnp.tril(jnp.ones((C, C), bool))[None, :, :, None, None]

    def step(state, xs):
        qc, kc, vc, gc = xs
        b = jnp.cumsum(gc, axis=1)
        w = jnp.where(lower_tri, jnp.exp(jnp.where(lower_tri, b[:, :, None] - b[:, None, :], 0.0)), 0.0)
        a = jnp.einsum('bihd,bijhd,bjhd->bhij', qc, w, kc)
        o = jnp.einsum('bhij,bjhe->bihe', a, vc) + jnp.einsum('bihd,bhde->bihe', qc * jnp.exp(b), state)
        b_last = b[:, -1]
        new_state = jnp.exp(b_last)[..., None] * state + jnp.einsum('bjhd,bjhe->bhde', kc * jnp.exp(b_last[:, None] - b), vc)
        return new_state, o

    init = jnp.zeros((B, H, Dk, Dv), q.dtype)
    _, o = lax.scan(step, init, (chunks(q), chunks(k), chunks(v), chunks(log_f)))
    return jnp.moveaxis(o, 0, 1).reshape(B, S, H, Dv)


def flip(t):
    return jnp.flip(t, axis=1)


def token_mixers(x, w_in, w_out, ret_decay, swa_sink, rel_bias, lb, hg_norm_w):
    B, S, _ = x.shape
    h = (x @ w_in).astype(jnp.float32)
    rq, rk, rv, rg, sq, sk, sv, hq, hz_f, hz_b, hi, hg = jnp.split(h, SPLIT_POINTS, axis=-1)

    pos = jnp.arange(S)
    rq = rotary(rq.reshape(B, S, RET_HEADS, RET_DK), pos)
    rk = rotary(rk.reshape(B, S, RET_HEADS, RET_DK), pos) * (RET_DK ** -0.5)
    rv = rv.reshape(B, S, RET_HEADS, RET_DV)
    log_gamma = jnp.log1p(-jnp.exp(ret_decay.astype(jnp.float32)))
    ret = retention_chunkwise(rq, rk, rv, log_gamma[0]) + flip(retention_chunkwise(flip(rq), flip(rk), flip(rv), log_gamma[1]))
    ret = rms_norm_last(ret).reshape(B, S, RET_HEADS * RET_DV) * jax.nn.silu(rg)

    swa = window_gqa(sq.reshape(B, S, SWA_Q_HEADS, SWA_DH), sk.reshape(B, S, SWA_KV_HEADS, SWA_DH),
                     sv.reshape(B, S, SWA_KV_HEADS, SWA_DH), swa_sink.astype(jnp.float32), rel_bias.astype(jnp.float32))

    hq = hq.reshape(B, S, HG_HEADS, HG_DK)
    hi = hi.reshape(B, S, HG_HEADS, HG_DV)
    lbh = lb.reshape(HG_HEADS, HG_DK)

    def gates(z):
        z = z.reshape(B, S, HG_HEADS, HG_DK)
        key = (1.0 - lbh) * jax.nn.sigmoid(-z)
        log_f = jnp.log1p(-key)
        return key, log_f

    k_f, lf_f = gates(hz_f)
    k_b, lf_b = gates(hz_b)
    hgo = hgrn2_chunkwise(hq, k_f, hi, lf_f) + flip(hgrn2_chunkwise(flip(hq), flip(k_b), flip(hi), flip(lf_b)))
    hgo = (rms_norm_last(hgo) * hg_norm_w.astype(jnp.float32).reshape(HG_HEADS, HG_DV)).reshape(B, S, HG_HEADS * HG_DV) * jax.nn.sigmoid(hg)

    mix = jnp.concatenate([ret, swa, hgo], axis=-1).astype(x.dtype)
    return mix @ w_out


def swiglu(x, w_gate, w_up, w_down):
    return (jax.nn.silu(x @ w_gate) * (x @ w_up)) @ w_down


def moe_swiglu(x, w_router, w_gate, w_up, w_down):
    B, S, D = x.shape
    xt = x.reshape(B * S, D)
    logits = (xt @ w_router).astype(jnp.float32)
    top_v, top_i = lax.top_k(logits, TOP_K)
    top_w = jax.nn.softmax(top_v, axis=-1)
    gate = jnp.sum(jax.nn.one_hot(top_i, N_EXPERTS, dtype=jnp.float32) * top_w[..., None], axis=1)
    y = jnp.zeros((B * S, D), jnp.float32)
    for e in range(N_EXPERTS):
        y = y + gate[:, e:e + 1] * swiglu(xt, w_gate[e], w_up[e], w_down[e])
    return y.astype(x.dtype).reshape(B, S, D)


def trunk(x, w_in, w_out, ret_decay, swa_sink, rel_bias, hg_lb, hg_norm_w, ln1_g, ln1_b, ln2_g, ln2_b,
          ffn_w_gate, ffn_w_up, ffn_w_down, moe_router, moe_w_gate, moe_w_up, moe_w_down):
    p = jax.nn.softmax(hg_lb.astype(jnp.float32), axis=0)
    lower = jnp.cumsum(p, axis=0) - p[0]
    for l in range(DEPTH):
        mix = token_mixers(x, w_in[l], w_out[l], ret_decay[l], swa_sink[l], rel_bias, lower[l], hg_norm_w[l])
        x = layer_norm(DEEPNORM_ALPHA * x + mix, ln1_g[l], ln1_b[l])
        i = l // 2
        if l % 2 == 0:
            f = swiglu(x, ffn_w_gate[i], ffn_w_up[i], ffn_w_down[i])
        else:
            f = moe_swiglu(x, moe_router[i], moe_w_gate[i], moe_w_up[i], moe_w_down[i])
        x = layer_norm(DEEPNORM_ALPHA * x + f, ln2_g[l], ln2_b[l])
    return x


def setup_inputs(seed: int = 0) -> dict:
    key = jax.random.key(seed)
    ks = jax.random.split(key, 20)

    def nrm(k, shape, scale):
        return jax.random.normal(k, shape, jnp.float32) * scale

    ret_heads = jnp.arange(RET_HEADS, dtype=jnp.float32)
    return {
        'x_prompt': nrm(ks[0], (BATCH, SEQ, D_MODEL), 1.0),
        'x_sample': nrm(ks[1], (DEC_BATCH, DEC_SEQ, D_MODEL), 1.0),
        'w_in': nrm(ks[2], (DEPTH, D_MODEL, IN_COLS), D_MODEL ** -0.5),
        'w_out': nrm(ks[3], (DEPTH, MIX_WIDTH, D_MODEL), DEEPNORM_BETA * MIX_WIDTH ** -0.5),
        'ret_decay': -(5.0 + ret_heads) * math.log(2.0) + nrm(ks[4], (DEPTH, 2, RET_HEADS), 0.1),
        'swa_sink': nrm(ks[5], (DEPTH, SWA_Q_HEADS), 0.5),
        'rel_bias': nrm(ks[6], (REL_BUCKETS, SWA_Q_HEADS), 0.3),
        'hg_lb': nrm(ks[7], (DEPTH, HG_HEADS * HG_DK), 0.5),
        'hg_norm_w': 1.0 + nrm(ks[8], (DEPTH, HG_HEADS * HG_DV), 0.02),
        'ln1_g': 1.0 + nrm(ks[9], (DEPTH, D_MODEL), 0.02),
        'ln1_b': nrm(ks[10], (DEPTH, D_MODEL), 0.02),
        'ln2_g': 1.0 + nrm(ks[11], (DEPTH, D_MODEL), 0.02),
        'ln2_b': nrm(ks[12], (DEPTH, D_MODEL), 0.02),
        'ffn_w_gate': nrm(ks[13], (N_DENSE, D_MODEL, FFN_DIM), D_MODEL ** -0.5),
        'ffn_w_up': nrm(ks[14], (N_DENSE, D_MODEL, FFN_DIM), D_MODEL ** -0.5),
        'ffn_w_down': nrm(ks[15], (N_DENSE, FFN_DIM, D_MODEL), DEEPNORM_BETA * FFN_DIM ** -0.5),
        'moe_router': nrm(ks[16], (N_MOE, D_MODEL, N_EXPERTS), D_MODEL ** -0.5),
        'moe_w_gate': nrm(ks[17], (N_MOE, N_EXPERTS, D_MODEL, EXPERT_DIM), D_MODEL ** -0.5),
        'moe_w_up': nrm(ks[18], (N_MOE, N_EXPERTS, D_MODEL, EXPERT_DIM), D_MODEL ** -0.5),
        'moe_w_down': nrm(ks[19], (N_MOE, N_EXPERTS, EXPERT_DIM, D_MODEL), DEEPNORM_BETA * EXPERT_DIM ** -0.5),
    }


def reference(x_prompt, x_sample, w_in, w_out, ret_decay, swa_sink, rel_bias, hg_lb, hg_norm_w,
              ln1_g, ln1_b, ln2_g, ln2_b, ffn_w_gate, ffn_w_up, ffn_w_down,
              moe_router, moe_w_gate, moe_w_up, moe_w_down):
    y_prompt = trunk(x_prompt, w_in, w_out, ret_decay, swa_sink, rel_bias, hg_lb, hg_norm_w, ln1_g, ln1_b, ln2_g, ln2_b,
                     ffn_w_gate, ffn_w_up, ffn_w_down, moe_router, moe_w_gate, moe_w_up, moe_w_down)
    y_sample = trunk(x_sample, w_in, w_out, ret_decay, swa_sink, rel_bias, hg_lb, hg_norm_w, ln1_g, ln1_b, ln2_g, ln2_b,
                     ffn_w_gate, ffn_w_up, ffn_w_down, moe_router, moe_w_gate, moe_w_up, moe_w_down)
    return (y_prompt, y_sample)
```

```python
import functools
import math

import numpy as np
import jax
import jax.numpy as jnp
from jax import lax
from jax.experimental import pallas as pl
from jax.experimental.pallas import tpu as pltpu

F32 = jnp.float32
BF16 = jnp.bfloat16

D_MODEL = 1024
DEPTH = 2
RET_HEADS = 4
RET_DK = 64
RET_CHUNK = 128
ROPE_BASE = 10000.0
SWA_Q_HEADS = 8
SWA_KV_HEADS = 2
SWA_DH = 64
WINDOW = 128
SWA_BLOCK = 128
REL_BUCKETS = 32
REL_MAX_DIST = 128
MASK_VALUE = -1e30
HG_HEADS = 4
HG_DK = 128
HG_DV = 64
FFN_DIM = 2816
N_EXPERTS = 8
EXPERT_DIM = 3584
LN_EPS = 1e-5
RMS_EPS = 1e-6
DEEPNORM_ALPHA = (2.0 * DEPTH) ** 0.25

LANES = 128
VMEM_LIMIT = 56 * 1024 * 1024

HG_W = 2048
RET_W = 1024
SWA_W = 768
IN_COLS = HG_W + RET_W + SWA_W

HG_BLOCK = 256
HG_BASE = 8
RET_TILE = 512
SWA_TILE = 512


def _cparams(sem):
    return pltpu.CompilerParams(dimension_semantics=sem, vmem_limit_bytes=VMEM_LIMIT)


def _dot(a, b):
    return jnp.dot(a, b, preferred_element_type=F32)


def _dot_nt(a, b):
    return lax.dot_general(a, b, (((1,), (1,)), ((), ())), preferred_element_type=F32)


def _dot_tn(a, b):
    return lax.dot_general(a, b, (((0,), (0,)), ((), ())), preferred_element_type=F32)


def _dot_split(a_bf16, x):
    hi = x.astype(BF16)
    lo = (x - hi.astype(F32)).astype(BF16)
    return _dot(a_bf16, hi) + _dot(a_bf16, lo)


def _dot_split_rhs_const(x, b_bf16):
    hi = x.astype(BF16)
    lo = (x - hi.astype(F32)).astype(BF16)
    return _dot(hi, b_bf16) + _dot(lo, b_bf16)


def _layer_norm(y, g, b):
    mu = jnp.mean(y, axis=-1, keepdims=True)
    yc = y - mu
    var = jnp.mean(yc * yc, axis=-1, keepdims=True)
    return yc * lax.rsqrt(var + LN_EPS) * g + b


def _lane_lo_mask(shape):
    return lax.broadcasted_iota(jnp.int32, shape, len(shape) - 1) < (LANES // 2)


def _head_block_ones():
    r = lax.broadcasted_iota(jnp.int32, (LANES, LANES), 0) // 64
    c = lax.broadcasted_iota(jnp.int32, (LANES, LANES), 1) // 64
    return r == c


def _inproj_kernel(x_ref, w_ref, o_ref):
    o_ref[...] = _dot(x_ref[...].astype(BF16), w_ref[...])


def _in_proj(x2d, w):
    T = x2d.shape[0]
    tm, tn = 1024, 768
    return pl.pallas_call(
        _inproj_kernel,
        grid=(T // tm, IN_COLS // tn),
        in_specs=[pl.BlockSpec((tm, D_MODEL), lambda i, j: (i, 0)),
                  pl.BlockSpec((D_MODEL, tn), lambda i, j: (0, j))],
        out_specs=pl.BlockSpec((tm, tn), lambda i, j: (i, j)),
        out_shape=jax.ShapeDtypeStruct((T, IN_COLS), F32),
        compiler_params=_cparams(("parallel", "arbitrary")),
        name="in_proj",
    )(x2d, w)


def _rotary(x, cos, sin_signed, even):
    sw = jnp.where(even, pltpu.roll(x, 96, 1), pltpu.roll(x, 32, 1))
    return x * cos + sw * sin_signed


def _ret_tables(rdp_ref):
    lg = jnp.log1p(-jnp.exp(rdp_ref[...]))
    return lg


def _ret_bwd_state_kernel(h_ref, cos_ref, sin_ref, rdp_ref, sb_ref, st_ref):
    C = RET_CHUNK
    nc = h_ref.shape[0] // C

    @pl.when(pl.program_id(1) == 0)
    def _():
        st_ref[...] = jnp.zeros_like(st_ref)

    lg = _ret_tables(rdp_ref)
    rows = lax.broadcasted_iota(jnp.int32, (C, LANES), 0).astype(F32)
    even = (lax.broadcasted_iota(jnp.int32, (C, LANES), 1) // 32) % 2 == 0
    bm = _head_block_ones()
    for c in range(nc - 1, -1, -1):
        sl = pl.ds(c * C, C)
        cos = cos_ref[sl, :]
        sin = sin_ref[sl, :]
        for p in range(2):
            lgb = lg[2 + p:3 + p, :]
            kwb = jnp.exp(lgb * rows)
            decb = jnp.where(bm, jnp.exp(lgb * float(C)), 0.0)
            k = _rotary(h_ref[sl, 256 + 128 * p:384 + 128 * p], cos, sin, even) * (RET_DK ** -0.5)
            v = h_ref[sl, 512 + 128 * p:640 + 128 * p]
            sb_ref[0, c, p] = st_ref[p]
            kv = _dot_tn((k * kwb).astype(BF16), v.astype(BF16))
            st_ref[p] = decb * st_ref[p] + jnp.where(bm, kv, 0.0)


def _ret_main_kernel(h_ref, cos_ref, sin_ref, rdp_ref, rdh_ref, sb_ref, o_ref, st_ref):
    C = RET_CHUNK
    nc = h_ref.shape[0] // C

    @pl.when(pl.program_id(1) == 0)
    def _():
        st_ref[...] = jnp.zeros_like(st_ref)

    lg = _ret_tables(rdp_ref)
    lgh = jnp.log1p(-jnp.exp(rdh_ref[...]))
    rows = lax.broadcasted_iota(jnp.int32, (C, LANES), 0).astype(F32)
    cols = lax.broadcasted_iota(jnp.int32, (C, LANES), 1).astype(F32)
    rel = rows - cols
    even = (lax.broadcasted_iota(jnp.int32, (C, LANES), 1) // 32) % 2 == 0
    lo = _lane_lo_mask((C, LANES))
    bm = _head_block_ones()
    bones = jnp.where(bm, 1.0, 0.0).astype(BF16)

    dmat = []
    for hd in range(RET_HEADS):
        f = jnp.where(rel >= 0, jnp.exp(lgh[hd:hd + 1, :] * jnp.maximum(rel, 0.0)), 0.0)
        b = jnp.where(rel <= 0, jnp.exp(lgh[4 + hd:5 + hd, :] * jnp.maximum(-rel, 0.0)), 0.0)
        dmat.append(f + b)

    for c in range(nc):
        sl = pl.ds(c * C, C)
        cos = cos_ref[sl, :]
        sin = sin_ref[sl, :]
        for p in range(2):
            lgf = lg[p:p + 1, :]
            lgb = lg[2 + p:3 + p, :]
            q = _rotary(h_ref[sl, 128 * p:128 * p + 128], cos, sin, even)
            k = _rotary(h_ref[sl, 256 + 128 * p:384 + 128 * p], cos, sin, even) * (RET_DK ** -0.5)
            v = h_ref[sl, 512 + 128 * p:640 + 128 * p]
            g = h_ref[sl, 768 + 128 * p:896 + 128 * p]
            kb = k.astype(BF16)
            vb = v.astype(BF16)
            qs = jnp.concatenate([jnp.where(lo, q, 0.0), jnp.where(lo, 0.0, q)], axis=0).astype(BF16)
            s = _dot_nt(qs, kb)
            pm = jnp.concatenate([s[:C] * dmat[2 * p], s[C:] * dmat[2 * p + 1]], axis=1).astype(BF16)
            vs = jnp.concatenate([jnp.where(lo, v, 0.0), jnp.where(lo, 0.0, v)], axis=0).astype(BF16)
            intra = _dot(pm, vs)
            qwf = jnp.exp(lgf * (rows + 1.0))
            qwb = jnp.exp(lgb * (float(C) - rows))
            qi = jnp.concatenate([q * qwf, q * qwb], axis=1).astype(BF16)
            sst = jnp.concatenate([st_ref[p], sb_ref[0, c, p]], axis=0).astype(BF16)
            o = intra + _dot(qi, sst)
            kwf = jnp.exp(lgf * (float(C) - 1.0 - rows))
            decf = jnp.where(bm, jnp.exp(lgf * float(C)), 0.0)
            kv = _dot_tn((k * kwf).astype(BF16), vb)
            st_ref[p] = decf * st_ref[p] + jnp.where(bm, kv, 0.0)
            ms = _dot_split_rhs_const(o * o, bones) * (1.0 / RET_DK)
            y = o * lax.rsqrt(ms + RMS_EPS) * (g * jax.nn.sigmoid(g))
            o_ref[sl, 128 * p:128 * p + 128] = y.astype(o_ref.dtype)


def _retention(h, cos_t, sin_t, rdp, rdh, B, S):
    L = RET_TILE
    nb = S // L
    ncb = L // RET_CHUNK
    n_chunks = S // RET_CHUNK
    colblk = HG_W // RET_W
    sb = pl.pallas_call(
        _ret_bwd_state_kernel,
        grid=(B, nb),
        in_specs=[pl.BlockSpec((L, RET_W), lambda b, n: (b * nb + nb - 1 - n, colblk)),
                  pl.BlockSpec((L, LANES), lambda b, n: (nb - 1 - n, 0)),
                  pl.BlockSpec((L, LANES), lambda b, n: (nb - 1 - n, 0)),
                  pl.BlockSpec((8, LANES), lambda b, n: (0, 0))],
        out_specs=pl.BlockSpec((1, ncb, 2, LANES, LANES), lambda b, n: (b, nb - 1 - n, 0, 0, 0)),
        out_shape=jax.ShapeDtypeStruct((B, n_chunks, 2, LANES, LANES), F32),
        scratch_shapes=[pltpu.VMEM((2, LANES, LANES), F32)],
        compiler_params=_cparams(("parallel", "arbitrary")),
        name="ret_bwd_state",
    )(h, cos_t, sin_t, rdp)
    return pl.pallas_call(
        _ret_main_kernel,
        grid=(B, nb),
        in_specs=[pl.BlockSpec((L, RET_W), lambda b, n: (b * nb + n, colblk)),
                  pl.BlockSpec((L, LANES), lambda b, n: (n, 0)),
                  pl.BlockSpec((L, LANES), lambda b, n: (n, 0)),
                  pl.BlockSpec((8, LANES), lambda b, n: (0, 0)),
                  pl.BlockSpec((8, LANES), lambda b, n: (0, 0)),
                  pl.BlockSpec((1, ncb, 2, LANES, LANES), lambda b, n: (b, n, 0, 0, 0))],
        out_specs=pl.BlockSpec((L, 256), lambda b, n: (b * nb + n, 0)),
        out_shape=jax.ShapeDtypeStruct((B * S, 256), BF16),
        scratch_shapes=[pltpu.VMEM((2, LANES, LANES), F32)],
        compiler_params=_cparams(("parallel", "arbitrary")),
        name="ret_main",
    )(h, cos_t, sin_t, rdp, rdh, sb)


def _swa_kernel(q_ref, kp_ref, kc_ref, kn_ref, vp_ref, vc_ref, vn_ref, bias_ref, sink_ref, o_ref, *, seq_len):
    W = SWA_BLOCK
    nq = q_ref.shape[0] // W
    t0 = pl.program_id(1) * q_ref.shape[0]
    kall = jnp.concatenate([kp_ref[...], kc_ref[...], kn_ref[...]], axis=0).astype(BF16)
    vall = jnp.concatenate([vp_ref[...], vc_ref[...], vn_ref[...]], axis=0)
    lo_v = _lane_lo_mask(vall.shape)
    vlo = jnp.where(lo_v, vall, 0.0).astype(BF16)
    vhi = jnp.where(lo_v, 0.0, vall).astype(BF16)
    lo = _lane_lo_mask((W, LANES))
    qi = lax.broadcasted_iota(jnp.int32, (2 * W, 3 * W), 0) % W
    ki = lax.broadcasted_iota(jnp.int32, (2 * W, 3 * W), 1)
    rel = ki - W - qi
    band = (rel <= WINDOW) & (rel >= -WINDOW)
    for r in range(nq):
        kpos = t0 + (r - 1) * W + ki
        valid = band & (kpos >= 0) & (kpos < seq_len)
        kw = kall[r * W:(r + 3) * W]
        vwl = vlo[r * W:(r + 3) * W]
        vwh = vhi[r * W:(r + 3) * W]
        for c in range(4):
            q = q_ref[pl.ds(r * W, W), 128 * c:128 * c + 128] * (SWA_DH ** -0.5)
            qs = jnp.concatenate([jnp.where(lo, q, 0.0), jnp.where(lo, 0.0, q)], axis=0).astype(BF16)
            s = _dot_nt(qs, kw) + bias_ref[c]
            s = jnp.where(valid, s, MASK_VALUE)
            sink = sink_ref[c][:, 0:1]
            m = jnp.maximum(jnp.max(s, axis=-1, keepdims=True), sink)
            e = jnp.exp(s - m)
            den = jnp.sum(e, axis=-1, keepdims=True) + jnp.exp(sink - m)
            eb = e.astype(BF16)
            o = _dot(eb[:W], vwl) / den[:W] + _dot(eb[W:], vwh) / den[W:]
            o_ref[pl.ds(r * W, W), 128 * c:128 * c + 128] = o.astype(o_ref.dtype)


def _swa(h, bias_t, sink_t, B, S):
    L = SWA_TILE
    W = SWA_BLOCK
    nb = S // L
    r = L // W
    nblk = S // W
    qcol = (HG_W + RET_W) // 512
    kcol = (HG_W + RET_W + 512) // LANES
    vcol = kcol + 1

    def prev_map(col):
        return lambda b, n: (b * nblk + jnp.maximum(n * r - 1, 0), col)

    def cur_map(col):
        return lambda b, n: (b * nb + n, col)

    def next_map(col):
        return lambda b, n: (b * nblk + jnp.minimum(n * r + r, nblk - 1), col)

    return pl.pallas_call(
        functools.partial(_swa_kernel, seq_len=S),
        grid=(B, nb),
        in_specs=[pl.BlockSpec((L, 512), cur_map(qcol)),
                  pl.BlockSpec((W, LANES), prev_map(kcol)),
                  pl.BlockSpec((L, LANES), cur_map(kcol)),
                  pl.BlockSpec((W, LANES), next_map(kcol)),
                  pl.BlockSpec((W, LANES), prev_map(vcol)),
                  pl.BlockSpec((L, LANES), cur_map(vcol)),
                  pl.BlockSpec((W, LANES), next_map(vcol)),
                  pl.BlockSpec((4, 2 * W, 3 * W), lambda b, n: (0, 0, 0)),
                  pl.BlockSpec((4, 2 * W, LANES), lambda b, n: (0, 0, 0))],
        out_specs=pl.BlockSpec((L, 512), lambda b, n: (b * nb + n, 0)),
        out_shape=jax.ShapeDtypeStruct((B * S, 512), BF16),
        compiler_params=_cparams(("parallel", "parallel")),
        name="swa",
    )(h, h, h, h, h, h, h, bias_t, sink_t)


def _hg_levels(L):
    g = L
    out = []
    while g >= 2 * HG_BASE:
        out.append(g)
        g //= 2
    return out


def _hg_constants(L):
    idx = np.arange(L)
    t = idx[None, :]
    i = idx[:, None]
    secs = []
    masks = []
    secs.append((t <= i).astype(np.float32))
    secs.append((t > i).astype(np.float32))
    r = HG_BASE * (i // HG_BASE) + HG_BASE // 2 - 1
    eq = ((t > r) & (t <= i)).astype(np.float32) - ((t > i) & (t <= r)).astype(np.float32)
    secs.append(eq)
    secs.append(-eq)
    masks.append(((idx[:, None] // HG_BASE) == (idx[None, :] // HG_BASE)) & (idx[:, None] >= idx[None, :]))
    for g in _hg_levels(L):
        m = g * (i // g) + g // 2
        up = (i % g) >= g // 2
        secs.append((up & (t >= m) & (t <= i)).astype(np.float32))
        secs.append(((~up) & (t > i) & (t <= m - 1)).astype(np.float32))
        ii, jj = idx[:, None], idx[None, :]
        masks.append(((ii // g) == (jj // g)) & ((ii % g) >= g // 2) & ((jj % g) < g // 2))
    mf = np.concatenate(secs, axis=0)
    mb = np.concatenate([s[::-1, ::-1] for s in secs], axis=0)
    kf = np.stack(masks).astype(np.float32)
    kb = np.stack([m[::-1, ::-1] for m in masks]).astype(np.float32)
    return mf, mb, kf, kb


def _hg_gate(z, lb):
    key = (1.0 - lb) * jax.nn.sigmoid(-z)
    return key, jnp.log1p(-key)


def _hg_state_mask():
    r = lax.broadcasted_iota(jnp.int32, (LANES, 2 * HG_DK), 0) // HG_DV
    c = lax.broadcasted_iota(jnp.int32, (LANES, 2 * HG_DK), 1) // HG_DK
    return r == c


def _hg_bwd_state_kernel(z_ref, v_ref, lb_ref, m_ref, sb_ref, st_ref):
    L = z_ref.shape[0]

    @pl.when(pl.program_id(1) == 0)
    def _():
        st_ref[...] = jnp.zeros_like(st_ref)

    key, lf = _hg_gate(z_ref[...], lb_ref[...])
    e = _dot_split(m_ref[...], lf)
    ke = key * jnp.exp(e[:L])
    dec = jnp.exp(e[L:L + 1])
    bmt = _hg_state_mask()
    for p in range(2):
        sb_ref[0, 0, p] = st_ref[p]
        kvt = _dot_tn(v_ref[:, 128 * p:128 * p + 128].astype(BF16), ke[:, 256 * p:256 * p + 256].astype(BF16))
        st_ref[p] = dec[:, 256 * p:256 * p + 256] * st_ref[p] + jnp.where(bmt, kvt, 0.0)


def _hg_main_kernel(h_ref, lb_ref, nw_ref, mf_ref, mb_ref, kf_ref, kb_ref, sb_ref, o_ref, st_ref, e_ref):
    L = h_ref.shape[0]
    nlev = kf_ref.shape[0]

    @pl.when(pl.program_id(1) == 0)
    def _():
        st_ref[...] = jnp.zeros_like(st_ref)

    lo_v = _lane_lo_mask((L, LANES))
    bmt = _hg_state_mask()
    bones = jnp.where(_head_block_ones(), 1.0, 0.0).astype(BF16)
    osum = [jnp.zeros((L, LANES), F32), jnp.zeros((L, LANES), F32)]
    for d in range(2):
        m_ref = mf_ref if d == 0 else mb_ref
        k_ref = kf_ref if d == 0 else kb_ref
        key, lf = _hg_gate(h_ref[:, 512 + 512 * d:1024 + 512 * d], lb_ref[...])
        e_ref[...] = _dot_split(m_ref[...], lf)
        tot_row = L - 1 if d == 0 else 0
        for p in range(2):
            v = h_ref[:, 1536 + 128 * p:1664 + 128 * p]
            vs = jnp.concatenate([jnp.where(lo_v, v, 0.0), jnp.where(lo_v, 0.0, v)], axis=0).astype(BF16)
            a_pair = []
            qe_pair = []
            for hh in range(2):
                hd = 2 * p + hh
                ls = slice(128 * hd, 128 * hd + 128)
                q = h_ref[:, ls]
                kk = key[:, ls]
                a = jnp.zeros((L, L), F32)
                for lv in range(nlev):
                    eq = e_ref[pl.ds((2 + 2 * lv) * L, L), ls]
                    ek = e_ref[pl.ds((3 + 2 * lv) * L, L), ls]
                    sc = _dot_nt((q * jnp.exp(eq)).astype(BF16), (kk * jnp.exp(ek)).astype(BF16))
                    a = a + jnp.where(k_ref[lv] > 0.5, sc, 0.0)
                a_pair.append(a.astype(BF16))
                qe_pair.append((q * jnp.exp(e_ref[pl.ds(0, L), ls])).astype(BF16))
            intra = _dot(jnp.concatenate(a_pair, axis=1), vs)
            state = st_ref[p] if d == 0 else sb_ref[0, 0, p]
            inter = _dot_nt(jnp.concatenate(qe_pair, axis=1), state.astype(BF16))
            osum[p] = osum[p] + intra + inter
            if d == 0:
                ps = slice(256 * p, 256 * p + 256)
                ke = (key[:, ps] * jnp.exp(e_ref[pl.ds(L, L), ps])).astype(BF16)
                dec = jnp.exp(e_ref[pl.ds(tot_row, 1), ps])
                kvt = _dot_tn(v.astype(BF16), ke)
                st_ref[p] = dec * st_ref[p] + jnp.where(bmt, kvt, 0.0)
    for p in range(2):
        o = osum[p]
        ms = _dot_split_rhs_const(o * o, bones) * (1.0 / HG_DV)
        g = h_ref[:, 1792 + 128 * p:1920 + 128 * p]
        y = o * lax.rsqrt(ms + RMS_EPS) * nw_ref[:, 128 * p:128 * p + 128] * jax.nn.sigmoid(g)
        o_ref[:, 128 * p:128 * p + 128] = y.astype(o_ref.dtype)


def _hgrn2(h, lb, nw, consts, B, S):
    L = HG_BLOCK
    nb = S // L
    mf, mb, kf, kb, mpre = consts
    nsec = mf.shape[0]
    nlev = kf.shape[0]
    sb = pl.pallas_call(
        _hg_bwd_state_kernel,
        grid=(B, nb),
        in_specs=[pl.BlockSpec((L, 512), lambda b, n: (b * nb + nb - 1 - n, 2)),
                  pl.BlockSpec((L, 256), lambda b, n: (b * nb + nb - 1 - n, 6)),
                  pl.BlockSpec((1, 512), lambda b, n: (0, 0)),
                  pl.BlockSpec((L + 8, L), lambda b, n: (0, 0))],
        out_specs=pl.BlockSpec((1, 1, 2, LANES, 2 * HG_DK), lambda b, n: (b, nb - 1 - n, 0, 0, 0)),
        out_shape=jax.ShapeDtypeStruct((B, nb, 2, LANES, 2 * HG_DK), F32),
        scratch_shapes=[pltpu.VMEM((2, LANES, 2 * HG_DK), F32)],
        compiler_params=_cparams(("parallel", "arbitrary")),
        name="hg_bwd_state",
    )(h, h, lb, mpre)
    return pl.pallas_call(
        _hg_main_kernel,
        grid=(B, nb),
        in_specs=[pl.BlockSpec((L, HG_W), lambda b, n: (b * nb + n, 0)),
                  pl.BlockSpec((1, 512), lambda b, n: (0, 0)),
                  pl.BlockSpec((1, 256), lambda b, n: (0, 0)),
                  pl.BlockSpec((nsec, L), lambda b, n: (0, 0)),
                  pl.BlockSpec((nsec, L), lambda b, n: (0, 0)),
                  pl.BlockSpec((nlev, L, L), lambda b, n: (0, 0, 0)),
                  pl.BlockSpec((nlev, L, L), lambda b, n: (0, 0, 0)),
                  pl.BlockSpec((1, 1, 2, LANES, 2 * HG_DK), lambda b, n: (b, n, 0, 0, 0))],
        out_specs=pl.BlockSpec((L, 256), lambda b, n: (b * nb + n, 0)),
        out_shape=jax.ShapeDtypeStruct((B * S, 256), BF16),
        scratch_shapes=[pltpu.VMEM((2, LANES, 2 * HG_DK), F32),
                        pltpu.VMEM((nsec, 512), F32)],
        compiler_params=_cparams(("parallel", "arbitrary")),
        name="hg_main",
    )(h, lb, nw, mf, mb, kf, kb, sb)


def _outproj_kernel(x_ref, r_ref, s_ref, g_ref, wr_ref, ws_ref, wg_ref, lg_ref, lb_ref, o_ref):
    mix = _dot(r_ref[...], wr_ref[...]) + _dot(s_ref[...], ws_ref[...]) + _dot(g_ref[...], wg_ref[...])
    o_ref[...] = _layer_norm(DEEPNORM_ALPHA * x_ref[...] + mix, lg_ref[...], lb_ref[...])


def _outproj_router_kernel(x_ref, r_ref, s_ref, g_ref, wr_ref, ws_ref, wg_ref, lg_ref, lb_ref, wrt_ref,
                           o_ref, gate_ref):
    mix = _dot(r_ref[...], wr_ref[...]) + _dot(s_ref[...], ws_ref[...]) + _dot(g_ref[...], wg_ref[...])
    y = _layer_norm(DEEPNORM_ALPHA * x_ref[...] + mix, lg_ref[...], lb_ref[...])
    o_ref[...] = y
    logits = jnp.dot(y, wrt_ref[...], preferred_element_type=F32, precision=lax.Precision.HIGHEST)
    lane = lax.broadcasted_iota(jnp.int32, logits.shape, 1).astype(F32)
    neg = jnp.float32(-jnp.inf)
    l1 = jnp.where(lane < N_EXPERTS, logits, neg)
    m1 = jnp.max(l1, axis=-1, keepdims=True)
    i1 = jnp.min(jnp.where(l1 == m1, lane, float(LANES)), axis=-1, keepdims=True)
    l2 = jnp.where(lane == i1, neg, l1)
    m2 = jnp.max(l2, axis=-1, keepdims=True)
    i2 = jnp.min(jnp.where(l2 == m2, lane, float(LANES)), axis=-1, keepdims=True)
    e2 = jnp.exp(m2 - m1)
    den = 1.0 + e2
    gate_ref[...] = jnp.where(lane == i1, 1.0 / den, 0.0) + jnp.where(lane == i2, e2 / den, 0.0)


def _out_proj(x2d, ret, swa, hgo, w_r, w_s, w_g, ln_g, ln_b, w_router=None):
    T = x2d.shape[0]
    tm = 512
    row = lambda i: (i, 0)
    full = lambda i: (0, 0)
    in_specs = [pl.BlockSpec((tm, D_MODEL), row),
                pl.BlockSpec((tm, 256), row), pl.BlockSpec((tm, 512), row), pl.BlockSpec((tm, 256), row),
                pl.BlockSpec((256, D_MODEL), full), pl.BlockSpec((512, D_MODEL), full),
                pl.BlockSpec((256, D_MODEL), full),
                pl.BlockSpec((1, D_MODEL), full), pl.BlockSpec((1, D_MODEL), full)]
    args = [x2d, ret, swa, hgo, w_r, w_s, w_g, ln_g, ln_b]
    if w_router is None:
        return pl.pallas_call(
            _outproj_kernel, grid=(T // tm,), in_specs=in_specs,
            out_specs=pl.BlockSpec((tm, D_MODEL), row),
            out_shape=jax.ShapeDtypeStruct((T, D_MODEL), F32),
            compiler_params=_cparams(("parallel",)), name="out_proj",
        )(*args)
    return pl.pallas_call(
        _outproj_router_kernel, grid=(T // tm,),
        in_specs=in_specs + [pl.BlockSpec((D_MODEL, LANES), full)],
        out_specs=[pl.BlockSpec((tm, D_MODEL), row), pl.BlockSpec((tm, LANES), row)],
        out_shape=[jax.ShapeDtypeStruct((T, D_MODEL), F32), jax.ShapeDtypeStruct((T, LANES), F32)],
        compiler_params=_cparams(("parallel",)), name="out_proj_router",
    )(*args, w_router)


def _ffn_kernel(x_ref, wg_ref, wu_ref, wd_ref, lg_ref, lb_ref, o_ref, acc_ref, xb_ref):
    j = pl.program_id(1)

    @pl.when(j == 0)
    def _():
        acc_ref[...] = jnp.zeros_like(acc_ref)
        xb_ref[...] = x_ref[...].astype(BF16)

    xb = xb_ref[...]
    g = _dot(xb, wg_ref[...])
    u = _dot(xb, wu_ref[...])
    a = (g * jax.nn.sigmoid(g)) * u
    acc_ref[...] += _dot(a.astype(BF16), wd_ref[...])

    @pl.when(j == pl.num_programs(1) - 1)
    def _():
        o_ref[...] = _layer_norm(DEEPNORM_ALPHA * x_ref[...] + acc_ref[...], lg_ref[...], lb_ref[...])


def _ffn(x2d, wg, wu, wd, ln_g, ln_b):
    T = x2d.shape[0]
    tm, th = 1024, 256
    nh = FFN_DIM // th
    return pl.pallas_call(
        _ffn_kernel, grid=(T // tm, nh),
        in_specs=[pl.BlockSpec((tm, D_MODEL), lambda i, j: (i, 0)),
                  pl.BlockSpec((D_MODEL, th), lambda i, j: (0, j)),
                  pl.BlockSpec((D_MODEL, th), lambda i, j: (0, j)),
                  pl.BlockSpec((th, D_MODEL), lambda i, j: (j, 0)),
                  pl.BlockSpec((1, D_MODEL), lambda i, j: (0, 0)),
                  pl.BlockSpec((1, D_MODEL), lambda i, j: (0, 0))],
        out_specs=pl.BlockSpec((tm, D_MODEL), lambda i, j: (i, 0)),
        out_shape=jax.ShapeDtypeStruct((T, D_MODEL), F32),
        scratch_shapes=[pltpu.VMEM((tm, D_MODEL), F32), pltpu.VMEM((tm, D_MODEL), BF16)],
        compiler_params=_cparams(("parallel", "arbitrary")), name="ffn",
    )(x2d, wg, wu, wd, ln_g, ln_b)


def _moe_dense_kernel(x_ref, gate_ref, wg_ref, wu_ref, wd_ref, lg_ref, lb_ref, o_ref, acc_ref, xb_ref):
    e = pl.program_id(1)
    j = pl.program_id(2)

    @pl.when((e == 0) & (j == 0))
    def _():
        acc_ref[...] = jnp.zeros_like(acc_ref)
        xb_ref[...] = x_ref[...].astype(BF16)

    xb = xb_ref[...]
    g = _dot(xb, wg_ref[...])
    u = _dot(xb, wu_ref[...])
    lane = lax.broadcasted_iota(jnp.int32, gate_ref.shape, 1)
    ge = jnp.sum(jnp.where(lane == e, gate_ref[...], 0.0), axis=-1, keepdims=True)
    a = (g * jax.nn.sigmoid(g)) * u * ge
    acc_ref[...] += _dot(a.astype(BF16), wd_ref[...])

    @pl.when((e == pl.num_programs(1) - 1) & (j == pl.num_programs(2) - 1))
    def _():
        o_ref[...] = _layer_norm(DEEPNORM_ALPHA * x_ref[...] + acc_ref[...], lg_ref[...], lb_ref[...])


def _moe_dense(x2d, gate, wg, wu, wd, ln_g, ln_b):
    T = x2d.shape[0]
    tm, th = 1024, 512
    nh = EXPERT_DIM // th
    return pl.pallas_call(
        _moe_dense_kernel, grid=(T // tm, N_EXPERTS, nh),
        in_specs=[pl.BlockSpec((tm, D_MODEL), lambda i, e, j: (i, 0)),
                  pl.BlockSpec((tm, LANES), lambda i, e, j: (i, 0)),
                  pl.BlockSpec((None, D_MODEL, th), lambda i, e, j: (e, 0, j)),
                  pl.BlockSpec((None, D_MODEL, th), lambda i, e, j: (e, 0, j)),
                  pl.BlockSpec((None, th, D_MODEL), lambda i, e, j: (e, j, 0)),
                  pl.BlockSpec((1, D_MODEL), lambda i, e, j: (0, 0)),
                  pl.BlockSpec((1, D_MODEL), lambda i, e, j: (0, 0))],
        out_specs=pl.BlockSpec((tm, D_MODEL), lambda i, e, j: (i, 0)),
        out_shape=jax.ShapeDtypeStruct((T, D_MODEL), F32),
        scratch_shapes=[pltpu.VMEM((tm, D_MODEL), F32), pltpu.VMEM((tm, D_MODEL), BF16)],
        compiler_params=_cparams(("parallel", "arbitrary", "arbitrary")), name="moe_dense",
    )(x2d, gate, wg, wu, wd, ln_g, ln_b)


def _t5_bucket(rel):
    nb = REL_BUCKETS // 2
    ret = (rel > 0).astype(np.int32) * nb
    n = np.abs(rel)
    max_exact = nb // 2
    large = max_exact + (np.log(np.maximum(n, 1) / max_exact) / np.log(REL_MAX_DIST / max_exact)
                         * (nb - max_exact)).astype(np.int32)
    large = np.minimum(large, nb - 1)
    return ret + np.where(n < max_exact, n, large)


def _swa_head_order():
    return [h for c in range(4) for h in (c, 4 + c)]


def _prepare(w_in, w_out, ret_decay, swa_sink, rel_bias, hg_lb, hg_norm_w, moe_router, seq_lens):
    order = _swa_head_order()
    w_in_p = jnp.concatenate(
        [w_in[:, :, 1792:3840], w_in[:, :, 0:1024]]
        + [w_in[:, :, 1024 + 64 * h:1088 + 64 * h] for h in order]
        + [w_in[:, :, 1536:1792]], axis=2).astype(BF16)
    w_r = w_out[:, 0:256, :].astype(BF16)
    w_s = jnp.concatenate([w_out[:, 256 + 64 * h:320 + 64 * h, :] for h in order], axis=1).astype(BF16)
    w_g = w_out[:, 768:1024, :].astype(BF16)

    rd = ret_decay.astype(F32)
    rdp = jnp.repeat(rd.reshape(DEPTH, 4, 2), 64, axis=-1)
    rdp = jnp.concatenate([rdp, rdp], axis=1)
    rdh = jnp.broadcast_to(rd.reshape(DEPTH, 8, 1), (DEPTH, 8, LANES))

    W = SWA_BLOCK
    rel = np.arange(3 * W)[None, :] - W - np.arange(W)[:, None]
    bias = jnp.transpose(rel_bias.astype(F32)[_t5_bucket(rel)], (2, 0, 1))
    bias_t = jnp.stack([jnp.concatenate([bias[c], bias[4 + c]], axis=0) for c in range(4)])
    sink = swa_sink.astype(F32)
    sink_t = jnp.stack([jnp.concatenate([jnp.broadcast_to(sink[:, c, None, None], (DEPTH, W, LANES)),
                                         jnp.broadcast_to(sink[:, 4 + c, None, None], (DEPTH, W, LANES))], axis=1)
                        for c in range(4)], axis=1)

    p = jax.nn.softmax(hg_lb.astype(F32), axis=0)
    lower = (jnp.cumsum(p, axis=0) - p[0]).reshape(DEPTH, 1, HG_HEADS * HG_DK)
    nw = hg_norm_w.astype(F32).reshape(DEPTH, 1, HG_HEADS * HG_DV)

    mf, mb, kf, kb = _hg_constants(HG_BLOCK)
    L = HG_BLOCK
    mpre = np.concatenate([mb[L:2 * L], np.ones((8, L), np.float32)], axis=0)
    consts = (jnp.asarray(mf, BF16), jnp.asarray(mb, BF16), jnp.asarray(kf, F32), jnp.asarray(kb, F32),
              jnp.asarray(mpre, BF16))

    half = RET_DK // 2
    inv = ROPE_BASE ** (-np.arange(half, dtype=np.float32) / half)
    rope = {}
    for S in set(seq_lens):
        ang = jnp.arange(S, dtype=F32)[:, None] * jnp.asarray(inv)[None, :]
        c32, s32 = jnp.cos(ang), jnp.sin(ang)
        rope[S] = (jnp.tile(c32, (1, 4)), jnp.tile(jnp.concatenate([-s32, s32], axis=1), (1, 2)))

    router = jnp.pad(moe_router.astype(F32), ((0, 0), (0, 0), (0, LANES - N_EXPERTS)))
    return dict(w_in=w_in_p, w_r=w_r, w_s=w_s, w_g=w_g, rdp=rdp, rdh=rdh, bias=bias_t, sink=sink_t,
                lower=lower, nw=nw, consts=consts, rope=rope, router=router)


def _trunk(x, prm, ln1_g, ln1_b, ln2_g, ln2_b, ffn_w, moe_w):
    B, S, _ = x.shape
    x2d = x.reshape(B * S, D_MODEL)
    cos_t, sin_t = prm["rope"][S]
    for l in range(DEPTH):
        h = _in_proj(x2d, prm["w_in"][l])
        ret = _retention(h, cos_t, sin_t, prm["rdp"][l], prm["rdh"][l], B, S)
        swa = _swa(h, prm["bias"], prm["sink"][l], B, S)
        hgo = _hgrn2(h, prm["lower"][l], prm["nw"][l], prm["consts"], B, S)
        g1, b1 = ln1_g[l].reshape(1, D_MODEL), ln1_b[l].reshape(1, D_MODEL)
        g2, b2 = ln2_g[l].reshape(1, D_MODEL), ln2_b[l].reshape(1, D_MODEL)
        i = l // 2
        if l % 2 == 0:
            x1 = _out_proj(x2d, ret, swa, hgo, prm["w_r"][l], prm["w_s"][l], prm["w_g"][l], g1, b1)
            wg, wu, wd = ffn_w
            x2d = _ffn(x1, wg[i], wu[i], wd[i], g2, b2)
        else:
            x1, gate = _out_proj(x2d, ret, swa, hgo, prm["w_r"][l], prm["w_s"][l], prm["w_g"][l], g1, b1,
                                 prm["router"][i])
            wg, wu, wd = moe_w
            x2d = _moe_dense(x1, gate, wg[i], wu[i], wd[i], g2, b2)
    return x2d.reshape(B, S, D_MODEL)


def kernel(x_prompt, x_sample, w_in, w_out, ret_decay, swa_sink, rel_bias, hg_lb, hg_norm_w, ln1_g, ln1_b,
           ln2_g, ln2_b, ffn_w_gate, ffn_w_up, ffn_w_down, moe_router, moe_w_gate, moe_w_up, moe_w_down):
    prm = _prepare(w_in, w_out, ret_decay, swa_sink, rel_bias, hg_lb, hg_norm_w, moe_router,
                   (x_prompt.shape[1], x_sample.shape[1]))
    ffn_w = (ffn_w_gate.astype(BF16), ffn_w_up.astype(BF16), ffn_w_down.astype(BF16))
    moe_w = (moe_w_gate.astype(BF16), moe_w_up.astype(BF16), moe_w_down.astype(BF16))
    outs = []
    for x in (x_prompt, x_sample):
        outs.append(_trunk(x, prm, ln1_g, ln1_b, ln2_g, ln2_b, ffn_w, moe_w))
    return tuple(outs)
```

```python
import functools
import math

import numpy as np
import jax
import jax.numpy as jnp
from jax import lax
from jax.experimental import pallas as pl
from jax.experimental.pallas import tpu as pltpu

F32 = jnp.float32
BF16 = jnp.bfloat16

D_MODEL = 1024
DEPTH = 2
RET_HEADS = 4
RET_DK = 64
RET_CHUNK = 128
ROPE_BASE = 10000.0
SWA_Q_HEADS = 8
SWA_KV_HEADS = 2
SWA_DH = 64
WINDOW = 128
SWA_BLOCK = 128
REL_BUCKETS = 32
REL_MAX_DIST = 128
MASK_VALUE = -1e30
HG_HEADS = 4
HG_DK = 128
HG_DV = 64
FFN_DIM = 2816
N_EXPERTS = 8
EXPERT_DIM = 3584
LN_EPS = 1e-5
RMS_EPS = 1e-6
DEEPNORM_ALPHA = (2.0 * DEPTH) ** 0.25

LANES = 128
VMEM_LIMIT = 56 * 1024 * 1024

HG_W = 2048
RET_W = 1024
SWA_W = 768
IN_COLS = HG_W + RET_W + SWA_W

HG_BLOCK = 256
HG_BASE = 8
RET_TILE = 512
SWA_TILE = 512


def _cparams(sem):
    return pltpu.CompilerParams(dimension_semantics=sem, vmem_limit_bytes=VMEM_LIMIT)


def _dot(a, b):
    return jnp.dot(a, b, preferred_element_type=F32)


def _dot_nt(a, b):
    return lax.dot_general(a, b, (((1,), (1,)), ((), ())), preferred_element_type=F32)


def _dot_tn(a, b):
    return lax.dot_general(a, b, (((0,), (0,)), ((), ())), preferred_element_type=F32)


def _dot_split(a_bf16, x):
    hi = x.astype(BF16)
    lo = (x - hi.astype(F32)).astype(BF16)
    return _dot(a_bf16, hi) + _dot(a_bf16, lo)


def _dot_split_rhs_const(x, b_bf16):
    hi = x.astype(BF16)
    lo = (x - hi.astype(F32)).astype(BF16)
    return _dot(hi, b_bf16) + _dot(lo, b_bf16)


def _layer_norm(y, g, b):
    mu = jnp.mean(y, axis=-1, keepdims=True)
    yc = y - mu
    var = jnp.mean(yc * yc, axis=-1, keepdims=True)
    return yc * lax.rsqrt(var + LN_EPS) * g + b


def _lane_lo_mask(shape):
    return lax.broadcasted_iota(jnp.int32, shape, len(shape) - 1) < (LANES // 2)


def _head_block_ones():
    r = lax.broadcasted_iota(jnp.int32, (LANES, LANES), 0) // 64
    c = lax.broadcasted_iota(jnp.int32, (LANES, LANES), 1) // 64
    return r == c


def _inproj_kernel(x_ref, w_ref, o_ref):
    o_ref[...] = _dot(x_ref[...].astype(BF16), w_ref[...])


def _in_proj(x2d, w):
    T = x2d.shape[0]
    tm, tn = 1024, 768
    return pl.pallas_call(
        _inproj_kernel,
        grid=(T // tm, IN_COLS // tn),
        in_specs=[pl.BlockSpec((tm, D_MODEL), lambda i, j: (i, 0)),
                  pl.BlockSpec((D_MODEL, tn), lambda i, j: (0, j))],
        out_specs=pl.BlockSpec((tm, tn), lambda i, j: (i, j)),
        out_shape=jax.ShapeDtypeStruct((T, IN_COLS), F32),
        compiler_params=_cparams(("parallel", "arbitrary")),
        name="in_proj",
    )(x2d, w)


def _rotary(x, cos, sin_signed, even):
    sw = jnp.where(even, pltpu.roll(x, 96, 1), pltpu.roll(x, 32, 1))
    return x * cos + sw * sin_signed


def _ret_tables(rdp_ref):
    lg = jnp.log1p(-jnp.exp(rdp_ref[...]))
    return lg


def _ret_bwd_state_kernel(h_ref, cos_ref, sin_ref, rdp_ref, sb_ref, st_ref):
    C = RET_CHUNK
    nc = h_ref.shape[0] // C

    @pl.when(pl.program_id(1) == 0)
    def _():
        st_ref[...] = jnp.zeros_like(st_ref)

    lg = _ret_tables(rdp_ref)
    rows = lax.broadcasted_iota(jnp.int32, (C, LANES), 0).astype(F32)
    even = (lax.broadcasted_iota(jnp.int32, (C, LANES), 1) // 32) % 2 == 0
    bm = _head_block_ones()
    for c in range(nc - 1, -1, -1):
        sl = pl.ds(c * C, C)
        cos = cos_ref[sl, :]
        sin = sin_ref[sl, :]
        for p in range(2):
            lgb = lg[2 + p:3 + p, :]
            kwb = jnp.exp(lgb * rows)
            decb = jnp.where(bm, jnp.exp(lgb * float(C)), 0.0)
            k = _rotary(h_ref[sl, 256 + 128 * p:384 + 128 * p], cos, sin, even) * (RET_DK ** -0.5)
            v = h_ref[sl, 512 + 128 * p:640 + 128 * p]
            sb_ref[0, c, p] = st_ref[p]
            kv = _dot_tn((k * kwb).astype(BF16), v.astype(BF16))
            st_ref[p] = decb * st_ref[p] + jnp.where(bm, kv, 0.0)


def _ret_main_kernel(h_ref, cos_ref, sin_ref, rdp_ref, rdh_ref, sb_ref, o_ref, st_ref):
    C = RET_CHUNK
    nc = h_ref.shape[0] // C

    @pl.when(pl.program_id(1) == 0)
    def _():
        st_ref[...] = jnp.zeros_like(st_ref)

    lg = _ret_tables(rdp_ref)
    lgh = jnp.log1p(-jnp.exp(rdh_ref[...]))
    rows = lax.broadcasted_iota(jnp.int32, (C, LANES), 0).astype(F32)
    cols = lax.broadcasted_iota(jnp.int32, (C, LANES), 1).astype(F32)
    rel = rows - cols
    even = (lax.broadcasted_iota(jnp.int32, (C, LANES), 1) // 32) % 2 == 0
    lo = _lane_lo_mask((C, LANES))
    bm = _head_block_ones()
    bones = jnp.where(bm, 1.0, 0.0).astype(BF16)

    dmat = []
    for hd in range(RET_HEADS):
        f = jnp.where(rel >= 0, jnp.exp(lgh[hd:hd + 1, :] * jnp.maximum(rel, 0.0)), 0.0)
        b = jnp.where(rel <= 0, jnp.exp(lgh[4 + hd:5 + hd, :] * jnp.maximum(-rel, 0.0)), 0.0)
        dmat.append(f + b)

    for c in range(nc):
        sl = pl.ds(c * C, C)
        cos = cos_ref[sl, :]
        sin = sin_ref[sl, :]
        for p in range(2):
            lgf = lg[p:p + 1, :]
            lgb = lg[2 + p:3 + p, :]
            q = _rotary(h_ref[sl, 128 * p:128 * p + 128], cos, sin, even)
            k = _rotary(h_ref[sl, 256 + 128 * p:384 + 128 * p], cos, sin, even) * (RET_DK ** -0.5)
            v = h_ref[sl, 512 + 128 * p:640 + 128 * p]
            g = h_ref[sl, 768 + 128 * p:896 + 128 * p]
            kb = k.astype(BF16)
            vb = v.astype(BF16)
            qs = jnp.concatenate([jnp.where(lo, q, 0.0), jnp.where(lo, 0.0, q)], axis=0).astype(BF16)
            s = _dot_nt(qs, kb)
            pm = jnp.concatenate([s[:C] * dmat[2 * p], s[C:] * dmat[2 * p + 1]], axis=1).astype(BF16)
            vs = jnp.concatenate([jnp.where(lo, v, 0.0), jnp.where(lo, 0.0, v)], axis=0).astype(BF16)
            intra = _dot(pm, vs)
            qwf = jnp.exp(lgf * (rows + 1.0))
            qwb = jnp.exp(lgb * (float(C) - rows))
            qi = jnp.concatenate([q * qwf, q * qwb], axis=1).astype(BF16)
            sst = jnp.concatenate([st_ref[p], sb_ref[0, c, p]], axis=0).astype(BF16)
            o = intra + _dot(qi, sst)
            kwf = jnp.exp(lgf * (float(C) - 1.0 - rows))
            decf = jnp.where(bm, jnp.exp(lgf * float(C)), 0.0)
            kv = _dot_tn((k * kwf).astype(BF16), vb)
            st_ref[p] = decf * st_ref[p] + jnp.where(bm, kv, 0.0)
            ms = _dot_split_rhs_const(o * o, bones) * (1.0 / RET_DK)
            y = o * lax.rsqrt(ms + RMS_EPS) * (g * jax.nn.sigmoid(g))
            o_ref[sl, 128 * p:128 * p + 128] = y.astype(o_ref.dtype)


def _retention(h, cos_t, sin_t, rdp, rdh, B, S):
    L = RET_TILE
    nb = S // L
    ncb = L // RET_CHUNK
    n_chunks = S // RET_CHUNK
    colblk = HG_W // RET_W
    sb = pl.pallas_call(
        _ret_bwd_state_kernel,
        grid=(B, nb),
        in_specs=[pl.BlockSpec((L, RET_W), lambda b, n: (b * nb + nb - 1 - n, colblk)),
                  pl.BlockSpec((L, LANES), lambda b, n: (nb - 1 - n, 0)),
                  pl.BlockSpec((L, LANES), lambda b, n: (nb - 1 - n, 0)),
                  pl.BlockSpec((8, LANES), lambda b, n: (0, 0))],
        out_specs=pl.BlockSpec((1, ncb, 2, LANES, LANES), lambda b, n: (b, nb - 1 - n, 0, 0, 0)),
        out_shape=jax.ShapeDtypeStruct((B, n_chunks, 2, LANES, LANES), F32),
        scratch_shapes=[pltpu.VMEM((2, LANES, LANES), F32)],
        compiler_params=_cparams(("parallel", "arbitrary")),
        name="ret_bwd_state",
    )(h, cos_t, sin_t, rdp)
    return pl.pallas_call(
        _ret_main_kernel,
        grid=(B, nb),
        in_specs=[pl.BlockSpec((L, RET_W), lambda b, n: (b * nb + n, colblk)),
                  pl.BlockSpec((L, LANES), lambda b, n: (n, 0)),
                  pl.BlockSpec((L, LANES), lambda b, n: (n, 0)),
                  pl.BlockSpec((8, LANES), lambda b, n: (0, 0)),
                  pl.BlockSpec((8, LANES), lambda b, n: (0, 0)),
                  pl.BlockSpec((1, ncb, 2, LANES, LANES), lambda b, n: (b, n, 0, 0, 0))],
        out_specs=pl.BlockSpec((L, 256), lambda b, n: (b * nb + n, 0)),
        out_shape=jax.ShapeDtypeStruct((B * S, 256), BF16),
        scratch_shapes=[pltpu.VMEM((2, LANES, LANES), F32)],
        compiler_params=_cparams(("parallel", "arbitrary")),
        name="ret_main",
    )(h, cos_t, sin_t, rdp, rdh, sb)


def _swa_kernel(q_ref, kp_ref, kc_ref, kn_ref, vp_ref, vc_ref, vn_ref, bias_ref, sink_ref, o_ref, *, seq_len):
    W = SWA_BLOCK
    nq = q_ref.shape[0] // W
    t0 = pl.program_id(1) * q_ref.shape[0]
    kall = jnp.concatenate([kp_ref[...], kc_ref[...], kn_ref[...]], axis=0).astype(BF16)
    vall = jnp.concatenate([vp_ref[...], vc_ref[...], vn_ref[...]], axis=0)
    lo_v = _lane_lo_mask(vall.shape)
    vlo = jnp.where(lo_v, vall, 0.0).astype(BF16)
    vhi = jnp.where(lo_v, 0.0, vall).astype(BF16)
    lo = _lane_lo_mask((W, LANES))
    qi = lax.broadcasted_iota(jnp.int32, (2 * W, 3 * W), 0) % W
    ki = lax.broadcasted_iota(jnp.int32, (2 * W, 3 * W), 1)
    rel = ki - W - qi
    band = (rel <= WINDOW) & (rel >= -WINDOW)
    for r in range(nq):
        kpos = t0 + (r - 1) * W + ki
        valid = band & (kpos >= 0) & (kpos < seq_len)
        kw = kall[r * W:(r + 3) * W]
        vwl = vlo[r * W:(r + 3) * W]
        vwh = vhi[r * W:(r + 3) * W]
        for c in range(4):
            q = q_ref[pl.ds(r * W, W), 128 * c:128 * c + 128] * (SWA_DH ** -0.5)
            qs = jnp.concatenate([jnp.where(lo, q, 0.0), jnp.where(lo, 0.0, q)], axis=0).astype(BF16)
            s = _dot_nt(qs, kw) + bias_ref[c]
            s = jnp.where(valid, s, MASK_VALUE)
            sink = sink_ref[c][:, 0:1]
            m = jnp.maximum(jnp.max(s, axis=-1, keepdims=True), sink)
            e = jnp.exp(s - m)
            den = jnp.sum(e, axis=-1, keepdims=True) + jnp.exp(sink - m)
            eb = e.astype(BF16)
            o = _dot(eb[:W], vwl) / den[:W] + _dot(eb[W:], vwh) / den[W:]
            o_ref[pl.ds(r * W, W), 128 * c:128 * c + 128] = o.astype(o_ref.dtype)


def _swa(h, bias_t, sink_t, B, S):
    L = SWA_TILE
    W = SWA_BLOCK
    nb = S // L
    r = L // W
    nblk = S // W
    qcol = (HG_W + RET_W) // 512
    kcol = (HG_W + RET_W + 512) // LANES
    vcol = kcol + 1

    def prev_map(col):
        return lambda b, n: (b * nblk + jnp.maximum(n * r - 1, 0), col)

    def cur_map(col):
        return lambda b, n: (b * nb + n, col)

    def next_map(col):
        return lambda b, n: (b * nblk + jnp.minimum(n * r + r, nblk - 1), col)

    return pl.pallas_call(
        functools.partial(_swa_kernel, seq_len=S),
        grid=(B, nb),
        in_specs=[pl.BlockSpec((L, 512), cur_map(qcol)),
                  pl.BlockSpec((W, LANES), prev_map(kcol)),
                  pl.BlockSpec((L, LANES), cur_map(kcol)),
                  pl.BlockSpec((W, LANES), next_map(kcol)),
                  pl.BlockSpec((W, LANES), prev_map(vcol)),
                  pl.BlockSpec((L, LANES), cur_map(vcol)),
                  pl.BlockSpec((W, LANES), next_map(vcol)),
                  pl.BlockSpec((4, 2 * W, 3 * W), lambda b, n: (0, 0, 0)),
                  pl.BlockSpec((4, 2 * W, LANES), lambda b, n: (0, 0, 0))],
        out_specs=pl.BlockSpec((L, 512), lambda b, n: (b * nb + n, 0)),
        out_shape=jax.ShapeDtypeStruct((B * S, 512), BF16),
        compiler_params=_cparams(("parallel", "parallel")),
        name="swa",
    )(h, h, h, h, h, h, h, bias_t, sink_t)


def _hg_levels(L):
    g = L
    out = []
    while g >= 2 * HG_BASE:
        out.append(g)
        g //= 2
    return out


def _hg_constants(L):
    idx = np.arange(L)
    t = idx[None, :]
    i = idx[:, None]
    secs = []
    masks = []
    secs.append((t <= i).astype(np.float32))
    secs.append((t > i).astype(np.float32))
    r = HG_BASE * (i // HG_BASE) + HG_BASE // 2 - 1
    eq = ((t > r) & (t <= i)).astype(np.float32) - ((t > i) & (t <= r)).astype(np.float32)
    secs.append(eq)
    secs.append(-eq)
    masks.append(((idx[:, None] // HG_BASE) == (idx[None, :] // HG_BASE)) & (idx[:, None] >= idx[None, :]))
    for g in _hg_levels(L):
        m = g * (i // g) + g // 2
        up = (i % g) >= g // 2
        secs.append((up & (t >= m) & (t <= i)).astype(np.float32))
        secs.append(((~up) & (t > i) & (t <= m - 1)).astype(np.float32))
        ii, jj = idx[:, None], idx[None, :]
        masks.append(((ii // g) == (jj // g)) & ((ii % g) >= g // 2) & ((jj % g) < g // 2))
    mf = np.concatenate(secs, axis=0)
    mb = np.concatenate([s[::-1, ::-1] for s in secs], axis=0)
    kf = np.stack(masks).astype(np.float32)
    kb = np.stack([m[::-1, ::-1] for m in masks]).astype(np.float32)
    return mf, mb, kf, kb


def _hg_gate(z, lb):
    key = (1.0 - lb) * jax.nn.sigmoid(-z)
    return key, jnp.log1p(-key)


def _hg_state_mask():
    r = lax.broadcasted_iota(jnp.int32, (LANES, 2 * HG_DK), 0) // HG_DV
    c = lax.broadcasted_iota(jnp.int32, (LANES, 2 * HG_DK), 1) // HG_DK
    return r == c


def _hg_bwd_state_kernel(z_ref, v_ref, lb_ref, m_ref, sb_ref, st_ref):
    L = z_ref.shape[0]

    @pl.when(pl.program_id(1) == 0)
    def _():
        st_ref[...] = jnp.zeros_like(st_ref)

    key, lf = _hg_gate(z_ref[...], lb_ref[...])
    e = _dot_split(m_ref[...], lf)
    ke = key * jnp.exp(e[:L])
    dec = jnp.exp(e[L:L + 1])
    bmt = _hg_state_mask()
    for p in range(2):
        sb_ref[0, 0, p] = st_ref[p]
        kvt = _dot_tn(v_ref[:, 128 * p:128 * p + 128].astype(BF16), ke[:, 256 * p:256 * p + 256].astype(BF16))
        st_ref[p] = dec[:, 256 * p:256 * p + 256] * st_ref[p] + jnp.where(bmt, kvt, 0.0)


def _hg_main_kernel(h_ref, lb_ref, nw_ref, mf_ref, mb_ref, kf_ref, kb_ref, sb_ref, o_ref, st_ref, e_ref):
    L = h_ref.shape[0]
    nlev = kf_ref.shape[0]

    @pl.when(pl.program_id(1) == 0)
    def _():
        st_ref[...] = jnp.zeros_like(st_ref)

    lo_v = _lane_lo_mask((L, LANES))
    bmt = _hg_state_mask()
    bones = jnp.where(_head_block_ones(), 1.0, 0.0).astype(BF16)
    osum = [jnp.zeros((L, LANES), F32), jnp.zeros((L, LANES), F32)]
    for d in range(2):
        m_ref = mf_ref if d == 0 else mb_ref
        k_ref = kf_ref if d == 0 else kb_ref
        key, lf = _hg_gate(h_ref[:, 512 + 512 * d:1024 + 512 * d], lb_ref[...])
        e_ref[...] = _dot_split(m_ref[...], lf)
        tot_row = L - 1 if d == 0 else 0
        for p in range(2):
            v = h_ref[:, 1536 + 128 * p:1664 + 128 * p]
            vs = jnp.concatenate([jnp.where(lo_v, v, 0.0), jnp.where(lo_v, 0.0, v)], axis=0).astype(BF16)
            a_pair = []
            qe_pair = []
            for hh in range(2):
                hd = 2 * p + hh
                ls = slice(128 * hd, 128 * hd + 128)
                q = h_ref[:, ls]
                kk = key[:, ls]
                a = jnp.zeros((L, L), F32)
                for lv in range(nlev):
                    eq = e_ref[pl.ds((2 + 2 * lv) * L, L), ls]
                    ek = e_ref[pl.ds((3 + 2 * lv) * L, L), ls]
                    sc = _dot_nt((q * jnp.exp(eq)).astype(BF16), (kk * jnp.exp(ek)).astype(BF16))
                    a = a + jnp.where(k_ref[lv] > 0.5, sc, 0.0)
                a_pair.append(a.astype(BF16))
                qe_pair.append((q * jnp.exp(e_ref[pl.ds(0, L), ls])).astype(BF16))
            intra = _dot(jnp.concatenate(a_pair, axis=1), vs)
            state = st_ref[p] if d == 0 else sb_ref[0, 0, p]
            inter = _dot_nt(jnp.concatenate(qe_pair, axis=1), state.astype(BF16))
            osum[p] = osum[p] + intra + inter
            if d == 0:
                ps = slice(256 * p, 256 * p + 256)
                ke = (key[:, ps] * jnp.exp(e_ref[pl.ds(L, L), ps])).astype(BF16)
                dec = jnp.exp(e_ref[pl.ds(tot_row, 1), ps])
                kvt = _dot_tn(v.astype(BF16), ke)
                st_ref[p] = dec * st_ref[p] + jnp.where(bmt, kvt, 0.0)
    for p in range(2):
        o = osum[p]
        ms = _dot_split_rhs_const(o * o, bones) * (1.0 / HG_DV)
        g = h_ref[:, 1792 + 128 * p:1920 + 128 * p]
        y = o * lax.rsqrt(ms + RMS_EPS) * nw_ref[:, 128 * p:128 * p + 128] * jax.nn.sigmoid(g)
        o_ref[:, 128 * p:128 * p + 128] = y.astype(o_ref.dtype)


def _hgrn2(h, lb, nw, consts, B, S):
    L = HG_BLOCK
    nb = S // L
    mf, mb, kf, kb, mpre = consts
    nsec = mf.shape[0]
    nlev = kf.shape[0]
    sb = pl.pallas_call(
        _hg_bwd_state_kernel,
        grid=(B, nb),
        in_specs=[pl.BlockSpec((L, 512), lambda b, n: (b * nb + nb - 1 - n, 2)),
                  pl.BlockSpec((L, 256), lambda b, n: (b * nb + nb - 1 - n, 6)),
                  pl.BlockSpec((1, 512), lambda b, n: (0, 0)),
                  pl.BlockSpec((L + 8, L), lambda b, n: (0, 0))],
        out_specs=pl.BlockSpec((1, 1, 2, LANES, 2 * HG_DK), lambda b, n: (b, nb - 1 - n, 0, 0, 0)),
        out_shape=jax.ShapeDtypeStruct((B, nb, 2, LANES, 2 * HG_DK), F32),
        scratch_shapes=[pltpu.VMEM((2, LANES, 2 * HG_DK), F32)],
        compiler_params=_cparams(("parallel", "arbitrary")),
        name="hg_bwd_state",
    )(h, h, lb, mpre)
    return pl.pallas_call(
        _hg_main_kernel,
        grid=(B, nb),
        in_specs=[pl.BlockSpec((L, HG_W), lambda b, n: (b * nb + n, 0)),
                  pl.BlockSpec((1, 512), lambda b, n: (0, 0)),
                  pl.BlockSpec((1, 256), lambda b, n: (0, 0)),
                  pl.BlockSpec((nsec, L), lambda b, n: (0, 0)),
                  pl.BlockSpec((nsec, L), lambda b, n: (0, 0)),
                  pl.BlockSpec((nlev, L, L), lambda b, n: (0, 0, 0)),
                  pl.BlockSpec((nlev, L, L), lambda b, n: (0, 0, 0)),
                  pl.BlockSpec((1, 1, 2, LANES, 2 * HG_DK), lambda b, n: (b, n, 0, 0, 0))],
        out_specs=pl.BlockSpec((L, 256), lambda b, n: (b * nb + n, 0)),
        out_shape=jax.ShapeDtypeStruct((B * S, 256), BF16),
        scratch_shapes=[pltpu.VMEM((2, LANES, 2 * HG_DK), F32),
                        pltpu.VMEM((nsec, 512), F32)],
        compiler_params=_cparams(("parallel", "arbitrary")),
        name="hg_main",
    )(h, lb, nw, mf, mb, kf, kb, sb)


def _outproj_kernel(x_ref, r_ref, s_ref, g_ref, wr_ref, ws_ref, wg_ref, lg_ref, lb_ref, o_ref):
    mix = _dot(r_ref[...], wr_ref[...]) + _dot(s_ref[...], ws_ref[...]) + _dot(g_ref[...], wg_ref[...])
    o_ref[...] = _layer_norm(DEEPNORM_ALPHA * x_ref[...] + mix, lg_ref[...], lb_ref[...])


def _outproj_router_kernel(x_ref, r_ref, s_ref, g_ref, wr_ref, ws_ref, wg_ref, lg_ref, lb_ref, wrt_ref,
                           o_ref, ot_ref, route_ref, cnt_ref):
    @pl.when(pl.program_id(0) == 0)
    def _():
        cnt_ref[...] = jnp.zeros_like(cnt_ref)

    mix = _dot(r_ref[...], wr_ref[...]) + _dot(s_ref[...], ws_ref[...]) + _dot(g_ref[...], wg_ref[...])
    y = _layer_norm(DEEPNORM_ALPHA * x_ref[...] + mix, lg_ref[...], lb_ref[...])
    o_ref[...] = y
    ot_ref[...] = _to_token_tiles(y)
    logits = jnp.dot(y, wrt_ref[...], preferred_element_type=F32, precision=lax.Precision.HIGHEST)
    tm = logits.shape[0]
    lane = lax.broadcasted_iota(jnp.int32, logits.shape, 1).astype(F32)
    neg = jnp.float32(-jnp.inf)
    l1 = jnp.where(lane < N_EXPERTS, logits, neg)
    m1 = jnp.max(l1, axis=-1, keepdims=True)
    i1 = jnp.min(jnp.where(l1 == m1, lane, float(LANES)), axis=-1, keepdims=True)
    l2 = jnp.where(lane == i1, neg, l1)
    m2 = jnp.max(l2, axis=-1, keepdims=True)
    i2 = jnp.min(jnp.where(l2 == m2, lane, float(LANES)), axis=-1, keepdims=True)
    e2 = jnp.exp(m2 - m1)
    den = 1.0 + e2
    picked = jnp.where((lane == i1) | (lane == i2), 1.0, 0.0)
    rr = lax.broadcasted_iota(jnp.int32, (tm, tm), 0)
    cc = lax.broadcasted_iota(jnp.int32, (tm, tm), 1)
    before = _dot(jnp.where(rr > cc, 1.0, 0.0).astype(BF16), picked.astype(BF16))
    rank = before + cnt_ref[0:1, :]
    r1 = jnp.sum(jnp.where(lane == i1, rank, 0.0), axis=-1, keepdims=True)
    r2 = jnp.sum(jnp.where(lane == i2, rank, 0.0), axis=-1, keepdims=True)
    cnt_ref[...] = cnt_ref[...] + jnp.sum(picked, axis=0, keepdims=True)
    route = jnp.where(lane == 0.0, i1, 0.0) + jnp.where(lane == 1.0, i2, 0.0)
    route = route + jnp.where(lane == 2.0, 1.0 / den, 0.0) + jnp.where(lane == 3.0, e2 / den, 0.0)
    route_ref[...] = route + jnp.where(lane == 4.0, r1, 0.0) + jnp.where(lane == 5.0, r2, 0.0)


def _out_proj(x2d, ret, swa, hgo, w_r, w_s, w_g, ln_g, ln_b, w_router=None):
    T = x2d.shape[0]
    tm = 512
    row = lambda i: (i, 0)
    full = lambda i: (0, 0)
    in_specs = [pl.BlockSpec((tm, D_MODEL), row),
                pl.BlockSpec((tm, 256), row), pl.BlockSpec((tm, 512), row), pl.BlockSpec((tm, 256), row),
                pl.BlockSpec((256, D_MODEL), full), pl.BlockSpec((512, D_MODEL), full),
                pl.BlockSpec((256, D_MODEL), full),
                pl.BlockSpec((1, D_MODEL), full), pl.BlockSpec((1, D_MODEL), full)]
    args = [x2d, ret, swa, hgo, w_r, w_s, w_g, ln_g, ln_b]
    if w_router is None:
        return pl.pallas_call(
            _outproj_kernel, grid=(T // tm,), in_specs=in_specs,
            out_specs=pl.BlockSpec((tm, D_MODEL), row),
            out_shape=jax.ShapeDtypeStruct((T, D_MODEL), F32),
            compiler_params=_cparams(("parallel",)), name="out_proj",
        )(*args)
    return pl.pallas_call(
        _outproj_router_kernel, grid=(T // tm,),
        in_specs=in_specs + [pl.BlockSpec((D_MODEL, LANES), full)],
        out_specs=[pl.BlockSpec((tm, D_MODEL), row), pl.BlockSpec((tm, 8, LANES), lambda i: (i, 0, 0)),
                   pl.BlockSpec((tm, LANES), row), pl.BlockSpec((8, LANES), full)],
        out_shape=[jax.ShapeDtypeStruct((T, D_MODEL), F32), jax.ShapeDtypeStruct((T, 8, LANES), F32),
                   jax.ShapeDtypeStruct((T, LANES), F32), jax.ShapeDtypeStruct((8, LANES), F32)],
        compiler_params=_cparams(("arbitrary",)), name="out_proj_router",
    )(*args, w_router)


def _ffn_kernel(x_ref, wg_ref, wu_ref, wd_ref, lg_ref, lb_ref, o_ref, acc_ref, xb_ref):
    j = pl.program_id(1)

    @pl.when(j == 0)
    def _():
        acc_ref[...] = jnp.zeros_like(acc_ref)
        xb_ref[...] = x_ref[...].astype(BF16)

    xb = xb_ref[...]
    g = _dot(xb, wg_ref[...])
    u = _dot(xb, wu_ref[...])
    a = (g * jax.nn.sigmoid(g)) * u
    acc_ref[...] += _dot(a.astype(BF16), wd_ref[...])

    @pl.when(j == pl.num_programs(1) - 1)
    def _():
        o_ref[...] = _layer_norm(DEEPNORM_ALPHA * x_ref[...] + acc_ref[...], lg_ref[...], lb_ref[...])


def _ffn(x2d, wg, wu, wd, ln_g, ln_b):
    T = x2d.shape[0]
    tm, th = 1024, 256
    nh = FFN_DIM // th
    return pl.pallas_call(
        _ffn_kernel, grid=(T // tm, nh),
        in_specs=[pl.BlockSpec((tm, D_MODEL), lambda i, j: (i, 0)),
                  pl.BlockSpec((D_MODEL, th), lambda i, j: (0, j)),
                  pl.BlockSpec((D_MODEL, th), lambda i, j: (0, j)),
                  pl.BlockSpec((th, D_MODEL), lambda i, j: (j, 0)),
                  pl.BlockSpec((1, D_MODEL), lambda i, j: (0, 0)),
                  pl.BlockSpec((1, D_MODEL), lambda i, j: (0, 0))],
        out_specs=pl.BlockSpec((tm, D_MODEL), lambda i, j: (i, 0)),
        out_shape=jax.ShapeDtypeStruct((T, D_MODEL), F32),
        scratch_shapes=[pltpu.VMEM((tm, D_MODEL), F32), pltpu.VMEM((tm, D_MODEL), BF16)],
        compiler_params=_cparams(("parallel", "arbitrary")), name="ffn",
    )(x2d, wg, wu, wd, ln_g, ln_b)


MOE_TILE = 1024
MOE_TH = 512
MOE_TOK = 512


def _moe_num_tiles(T):
    return (2 * T) // MOE_TILE + N_EXPERTS + 1


def _moe_tables(route, counts):
    T = route.shape[0]
    tm = MOE_TILE
    cnt = counts[0, :N_EXPERTS].astype(jnp.int32)
    pc = ((cnt + tm - 1) // tm) * tm
    pend = jnp.cumsum(pc)
    pstart = pend - pc
    e = route[:, 0:2].astype(jnp.int32)
    rank = route[:, 4:6].astype(jnp.int32)
    sel = e[:, :, None] == jnp.arange(N_EXPERTS, dtype=jnp.int32)[None, None, :]
    pos = jnp.sum(jnp.where(sel, pstart[None, None, :], 0), axis=-1) + rank
    n_tiles = _moe_num_tiles(T)
    tile_row = jnp.arange(n_tiles, dtype=jnp.int32) * tm
    texp = jnp.minimum(jnp.sum((tile_row[:, None] >= pend[None, :]).astype(jnp.int32), axis=1), N_EXPERTS - 1)
    n_used = pend[-1] // tm
    last = jnp.sum(jnp.where(jnp.arange(n_tiles) == n_used - 1, texp, 0))
    texp = jnp.where(jnp.arange(n_tiles) < n_used, texp, last)
    return pos.reshape(-1), texp, n_used.reshape(1), pstart + cnt


def _to_token_tiles(x):
    cols = jnp.stack([x[:, LANES * c:LANES * c + LANES] for c in range(D_MODEL // LANES)], axis=0)
    return pltpu.einshape("csl->scl", cols)


def _from_token_tiles(xt):
    y = pltpu.einshape("scl->csl", xt)
    return jnp.concatenate([y[c] for c in range(D_MODEL // LANES)], axis=1)


MOE_DMA_UNROLL = 8


def _moe_dispatch_kernel(pos_ref, pad_ref, nused_ref, xt_hbm, xs_hbm, zero_ref, sem, zsem):
    i = pl.program_id(0)
    n = pl.num_programs(0)
    tm = MOE_TOK
    n_tiles = xs_hbm.shape[0] // MOE_TILE

    def zero_copy(row):
        return pltpu.make_async_copy(zero_ref, xs_hbm.at[pl.ds(row, MOE_TILE)], zsem)

    def step_wait(slot):
        for s in range(2):
            pltpu.make_async_copy(xt_hbm.at[pl.ds(0, tm)], xs_hbm.at[pl.ds(0, tm)], sem.at[slot]).wait()

    @pl.when(i == 0)
    def _():
        zero_ref[...] = jnp.zeros_like(zero_ref)
        for e in range(N_EXPERTS):
            zero_copy(pad_ref[e]).start()
        for e in range(N_EXPERTS):
            zero_copy(pad_ref[e]).wait()
        for t in range(n_tiles - N_EXPERTS - 1, n_tiles):
            @pl.when(t >= nused_ref[0])
            def _():
                zero_copy(t * MOE_TILE).start()
                zero_copy(t * MOE_TILE).wait()

    slot = i % 2

    def body(kk, c):
        t0 = i * tm + kk * MOE_DMA_UNROLL
        dst = [pos_ref[2 * t0 + u] for u in range(2 * MOE_DMA_UNROLL)]
        for u in range(2 * MOE_DMA_UNROLL):
            pltpu.make_async_copy(xt_hbm.at[t0 + u // 2], xs_hbm.at[dst[u]], sem.at[slot]).start()
        return c

    lax.fori_loop(0, tm // MOE_DMA_UNROLL, body, 0)

    @pl.when(i > 0)
    def _():
        step_wait(1 - slot)

    @pl.when(i == n - 1)
    def _():
        step_wait(slot)


def _moe_dispatch(x1t, pos, padstart, n_used):
    T = x1t.shape[0]
    P = _moe_num_tiles(T) * MOE_TILE
    return pl.pallas_call(
        _moe_dispatch_kernel,
        grid_spec=pltpu.PrefetchScalarGridSpec(
            num_scalar_prefetch=3, grid=(T // MOE_TOK,),
            in_specs=[pl.BlockSpec(memory_space=pl.ANY)],
            out_specs=pl.BlockSpec(memory_space=pl.ANY),
            scratch_shapes=[pltpu.VMEM((MOE_TILE, 8, LANES), F32), pltpu.SemaphoreType.DMA((2,)),
                            pltpu.SemaphoreType.DMA]),
        out_shape=jax.ShapeDtypeStruct((P, 8, LANES), F32),
        compiler_params=_cparams(("arbitrary",)), name="moe_dispatch",
    )(pos, padstart, n_used, x1t)


def _moe_group_kernel(texp_ref, nused_ref, xs_ref, wg_ref, wu_ref, wd_ref, y_ref, xb_ref, acc_ref):
    i = pl.program_id(0)
    j = pl.program_id(1)

    @pl.when(i < nused_ref[0])
    def _():
        @pl.when(j == 0)
        def _():
            xb_ref[...] = _from_token_tiles(xs_ref[...]).astype(BF16)
            acc_ref[...] = jnp.zeros_like(acc_ref)

        xb = xb_ref[...]
        g = _dot(xb, wg_ref[...])
        u = _dot(xb, wu_ref[...])
        a = (g * jax.nn.sigmoid(g)) * u
        acc_ref[...] += _dot(a.astype(BF16), wd_ref[...])

        @pl.when(j == pl.num_programs(1) - 1)
        def _():
            y_ref[...] = _to_token_tiles(acc_ref[...])

    @pl.when((i >= nused_ref[0]) & (j == 0))
    def _():
        y_ref[...] = jnp.zeros_like(y_ref)


def _moe_group(xs, texp, n_used, wg, wu, wd):
    P = xs.shape[0]
    tm, th = MOE_TILE, MOE_TH
    nh = EXPERT_DIM // th

    def row_map(i, j, texp, nused):
        return (jnp.minimum(i, nused[0] - 1), 0, 0)

    def col(i, j, nused):
        return jnp.where(i < nused[0], j, nh - 1)

    return pl.pallas_call(
        _moe_group_kernel,
        grid_spec=pltpu.PrefetchScalarGridSpec(
            num_scalar_prefetch=2, grid=(P // tm, nh),
            in_specs=[pl.BlockSpec((tm, 8, LANES), row_map),
                      pl.BlockSpec((None, D_MODEL, th), lambda i, j, texp, nused: (texp[i], 0, col(i, j, nused))),
                      pl.BlockSpec((None, D_MODEL, th), lambda i, j, texp, nused: (texp[i], 0, col(i, j, nused))),
                      pl.BlockSpec((None, th, D_MODEL), lambda i, j, texp, nused: (texp[i], col(i, j, nused), 0))],
            out_specs=pl.BlockSpec((tm, 8, LANES), lambda i, j, texp, nused: (i, 0, 0)),
            scratch_shapes=[pltpu.VMEM((tm, D_MODEL), BF16), pltpu.VMEM((tm, D_MODEL), F32)]),
        out_shape=jax.ShapeDtypeStruct((P, 8, LANES), F32),
        compiler_params=_cparams(("arbitrary", "arbitrary")), name="moe_group",
    )(texp, n_used, xs, wg, wu, wd)


def _moe_combine_kernel(pos_ref, x_ref, route_ref, y_hbm, lg_ref, lb_ref, o_ref, buf_ref, sem):
    i = pl.program_id(0)
    tm = x_ref.shape[0]

    def issue(tile, slot):
        def body(kk, c):
            k0 = kk * MOE_DMA_UNROLL
            t0 = tile * tm + k0
            src = [pos_ref[2 * t0 + u] for u in range(2 * MOE_DMA_UNROLL)]
            for u in range(2 * MOE_DMA_UNROLL):
                pltpu.make_async_copy(y_hbm.at[src[u]], buf_ref.at[slot, (u % 2) * tm + k0 + u // 2],
                                      sem.at[slot]).start()
            return c
        lax.fori_loop(0, tm // MOE_DMA_UNROLL, body, 0)

    @pl.when(i == 0)
    def _():
        issue(0, 0)

    slot = i % 2
    pltpu.make_async_copy(y_hbm.at[pl.ds(0, 2 * tm)], buf_ref.at[slot], sem.at[slot]).wait()

    @pl.when(i + 1 < pl.num_programs(0))
    def _():
        issue(i + 1, 1 - slot)

    a = _from_token_tiles(buf_ref[slot, pl.ds(0, tm)])
    b = _from_token_tiles(buf_ref[slot, pl.ds(tm, tm)])
    f = route_ref[:, 2:3] * a + route_ref[:, 3:4] * b
    o_ref[...] = _layer_norm(DEEPNORM_ALPHA * x_ref[...] + f, lg_ref[...], lb_ref[...])


def _moe_combine(x1, route, y, pos, ln_g, ln_b):
    T = x1.shape[0]
    tm = MOE_TOK
    return pl.pallas_call(
        _moe_combine_kernel,
        grid_spec=pltpu.PrefetchScalarGridSpec(
            num_scalar_prefetch=1, grid=(T // tm,),
            in_specs=[pl.BlockSpec((tm, D_MODEL), lambda i, pos: (i, 0)),
                      pl.BlockSpec((tm, LANES), lambda i, pos: (i, 0)),
                      pl.BlockSpec(memory_space=pl.ANY),
                      pl.BlockSpec((1, D_MODEL), lambda i, pos: (0, 0)),
                      pl.BlockSpec((1, D_MODEL), lambda i, pos: (0, 0))],
            out_specs=pl.BlockSpec((tm, D_MODEL), lambda i, pos: (i, 0)),
            scratch_shapes=[pltpu.VMEM((2, 2 * tm, 8, LANES), F32), pltpu.SemaphoreType.DMA((2,))]),
        out_shape=jax.ShapeDtypeStruct((T, D_MODEL), F32),
        compiler_params=_cparams(("arbitrary",)), name="moe_combine",
    )(pos, x1, route, y, ln_g, ln_b)


def _moe(x1, x1t, route, counts, wg, wu, wd, ln_g, ln_b):
    pos, texp, n_used, padstart = _moe_tables(route, counts)
    xs = _moe_dispatch(x1t, pos, padstart, n_used)
    y = _moe_group(xs, texp, n_used, wg, wu, wd)
    return _moe_combine(x1, route, y, pos, ln_g, ln_b)


def _t5_bucket(rel):
    nb = REL_BUCKETS // 2
    ret = (rel > 0).astype(np.int32) * nb
    n = np.abs(rel)
    max_exact = nb // 2
    large = max_exact + (np.log(np.maximum(n, 1) / max_exact) / np.log(REL_MAX_DIST / max_exact)
                         * (nb - max_exact)).astype(np.int32)
    large = np.minimum(large, nb - 1)
    return ret + np.where(n < max_exact, n, large)


def _swa_head_order():
    return [h for c in range(4) for h in (c, 4 + c)]


def _prepare(w_in, w_out, ret_decay, swa_sink, rel_bias, hg_lb, hg_norm_w, moe_router, seq_lens):
    order = _swa_head_order()
    w_in_p = jnp.concatenate(
        [w_in[:, :, 1792:3840], w_in[:, :, 0:1024]]
        + [w_in[:, :, 1024 + 64 * h:1088 + 64 * h] for h in order]
        + [w_in[:, :, 1536:1792]], axis=2).astype(BF16)
    w_r = w_out[:, 0:256, :].astype(BF16)
    w_s = jnp.concatenate([w_out[:, 256 + 64 * h:320 + 64 * h, :] for h in order], axis=1).astype(BF16)
    w_g = w_out[:, 768:1024, :].astype(BF16)

    rd = ret_decay.astype(F32)
    rdp = jnp.repeat(rd.reshape(DEPTH, 4, 2), 64, axis=-1)
    rdp = jnp.concatenate([rdp, rdp], axis=1)
    rdh = jnp.broadcast_to(rd.reshape(DEPTH, 8, 1), (DEPTH, 8, LANES))

    W = SWA_BLOCK
    rel = np.arange(3 * W)[None, :] - W - np.arange(W)[:, None]
    onehot = jnp.asarray(_t5_bucket(rel)[..., None] == np.arange(REL_BUCKETS), F32)
    bias = jnp.einsum("qkb,bh->hqk", onehot, rel_bias.astype(F32), precision=lax.Precision.HIGHEST)
    bias_t = jnp.stack([jnp.concatenate([bias[c], bias[4 + c]], axis=0) for c in range(4)])
    sink = swa_sink.astype(F32)
    sink_t = jnp.stack([jnp.concatenate([jnp.broadcast_to(sink[:, c, None, None], (DEPTH, W, LANES)),
                                         jnp.broadcast_to(sink[:, 4 + c, None, None], (DEPTH, W, LANES))], axis=1)
                        for c in range(4)], axis=1)

    p = jax.nn.softmax(hg_lb.astype(F32), axis=0)
    lower = (jnp.cumsum(p, axis=0) - p[0]).reshape(DEPTH, 1, HG_HEADS * HG_DK)
    nw = hg_norm_w.astype(F32).reshape(DEPTH, 1, HG_HEADS * HG_DV)

    mf, mb, kf, kb = _hg_constants(HG_BLOCK)
    L = HG_BLOCK
    mpre = np.concatenate([mb[L:2 * L], np.ones((8, L), np.float32)], axis=0)
    consts = (jnp.asarray(mf, BF16), jnp.asarray(mb, BF16), jnp.asarray(kf, F32), jnp.asarray(kb, F32),
              jnp.asarray(mpre, BF16))

    half = RET_DK // 2
    inv = ROPE_BASE ** (-np.arange(half, dtype=np.float32) / half)
    rope = {}
    for S in set(seq_lens):
        ang = jnp.arange(S, dtype=F32)[:, None] * jnp.asarray(inv)[None, :]
        c32, s32 = jnp.cos(ang), jnp.sin(ang)
        rope[S] = (jnp.tile(c32, (1, 4)), jnp.tile(jnp.concatenate([-s32, s32], axis=1), (1, 2)))

    router = jnp.pad(moe_router.astype(F32), ((0, 0), (0, 0), (0, LANES - N_EXPERTS)))
    return dict(w_in=w_in_p, w_r=w_r, w_s=w_s, w_g=w_g, rdp=rdp, rdh=rdh, bias=bias_t, sink=sink_t,
                lower=lower, nw=nw, consts=consts, rope=rope, router=router)


def _trunk(x, prm, ln1_g, ln1_b, ln2_g, ln2_b, ffn_w, moe_w):
    B, S, _ = x.shape
    x2d = x.reshape(B * S, D_MODEL)
    cos_t, sin_t = prm["rope"][S]
    for l in range(DEPTH):
        h = _in_proj(x2d, prm["w_in"][l])
        ret = _retention(h, cos_t, sin_t, prm["rdp"][l], prm["rdh"][l], B, S)
        swa = _swa(h, prm["bias"], prm["sink"][l], B, S)
        hgo = _hgrn2(h, prm["lower"][l], prm["nw"][l], prm["consts"], B, S)
        g1, b1 = ln1_g[l].reshape(1, D_MODEL), ln1_b[l].reshape(1, D_MODEL)
        g2, b2 = ln2_g[l].reshape(1, D_MODEL), ln2_b[l].reshape(1, D_MODEL)
        i = l // 2
        if l % 2 == 0:
            x1 = _out_proj(x2d, ret, swa, hgo, prm["w_r"][l], prm["w_s"][l], prm["w_g"][l], g1, b1)
            wg, wu, wd = ffn_w
            x2d = _ffn(x1, wg[i], wu[i], wd[i], g2, b2)
        else:
            x1, x1t, route, counts = _out_proj(x2d, ret, swa, hgo, prm["w_r"][l], prm["w_s"][l], prm["w_g"][l],
                                               g1, b1, prm["router"][i])
            wg, wu, wd = moe_w
            x2d = _moe(x1, x1t, route, counts, wg[i], wu[i], wd[i], g2, b2)
    return x2d.reshape(B, S, D_MODEL)


def kernel(x_prompt, x_sample, w_in, w_out, ret_decay, swa_sink, rel_bias, hg_lb, hg_norm_w, ln1_g, ln1_b,
           ln2_g, ln2_b, ffn_w_gate, ffn_w_up, ffn_w_down, moe_router, moe_w_gate, moe_w_up, moe_w_down):
    prm = _prepare(w_in, w_out, ret_decay, swa_sink, rel_bias, hg_lb, hg_norm_w, moe_router,
                   (x_prompt.shape[1], x_sample.shape[1]))
    ffn_w = (ffn_w_gate.astype(BF16), ffn_w_up.astype(BF16), ffn_w_down.astype(BF16))
    moe_w = (moe_w_gate.astype(BF16), moe_w_up.astype(BF16), moe_w_down.astype(BF16))
    outs = []
    for x in (x_prompt, x_sample):
        outs.append(_trunk(x, prm, ln1_g, ln1_b, ln2_g, ln2_b, ffn_w, moe_w))
    return tuple(outs)
```

```python
import functools
import math

import numpy as np
import jax
import jax.numpy as jnp
from jax import lax
from jax.experimental import pallas as pl
from jax.experimental.pallas import tpu as pltpu

F32 = jnp.float32
BF16 = jnp.bfloat16

D_MODEL = 1024
DEPTH = 2
RET_HEADS = 4
RET_DK = 64
RET_CHUNK = 128
ROPE_BASE = 10000.0
SWA_Q_HEADS = 8
SWA_KV_HEADS = 2
SWA_DH = 64
WINDOW = 128
SWA_BLOCK = 128
REL_BUCKETS = 32
REL_MAX_DIST = 128
MASK_VALUE = -1e30
HG_HEADS = 4
HG_DK = 128
HG_DV = 64
FFN_DIM = 2816
N_EXPERTS = 8
EXPERT_DIM = 3584
LN_EPS = 1e-5
RMS_EPS = 1e-6
DEEPNORM_ALPHA = (2.0 * DEPTH) ** 0.25

LANES = 128
VMEM_LIMIT = 56 * 1024 * 1024

HG_W = 2048
RET_W = 1024
SWA_W = 768
IN_COLS = HG_W + RET_W + SWA_W

HG_BLOCK = 256
HG_BASE = 8
RET_TILE = 512
SWA_TILE = 512


def _cparams(sem):
    return pltpu.CompilerParams(dimension_semantics=sem, vmem_limit_bytes=VMEM_LIMIT)


def _dot(a, b):
    return jnp.dot(a, b, preferred_element_type=F32)


def _dot_nt(a, b):
    return lax.dot_general(a, b, (((1,), (1,)), ((), ())), preferred_element_type=F32)


def _dot_tn(a, b):
    return lax.dot_general(a, b, (((0,), (0,)), ((), ())), preferred_element_type=F32)


def _dot_split(a_bf16, x):
    hi = x.astype(BF16)
    lo = (x - hi.astype(F32)).astype(BF16)
    return _dot(a_bf16, hi) + _dot(a_bf16, lo)


def _dot_split_rhs_const(x, b_bf16):
    hi = x.astype(BF16)
    lo = (x - hi.astype(F32)).astype(BF16)
    return _dot(hi, b_bf16) + _dot(lo, b_bf16)


def _layer_norm(y, g, b):
    mu = jnp.mean(y, axis=-1, keepdims=True)
    yc = y - mu
    var = jnp.mean(yc * yc, axis=-1, keepdims=True)
    return yc * lax.rsqrt(var + LN_EPS) * g + b


def _lane_lo_mask(shape):
    return lax.broadcasted_iota(jnp.int32, shape, len(shape) - 1) < (LANES // 2)


def _head_block_ones():
    r = lax.broadcasted_iota(jnp.int32, (LANES, LANES), 0) // 64
    c = lax.broadcasted_iota(jnp.int32, (LANES, LANES), 1) // 64
    return r == c


def _inproj_kernel(x_ref, w_ref, o_ref):
    o_ref[...] = _dot(x_ref[...].astype(BF16), w_ref[...])


def _in_proj(x2d, w):
    T = x2d.shape[0]
    tm, tn = 1024, 768
    return pl.pallas_call(
        _inproj_kernel,
        grid=(T // tm, IN_COLS // tn),
        in_specs=[pl.BlockSpec((tm, D_MODEL), lambda i, j: (i, 0)),
                  pl.BlockSpec((D_MODEL, tn), lambda i, j: (0, j))],
        out_specs=pl.BlockSpec((tm, tn), lambda i, j: (i, j)),
        out_shape=jax.ShapeDtypeStruct((T, IN_COLS), F32),
        compiler_params=_cparams(("parallel", "arbitrary")),
        name="in_proj",
    )(x2d, w)


def _rotary(x, cos, sin_signed, even):
    sw = jnp.where(even, pltpu.roll(x, 96, 1), pltpu.roll(x, 32, 1))
    return x * cos + sw * sin_signed


def _ret_tables(rdp_ref):
    lg = jnp.log1p(-jnp.exp(rdp_ref[...]))
    return lg


def _ret_bwd_state_kernel(h_ref, cos_ref, sin_ref, rdp_ref, sb_ref, st_ref):
    C = RET_CHUNK
    nc = h_ref.shape[0] // C

    @pl.when(pl.program_id(1) == 0)
    def _():
        st_ref[...] = jnp.zeros_like(st_ref)

    lg = _ret_tables(rdp_ref)
    rows = lax.broadcasted_iota(jnp.int32, (C, LANES), 0).astype(F32)
    even = (lax.broadcasted_iota(jnp.int32, (C, LANES), 1) // 32) % 2 == 0
    bm = _head_block_ones()
    for c in range(nc - 1, -1, -1):
        sl = pl.ds(c * C, C)
        cos = cos_ref[sl, :]
        sin = sin_ref[sl, :]
        for p in range(2):
            lgb = lg[2 + p:3 + p, :]
            kwb = jnp.exp(lgb * rows)
            decb = jnp.where(bm, jnp.exp(lgb * float(C)), 0.0)
            k = _rotary(h_ref[sl, 256 + 128 * p:384 + 128 * p], cos, sin, even) * (RET_DK ** -0.5)
            v = h_ref[sl, 512 + 128 * p:640 + 128 * p]
            sb_ref[0, c, p] = st_ref[p]
            kv = _dot_tn((k * kwb).astype(BF16), v.astype(BF16))
            st_ref[p] = decb * st_ref[p] + jnp.where(bm, kv, 0.0)


def _ret_main_kernel(h_ref, cos_ref, sin_ref, rdp_ref, rdh_ref, sb_ref, o_ref, st_ref):
    C = RET_CHUNK
    nc = h_ref.shape[0] // C

    @pl.when(pl.program_id(1) == 0)
    def _():
        st_ref[...] = jnp.zeros_like(st_ref)

    lg = _ret_tables(rdp_ref)
    lgh = jnp.log1p(-jnp.exp(rdh_ref[...]))
    rows = lax.broadcasted_iota(jnp.int32, (C, LANES), 0).astype(F32)
    cols = lax.broadcasted_iota(jnp.int32, (C, LANES), 1).astype(F32)
    rel = rows - cols
    even = (lax.broadcasted_iota(jnp.int32, (C, LANES), 1) // 32) % 2 == 0
    lo = _lane_lo_mask((C, LANES))
    bm = _head_block_ones()
    bones = jnp.where(bm, 1.0, 0.0).astype(BF16)

    dmat = []
    for hd in range(RET_HEADS):
        f = jnp.where(rel >= 0, jnp.exp(lgh[hd:hd + 1, :] * jnp.maximum(rel, 0.0)), 0.0)
        b = jnp.where(rel <= 0, jnp.exp(lgh[4 + hd:5 + hd, :] * jnp.maximum(-rel, 0.0)), 0.0)
        dmat.append(f + b)

    for c in range(nc):
        sl = pl.ds(c * C, C)
        cos = cos_ref[sl, :]
        sin = sin_ref[sl, :]
        for p in range(2):
            lgf = lg[p:p + 1, :]
            lgb = lg[2 + p:3 + p, :]
            q = _rotary(h_ref[sl, 128 * p:128 * p + 128], cos, sin, even)
            k = _rotary(h_ref[sl, 256 + 128 * p:384 + 128 * p], cos, sin, even) * (RET_DK ** -0.5)
            v = h_ref[sl, 512 + 128 * p:640 + 128 * p]
            g = h_ref[sl, 768 + 128 * p:896 + 128 * p]
            kb = k.astype(BF16)
            vb = v.astype(BF16)
            qs = jnp.concatenate([jnp.where(lo, q, 0.0), jnp.where(lo, 0.0, q)], axis=0).astype(BF16)
            s = _dot_nt(qs, kb)
            pm = jnp.concatenate([s[:C] * dmat[2 * p], s[C:] * dmat[2 * p + 1]], axis=1).astype(BF16)
            vs = jnp.concatenate([jnp.where(lo, v, 0.0), jnp.where(lo, 0.0, v)], axis=0).astype(BF16)
            intra = _dot(pm, vs)
            qwf = jnp.exp(lgf * (rows + 1.0))
            qwb = jnp.exp(lgb * (float(C) - rows))
            qi = jnp.concatenate([q * qwf, q * qwb], axis=1).astype(BF16)
            sst = jnp.concatenate([st_ref[p], sb_ref[0, c, p]], axis=0).astype(BF16)
            o = intra + _dot(qi, sst)
            kwf = jnp.exp(lgf * (float(C) - 1.0 - rows))
            decf = jnp.where(bm, jnp.exp(lgf * float(C)), 0.0)
            kv = _dot_tn((k * kwf).astype(BF16), vb)
            st_ref[p] = decf * st_ref[p] + jnp.where(bm, kv, 0.0)
            ms = _dot_split_rhs_const(o * o, bones) * (1.0 / RET_DK)
            y = o * lax.rsqrt(ms + RMS_EPS) * (g * jax.nn.sigmoid(g))
            o_ref[sl, 128 * p:128 * p + 128] = y.astype(o_ref.dtype)


def _retention(h, cos_t, sin_t, rdp, rdh, B, S):
    L = RET_TILE
    nb = S // L
    ncb = L // RET_CHUNK
    n_chunks = S // RET_CHUNK
    colblk = HG_W // RET_W
    sb = pl.pallas_call(
        _ret_bwd_state_kernel,
        grid=(B, nb),
        in_specs=[pl.BlockSpec((L, RET_W), lambda b, n: (b * nb + nb - 1 - n, colblk)),
                  pl.BlockSpec((L, LANES), lambda b, n: (nb - 1 - n, 0)),
                  pl.BlockSpec((L, LANES), lambda b, n: (nb - 1 - n, 0)),
                  pl.BlockSpec((8, LANES), lambda b, n: (0, 0))],
        out_specs=pl.BlockSpec((1, ncb, 2, LANES, LANES), lambda b, n: (b, nb - 1 - n, 0, 0, 0)),
        out_shape=jax.ShapeDtypeStruct((B, n_chunks, 2, LANES, LANES), F32),
        scratch_shapes=[pltpu.VMEM((2, LANES, LANES), F32)],
        compiler_params=_cparams(("parallel", "arbitrary")),
        name="ret_bwd_state",
    )(h, cos_t, sin_t, rdp)
    return pl.pallas_call(
        _ret_main_kernel,
        grid=(B, nb),
        in_specs=[pl.BlockSpec((L, RET_W), lambda b, n: (b * nb + n, colblk)),
                  pl.BlockSpec((L, LANES), lambda b, n: (n, 0)),
                  pl.BlockSpec((L, LANES), lambda b, n: (n, 0)),
                  pl.BlockSpec((8, LANES), lambda b, n: (0, 0)),
                  pl.BlockSpec((8, LANES), lambda b, n: (0, 0)),
                  pl.BlockSpec((1, ncb, 2, LANES, LANES), lambda b, n: (b, n, 0, 0, 0))],
        out_specs=pl.BlockSpec((L, 256), lambda b, n: (b * nb + n, 0)),
        out_shape=jax.ShapeDtypeStruct((B * S, 256), BF16),
        scratch_shapes=[pltpu.VMEM((2, LANES, LANES), F32)],
        compiler_params=_cparams(("parallel", "arbitrary")),
        name="ret_main",
    )(h, cos_t, sin_t, rdp, rdh, sb)


def _swa_kernel(q_ref, kp_ref, kc_ref, kn_ref, vp_ref, vc_ref, vn_ref, bias_ref, sink_ref, o_ref, *, seq_len):
    W = SWA_BLOCK
    nq = q_ref.shape[0] // W
    t0 = pl.program_id(1) * q_ref.shape[0]
    kall = jnp.concatenate([kp_ref[...], kc_ref[...], kn_ref[...]], axis=0).astype(BF16)
    vall = jnp.concatenate([vp_ref[...], vc_ref[...], vn_ref[...]], axis=0)
    lo_v = _lane_lo_mask(vall.shape)
    vlo = jnp.where(lo_v, vall, 0.0).astype(BF16)
    vhi = jnp.where(lo_v, 0.0, vall).astype(BF16)
    lo = _lane_lo_mask((W, LANES))
    qi = lax.broadcasted_iota(jnp.int32, (2 * W, 3 * W), 0) % W
    ki = lax.broadcasted_iota(jnp.int32, (2 * W, 3 * W), 1)
    rel = ki - W - qi
    band = (rel <= WINDOW) & (rel >= -WINDOW)
    for r in range(nq):
        kpos = t0 + (r - 1) * W + ki
        valid = band & (kpos >= 0) & (kpos < seq_len)
        kw = kall[r * W:(r + 3) * W]
        vwl = vlo[r * W:(r + 3) * W]
        vwh = vhi[r * W:(r + 3) * W]
        for c in range(4):
            q = q_ref[pl.ds(r * W, W), 128 * c:128 * c + 128] * (SWA_DH ** -0.5)
            qs = jnp.concatenate([jnp.where(lo, q, 0.0), jnp.where(lo, 0.0, q)], axis=0).astype(BF16)
            s = _dot_nt(qs, kw) + bias_ref[c]
            s = jnp.where(valid, s, MASK_VALUE)
            sink = sink_ref[c][:, 0:1]
            m = jnp.maximum(jnp.max(s, axis=-1, keepdims=True), sink)
            e = jnp.exp(s - m)
            den = jnp.sum(e, axis=-1, keepdims=True) + jnp.exp(sink - m)
            eb = e.astype(BF16)
            o = _dot(eb[:W], vwl) / den[:W] + _dot(eb[W:], vwh) / den[W:]
            o_ref[pl.ds(r * W, W), 128 * c:128 * c + 128] = o.astype(o_ref.dtype)


def _swa(h, bias_t, sink_t, B, S):
    L = SWA_TILE
    W = SWA_BLOCK
    nb = S // L
    r = L // W
    nblk = S // W
    qcol = (HG_W + RET_W) // 512
    kcol = (HG_W + RET_W + 512) // LANES
    vcol = kcol + 1

    def prev_map(col):
        return lambda b, n: (b * nblk + jnp.maximum(n * r - 1, 0), col)

    def cur_map(col):
        return lambda b, n: (b * nb + n, col)

    def next_map(col):
        return lambda b, n: (b * nblk + jnp.minimum(n * r + r, nblk - 1), col)

    return pl.pallas_call(
        functools.partial(_swa_kernel, seq_len=S),
        grid=(B, nb),
        in_specs=[pl.BlockSpec((L, 512), cur_map(qcol)),
                  pl.BlockSpec((W, LANES), prev_map(kcol)),
                  pl.BlockSpec((L, LANES), cur_map(kcol)),
                  pl.BlockSpec((W, LANES), next_map(kcol)),
                  pl.BlockSpec((W, LANES), prev_map(vcol)),
                  pl.BlockSpec((L, LANES), cur_map(vcol)),
                  pl.BlockSpec((W, LANES), next_map(vcol)),
                  pl.BlockSpec((4, 2 * W, 3 * W), lambda b, n: (0, 0, 0)),
                  pl.BlockSpec((4, 2 * W, LANES), lambda b, n: (0, 0, 0))],
        out_specs=pl.BlockSpec((L, 512), lambda b, n: (b * nb + n, 0)),
        out_shape=jax.ShapeDtypeStruct((B * S, 512), BF16),
        compiler_params=_cparams(("parallel", "parallel")),
        name="swa",
    )(h, h, h, h, h, h, h, bias_t, sink_t)


def _hg_levels(L):
    g = L
    out = []
    while g >= 2 * HG_BASE:
        out.append(g)
        g //= 2
    return out


def _hg_constants(L):
    nb = L // HG_BASE
    idx = np.arange(L)
    t = idx[None, :]
    i = idx[:, None]
    r = HG_BASE * (i // HG_BASE) + HG_BASE // 2 - 1
    secs = [((t > r) & (t <= i)).astype(np.float32) - ((t > i) & (t <= r)).astype(np.float32)]
    u = np.arange(nb)[:, None]
    ru = HG_BASE * u + HG_BASE // 2 - 1
    secs.append((t <= ru).astype(np.float32))
    secs.append((t > ru).astype(np.float32))
    masks = [((idx[:, None] // HG_BASE) == (idx[None, :] // HG_BASE)) & (idx[:, None] >= idx[None, :])]
    upper = []
    for g in _hg_levels(L):
        gb = g // HG_BASE
        ref = HG_BASE * ((u // gb) * gb + gb // 2) - 1
        up = (u % gb) >= gb // 2
        secs.append(np.where(up, (t > ref) & (t <= ru), (t > ru) & (t <= ref)).astype(np.float32))
        upper.append([bool(v) for v in up[:, 0]])
        ii, jj = idx[:, None], idx[None, :]
        masks.append(((ii // g) == (jj // g)) & ((ii % g) >= g // 2) & ((jj % g) < g // 2))
    mf = np.concatenate(secs, axis=0)
    mb = np.concatenate([s[::-1, ::-1] for s in secs], axis=0)
    kf = np.stack(masks).astype(np.float32)
    kb = np.stack([m[::-1, ::-1] for m in masks]).astype(np.float32)
    upper_fb = (tuple(tuple(x) for x in upper), tuple(tuple(x[::-1]) for x in upper))
    return mf, mb, kf, kb, upper_fb


def _hg_gate(z, lb):
    key = (1.0 - lb) * jax.nn.sigmoid(-z)
    return key, jnp.log1p(-key)


def _hg_state_mask():
    r = lax.broadcasted_iota(jnp.int32, (LANES, 2 * HG_DK), 0) // HG_DV
    c = lax.broadcasted_iota(jnp.int32, (LANES, 2 * HG_DK), 1) // HG_DK
    return r == c


def _hg_bwd_state_kernel(z_ref, v_ref, lb_ref, m_ref, sb_ref, st_ref):
    L = z_ref.shape[0]

    @pl.when(pl.program_id(1) == 0)
    def _():
        st_ref[...] = jnp.zeros_like(st_ref)

    key, lf = _hg_gate(z_ref[...], lb_ref[...])
    e = _dot_split(m_ref[...], lf)
    ke = key * jnp.exp(e[:L])
    dec = jnp.exp(e[L:L + 1])
    bmt = _hg_state_mask()
    for p in range(2):
        sb_ref[0, 0, p] = st_ref[p]
        kvt = _dot_tn(v_ref[:, 128 * p:128 * p + 128].astype(BF16), ke[:, 256 * p:256 * p + 256].astype(BF16))
        st_ref[p] = dec[:, 256 * p:256 * p + 256] * st_ref[p] + jnp.where(bmt, kvt, 0.0)


def _hg_scale_blocks(dst_ref, up_ref, low_ref, fac_ref, row0, ls, upper):
    for u, is_up in enumerate(upper):
        rows = slice(HG_BASE * u, HG_BASE * u + HG_BASE)
        src = up_ref if is_up else low_ref
        dst_ref[rows, :] = src[rows, ls] * fac_ref[row0 + u:row0 + u + 1, ls]


def _hg_main_kernel(h_ref, lb_ref, nw_ref, mf_ref, mb_ref, kf_ref, kb_ref, sb_ref, o_ref,
                    st_ref, e_ref, qp_ref, kp_ref, fac_ref, s_ref, *, upper):
    L = h_ref.shape[0]
    nb = L // HG_BASE
    nlev = kf_ref.shape[0]
    all_up = (True,) * nb

    @pl.when(pl.program_id(1) == 0)
    def _():
        st_ref[...] = jnp.zeros_like(st_ref)

    lo_v = _lane_lo_mask((L, LANES))
    bmt = _hg_state_mask()
    bones = jnp.where(_head_block_ones(), 1.0, 0.0).astype(BF16)
    osum = [jnp.zeros((L, LANES), F32), jnp.zeros((L, LANES), F32)]
    for d in range(2):
        m_ref = mf_ref if d == 0 else mb_ref
        k_ref = kf_ref if d == 0 else kb_ref
        key, lf = _hg_gate(h_ref[:, 512 + 512 * d:1024 + 512 * d], lb_ref[...])
        e_ref[...] = _dot_split(m_ref[...], lf)
        eb = e_ref[pl.ds(0, L), :]
        qp_ref[...] = h_ref[:, 0:512] * jnp.exp(eb)
        kp_ref[...] = key * jnp.exp(-eb)
        fac_ref[...] = jnp.exp(e_ref[pl.ds(L, (nlev + 1) * nb), :])
        masks = [k_ref[lv] > 0.5 for lv in range(nlev)]
        for p in range(2):
            v = h_ref[:, 1536 + 128 * p:1664 + 128 * p]
            vs = jnp.concatenate([jnp.where(lo_v, v, 0.0), jnp.where(lo_v, 0.0, v)], axis=0).astype(BF16)
            a_pair = []
            qe_pair = []
            ke_pair = []
            for hh in range(2):
                hd = 2 * p + hh
                ls = slice(128 * hd, 128 * hd + 128)
                a = jnp.where(masks[0], _dot_nt(qp_ref[:, ls].astype(BF16), kp_ref[:, ls].astype(BF16)), 0.0)
                for lv in range(1, nlev):
                    _hg_scale_blocks(s_ref, qp_ref, kp_ref, fac_ref, (1 + lv) * nb, ls, upper[d][lv - 1])
                    sv = s_ref[...].astype(BF16)
                    a = jnp.where(masks[lv], _dot_nt(sv, sv), a)
                a_pair.append(a.astype(BF16))
                _hg_scale_blocks(s_ref, qp_ref, qp_ref, fac_ref, 0, ls, all_up)
                qe_pair.append(s_ref[...].astype(BF16))
                if d == 0:
                    _hg_scale_blocks(s_ref, kp_ref, kp_ref, fac_ref, nb, ls, all_up)
                    ke_pair.append(s_ref[...].astype(BF16))
            intra = _dot(jnp.concatenate(a_pair, axis=1), vs)
            state = st_ref[p] if d == 0 else sb_ref[0, 0, p]
            inter = _dot_nt(jnp.concatenate(qe_pair, axis=1), state.astype(BF16))
            osum[p] = osum[p] + intra + inter
            if d == 0:
                ps = slice(256 * p, 256 * p + 256)
                dec = fac_ref[nb - 1:nb, ps] * fac_ref[2 * nb - 1:2 * nb, ps]
                kvt = _dot_tn(v.astype(BF16), jnp.concatenate(ke_pair, axis=1))
                st_ref[p] = dec * st_ref[p] + jnp.where(bmt, kvt, 0.0)
    for p in range(2):
        o = osum[p]
        ms = _dot_split_rhs_const(o * o, bones) * (1.0 / HG_DV)
        g = h_ref[:, 1792 + 128 * p:1920 + 128 * p]
        y = o * lax.rsqrt(ms + RMS_EPS) * nw_ref[:, 128 * p:128 * p + 128] * jax.nn.sigmoid(g)
        o_ref[:, 128 * p:128 * p + 128] = y.astype(o_ref.dtype)


def _hgrn2(h, lb, nw, consts, B, S):
    L = HG_BLOCK
    nb = S // L
    mf, mb, kf, kb, mpre, upper = consts
    nsec = mf.shape[0]
    nlev = kf.shape[0]
    nfac = nsec - L
    sb = pl.pallas_call(
        _hg_bwd_state_kernel,
        grid=(B, nb),
        in_specs=[pl.BlockSpec((L, 512), lambda b, n: (b * nb + nb - 1 - n, 2)),
                  pl.BlockSpec((L, 256), lambda b, n: (b * nb + nb - 1 - n, 6)),
                  pl.BlockSpec((1, 512), lambda b, n: (0, 0)),
                  pl.BlockSpec((L + 8, L), lambda b, n: (0, 0))],
        out_specs=pl.BlockSpec((1, 1, 2, LANES, 2 * HG_DK), lambda b, n: (b, nb - 1 - n, 0, 0, 0)),
        out_shape=jax.ShapeDtypeStruct((B, nb, 2, LANES, 2 * HG_DK), F32),
        scratch_shapes=[pltpu.VMEM((2, LANES, 2 * HG_DK), F32)],
        compiler_params=_cparams(("parallel", "arbitrary")),
        name="hg_bwd_state",
    )(h, h, lb, mpre)
    return pl.pallas_call(
        functools.partial(_hg_main_kernel, upper=upper),
        grid=(B, nb),
        in_specs=[pl.BlockSpec((L, HG_W), lambda b, n: (b * nb + n, 0)),
                  pl.BlockSpec((1, 512), lambda b, n: (0, 0)),
                  pl.BlockSpec((1, 256), lambda b, n: (0, 0)),
                  pl.BlockSpec((nsec, L), lambda b, n: (0, 0)),
                  pl.BlockSpec((nsec, L), lambda b, n: (0, 0)),
                  pl.BlockSpec((nlev, L, L), lambda b, n: (0, 0, 0)),
                  pl.BlockSpec((nlev, L, L), lambda b, n: (0, 0, 0)),
                  pl.BlockSpec((1, 1, 2, LANES, 2 * HG_DK), lambda b, n: (b, n, 0, 0, 0))],
        out_specs=pl.BlockSpec((L, 256), lambda b, n: (b * nb + n, 0)),
        out_shape=jax.ShapeDtypeStruct((B * S, 256), BF16),
        scratch_shapes=[pltpu.VMEM((2, LANES, 2 * HG_DK), F32),
                        pltpu.VMEM((nsec, 512), F32),
                        pltpu.VMEM((L, 512), F32), pltpu.VMEM((L, 512), F32),
                        pltpu.VMEM((nfac, 512), F32), pltpu.VMEM((L, LANES), F32)],
        compiler_params=_cparams(("parallel", "arbitrary")),
        name="hg_main",
    )(h, lb, nw, mf, mb, kf, kb, sb)


def _outproj_kernel(x_ref, r_ref, s_ref, g_ref, wr_ref, ws_ref, wg_ref, lg_ref, lb_ref, o_ref):
    mix = _dot(r_ref[...], wr_ref[...]) + _dot(s_ref[...], ws_ref[...]) + _dot(g_ref[...], wg_ref[...])
    o_ref[...] = _layer_norm(DEEPNORM_ALPHA * x_ref[...] + mix, lg_ref[...], lb_ref[...])


def _outproj_router_kernel(x_ref, r_ref, s_ref, g_ref, wr_ref, ws_ref, wg_ref, lg_ref, lb_ref, wrt_ref,
                           o_ref, ot_ref, route_ref, cnt_ref):
    @pl.when(pl.program_id(0) == 0)
    def _():
        cnt_ref[...] = jnp.zeros_like(cnt_ref)

    mix = _dot(r_ref[...], wr_ref[...]) + _dot(s_ref[...], ws_ref[...]) + _dot(g_ref[...], wg_ref[...])
    y = _layer_norm(DEEPNORM_ALPHA * x_ref[...] + mix, lg_ref[...], lb_ref[...])
    o_ref[...] = y
    ot_ref[...] = _to_token_tiles(y)
    w = wrt_ref[...]
    w_hi = w.astype(BF16)
    w_lo = (w - w_hi.astype(F32)).astype(BF16)
    y_hi = y.astype(BF16)
    y_lo = (y - y_hi.astype(F32)).astype(BF16)
    logits = _dot(y_hi, w_hi) + _dot(y_lo, w_hi) + _dot(y_hi, w_lo)
    tm = logits.shape[0]
    lane = lax.broadcasted_iota(jnp.int32, logits.shape, 1).astype(F32)
    neg = jnp.float32(-jnp.inf)
    l1 = jnp.where(lane < N_EXPERTS, logits, neg)
    m1 = jnp.max(l1, axis=-1, keepdims=True)
    i1 = jnp.min(jnp.where(l1 == m1, lane, float(LANES)), axis=-1, keepdims=True)
    l2 = jnp.where(lane == i1, neg, l1)
    m2 = jnp.max(l2, axis=-1, keepdims=True)
    i2 = jnp.min(jnp.where(l2 == m2, lane, float(LANES)), axis=-1, keepdims=True)
    e2 = jnp.exp(m2 - m1)
    den = 1.0 + e2
    picked = jnp.where((lane == i1) | (lane == i2), 1.0, 0.0)
    rr = lax.broadcasted_iota(jnp.int32, (tm, tm), 0)
    cc = lax.broadcasted_iota(jnp.int32, (tm, tm), 1)
    before = _dot(jnp.where(rr > cc, 1.0, 0.0).astype(BF16), picked.astype(BF16))
    rank = before + cnt_ref[0:1, :]
    r1 = jnp.sum(jnp.where(lane == i1, rank, 0.0), axis=-1, keepdims=True)
    r2 = jnp.sum(jnp.where(lane == i2, rank, 0.0), axis=-1, keepdims=True)
    cnt_ref[...] = cnt_ref[...] + jnp.sum(picked, axis=0, keepdims=True)
    route = jnp.where(lane == 0.0, i1, 0.0) + jnp.where(lane == 1.0, i2, 0.0)
    route = route + jnp.where(lane == 2.0, 1.0 / den, 0.0) + jnp.where(lane == 3.0, e2 / den, 0.0)
    route_ref[...] = route + jnp.where(lane == 4.0, r1, 0.0) + jnp.where(lane == 5.0, r2, 0.0)


def _out_proj(x2d, ret, swa, hgo, w_r, w_s, w_g, ln_g, ln_b, w_router=None):
    T = x2d.shape[0]
    tm = 512
    row = lambda i: (i, 0)
    full = lambda i: (0, 0)
    in_specs = [pl.BlockSpec((tm, D_MODEL), row),
                pl.BlockSpec((tm, 256), row), pl.BlockSpec((tm, 512), row), pl.BlockSpec((tm, 256), row),
                pl.BlockSpec((256, D_MODEL), full), pl.BlockSpec((512, D_MODEL), full),
                pl.BlockSpec((256, D_MODEL), full),
                pl.BlockSpec((1, D_MODEL), full), pl.BlockSpec((1, D_MODEL), full)]
    args = [x2d, ret, swa, hgo, w_r, w_s, w_g, ln_g, ln_b]
    if w_router is None:
        return pl.pallas_call(
            _outproj_kernel, grid=(T // tm,), in_specs=in_specs,
            out_specs=pl.BlockSpec((tm, D_MODEL), row),
            out_shape=jax.ShapeDtypeStruct((T, D_MODEL), F32),
            compiler_params=_cparams(("parallel",)), name="out_proj",
        )(*args)
    return pl.pallas_call(
        _outproj_router_kernel, grid=(T // tm,),
        in_specs=in_specs + [pl.BlockSpec((D_MODEL, LANES), full)],
        out_specs=[pl.BlockSpec((tm, D_MODEL), row), pl.BlockSpec((tm, 8, LANES), lambda i: (i, 0, 0)),
                   pl.BlockSpec((tm, LANES), row), pl.BlockSpec((8, LANES), full)],
        out_shape=[jax.ShapeDtypeStruct((T, D_MODEL), F32), jax.ShapeDtypeStruct((T, 8, LANES), F32),
                   jax.ShapeDtypeStruct((T, LANES), F32), jax.ShapeDtypeStruct((8, LANES), F32)],
        compiler_params=_cparams(("arbitrary",)), name="out_proj_router",
    )(*args, w_router)


def _ffn_kernel(x_ref, wg_ref, wu_ref, wd_ref, lg_ref, lb_ref, o_ref, acc_ref, xb_ref):
    j = pl.program_id(1)

    @pl.when(j == 0)
    def _():
        acc_ref[...] = jnp.zeros_like(acc_ref)
        xb_ref[...] = x_ref[...].astype(BF16)

    xb = xb_ref[...]
    g = _dot(xb, wg_ref[...])
    u = _dot(xb, wu_ref[...])
    a = (g * jax.nn.sigmoid(g)) * u
    acc_ref[...] += _dot(a.astype(BF16), wd_ref[...])

    @pl.when(j == pl.num_programs(1) - 1)
    def _():
        o_ref[...] = _layer_norm(DEEPNORM_ALPHA * x_ref[...] + acc_ref[...], lg_ref[...], lb_ref[...])


def _ffn(x2d, wg, wu, wd, ln_g, ln_b):
    T = x2d.shape[0]
    tm, th = 1024, 256
    nh = FFN_DIM // th
    return pl.pallas_call(
        _ffn_kernel, grid=(T // tm, nh),
        in_specs=[pl.BlockSpec((tm, D_MODEL), lambda i, j: (i, 0)),
                  pl.BlockSpec((D_MODEL, th), lambda i, j: (0, j)),
                  pl.BlockSpec((D_MODEL, th), lambda i, j: (0, j)),
                  pl.BlockSpec((th, D_MODEL), lambda i, j: (j, 0)),
                  pl.BlockSpec((1, D_MODEL), lambda i, j: (0, 0)),
                  pl.BlockSpec((1, D_MODEL), lambda i, j: (0, 0))],
        out_specs=pl.BlockSpec((tm, D_MODEL), lambda i, j: (i, 0)),
        out_shape=jax.ShapeDtypeStruct((T, D_MODEL), F32),
        scratch_shapes=[pltpu.VMEM((tm, D_MODEL), F32), pltpu.VMEM((tm, D_MODEL), BF16)],
        compiler_params=_cparams(("parallel", "arbitrary")), name="ffn",
    )(x2d, wg, wu, wd, ln_g, ln_b)


MOE_TILE = 1024
MOE_TH = 512
MOE_TOK = 512


def _moe_num_tiles(T):
    return (2 * T) // MOE_TILE + N_EXPERTS + 1


def _moe_tables(route, counts):
    T = route.shape[0]
    tm = MOE_TILE
    cnt = counts[0, :N_EXPERTS].astype(jnp.int32)
    pc = ((cnt + tm - 1) // tm) * tm
    pend = jnp.cumsum(pc)
    pstart = pend - pc
    e = route[:, 0:2].astype(jnp.int32)
    rank = route[:, 4:6].astype(jnp.int32)
    sel = e[:, :, None] == jnp.arange(N_EXPERTS, dtype=jnp.int32)[None, None, :]
    pos = jnp.sum(jnp.where(sel, pstart[None, None, :], 0), axis=-1) + rank
    n_tiles = _moe_num_tiles(T)
    tile_row = jnp.arange(n_tiles, dtype=jnp.int32) * tm
    texp = jnp.minimum(jnp.sum((tile_row[:, None] >= pend[None, :]).astype(jnp.int32), axis=1), N_EXPERTS - 1)
    n_used = pend[-1] // tm
    last = jnp.sum(jnp.where(jnp.arange(n_tiles) == n_used - 1, texp, 0))
    texp = jnp.where(jnp.arange(n_tiles) < n_used, texp, last)
    return pos.reshape(-1), texp, n_used.reshape(1), pstart + cnt


def _to_token_tiles(x):
    cols = jnp.stack([x[:, LANES * c:LANES * c + LANES] for c in range(D_MODEL // LANES)], axis=0)
    return pltpu.einshape("csl->scl", cols)


def _from_token_tiles(xt):
    y = pltpu.einshape("scl->csl", xt)
    return jnp.concatenate([y[c] for c in range(D_MODEL // LANES)], axis=1)


MOE_DMA_UNROLL = 8


def _moe_dispatch_kernel(pos_ref, pad_ref, nused_ref, xt_ref, xs_hbm, zero_ref, sem, zsem):
    i = pl.program_id(0)
    tm = xt_ref.shape[0]
    n_tiles = xs_hbm.shape[0] // MOE_TILE

    def zero_copy(row):
        return pltpu.make_async_copy(zero_ref, xs_hbm.at[pl.ds(row, MOE_TILE)], zsem)

    @pl.when(i == 0)
    def _():
        zero_ref[...] = jnp.zeros_like(zero_ref)
        for e in range(N_EXPERTS):
            zero_copy(pad_ref[e]).start()
        for e in range(N_EXPERTS):
            zero_copy(pad_ref[e]).wait()
        for t in range(n_tiles - N_EXPERTS - 1, n_tiles):
            @pl.when(t >= nused_ref[0])
            def _():
                zero_copy(t * MOE_TILE).start()
                zero_copy(t * MOE_TILE).wait()

    def body(kk, c):
        k0 = kk * MOE_DMA_UNROLL
        t0 = i * tm + k0
        dst = [pos_ref[2 * t0 + u] for u in range(2 * MOE_DMA_UNROLL)]
        for u in range(2 * MOE_DMA_UNROLL):
            pltpu.make_async_copy(xt_ref.at[k0 + u // 2], xs_hbm.at[dst[u]], sem).start()
        return c

    lax.fori_loop(0, tm // MOE_DMA_UNROLL, body, 0)
    for s in range(2):
        pltpu.make_async_copy(xt_ref, xs_hbm.at[pl.ds(0, tm)], sem).wait()


def _moe_dispatch(x1t, pos, padstart, n_used):
    T = x1t.shape[0]
    P = _moe_num_tiles(T) * MOE_TILE
    return pl.pallas_call(
        _moe_dispatch_kernel,
        grid_spec=pltpu.PrefetchScalarGridSpec(
            num_scalar_prefetch=3, grid=(T // MOE_TOK,),
            in_specs=[pl.BlockSpec((MOE_TOK, 8, LANES), lambda i, pos, pad, nused: (i, 0, 0))],
            out_specs=pl.BlockSpec(memory_space=pl.ANY),
            scratch_shapes=[pltpu.VMEM((MOE_TILE, 8, LANES), F32), pltpu.SemaphoreType.DMA,
                            pltpu.SemaphoreType.DMA]),
        out_shape=jax.ShapeDtypeStruct((P, 8, LANES), F32),
        compiler_params=_cparams(("arbitrary",)), name="moe_dispatch",
    )(pos, padstart, n_used, x1t)


def _moe_group_kernel(texp_ref, nused_ref, xs_ref, wg_ref, wu_ref, wd_ref, y_ref, xb_ref, acc_ref):
    i = pl.program_id(0)
    j = pl.program_id(1)

    @pl.when(i < nused_ref[0])
    def _():
        @pl.when(j == 0)
        def _():
            xb_ref[...] = _from_token_tiles(xs_ref[...]).astype(BF16)
            acc_ref[...] = jnp.zeros_like(acc_ref)

        xb = xb_ref[...]
        g = _dot(xb, wg_ref[...])
        u = _dot(xb, wu_ref[...])
        a = (g * jax.nn.sigmoid(g)) * u
        acc_ref[...] += _dot(a.astype(BF16), wd_ref[...])

        @pl.when(j == pl.num_programs(1) - 1)
        def _():
            y_ref[...] = _to_token_tiles(acc_ref[...])

    @pl.when((i >= nused_ref[0]) & (j == 0))
    def _():
        y_ref[...] = jnp.zeros_like(y_ref)


def _moe_group(xs, texp, n_used, wg, wu, wd):
    P = xs.shape[0]
    tm, th = MOE_TILE, MOE_TH
    nh = EXPERT_DIM // th

    def row_map(i, j, texp, nused):
        return (jnp.minimum(i, nused[0] - 1), 0, 0)

    def col(i, j, nused):
        return jnp.where(i < nused[0], j, nh - 1)

    return pl.pallas_call(
        _moe_group_kernel,
        grid_spec=pltpu.PrefetchScalarGridSpec(
            num_scalar_prefetch=2, grid=(P // tm, nh),
            in_specs=[pl.BlockSpec((tm, 8, LANES), row_map),
                      pl.BlockSpec((None, D_MODEL, th), lambda i, j, texp, nused: (texp[i], 0, col(i, j, nused))),
                      pl.BlockSpec((None, D_MODEL, th), lambda i, j, texp, nused: (texp[i], 0, col(i, j, nused))),
                      pl.BlockSpec((None, th, D_MODEL), lambda i, j, texp, nused: (texp[i], col(i, j, nused), 0))],
            out_specs=pl.BlockSpec((tm, 8, LANES), lambda i, j, texp, nused: (i, 0, 0)),
            scratch_shapes=[pltpu.VMEM((tm, D_MODEL), BF16), pltpu.VMEM((tm, D_MODEL), F32)]),
        out_shape=jax.ShapeDtypeStruct((P, 8, LANES), F32),
        compiler_params=_cparams(("arbitrary", "arbitrary")), name="moe_group",
    )(texp, n_used, xs, wg, wu, wd)


def _moe_combine_kernel(pos_ref, x_ref, route_ref, y_hbm, lg_ref, lb_ref, o_ref, buf_ref, sem):
    i = pl.program_id(0)
    tm = x_ref.shape[0]

    def issue(tile, slot):
        def body(kk, c):
            k0 = kk * MOE_DMA_UNROLL
            t0 = tile * tm + k0
            src = [pos_ref[2 * t0 + u] for u in range(2 * MOE_DMA_UNROLL)]
            for u in range(2 * MOE_DMA_UNROLL):
                pltpu.make_async_copy(y_hbm.at[src[u]], buf_ref.at[slot, (u % 2) * tm + k0 + u // 2],
                                      sem.at[slot]).start()
            return c
        lax.fori_loop(0, tm // MOE_DMA_UNROLL, body, 0)

    @pl.when(i == 0)
    def _():
        issue(0, 0)

    slot = i % 2
    pltpu.make_async_copy(y_hbm.at[pl.ds(0, 2 * tm)], buf_ref.at[slot], sem.at[slot]).wait()

    @pl.when(i + 1 < pl.num_programs(0))
    def _():
        issue(i + 1, 1 - slot)

    a = _from_token_tiles(buf_ref[slot, pl.ds(0, tm)])
    b = _from_token_tiles(buf_ref[slot, pl.ds(tm, tm)])
    f = route_ref[:, 2:3] * a + route_ref[:, 3:4] * b
    o_ref[...] = _layer_norm(DEEPNORM_ALPHA * x_ref[...] + f, lg_ref[...], lb_ref[...])


def _moe_combine(x1, route, y, pos, ln_g, ln_b):
    T = x1.shape[0]
    tm = MOE_TOK
    return pl.pallas_call(
        _moe_combine_kernel,
        grid_spec=pltpu.PrefetchScalarGridSpec(
            num_scalar_prefetch=1, grid=(T // tm,),
            in_specs=[pl.BlockSpec((tm, D_MODEL), lambda i, pos: (i, 0)),
                      pl.BlockSpec((tm, LANES), lambda i, pos: (i, 0)),
                      pl.BlockSpec(memory_space=pl.ANY),
                      pl.BlockSpec((1, D_MODEL), lambda i, pos: (0, 0)),
                      pl.BlockSpec((1, D_MODEL), lambda i, pos: (0, 0))],
            out_specs=pl.BlockSpec((tm, D_MODEL), lambda i, pos: (i, 0)),
            scratch_shapes=[pltpu.VMEM((2, 2 * tm, 8, LANES), F32), pltpu.SemaphoreType.DMA((2,))]),
        out_shape=jax.ShapeDtypeStruct((T, D_MODEL), F32),
        compiler_params=_cparams(("arbitrary",)), name="moe_combine",
    )(pos, x1, route, y, ln_g, ln_b)


def _moe(x1, x1t, route, counts, wg, wu, wd, ln_g, ln_b):
    pos, texp, n_used, padstart = _moe_tables(route, counts)
    xs = _moe_dispatch(x1t, pos, padstart, n_used)
    y = _moe_group(xs, texp, n_used, wg, wu, wd)
    return _moe_combine(x1, route, y, pos, ln_g, ln_b)


def _t5_bucket(rel):
    nb = REL_BUCKETS // 2
    ret = (rel > 0).astype(np.int32) * nb
    n = np.abs(rel)
    max_exact = nb // 2
    large = max_exact + (np.log(np.maximum(n, 1) / max_exact) / np.log(REL_MAX_DIST / max_exact)
                         * (nb - max_exact)).astype(np.int32)
    large = np.minimum(large, nb - 1)
    return ret + np.where(n < max_exact, n, large)


def _swa_head_order():
    return [h for c in range(4) for h in (c, 4 + c)]


def _prepare(w_in, w_out, ret_decay, swa_sink, rel_bias, hg_lb, hg_norm_w, moe_router, seq_lens):
    order = _swa_head_order()
    w_in_p = jnp.concatenate(
        [w_in[:, :, 1792:3840], w_in[:, :, 0:1024]]
        + [w_in[:, :, 1024 + 64 * h:1088 + 64 * h] for h in order]
        + [w_in[:, :, 1536:1792]], axis=2).astype(BF16)
    w_r = w_out[:, 0:256, :].astype(BF16)
    w_s = jnp.concatenate([w_out[:, 256 + 64 * h:320 + 64 * h, :] for h in order], axis=1).astype(BF16)
    w_g = w_out[:, 768:1024, :].astype(BF16)

    rd = ret_decay.astype(F32)
    rdp = jnp.repeat(rd.reshape(DEPTH, 4, 2), 64, axis=-1)
    rdp = jnp.concatenate([rdp, rdp], axis=1)
    rdh = jnp.broadcast_to(rd.reshape(DEPTH, 8, 1), (DEPTH, 8, LANES))

    W = SWA_BLOCK
    rel = np.arange(3 * W)[None, :] - W - np.arange(W)[:, None]
    onehot = jnp.asarray(_t5_bucket(rel)[..., None] == np.arange(REL_BUCKETS), F32)
    bias = jnp.einsum("qkb,bh->hqk", onehot, rel_bias.astype(F32), precision=lax.Precision.HIGHEST)
    bias_t = jnp.stack([jnp.concatenate([bias[c], bias[4 + c]], axis=0) for c in range(4)])
    sink = swa_sink.astype(F32)
    sink_t = jnp.stack([jnp.concatenate([jnp.broadcast_to(sink[:, c, None, None], (DEPTH, W, LANES)),
                                         jnp.broadcast_to(sink[:, 4 + c, None, None], (DEPTH, W, LANES))], axis=1)
                        for c in range(4)], axis=1)

    p = jax.nn.softmax(hg_lb.astype(F32), axis=0)
    lower = (jnp.cumsum(p, axis=0) - p[0]).reshape(DEPTH, 1, HG_HEADS * HG_DK)
    nw = hg_norm_w.astype(F32).reshape(DEPTH, 1, HG_HEADS * HG_DV)

    mf, mb, kf, kb, upper = _hg_constants(HG_BLOCK)
    L = HG_BLOCK
    mpre = np.concatenate([np.tril(np.ones((L, L), np.float32), -1), np.ones((8, L), np.float32)], axis=0)
    consts = (jnp.asarray(mf, BF16), jnp.asarray(mb, BF16), jnp.asarray(kf, F32), jnp.asarray(kb, F32),
              jnp.asarray(mpre, BF16), upper)

    half = RET_DK // 2
    inv = ROPE_BASE ** (-np.arange(half, dtype=np.float32) / half)
    rope = {}
    for S in set(seq_lens):
        ang = jnp.arange(S, dtype=F32)[:, None] * jnp.asarray(inv)[None, :]
        c32, s32 = jnp.cos(ang), jnp.sin(ang)
        rope[S] = (jnp.tile(c32, (1, 4)), jnp.tile(jnp.concatenate([-s32, s32], axis=1), (1, 2)))

    router = jnp.pad(moe_router.astype(F32), ((0, 0), (0, 0), (0, LANES - N_EXPERTS)))
    return dict(w_in=w_in_p, w_r=w_r, w_s=w_s, w_g=w_g, rdp=rdp, rdh=rdh, bias=bias_t, sink=sink_t,
                lower=lower, nw=nw, consts=consts, rope=rope, router=router)


def _trunk(x, prm, ln1_g, ln1_b, ln2_g, ln2_b, ffn_w, moe_w):
    B, S, _ = x.shape
    x2d = x.reshape(B * S, D_MODEL)
    cos_t, sin_t = prm["rope"][S]
    for l in range(DEPTH):
        h = _in_proj(x2d, prm["w_in"][l])
        ret = _retention(h, cos_t, sin_t, prm["rdp"][l], prm["rdh"][l], B, S)
        swa = _swa(h, prm["bias"], prm["sink"][l], B, S)
        hgo = _hgrn2(h, prm["lower"][l], prm["nw"][l], prm["consts"], B, S)
        g1, b1 = ln1_g[l].reshape(1, D_MODEL), ln1_b[l].reshape(1, D_MODEL)
        g2, b2 = ln2_g[l].reshape(1, D_MODEL), ln2_b[l].reshape(1, D_MODEL)
        i = l // 2
        if l % 2 == 0:
            x1 = _out_proj(x2d, ret, swa, hgo, prm["w_r"][l], prm["w_s"][l], prm["w_g"][l], g1, b1)
            wg, wu, wd = ffn_w
            x2d = _ffn(x1, wg[i], wu[i], wd[i], g2, b2)
        else:
            x1, x1t, route, counts = _out_proj(x2d, ret, swa, hgo, prm["w_r"][l], prm["w_s"][l], prm["w_g"][l],
                                               g1, b1, prm["router"][i])
            wg, wu, wd = moe_w
            x2d = _moe(x1, x1t, route, counts, wg[i], wu[i], wd[i], g2, b2)
    return x2d.reshape(B, S, D_MODEL)


def kernel(x_prompt, x_sample, w_in, w_out, ret_decay, swa_sink, rel_bias, hg_lb, hg_norm_w, ln1_g, ln1_b,
           ln2_g, ln2_b, ffn_w_gate, ffn_w_up, ffn_w_down, moe_router, moe_w_gate, moe_w_up, moe_w_down):
    prm = _prepare(w_in, w_out, ret_decay, swa_sink, rel_bias, hg_lb, hg_norm_w, moe_router,
                   (x_prompt.shape[1], x_sample.shape[1]))
    ffn_w = (ffn_w_gate.astype(BF16), ffn_w_up.astype(BF16), ffn_w_down.astype(BF16))
    moe_w = (moe_w_gate.astype(BF16), moe_w_up.astype(BF16), moe_w_down.astype(BF16))
    outs = []
    for x in (x_prompt, x_sample):
        outs.append(_trunk(x, prm, ln1_g, ln1_b, ln2_g, ln2_b, ffn_w, moe_w))
    return tuple(outs)
```

```python
import functools
import math

import numpy as np
import jax
import jax.numpy as jnp
from jax import lax
from jax.experimental import pallas as pl
from jax.experimental.pallas import tpu as pltpu

F32 = jnp.float32
BF16 = jnp.bfloat16

D_MODEL = 1024
DEPTH = 2
RET_HEADS = 4
RET_DK = 64
RET_CHUNK = 128
ROPE_BASE = 10000.0
SWA_Q_HEADS = 8
SWA_KV_HEADS = 2
SWA_DH = 64
WINDOW = 128
SWA_BLOCK = 128
REL_BUCKETS = 32
REL_MAX_DIST = 128
MASK_VALUE = -1e30
HG_HEADS = 4
HG_DK = 128
HG_DV = 64
FFN_DIM = 2816
N_EXPERTS = 8
EXPERT_DIM = 3584
LN_EPS = 1e-5
RMS_EPS = 1e-6
DEEPNORM_ALPHA = (2.0 * DEPTH) ** 0.25

LANES = 128
VMEM_LIMIT = 56 * 1024 * 1024

HG_W = 2048
RET_W = 1024
SWA_W = 768
IN_COLS = HG_W + RET_W + SWA_W

HG_BLOCK = 256
HG_BASE = 8
RET_TILE = 512
SWA_TILE = 512


def _cparams(sem):
    return pltpu.CompilerParams(dimension_semantics=sem, vmem_limit_bytes=VMEM_LIMIT)


def _dot(a, b):
    return jnp.dot(a, b, preferred_element_type=F32)


def _dot_nt(a, b):
    return lax.dot_general(a, b, (((1,), (1,)), ((), ())), preferred_element_type=F32)


def _dot_tn(a, b):
    return lax.dot_general(a, b, (((0,), (0,)), ((), ())), preferred_element_type=F32)


def _dot_split(a_bf16, x):
    hi = x.astype(BF16)
    lo = (x - hi.astype(F32)).astype(BF16)
    return _dot(a_bf16, hi) + _dot(a_bf16, lo)


def _dot_split_rhs_const(x, b_bf16):
    hi = x.astype(BF16)
    lo = (x - hi.astype(F32)).astype(BF16)
    return _dot(hi, b_bf16) + _dot(lo, b_bf16)


def _layer_norm(y, g, b):
    mu = jnp.mean(y, axis=-1, keepdims=True)
    yc = y - mu
    var = jnp.mean(yc * yc, axis=-1, keepdims=True)
    return yc * lax.rsqrt(var + LN_EPS) * g + b


def _lane_lo_mask(shape):
    return lax.broadcasted_iota(jnp.int32, shape, len(shape) - 1) < (LANES // 2)


def _head_block_ones():
    r = lax.broadcasted_iota(jnp.int32, (LANES, LANES), 0) // 64
    c = lax.broadcasted_iota(jnp.int32, (LANES, LANES), 1) // 64
    return r == c


def _inproj_kernel(x_ref, w_ref, o_ref):
    o_ref[...] = _dot(x_ref[...].astype(BF16), w_ref[...])


def _in_proj(x2d, w):
    T = x2d.shape[0]
    tm, tn = 2048, 1280
    return pl.pallas_call(
        _inproj_kernel,
        grid=(T // tm, IN_COLS // tn),
        in_specs=[pl.BlockSpec((tm, D_MODEL), lambda i, j: (i, 0)),
                  pl.BlockSpec((D_MODEL, tn), lambda i, j: (0, j))],
        out_specs=pl.BlockSpec((tm, tn), lambda i, j: (i, j)),
        out_shape=jax.ShapeDtypeStruct((T, IN_COLS), F32),
        compiler_params=_cparams(("parallel", "arbitrary")),
        name="in_proj",
    )(x2d, w)


def _rotary(x, cos, sin_signed, even):
    sw = jnp.where(even, pltpu.roll(x, 96, 1), pltpu.roll(x, 32, 1))
    return x * cos + sw * sin_signed


def _ret_tables(rdp_ref):
    lg = jnp.log1p(-jnp.exp(rdp_ref[...]))
    return lg


def _ret_bwd_state_kernel(h_ref, cos_ref, sin_ref, rdp_ref, sb_ref, st_ref):
    C = RET_CHUNK
    nc = h_ref.shape[0] // C

    @pl.when(pl.program_id(1) == 0)
    def _():
        st_ref[...] = jnp.zeros_like(st_ref)

    lg = _ret_tables(rdp_ref)
    rows = lax.broadcasted_iota(jnp.int32, (C, LANES), 0).astype(F32)
    even = (lax.broadcasted_iota(jnp.int32, (C, LANES), 1) // 32) % 2 == 0
    bm = _head_block_ones()
    for c in range(nc - 1, -1, -1):
        sl = pl.ds(c * C, C)
        cos = cos_ref[sl, :]
        sin = sin_ref[sl, :]
        for p in range(2):
            lgb = lg[2 + p:3 + p, :]
            kwb = jnp.exp(lgb * rows)
            decb = jnp.where(bm, jnp.exp(lgb * float(C)), 0.0)
            k = _rotary(h_ref[sl, 256 + 128 * p:384 + 128 * p], cos, sin, even) * (RET_DK ** -0.5)
            v = h_ref[sl, 512 + 128 * p:640 + 128 * p]
            sb_ref[0, c, p] = st_ref[p]
            kv = _dot_tn((k * kwb).astype(BF16), v.astype(BF16))
            st_ref[p] = decb * st_ref[p] + jnp.where(bm, kv, 0.0)


def _ret_main_kernel(h_ref, cos_ref, sin_ref, rdp_ref, rdh_ref, sb_ref, o_ref, st_ref):
    C = RET_CHUNK
    nc = h_ref.shape[0] // C

    @pl.when(pl.program_id(1) == 0)
    def _():
        st_ref[...] = jnp.zeros_like(st_ref)

    lg = _ret_tables(rdp_ref)
    lgh = jnp.log1p(-jnp.exp(rdh_ref[...]))
    rows = lax.broadcasted_iota(jnp.int32, (C, LANES), 0).astype(F32)
    cols = lax.broadcasted_iota(jnp.int32, (C, LANES), 1).astype(F32)
    rel = rows - cols
    even = (lax.broadcasted_iota(jnp.int32, (C, LANES), 1) // 32) % 2 == 0
    lo = _lane_lo_mask((C, LANES))
    bm = _head_block_ones()
    bones = jnp.where(bm, 1.0, 0.0).astype(BF16)

    dmat = []
    for hd in range(RET_HEADS):
        f = jnp.where(rel >= 0, jnp.exp(lgh[hd:hd + 1, :] * jnp.maximum(rel, 0.0)), 0.0)
        b = jnp.where(rel <= 0, jnp.exp(lgh[4 + hd:5 + hd, :] * jnp.maximum(-rel, 0.0)), 0.0)
        dmat.append(f + b)

    for c in range(nc):
        sl = pl.ds(c * C, C)
        cos = cos_ref[sl, :]
        sin = sin_ref[sl, :]
        for p in range(2):
            lgf = lg[p:p + 1, :]
            lgb = lg[2 + p:3 + p, :]
            q = _rotary(h_ref[sl, 128 * p:128 * p + 128], cos, sin, even)
            k = _rotary(h_ref[sl, 256 + 128 * p:384 + 128 * p], cos, sin, even) * (RET_DK ** -0.5)
            v = h_ref[sl, 512 + 128 * p:640 + 128 * p]
            g = h_ref[sl, 768 + 128 * p:896 + 128 * p]
            kb = k.astype(BF16)
            vb = v.astype(BF16)
            qs = jnp.concatenate([jnp.where(lo, q, 0.0), jnp.where(lo, 0.0, q)], axis=0).astype(BF16)
            s = _dot_nt(qs, kb)
            pm = jnp.concatenate([s[:C] * dmat[2 * p], s[C:] * dmat[2 * p + 1]], axis=1).astype(BF16)
            vs = jnp.concatenate([jnp.where(lo, v, 0.0), jnp.where(lo, 0.0, v)], axis=0).astype(BF16)
            intra = _dot(pm, vs)
            qwf = jnp.exp(lgf * (rows + 1.0))
            qwb = jnp.exp(lgb * (float(C) - rows))
            qi = jnp.concatenate([q * qwf, q * qwb], axis=1).astype(BF16)
            sst = jnp.concatenate([st_ref[p], sb_ref[0, c, p]], axis=0).astype(BF16)
            o = intra + _dot(qi, sst)
            kwf = jnp.exp(lgf * (float(C) - 1.0 - rows))
            decf = jnp.where(bm, jnp.exp(lgf * float(C)), 0.0)
            kv = _dot_tn((k * kwf).astype(BF16), vb)
            st_ref[p] = decf * st_ref[p] + jnp.where(bm, kv, 0.0)
            ms = _dot_split_rhs_const(o * o, bones) * (1.0 / RET_DK)
            y = o * lax.rsqrt(ms + RMS_EPS) * (g * jax.nn.sigmoid(g))
            o_ref[sl, 128 * p:128 * p + 128] = y.astype(o_ref.dtype)


def _retention(h, cos_t, sin_t, rdp, rdh, B, S):
    L = RET_TILE
    nb = S // L
    ncb = L // RET_CHUNK
    n_chunks = S // RET_CHUNK
    colblk = HG_W // RET_W
    sb = pl.pallas_call(
        _ret_bwd_state_kernel,
        grid=(B, nb),
        in_specs=[pl.BlockSpec((L, RET_W), lambda b, n: (b * nb + nb - 1 - n, colblk)),
                  pl.BlockSpec((L, LANES), lambda b, n: (nb - 1 - n, 0)),
                  pl.BlockSpec((L, LANES), lambda b, n: (nb - 1 - n, 0)),
                  pl.BlockSpec((8, LANES), lambda b, n: (0, 0))],
        out_specs=pl.BlockSpec((1, ncb, 2, LANES, LANES), lambda b, n: (b, nb - 1 - n, 0, 0, 0)),
        out_shape=jax.ShapeDtypeStruct((B, n_chunks, 2, LANES, LANES), F32),
        scratch_shapes=[pltpu.VMEM((2, LANES, LANES), F32)],
        compiler_params=_cparams(("parallel", "arbitrary")),
        name="ret_bwd_state",
    )(h, cos_t, sin_t, rdp)
    return pl.pallas_call(
        _ret_main_kernel,
        grid=(B, nb),
        in_specs=[pl.BlockSpec((L, RET_W), lambda b, n: (b * nb + n, colblk)),
                  pl.BlockSpec((L, LANES), lambda b, n: (n, 0)),
                  pl.BlockSpec((L, LANES), lambda b, n: (n, 0)),
                  pl.BlockSpec((8, LANES), lambda b, n: (0, 0)),
                  pl.BlockSpec((8, LANES), lambda b, n: (0, 0)),
                  pl.BlockSpec((1, ncb, 2, LANES, LANES), lambda b, n: (b, n, 0, 0, 0))],
        out_specs=pl.BlockSpec((L, 256), lambda b, n: (b * nb + n, 0)),
        out_shape=jax.ShapeDtypeStruct((B * S, 256), BF16),
        scratch_shapes=[pltpu.VMEM((2, LANES, LANES), F32)],
        compiler_params=_cparams(("parallel", "arbitrary")),
        name="ret_main",
    )(h, cos_t, sin_t, rdp, rdh, sb)


def _swa_kernel(q_ref, kp_ref, kc_ref, kn_ref, vp_ref, vc_ref, vn_ref, bias_ref, sink_ref, o_ref):
    W = SWA_BLOCK
    nq = q_ref.shape[0] // W
    n = pl.program_id(1)
    last = pl.num_programs(1) - 1
    kall = jnp.concatenate([kp_ref[...], kc_ref[...], kn_ref[...]], axis=0).astype(BF16)
    vall = jnp.concatenate([vp_ref[...], vc_ref[...], vn_ref[...]], axis=0)
    lo_v = _lane_lo_mask(vall.shape)
    vlo = jnp.where(lo_v, vall, 0.0).astype(BF16)
    vhi = jnp.where(lo_v, 0.0, vall).astype(BF16)
    lo = _lane_lo_mask((W, LANES))
    ki = lax.broadcasted_iota(jnp.int32, (1, 3 * W), 1)
    for r in range(nq):
        kw = kall[r * W:(r + 3) * W]
        vwl = vlo[r * W:(r + 3) * W]
        vwh = vhi[r * W:(r + 3) * W]
        edge = None
        if r == 0:
            edge = jnp.where(ki >= W, 0.0, jnp.where(n > 0, 0.0, MASK_VALUE))
        if r == nq - 1:
            e2 = jnp.where(ki < 2 * W, 0.0, jnp.where(n < last, 0.0, MASK_VALUE))
            edge = e2 if edge is None else edge + e2
        for c in range(4):
            q = q_ref[pl.ds(r * W, W), 128 * c:128 * c + 128] * (SWA_DH ** -0.5)
            qs = jnp.concatenate([jnp.where(lo, q, 0.0), jnp.where(lo, 0.0, q)], axis=0).astype(BF16)
            s = _dot_nt(qs, kw) + jnp.concatenate([bias_ref[pl.ds(c * W, W), :], bias_ref[pl.ds((4 + c) * W, W), :]],
                                                  axis=0)
            if edge is not None:
                s = s + edge
            sink = jnp.concatenate([sink_ref[pl.ds(c * W, W), 0:1], sink_ref[pl.ds((4 + c) * W, W), 0:1]], axis=0)
            m = jnp.maximum(jnp.max(s, axis=-1, keepdims=True), sink)
            e = jnp.exp(s - m)
            den = jnp.sum(e, axis=-1, keepdims=True) + jnp.exp(sink - m)
            eb = e.astype(BF16)
            o = _dot(eb[:W], vwl) / den[:W] + _dot(eb[W:], vwh) / den[W:]
            o_ref[pl.ds(r * W, W), 128 * c:128 * c + 128] = o.astype(o_ref.dtype)


def _swa(h, bias_t, sink_t, B, S):
    L = SWA_TILE
    W = SWA_BLOCK
    nb = S // L
    r = L // W
    nblk = S // W
    qcol = (HG_W + RET_W) // 512
    kcol = (HG_W + RET_W + 512) // LANES
    vcol = kcol + 1

    def prev_map(col):
        return lambda b, n: (b * nblk + jnp.maximum(n * r - 1, 0), col)

    def cur_map(col):
        return lambda b, n: (b * nb + n, col)

    def next_map(col):
        return lambda b, n: (b * nblk + jnp.minimum(n * r + r, nblk - 1), col)

    return pl.pallas_call(
        _swa_kernel,
        grid=(B, nb),
        in_specs=[pl.BlockSpec((L, 512), cur_map(qcol)),
                  pl.BlockSpec((W, LANES), prev_map(kcol)),
                  pl.BlockSpec((L, LANES), cur_map(kcol)),
                  pl.BlockSpec((W, LANES), next_map(kcol)),
                  pl.BlockSpec((W, LANES), prev_map(vcol)),
                  pl.BlockSpec((L, LANES), cur_map(vcol)),
                  pl.BlockSpec((W, LANES), next_map(vcol)),
                  pl.BlockSpec((SWA_Q_HEADS * W, 3 * W), lambda b, n: (0, 0)),
                  pl.BlockSpec((SWA_Q_HEADS * W, LANES), lambda b, n: (0, 0))],
        out_specs=pl.BlockSpec((L, 512), lambda b, n: (b * nb + n, 0)),
        out_shape=jax.ShapeDtypeStruct((B * S, 512), BF16),
        compiler_params=_cparams(("parallel", "parallel")),
        name="swa",
    )(h, h, h, h, h, h, h, bias_t, sink_t)


def _hg_levels(L):
    g = L
    out = []
    while g >= 2 * HG_BASE:
        out.append(g)
        g //= 2
    return out


def _hg_constants(L):
    nb = L // HG_BASE
    idx = np.arange(L)
    t = idx[None, :]
    i = idx[:, None]
    r = HG_BASE * (i // HG_BASE) + HG_BASE // 2 - 1
    secs = [((t > r) & (t <= i)).astype(np.float32) - ((t > i) & (t <= r)).astype(np.float32)]
    u = np.arange(nb)[:, None]
    ru = HG_BASE * u + HG_BASE // 2 - 1
    secs.append((t <= ru).astype(np.float32))
    secs.append((t > ru).astype(np.float32))
    masks = [((idx[:, None] // HG_BASE) == (idx[None, :] // HG_BASE)) & (idx[:, None] >= idx[None, :])]
    upper = []
    for g in _hg_levels(L):
        gb = g // HG_BASE
        ref = HG_BASE * ((u // gb) * gb + gb // 2) - 1
        up = (u % gb) >= gb // 2
        secs.append(np.where(up, (t > ref) & (t <= ru), (t > ru) & (t <= ref)).astype(np.float32))
        upper.append([bool(v) for v in up[:, 0]])
        ii, jj = idx[:, None], idx[None, :]
        masks.append(((ii // g) == (jj // g)) & ((ii % g) >= g // 2) & ((jj % g) < g // 2))
    mf = np.concatenate(secs, axis=0)
    mb = np.concatenate([s[::-1, ::-1] for s in secs], axis=0)
    H = L // 2
    sub = [masks[0][:H, :H]] + [m[:H, :H] for m in masks[2:]]
    kf = np.stack(sub).astype(np.float32)
    kb = np.stack([m[::-1, ::-1] for m in sub]).astype(np.float32)
    upper_fb = (tuple(tuple(x) for x in upper), tuple(tuple(x[::-1]) for x in upper))
    return mf, mb, kf, kb, upper_fb


def _hg_gate(z, lb):
    key = (1.0 - lb) * jax.nn.sigmoid(-z)
    return key, jnp.log1p(-key)


def _hg_state_mask():
    r = lax.broadcasted_iota(jnp.int32, (LANES, 2 * HG_DK), 0) // HG_DV
    c = lax.broadcasted_iota(jnp.int32, (LANES, 2 * HG_DK), 1) // HG_DK
    return r == c


def _hg_bwd_state_kernel(z_ref, v_ref, lb_ref, m_ref, sb_ref, st_ref):
    L = z_ref.shape[0]

    @pl.when(pl.program_id(1) == 0)
    def _():
        st_ref[...] = jnp.zeros_like(st_ref)

    key, lf = _hg_gate(z_ref[...], lb_ref[...])
    e = _dot_split(m_ref[...], lf)
    ke = key * jnp.exp(e[:L])
    dec = jnp.exp(e[L:L + 1])
    bmt = _hg_state_mask()
    for p in range(2):
        sb_ref[0, 0, p] = st_ref[p]
        kvt = _dot_tn(v_ref[:, 128 * p:128 * p + 128].astype(BF16), ke[:, 256 * p:256 * p + 256].astype(BF16))
        st_ref[p] = dec[:, 256 * p:256 * p + 256] * st_ref[p] + jnp.where(bmt, kvt, 0.0)


def _hg_scale_blocks(dst_ref, up_ref, low_ref, fac_ref, row0, ls, upper):
    for u, is_up in enumerate(upper):
        rows = slice(HG_BASE * u, HG_BASE * u + HG_BASE)
        src = up_ref if is_up else low_ref
        dst_ref[rows, :] = src[rows, ls] * fac_ref[row0 + u:row0 + u + 1, ls]


def _hg_main_kernel(h_ref, lb_ref, nw_ref, mf_ref, mb_ref, kf_ref, kb_ref, sb_ref, o_ref,
                    st_ref, e_ref, qp_ref, kp_ref, fac_ref, s_ref, *, upper):
    L = h_ref.shape[0]
    nb = L // HG_BASE
    nlev = kf_ref.shape[0] + 1
    all_up = (True,) * nb

    @pl.when(pl.program_id(1) == 0)
    def _():
        st_ref[...] = jnp.zeros_like(st_ref)

    lo_v = _lane_lo_mask((L, LANES))
    bmt = _hg_state_mask()
    bones = jnp.where(_head_block_ones(), 1.0, 0.0).astype(BF16)
    osum = [jnp.zeros((L, LANES), F32), jnp.zeros((L, LANES), F32)]
    for d in range(2):
        m_ref = mf_ref if d == 0 else mb_ref
        k_ref = kf_ref if d == 0 else kb_ref
        key, lf = _hg_gate(h_ref[:, 512 + 512 * d:1024 + 512 * d], lb_ref[...])
        e_ref[...] = _dot_split(m_ref[...], lf)
        eb = e_ref[pl.ds(0, L), :]
        qp_ref[...] = h_ref[:, 0:512] * jnp.exp(eb)
        kp_ref[...] = key * jnp.exp(-eb)
        fac_ref[...] = jnp.exp(e_ref[pl.ds(L, (nlev + 1) * nb), :])
        H = L // 2
        halves = (slice(0, H), slice(H, L))
        qh, kh = (1, 0) if d == 0 else (0, 1)
        masks = [k_ref[lv] > 0.5 for lv in range(nlev - 1)]
        for p in range(2):
            v = h_ref[:, 1536 + 128 * p:1664 + 128 * p]
            vlo = jnp.where(lo_v, v, 0.0).astype(BF16)
            vhi = jnp.where(lo_v, 0.0, v).astype(BF16)
            diag = [[None, None], [None, None]]
            cross = [None, None]
            qe_pair = []
            ke_pair = []
            for hh in range(2):
                hd = 2 * p + hh
                ls = slice(128 * hd, 128 * hd + 128)
                for hb in range(2):
                    hs = halves[hb]
                    diag[hh][hb] = jnp.where(masks[0], _dot_nt(qp_ref[hs, ls].astype(BF16),
                                                               kp_ref[hs, ls].astype(BF16)), 0.0)
                for lv in range(1, nlev):
                    _hg_scale_blocks(s_ref, qp_ref, kp_ref, fac_ref, (1 + lv) * nb, ls, upper[d][lv - 1])
                    if lv == 1:
                        cross[hh] = _dot_nt(s_ref[halves[qh], :].astype(BF16),
                                            s_ref[halves[kh], :].astype(BF16)).astype(BF16)
                    else:
                        for hb in range(2):
                            sv = s_ref[halves[hb], :].astype(BF16)
                            diag[hh][hb] = jnp.where(masks[lv - 1], _dot_nt(sv, sv), diag[hh][hb])
                _hg_scale_blocks(s_ref, qp_ref, qp_ref, fac_ref, 0, ls, all_up)
                qe_pair.append(s_ref[...].astype(BF16))
                if d == 0:
                    _hg_scale_blocks(s_ref, kp_ref, kp_ref, fac_ref, nb, ls, all_up)
                    ke_pair.append(s_ref[...].astype(BF16))
            dg = [[diag[hh][hb].astype(BF16) for hb in range(2)] for hh in range(2)]
            vs = jnp.concatenate([vlo, vhi], axis=0)
            if d == 0:
                o_q = _dot(jnp.concatenate([cross[0], dg[0][1], cross[1], dg[1][1]], axis=1), vs)
                o_k = _dot(jnp.concatenate([dg[0][0], dg[1][0]], axis=1),
                           jnp.concatenate([vlo[halves[0]], vhi[halves[0]]], axis=0))
                intra = jnp.concatenate([o_k, o_q], axis=0)
            else:
                o_q = _dot(jnp.concatenate([dg[0][0], cross[0], dg[1][0], cross[1]], axis=1), vs)
                o_k = _dot(jnp.concatenate([dg[0][1], dg[1][1]], axis=1),
                           jnp.concatenate([vlo[halves[1]], vhi[halves[1]]], axis=0))
                intra = jnp.concatenate([o_q, o_k], axis=0)
            state = st_ref[p] if d == 0 else sb_ref[0, 0, p]
            inter = _dot_nt(jnp.concatenate(qe_pair, axis=1), state.astype(BF16))
            osum[p] = osum[p] + intra + inter
            if d == 0:
                ps = slice(256 * p, 256 * p + 256)
                dec = fac_ref[nb - 1:nb, ps] * fac_ref[2 * nb - 1:2 * nb, ps]
                kvt = _dot_tn(v.astype(BF16), jnp.concatenate(ke_pair, axis=1))
                st_ref[p] = dec * st_ref[p] + jnp.where(bmt, kvt, 0.0)
    for p in range(2):
        o = osum[p]
        ms = _dot_split_rhs_const(o * o, bones) * (1.0 / HG_DV)
        g = h_ref[:, 1792 + 128 * p:1920 + 128 * p]
        y = o * lax.rsqrt(ms + RMS_EPS) * nw_ref[:, 128 * p:128 * p + 128] * jax.nn.sigmoid(g)
        o_ref[:, 128 * p:128 * p + 128] = y.astype(o_ref.dtype)


def _hgrn2(h, lb, nw, consts, B, S):
    L = HG_BLOCK
    nb = S // L
    mf, mb, kf, kb, mpre, upper = consts
    nsec = mf.shape[0]
    nfac = nsec - L
    sb = pl.pallas_call(
        _hg_bwd_state_kernel,
        grid=(B, nb),
        in_specs=[pl.BlockSpec((L, 512), lambda b, n: (b * nb + nb - 1 - n, 2)),
                  pl.BlockSpec((L, 256), lambda b, n: (b * nb + nb - 1 - n, 6)),
                  pl.BlockSpec((1, 512), lambda b, n: (0, 0)),
                  pl.BlockSpec((L + 8, L), lambda b, n: (0, 0))],
        out_specs=pl.BlockSpec((1, 1, 2, LANES, 2 * HG_DK), lambda b, n: (b, nb - 1 - n, 0, 0, 0)),
        out_shape=jax.ShapeDtypeStruct((B, nb, 2, LANES, 2 * HG_DK), F32),
        scratch_shapes=[pltpu.VMEM((2, LANES, 2 * HG_DK), F32)],
        compiler_params=_cparams(("parallel", "arbitrary")),
        name="hg_bwd_state",
    )(h, h, lb, mpre)
    return pl.pallas_call(
        functools.partial(_hg_main_kernel, upper=upper),
        grid=(B, nb),
        in_specs=[pl.BlockSpec((L, HG_W), lambda b, n: (b * nb + n, 0)),
                  pl.BlockSpec((1, 512), lambda b, n: (0, 0)),
                  pl.BlockSpec((1, 256), lambda b, n: (0, 0)),
                  pl.BlockSpec((nsec, L), lambda b, n: (0, 0)),
                  pl.BlockSpec((nsec, L), lambda b, n: (0, 0)),
                  pl.BlockSpec(kf.shape, lambda b, n: (0, 0, 0)),
                  pl.BlockSpec(kb.shape, lambda b, n: (0, 0, 0)),
                  pl.BlockSpec((1, 1, 2, LANES, 2 * HG_DK), lambda b, n: (b, n, 0, 0, 0))],
        out_specs=pl.BlockSpec((L, 256), lambda b, n: (b * nb + n, 0)),
        out_shape=jax.ShapeDtypeStruct((B * S, 256), BF16),
        scratch_shapes=[pltpu.VMEM((2, LANES, 2 * HG_DK), F32),
                        pltpu.VMEM((nsec, 512), F32),
                        pltpu.VMEM((L, 512), F32), pltpu.VMEM((L, 512), F32),
                        pltpu.VMEM((nfac, 512), F32), pltpu.VMEM((L, LANES), F32)],
        compiler_params=_cparams(("parallel", "arbitrary")),
        name="hg_main",
    )(h, lb, nw, mf, mb, kf, kb, sb)


def _outproj_kernel(x_ref, r_ref, s_ref, g_ref, wr_ref, ws_ref, wg_ref, lg_ref, lb_ref, o_ref):
    mix = _dot(r_ref[...], wr_ref[...]) + _dot(s_ref[...], ws_ref[...]) + _dot(g_ref[...], wg_ref[...])
    o_ref[...] = _layer_norm(DEEPNORM_ALPHA * x_ref[...] + mix, lg_ref[...], lb_ref[...])


def _outproj_router_kernel(x_ref, r_ref, s_ref, g_ref, wr_ref, ws_ref, wg_ref, lg_ref, lb_ref, wrt_ref,
                           o_ref, ot_ref, route_ref, cnt_ref):
    @pl.when(pl.program_id(0) == 0)
    def _():
        cnt_ref[...] = jnp.zeros_like(cnt_ref)

    mix = _dot(r_ref[...], wr_ref[...]) + _dot(s_ref[...], ws_ref[...]) + _dot(g_ref[...], wg_ref[...])
    y = _layer_norm(DEEPNORM_ALPHA * x_ref[...] + mix, lg_ref[...], lb_ref[...])
    o_ref[...] = y
    ot_ref[...] = _to_token_tiles(y)
    w = wrt_ref[...]
    w_hi = w.astype(BF16)
    w_lo = (w - w_hi.astype(F32)).astype(BF16)
    y_hi = y.astype(BF16)
    y_lo = (y - y_hi.astype(F32)).astype(BF16)
    logits = _dot(y_hi, w_hi) + _dot(y_lo, w_hi) + _dot(y_hi, w_lo)
    tm = logits.shape[0]
    lane = lax.broadcasted_iota(jnp.int32, logits.shape, 1).astype(F32)
    neg = jnp.float32(-jnp.inf)
    l1 = jnp.where(lane < N_EXPERTS, logits, neg)
    m1 = jnp.max(l1, axis=-1, keepdims=True)
    i1 = jnp.min(jnp.where(l1 == m1, lane, float(LANES)), axis=-1, keepdims=True)
    l2 = jnp.where(lane == i1, neg, l1)
    m2 = jnp.max(l2, axis=-1, keepdims=True)
    i2 = jnp.min(jnp.where(l2 == m2, lane, float(LANES)), axis=-1, keepdims=True)
    e2 = jnp.exp(m2 - m1)
    den = 1.0 + e2
    picked = jnp.where((lane == i1) | (lane == i2), 1.0, 0.0)
    rr = lax.broadcasted_iota(jnp.int32, (tm, tm), 0)
    cc = lax.broadcasted_iota(jnp.int32, (tm, tm), 1)
    before = _dot(jnp.where(rr > cc, 1.0, 0.0).astype(BF16), picked.astype(BF16))
    rank = before + cnt_ref[0:1, :]
    r1 = jnp.sum(jnp.where(lane == i1, rank, 0.0), axis=-1, keepdims=True)
    r2 = jnp.sum(jnp.where(lane == i2, rank, 0.0), axis=-1, keepdims=True)
    cnt_ref[...] = cnt_ref[...] + jnp.sum(picked, axis=0, keepdims=True)
    route = jnp.where(lane == 0.0, i1, 0.0) + jnp.where(lane == 1.0, i2, 0.0)
    route = route + jnp.where(lane == 2.0, 1.0 / den, 0.0) + jnp.where(lane == 3.0, e2 / den, 0.0)
    route_ref[...] = route + jnp.where(lane == 4.0, r1, 0.0) + jnp.where(lane == 5.0, r2, 0.0)


def _out_proj(x2d, ret, swa, hgo, w_r, w_s, w_g, ln_g, ln_b, w_router=None):
    T = x2d.shape[0]
    tm = 512
    row = lambda i: (i, 0)
    full = lambda i: (0, 0)
    in_specs = [pl.BlockSpec((tm, D_MODEL), row),
                pl.BlockSpec((tm, 256), row), pl.BlockSpec((tm, 512), row), pl.BlockSpec((tm, 256), row),
                pl.BlockSpec((256, D_MODEL), full), pl.BlockSpec((512, D_MODEL), full),
                pl.BlockSpec((256, D_MODEL), full),
                pl.BlockSpec((1, D_MODEL), full), pl.BlockSpec((1, D_MODEL), full)]
    args = [x2d, ret, swa, hgo, w_r, w_s, w_g, ln_g, ln_b]
    if w_router is None:
        return pl.pallas_call(
            _outproj_kernel, grid=(T // tm,), in_specs=in_specs,
            out_specs=pl.BlockSpec((tm, D_MODEL), row),
            out_shape=jax.ShapeDtypeStruct((T, D_MODEL), F32),
            compiler_params=_cparams(("parallel",)), name="out_proj",
        )(*args)
    return pl.pallas_call(
        _outproj_router_kernel, grid=(T // tm,),
        in_specs=in_specs + [pl.BlockSpec((D_MODEL, LANES), full)],
        out_specs=[pl.BlockSpec((tm, D_MODEL), row), pl.BlockSpec((tm, 8, LANES), lambda i: (i, 0, 0)),
                   pl.BlockSpec((tm, LANES), row), pl.BlockSpec((8, LANES), full)],
        out_shape=[jax.ShapeDtypeStruct((T, D_MODEL), F32), jax.ShapeDtypeStruct((T, 8, LANES), F32),
                   jax.ShapeDtypeStruct((T, LANES), F32), jax.ShapeDtypeStruct((8, LANES), F32)],
        compiler_params=_cparams(("arbitrary",)), name="out_proj_router",
    )(*args, w_router)


FFN_CHUNKS = 2


def _ffn_kernel(x_ref, wg_ref, wu_ref, wd_ref, lg_ref, lb_ref, o_ref):
    x = x_ref[...]
    xb = x.astype(BF16)
    th = FFN_DIM // FFN_CHUNKS
    f = jnp.zeros(x.shape, F32)
    for c in range(FFN_CHUNKS):
        g = _dot(xb, wg_ref[:, c * th:(c + 1) * th])
        u = _dot(xb, wu_ref[:, c * th:(c + 1) * th])
        a = (g * jax.nn.sigmoid(g)) * u
        f = f + _dot(a.astype(BF16), wd_ref[c * th:(c + 1) * th, :])
    o_ref[...] = _layer_norm(DEEPNORM_ALPHA * x + f, lg_ref[...], lb_ref[...])


def _ffn(x2d, wg, wu, wd, ln_g, ln_b):
    T = x2d.shape[0]
    tm = 512
    once = pl.Buffered(1)
    return pl.pallas_call(
        _ffn_kernel, grid=(T // tm,),
        in_specs=[pl.BlockSpec((tm, D_MODEL), lambda i: (i, 0)),
                  pl.BlockSpec((D_MODEL, FFN_DIM), lambda i: (0, 0), pipeline_mode=once),
                  pl.BlockSpec((D_MODEL, FFN_DIM), lambda i: (0, 0), pipeline_mode=once),
                  pl.BlockSpec((FFN_DIM, D_MODEL), lambda i: (0, 0), pipeline_mode=once),
                  pl.BlockSpec((1, D_MODEL), lambda i: (0, 0)),
                  pl.BlockSpec((1, D_MODEL), lambda i: (0, 0))],
        out_specs=pl.BlockSpec((tm, D_MODEL), lambda i: (i, 0)),
        out_shape=jax.ShapeDtypeStruct((T, D_MODEL), F32),
        compiler_params=_cparams(("parallel",)), name="ffn",
    )(x2d, wg, wu, wd, ln_g, ln_b)


MOE_TILE = 512
MOE_TH = 1792
MOE_TOK = 512


def _moe_num_tiles(T):
    return (2 * T) // MOE_TILE + N_EXPERTS + 1


def _moe_tables(route, counts):
    T = route.shape[0]
    tm = MOE_TILE
    cnt = counts[0, :N_EXPERTS].astype(jnp.int32)
    pc = ((cnt + tm - 1) // tm) * tm
    pend = jnp.cumsum(pc)
    pstart = pend - pc
    e = route[:, 0:2].astype(jnp.int32)
    rank = route[:, 4:6].astype(jnp.int32)
    sel = e[:, :, None] == jnp.arange(N_EXPERTS, dtype=jnp.int32)[None, None, :]
    pos = jnp.sum(jnp.where(sel, pstart[None, None, :], 0), axis=-1) + rank
    n_tiles = _moe_num_tiles(T)
    tile_row = jnp.arange(n_tiles, dtype=jnp.int32) * tm
    texp = jnp.minimum(jnp.sum((tile_row[:, None] >= pend[None, :]).astype(jnp.int32), axis=1), N_EXPERTS - 1)
    n_used = pend[-1] // tm
    last = jnp.sum(jnp.where(jnp.arange(n_tiles) == n_used - 1, texp, 0))
    texp = jnp.where(jnp.arange(n_tiles) < n_used, texp, last)
    return pos.reshape(-1), texp, n_used.reshape(1), pstart + cnt


def _to_token_tiles(x):
    cols = jnp.stack([x[:, LANES * c:LANES * c + LANES] for c in range(D_MODEL // LANES)], axis=0)
    return pltpu.einshape("csl->scl", cols)


def _from_token_tiles(xt):
    y = pltpu.einshape("scl->csl", xt)
    return jnp.concatenate([y[c] for c in range(D_MODEL // LANES)], axis=1)


MOE_DMA_UNROLL = 8


def _moe_dispatch_kernel(pos_ref, pad_ref, nused_ref, xt_ref, xs_hbm, zero_ref, sem, zsem):
    i = pl.program_id(0)
    tm = xt_ref.shape[0]
    n_tiles = xs_hbm.shape[0] // MOE_TILE

    def zero_copy(row):
        return pltpu.make_async_copy(zero_ref, xs_hbm.at[pl.ds(row, MOE_TILE)], zsem)

    @pl.when(i == 0)
    def _():
        zero_ref[...] = jnp.zeros_like(zero_ref)
        for e in range(N_EXPERTS):
            zero_copy(pad_ref[e]).start()
        for e in range(N_EXPERTS):
            zero_copy(pad_ref[e]).wait()
        for t in range(n_tiles - N_EXPERTS - 1, n_tiles):
            @pl.when(t >= nused_ref[0])
            def _():
                zero_copy(t * MOE_TILE).start()
                zero_copy(t * MOE_TILE).wait()

    def body(kk, c):
        k0 = kk * MOE_DMA_UNROLL
        t0 = i * tm + k0
        dst = [pos_ref[2 * t0 + u] for u in range(2 * MOE_DMA_UNROLL)]
        for u in range(2 * MOE_DMA_UNROLL):
            pltpu.make_async_copy(xt_ref.at[k0 + u // 2], xs_hbm.at[dst[u]], sem).start()
        return c

    lax.fori_loop(0, tm // MOE_DMA_UNROLL, body, 0)
    for s in range(2):
        pltpu.make_async_copy(xt_ref, xs_hbm.at[pl.ds(0, tm)], sem).wait()


def _moe_dispatch(x1t, pos, padstart, n_used):
    T = x1t.shape[0]
    P = _moe_num_tiles(T) * MOE_TILE
    return pl.pallas_call(
        _moe_dispatch_kernel,
        grid_spec=pltpu.PrefetchScalarGridSpec(
            num_scalar_prefetch=3, grid=(T // MOE_TOK,),
            in_specs=[pl.BlockSpec((MOE_TOK, 8, LANES), lambda i, pos, pad, nused: (i, 0, 0))],
            out_specs=pl.BlockSpec(memory_space=pl.ANY),
            scratch_shapes=[pltpu.VMEM((MOE_TILE, 8, LANES), F32), pltpu.SemaphoreType.DMA,
                            pltpu.SemaphoreType.DMA]),
        out_shape=jax.ShapeDtypeStruct((P, 8, LANES), F32),
        compiler_params=_cparams(("arbitrary",)), name="moe_dispatch",
    )(pos, padstart, n_used, x1t)


def _moe_group_kernel(texp_ref, nused_ref, xs_ref, wg_ref, wu_ref, wd_ref, y_ref, xb_ref, acc_ref):
    i = pl.program_id(0)
    j = pl.program_id(1)

    @pl.when(i < nused_ref[0])
    def _():
        @pl.when(j == 0)
        def _():
            xb_ref[...] = _from_token_tiles(xs_ref[...]).astype(BF16)
            acc_ref[...] = jnp.zeros_like(acc_ref)

        xb = xb_ref[...]
        g = _dot(xb, wg_ref[...])
        u = _dot(xb, wu_ref[...])
        a = (g * jax.nn.sigmoid(g)) * u
        acc_ref[...] += _dot(a.astype(BF16), wd_ref[...])

        @pl.when(j == pl.num_programs(1) - 1)
        def _():
            y_ref[...] = _to_token_tiles(acc_ref[...])

    @pl.when((i >= nused_ref[0]) & (j == 0))
    def _():
        y_ref[...] = jnp.zeros_like(y_ref)


def _moe_group(xs, texp, n_used, wg, wu, wd):
    P = xs.shape[0]
    tm, th = MOE_TILE, MOE_TH
    nh = EXPERT_DIM // th

    def row_map(i, j, texp, nused):
        return (jnp.minimum(i, nused[0] - 1), 0, 0)

    def col(i, j, nused):
        return jnp.where(i < nused[0], j, nh - 1)

    return pl.pallas_call(
        _moe_group_kernel,
        grid_spec=pltpu.PrefetchScalarGridSpec(
            num_scalar_prefetch=2, grid=(P // tm, nh),
            in_specs=[pl.BlockSpec((tm, 8, LANES), row_map),
                      pl.BlockSpec((None, D_MODEL, th), lambda i, j, texp, nused: (texp[i], 0, col(i, j, nused))),
                      pl.BlockSpec((None, D_MODEL, th), lambda i, j, texp, nused: (texp[i], 0, col(i, j, nused))),
                      pl.BlockSpec((None, th, D_MODEL), lambda i, j, texp, nused: (texp[i], col(i, j, nused), 0))],
            out_specs=pl.BlockSpec((tm, 8, LANES), lambda i, j, texp, nused: (i, 0, 0)),
            scratch_shapes=[pltpu.VMEM((tm, D_MODEL), BF16), pltpu.VMEM((tm, D_MODEL), F32)]),
        out_shape=jax.ShapeDtypeStruct((P, 8, LANES), F32),
        compiler_params=_cparams(("arbitrary", "arbitrary")), name="moe_group",
    )(texp, n_used, xs, wg, wu, wd)


def _moe_combine_kernel(pos_ref, x_ref, route_ref, y_hbm, lg_ref, lb_ref, o_ref, buf_ref, sem):
    i = pl.program_id(0)
    tm = x_ref.shape[0]

    def issue(tile, slot):
        def body(kk, c):
            k0 = kk * MOE_DMA_UNROLL
            t0 = tile * tm + k0
            src = [pos_ref[2 * t0 + u] for u in range(2 * MOE_DMA_UNROLL)]
            for u in range(2 * MOE_DMA_UNROLL):
                pltpu.make_async_copy(y_hbm.at[src[u]], buf_ref.at[slot, (u % 2) * tm + k0 + u // 2],
                                      sem.at[slot]).start()
            return c
        lax.fori_loop(0, tm // MOE_DMA_UNROLL, body, 0)

    @pl.when(i == 0)
    def _():
        issue(0, 0)

    slot = i % 2
    pltpu.make_async_copy(y_hbm.at[pl.ds(0, 2 * tm)], buf_ref.at[slot], sem.at[slot]).wait()

    @pl.when(i + 1 < pl.num_programs(0))
    def _():
        issue(i + 1, 1 - slot)

    a = _from_token_tiles(buf_ref[slot, pl.ds(0, tm)])
    b = _from_token_tiles(buf_ref[slot, pl.ds(tm, tm)])
    f = route_ref[:, 2:3] * a + route_ref[:, 3:4] * b
    o_ref[...] = _layer_norm(DEEPNORM_ALPHA * x_ref[...] + f, lg_ref[...], lb_ref[...])


def _moe_combine(x1, route, y, pos, ln_g, ln_b):
    T = x1.shape[0]
    tm = MOE_TOK
    return pl.pallas_call(
        _moe_combine_kernel,
        grid_spec=pltpu.PrefetchScalarGridSpec(
            num_scalar_prefetch=1, grid=(T // tm,),
            in_specs=[pl.BlockSpec((tm, D_MODEL), lambda i, pos: (i, 0)),
                      pl.BlockSpec((tm, LANES), lambda i, pos: (i, 0)),
                      pl.BlockSpec(memory_space=pl.ANY),
                      pl.BlockSpec((1, D_MODEL), lambda i, pos: (0, 0)),
                      pl.BlockSpec((1, D_MODEL), lambda i, pos: (0, 0))],
            out_specs=pl.BlockSpec((tm, D_MODEL), lambda i, pos: (i, 0)),
            scratch_shapes=[pltpu.VMEM((2, 2 * tm, 8, LANES), F32), pltpu.SemaphoreType.DMA((2,))]),
        out_shape=jax.ShapeDtypeStruct((T, D_MODEL), F32),
        compiler_params=_cparams(("arbitrary",)), name="moe_combine",
    )(pos, x1, route, y, ln_g, ln_b)


def _moe(x1, x1t, route, counts, wg, wu, wd, ln_g, ln_b):
    pos, texp, n_used, padstart = _moe_tables(route, counts)
    xs = _moe_dispatch(x1t, pos, padstart, n_used)
    y = _moe_group(xs, texp, n_used, wg, wu, wd)
    return _moe_combine(x1, route, y, pos, ln_g, ln_b)


def _t5_bucket(rel):
    nb = REL_BUCKETS // 2
    ret = (rel > 0).astype(np.int32) * nb
    n = np.abs(rel)
    max_exact = nb // 2
    large = max_exact + (np.log(np.maximum(n, 1) / max_exact) / np.log(REL_MAX_DIST / max_exact)
                         * (nb - max_exact)).astype(np.int32)
    large = np.minimum(large, nb - 1)
    return ret + np.where(n < max_exact, n, large)


def _swa_head_order():
    return [h for c in range(4) for h in (c, 4 + c)]


def _prepare(w_in, w_out, ret_decay, swa_sink, rel_bias, hg_lb, hg_norm_w, moe_router, seq_lens):
    order = _swa_head_order()
    w_in_p = jnp.concatenate(
        [w_in[:, :, 1792:3840], w_in[:, :, 0:1024]]
        + [w_in[:, :, 1024 + 64 * h:1088 + 64 * h] for h in order]
        + [w_in[:, :, 1536:1792]], axis=2).astype(BF16)
    w_r = w_out[:, 0:256, :].astype(BF16)
    w_s = jnp.concatenate([w_out[:, 256 + 64 * h:320 + 64 * h, :] for h in order], axis=1).astype(BF16)
    w_g = w_out[:, 768:1024, :].astype(BF16)

    rd = ret_decay.astype(F32)
    rdp = jnp.repeat(rd.reshape(DEPTH, 4, 2), 64, axis=-1)
    rdp = jnp.concatenate([rdp, rdp], axis=1)
    rdh = jnp.broadcast_to(rd.reshape(DEPTH, 8, 1), (DEPTH, 8, LANES))

    W = SWA_BLOCK
    rel = np.arange(3 * W)[None, :] - W - np.arange(W)[:, None]
    onehot = jnp.asarray(_t5_bucket(rel)[..., None] == np.arange(REL_BUCKETS), F32)
    bias = jnp.einsum("qkb,bh->hqk", onehot, rel_bias.astype(F32), precision=lax.Precision.HIGHEST)
    band = jnp.asarray(np.abs(rel) <= WINDOW)
    bias_t = jnp.where(band[None], bias, MASK_VALUE).reshape(SWA_Q_HEADS * W, 3 * W)
    sink = swa_sink.astype(F32)
    sink_t = jnp.broadcast_to(sink[:, :, None, None], (DEPTH, SWA_Q_HEADS, W, LANES)).reshape(
        DEPTH, SWA_Q_HEADS * W, LANES)

    p = jax.nn.softmax(hg_lb.astype(F32), axis=0)
    lower = (jnp.cumsum(p, axis=0) - p[0]).reshape(DEPTH, 1, HG_HEADS * HG_DK)
    nw = hg_norm_w.astype(F32).reshape(DEPTH, 1, HG_HEADS * HG_DV)

    mf, mb, kf, kb, upper = _hg_constants(HG_BLOCK)
    L = HG_BLOCK
    mpre = np.concatenate([np.tril(np.ones((L, L), np.float32), -1), np.ones((8, L), np.float32)], axis=0)
    consts = (jnp.asarray(mf, BF16), jnp.asarray(mb, BF16), jnp.asarray(kf, F32), jnp.asarray(kb, F32),
              jnp.asarray(mpre, BF16), upper)

    half = RET_DK // 2
    inv = ROPE_BASE ** (-np.arange(half, dtype=np.float32) / half)
    rope = {}
    for S in set(seq_lens):
        ang = jnp.arange(S, dtype=F32)[:, None] * jnp.asarray(inv)[None, :]
        c32, s32 = jnp.cos(ang), jnp.sin(ang)
        rope[S] = (jnp.tile(c32, (1, 4)), jnp.tile(jnp.concatenate([-s32, s32], axis=1), (1, 2)))

    router = jnp.pad(moe_router.astype(F32), ((0, 0), (0, 0), (0, LANES - N_EXPERTS)))
    return dict(w_in=w_in_p, w_r=w_r, w_s=w_s, w_g=w_g, rdp=rdp, rdh=rdh, bias=bias_t, sink=sink_t,
                lower=lower, nw=nw, consts=consts, rope=rope, router=router)


def _trunk(x, prm, ln1_g, ln1_b, ln2_g, ln2_b, ffn_w, moe_w):
    B, S, _ = x.shape
    x2d = x.reshape(B * S, D_MODEL)
    cos_t, sin_t = prm["rope"][S]
    for l in range(DEPTH):
        h = _in_proj(x2d, prm["w_in"][l])
        ret = _retention(h, cos_t, sin_t, prm["rdp"][l], prm["rdh"][l], B, S)
        swa = _swa(h, prm["bias"], prm["sink"][l], B, S)
        hgo = _hgrn2(h, prm["lower"][l], prm["nw"][l], prm["consts"], B, S)
        g1, b1 = ln1_g[l].reshape(1, D_MODEL), ln1_b[l].reshape(1, D_MODEL)
        g2, b2 = ln2_g[l].reshape(1, D_MODEL), ln2_b[l].reshape(1, D_MODEL)
        i = l // 2
        if l % 2 == 0:
            x1 = _out_proj(x2d, ret, swa, hgo, prm["w_r"][l], prm["w_s"][l], prm["w_g"][l], g1, b1)
            wg, wu, wd = ffn_w
            x2d = _ffn(x1, wg[i], wu[i], wd[i], g2, b2)
        else:
            x1, x1t, route, counts = _out_proj(x2d, ret, swa, hgo, prm["w_r"][l], prm["w_s"][l], prm["w_g"][l],
                                               g1, b1, prm["router"][i])
            wg, wu, wd = moe_w
            x2d = _moe(x1, x1t, route, counts, wg[i], wu[i], wd[i], g2, b2)
    return x2d.reshape(B, S, D_MODEL)


def kernel(x_prompt, x_sample, w_in, w_out, ret_decay, swa_sink, rel_bias, hg_lb, hg_norm_w, ln1_g, ln1_b,
           ln2_g, ln2_b, ffn_w_gate, ffn_w_up, ffn_w_down, moe_router, moe_w_gate, moe_w_up, moe_w_down):
    prm = _prepare(w_in, w_out, ret_decay, swa_sink, rel_bias, hg_lb, hg_norm_w, moe_router,
                   (x_prompt.shape[1], x_sample.shape[1]))
    ffn_w = (ffn_w_gate.astype(BF16), ffn_w_up.astype(BF16), ffn_w_down.astype(BF16))
    moe_w = (moe_w_gate.astype(BF16), moe_w_up.astype(BF16), moe_w_down.astype(BF16))
    outs = []
    for x in (x_prompt, x_sample):
        outs.append(_trunk(x, prm, ln1_g, ln1_b, ln2_g, ln2_b, ffn_w, moe_w))
    return tuple(outs)
```

```python
import functools
import math

import numpy as np
import jax
import jax.numpy as jnp
from jax import lax
from jax.experimental import pallas as pl
from jax.experimental.pallas import tpu as pltpu

F32 = jnp.float32
BF16 = jnp.bfloat16

D_MODEL = 1024
DEPTH = 2
RET_HEADS = 4
RET_DK = 64
RET_CHUNK = 128
ROPE_BASE = 10000.0
SWA_Q_HEADS = 8
SWA_KV_HEADS = 2
SWA_DH = 64
WINDOW = 128
SWA_BLOCK = 128
REL_BUCKETS = 32
REL_MAX_DIST = 128
MASK_VALUE = -1e30
HG_HEADS = 4
HG_DK = 128
HG_DV = 64
FFN_DIM = 2816
N_EXPERTS = 8
EXPERT_DIM = 3584
LN_EPS = 1e-5
RMS_EPS = 1e-6
DEEPNORM_ALPHA = (2.0 * DEPTH) ** 0.25

LANES = 128
VMEM_LIMIT = 56 * 1024 * 1024

HG_W = 2048
RET_W = 1024
SWA_W = 768
IN_COLS = HG_W + RET_W + SWA_W

HG_BLOCK = 256
HG_BASE = 8
RET_TILE = 512
SWA_TILE = 512


def _cparams(sem):
    return pltpu.CompilerParams(dimension_semantics=sem, vmem_limit_bytes=VMEM_LIMIT)


def _dot(a, b):
    return jnp.dot(a, b, preferred_element_type=F32)


def _dot_nt(a, b):
    return lax.dot_general(a, b, (((1,), (1,)), ((), ())), preferred_element_type=F32)


def _dot_tn(a, b):
    return lax.dot_general(a, b, (((0,), (0,)), ((), ())), preferred_element_type=F32)


def _dot_split(a_bf16, x):
    hi = x.astype(BF16)
    lo = (x - hi.astype(F32)).astype(BF16)
    return _dot(a_bf16, hi) + _dot(a_bf16, lo)


def _dot_split_rhs_const(x, b_bf16):
    hi = x.astype(BF16)
    lo = (x - hi.astype(F32)).astype(BF16)
    return _dot(hi, b_bf16) + _dot(lo, b_bf16)


def _layer_norm(y, g, b):
    mu = jnp.mean(y, axis=-1, keepdims=True)
    yc = y - mu
    var = jnp.mean(yc * yc, axis=-1, keepdims=True)
    return yc * lax.rsqrt(var + LN_EPS) * g + b


def _lane_lo_mask(shape):
    return lax.broadcasted_iota(jnp.int32, shape, len(shape) - 1) < (LANES // 2)


def _head_block_ones():
    r = lax.broadcasted_iota(jnp.int32, (LANES, LANES), 0) // 64
    c = lax.broadcasted_iota(jnp.int32, (LANES, LANES), 1) // 64
    return r == c


def _inproj_kernel(x_ref, w_ref, o_ref):
    o_ref[...] = _dot(x_ref[...].astype(BF16), w_ref[...])


def _in_proj(x2d, w):
    T = x2d.shape[0]
    tm, tn = 2048, 1280
    return pl.pallas_call(
        _inproj_kernel,
        grid=(T // tm, IN_COLS // tn),
        in_specs=[pl.BlockSpec((tm, D_MODEL), lambda i, j: (i, 0)),
                  pl.BlockSpec((D_MODEL, tn), lambda i, j: (0, j))],
        out_specs=pl.BlockSpec((tm, tn), lambda i, j: (i, j)),
        out_shape=jax.ShapeDtypeStruct((T, IN_COLS), F32),
        compiler_params=_cparams(("parallel", "arbitrary")),
        name="in_proj",
    )(x2d, w)


def _rotary(x, cos, sin_signed, even):
    sw = jnp.where(even, pltpu.roll(x, 96, 1), pltpu.roll(x, 32, 1))
    return x * cos + sw * sin_signed


def _ret_tables(rdp_ref):
    lg = jnp.log1p(-jnp.exp(rdp_ref[...]))
    return lg


def _ret_bwd_state_kernel(h_ref, cos_ref, sin_ref, rdp_ref, sb_ref, st_ref):
    C = RET_CHUNK
    nc = h_ref.shape[0] // C

    @pl.when(pl.program_id(1) == 0)
    def _():
        st_ref[...] = jnp.zeros_like(st_ref)

    lg = _ret_tables(rdp_ref)
    rows = lax.broadcasted_iota(jnp.int32, (C, LANES), 0).astype(F32)
    even = (lax.broadcasted_iota(jnp.int32, (C, LANES), 1) // 32) % 2 == 0
    bm = _head_block_ones()
    for c in range(nc - 1, -1, -1):
        sl = pl.ds(c * C, C)
        cos = cos_ref[sl, :]
        sin = sin_ref[sl, :]
        for p in range(2):
            lgb = lg[2 + p:3 + p, :]
            kwb = jnp.exp(lgb * rows)
            decb = jnp.where(bm, jnp.exp(lgb * float(C)), 0.0)
            k = _rotary(h_ref[sl, 256 + 128 * p:384 + 128 * p], cos, sin, even) * (RET_DK ** -0.5)
            v = h_ref[sl, 512 + 128 * p:640 + 128 * p]
            sb_ref[0, c, p] = st_ref[p]
            kv = _dot_tn((k * kwb).astype(BF16), v.astype(BF16))
            st_ref[p] = decb * st_ref[p] + jnp.where(bm, kv, 0.0)


def _ret_main_kernel(h_ref, cos_ref, sin_ref, rdp_ref, rdh_ref, sb_ref, o_ref, st_ref):
    C = RET_CHUNK
    nc = h_ref.shape[0] // C

    @pl.when(pl.program_id(1) == 0)
    def _():
        st_ref[...] = jnp.zeros_like(st_ref)

    lg = _ret_tables(rdp_ref)
    lgh = jnp.log1p(-jnp.exp(rdh_ref[...]))
    rows = lax.broadcasted_iota(jnp.int32, (C, LANES), 0).astype(F32)
    cols = lax.broadcasted_iota(jnp.int32, (C, LANES), 1).astype(F32)
    rel = rows - cols
    even = (lax.broadcasted_iota(jnp.int32, (C, LANES), 1) // 32) % 2 == 0
    lo = _lane_lo_mask((C, LANES))
    bm = _head_block_ones()
    bones = jnp.where(bm, 1.0, 0.0).astype(BF16)

    dmat = []
    for hd in range(RET_HEADS):
        f = jnp.where(rel >= 0, jnp.exp(lgh[hd:hd + 1, :] * jnp.maximum(rel, 0.0)), 0.0)
        b = jnp.where(rel <= 0, jnp.exp(lgh[4 + hd:5 + hd, :] * jnp.maximum(-rel, 0.0)), 0.0)
        dmat.append(f + b)

    for c in range(nc):
        sl = pl.ds(c * C, C)
        cos = cos_ref[sl, :]
        sin = sin_ref[sl, :]
        for p in range(2):
            lgf = lg[p:p + 1, :]
            lgb = lg[2 + p:3 + p, :]
            q = _rotary(h_ref[sl, 128 * p:128 * p + 128], cos, sin, even)
            k = _rotary(h_ref[sl, 256 + 128 * p:384 + 128 * p], cos, sin, even) * (RET_DK ** -0.5)
            v = h_ref[sl, 512 + 128 * p:640 + 128 * p]
            g = h_ref[sl, 768 + 128 * p:896 + 128 * p]
            kb = k.astype(BF16)
            vb = v.astype(BF16)
            qs = jnp.concatenate([jnp.where(lo, q, 0.0), jnp.where(lo, 0.0, q)], axis=0).astype(BF16)
            s = _dot_nt(qs, kb)
            pm = jnp.concatenate([s[:C] * dmat[2 * p], s[C:] * dmat[2 * p + 1]], axis=1).astype(BF16)
            vs = jnp.concatenate([jnp.where(lo, v, 0.0), jnp.where(lo, 0.0, v)], axis=0).astype(BF16)
            intra = _dot(pm, vs)
            qwf = jnp.exp(lgf * (rows + 1.0))
            qwb = jnp.exp(lgb * (float(C) - rows))
            qi = jnp.concatenate([q * qwf, q * qwb], axis=1).astype(BF16)
            sst = jnp.concatenate([st_ref[p], sb_ref[0, c, p]], axis=0).astype(BF16)
            o = intra + _dot(qi, sst)
            kwf = jnp.exp(lgf * (float(C) - 1.0 - rows))
            decf = jnp.where(bm, jnp.exp(lgf * float(C)), 0.0)
            kv = _dot_tn((k * kwf).astype(BF16), vb)
            st_ref[p] = decf * st_ref[p] + jnp.where(bm, kv, 0.0)
            ms = _dot_split_rhs_const(o * o, bones) * (1.0 / RET_DK)
            y = o * lax.rsqrt(ms + RMS_EPS) * (g * jax.nn.sigmoid(g))
            o_ref[sl, 128 * p:128 * p + 128] = y.astype(o_ref.dtype)


def _retention(h, cos_t, sin_t, rdp, rdh, B, S):
    L = RET_TILE
    nb = S // L
    ncb = L // RET_CHUNK
    n_chunks = S // RET_CHUNK
    colblk = HG_W // RET_W
    sb = pl.pallas_call(
        _ret_bwd_state_kernel,
        grid=(B, nb),
        in_specs=[pl.BlockSpec((L, RET_W), lambda b, n: (b * nb + nb - 1 - n, colblk)),
                  pl.BlockSpec((L, LANES), lambda b, n: (nb - 1 - n, 0)),
                  pl.BlockSpec((L, LANES), lambda b, n: (nb - 1 - n, 0)),
                  pl.BlockSpec((8, LANES), lambda b, n: (0, 0))],
        out_specs=pl.BlockSpec((1, ncb, 2, LANES, LANES), lambda b, n: (b, nb - 1 - n, 0, 0, 0)),
        out_shape=jax.ShapeDtypeStruct((B, n_chunks, 2, LANES, LANES), F32),
        scratch_shapes=[pltpu.VMEM((2, LANES, LANES), F32)],
        compiler_params=_cparams(("parallel", "arbitrary")),
        name="ret_bwd_state",
    )(h, cos_t, sin_t, rdp)
    return pl.pallas_call(
        _ret_main_kernel,
        grid=(B, nb),
        in_specs=[pl.BlockSpec((L, RET_W), lambda b, n: (b * nb + n, colblk)),
                  pl.BlockSpec((L, LANES), lambda b, n: (n, 0)),
                  pl.BlockSpec((L, LANES), lambda b, n: (n, 0)),
                  pl.BlockSpec((8, LANES), lambda b, n: (0, 0)),
                  pl.BlockSpec((8, LANES), lambda b, n: (0, 0)),
                  pl.BlockSpec((1, ncb, 2, LANES, LANES), lambda b, n: (b, n, 0, 0, 0))],
        out_specs=pl.BlockSpec((L, 256), lambda b, n: (b * nb + n, 0)),
        out_shape=jax.ShapeDtypeStruct((B * S, 256), BF16),
        scratch_shapes=[pltpu.VMEM((2, LANES, LANES), F32)],
        compiler_params=_cparams(("parallel", "arbitrary")),
        name="ret_main",
    )(h, cos_t, sin_t, rdp, rdh, sb)


def _swa_kernel(q_ref, kp_ref, kc_ref, kn_ref, vp_ref, vc_ref, vn_ref, bias_ref, sink_ref, o_ref):
    W = SWA_BLOCK
    nq = q_ref.shape[0] // W
    n = pl.program_id(1)
    last = pl.num_programs(1) - 1
    kall = jnp.concatenate([kp_ref[...], kc_ref[...], kn_ref[...]], axis=0).astype(BF16)
    vt = jnp.concatenate([vp_ref[...], vc_ref[...], vn_ref[...]], axis=0).T.astype(BF16)
    lo = _lane_lo_mask((W, LANES))
    ki = lax.broadcasted_iota(jnp.int32, (3 * W, SWA_Q_HEADS * W), 0)
    edge_first = jnp.where(ki >= W, 0.0, jnp.where(n > 0, 0.0, MASK_VALUE))
    edge_last = jnp.where(ki < 2 * W, 0.0, jnp.where(n < last, 0.0, MASK_VALUE))
    sink = sink_ref[0:1, :]
    for r in range(nq):
        qs = []
        for c in range(4):
            q = q_ref[pl.ds(r * W, W), 128 * c:128 * c + 128] * (SWA_DH ** -0.5)
            qs += [jnp.where(lo, q, 0.0), jnp.where(lo, 0.0, q)]
        qs = jnp.concatenate(qs, axis=0).astype(BF16)
        s = _dot_nt(kall[r * W:(r + 3) * W], qs) + bias_ref[...]
        if r == 0:
            s = s + edge_first
        if r == nq - 1:
            s = s + edge_last
        m = jnp.maximum(jnp.max(s, axis=0, keepdims=True), sink)
        e = jnp.exp(s - m)
        den = jnp.sum(e, axis=0, keepdims=True) + jnp.exp(sink - m)
        ot = _dot(vt[:, r * W:(r + 3) * W], e.astype(BF16)) / den
        for c in range(4):
            sel = jnp.concatenate([ot[0:SWA_DH, 2 * c * W:(2 * c + 1) * W],
                                   ot[SWA_DH:2 * SWA_DH, (2 * c + 1) * W:(2 * c + 2) * W]], axis=0)
            o_ref[pl.ds(r * W, W), 128 * c:128 * c + 128] = sel.T.astype(o_ref.dtype)


def _swa(h, bias_t, sink_t, B, S):
    L = SWA_TILE
    W = SWA_BLOCK
    nb = S // L
    r = L // W
    nblk = S // W
    qcol = (HG_W + RET_W) // 512
    kcol = (HG_W + RET_W + 512) // LANES
    vcol = kcol + 1

    def prev_map(col):
        return lambda b, n: (b * nblk + jnp.maximum(n * r - 1, 0), col)

    def cur_map(col):
        return lambda b, n: (b * nb + n, col)

    def next_map(col):
        return lambda b, n: (b * nblk + jnp.minimum(n * r + r, nblk - 1), col)

    return pl.pallas_call(
        _swa_kernel,
        grid=(B, nb),
        in_specs=[pl.BlockSpec((L, 512), cur_map(qcol)),
                  pl.BlockSpec((W, LANES), prev_map(kcol)),
                  pl.BlockSpec((L, LANES), cur_map(kcol)),
                  pl.BlockSpec((W, LANES), next_map(kcol)),
                  pl.BlockSpec((W, LANES), prev_map(vcol)),
                  pl.BlockSpec((L, LANES), cur_map(vcol)),
                  pl.BlockSpec((W, LANES), next_map(vcol)),
                  pl.BlockSpec((3 * W, SWA_Q_HEADS * W), lambda b, n: (0, 0)),
                  pl.BlockSpec((8, SWA_Q_HEADS * W), lambda b, n: (0, 0))],
        out_specs=pl.BlockSpec((L, 512), lambda b, n: (b * nb + n, 0)),
        out_shape=jax.ShapeDtypeStruct((B * S, 512), BF16),
        compiler_params=_cparams(("parallel", "parallel")),
        name="swa",
    )(h, h, h, h, h, h, h, bias_t, sink_t)


def _hg_levels(L):
    g = L
    out = []
    while g >= 2 * HG_BASE:
        out.append(g)
        g //= 2
    return out


def _hg_constants(L):
    nb = L // HG_BASE
    idx = np.arange(L)
    t = idx[None, :]
    i = idx[:, None]
    r = HG_BASE * (i // HG_BASE) + HG_BASE // 2 - 1
    secs = [((t > r) & (t <= i)).astype(np.float32) - ((t > i) & (t <= r)).astype(np.float32)]
    u = np.arange(nb)[:, None]
    ru = HG_BASE * u + HG_BASE // 2 - 1
    secs.append((t <= ru).astype(np.float32))
    secs.append((t > ru).astype(np.float32))
    masks = [((idx[:, None] // HG_BASE) == (idx[None, :] // HG_BASE)) & (idx[:, None] >= idx[None, :])]
    upper = []
    for g in _hg_levels(L):
        gb = g // HG_BASE
        ref = HG_BASE * ((u // gb) * gb + gb // 2) - 1
        up = (u % gb) >= gb // 2
        secs.append(np.where(up, (t > ref) & (t <= ru), (t > ru) & (t <= ref)).astype(np.float32))
        upper.append([bool(v) for v in up[:, 0]])
        ii, jj = idx[:, None], idx[None, :]
        masks.append(((ii // g) == (jj // g)) & ((ii % g) >= g // 2) & ((jj % g) < g // 2))
    mf = np.concatenate(secs, axis=0)
    mb = np.concatenate([s[::-1, ::-1] for s in secs], axis=0)
    H = L // 2
    sub = [masks[0][:H, :H]] + [m[:H, :H] for m in masks[2:]]
    kf = np.stack(sub).astype(np.float32)
    kb = np.stack([m[::-1, ::-1] for m in sub]).astype(np.float32)
    upper_fb = (tuple(tuple(x) for x in upper), tuple(tuple(x[::-1]) for x in upper))
    return mf, mb, kf, kb, upper_fb


def _hg_gate(z, lb):
    key = (1.0 - lb) * jax.nn.sigmoid(-z)
    return key, jnp.log1p(-key)


def _hg_state_mask():
    r = lax.broadcasted_iota(jnp.int32, (LANES, 2 * HG_DK), 0) // HG_DV
    c = lax.broadcasted_iota(jnp.int32, (LANES, 2 * HG_DK), 1) // HG_DK
    return r == c


def _hg_bwd_state_kernel(z_ref, v_ref, lb_ref, m_ref, sb_ref, st_ref):
    L = z_ref.shape[0]

    @pl.when(pl.program_id(1) == 0)
    def _():
        st_ref[...] = jnp.zeros_like(st_ref)

    key, lf = _hg_gate(z_ref[...], lb_ref[...])
    e = _dot_split(m_ref[...], lf)
    ke = key * jnp.exp(e[:L])
    dec = jnp.exp(e[L:L + 1])
    bmt = _hg_state_mask()
    for p in range(2):
        sb_ref[0, 0, p] = st_ref[p]
        kvt = _dot_tn(v_ref[:, 128 * p:128 * p + 128].astype(BF16), ke[:, 256 * p:256 * p + 256].astype(BF16))
        st_ref[p] = dec[:, 256 * p:256 * p + 256] * st_ref[p] + jnp.where(bmt, kvt, 0.0)


def _hg_scale_blocks(dst_ref, up_ref, low_ref, fac_ref, row0, ls, upper):
    for u, is_up in enumerate(upper):
        rows = slice(HG_BASE * u, HG_BASE * u + HG_BASE)
        src = up_ref if is_up else low_ref
        dst_ref[rows, :] = src[rows, ls] * fac_ref[row0 + u:row0 + u + 1, ls]


def _hg_main_kernel(h_ref, lb_ref, nw_ref, mf_ref, mb_ref, kf_ref, kb_ref, sb_ref, o_ref,
                    st_ref, e_ref, qp_ref, kp_ref, fac_ref, s_ref, *, upper):
    L = h_ref.shape[0]
    nb = L // HG_BASE
    nlev = kf_ref.shape[0] + 1
    all_up = (True,) * nb

    @pl.when(pl.program_id(1) == 0)
    def _():
        st_ref[...] = jnp.zeros_like(st_ref)

    lo_v = _lane_lo_mask((L, LANES))
    bmt = _hg_state_mask()
    bones = jnp.where(_head_block_ones(), 1.0, 0.0).astype(BF16)
    osum = [jnp.zeros((L, LANES), F32), jnp.zeros((L, LANES), F32)]
    for d in range(2):
        m_ref = mf_ref if d == 0 else mb_ref
        k_ref = kf_ref if d == 0 else kb_ref
        key, lf = _hg_gate(h_ref[:, 512 + 512 * d:1024 + 512 * d], lb_ref[...])
        e_ref[...] = _dot_split(m_ref[...], lf)
        eb = e_ref[pl.ds(0, L), :]
        qp_ref[...] = h_ref[:, 0:512] * jnp.exp(eb)
        kp_ref[...] = key * jnp.exp(-eb)
        fac_ref[...] = jnp.exp(e_ref[pl.ds(L, (nlev + 1) * nb), :])
        H = L // 2
        halves = (slice(0, H), slice(H, L))
        qh, kh = (1, 0) if d == 0 else (0, 1)
        masks = [k_ref[lv] > 0.5 for lv in range(nlev - 1)]
        for p in range(2):
            v = h_ref[:, 1536 + 128 * p:1664 + 128 * p]
            vlo = jnp.where(lo_v, v, 0.0).astype(BF16)
            vhi = jnp.where(lo_v, 0.0, v).astype(BF16)
            diag = [[None, None], [None, None]]
            cross = [None, None]
            qe_pair = []
            ke_pair = []
            for hh in range(2):
                hd = 2 * p + hh
                ls = slice(128 * hd, 128 * hd + 128)
                for hb in range(2):
                    hs = halves[hb]
                    diag[hh][hb] = jnp.where(masks[0], _dot_nt(qp_ref[hs, ls].astype(BF16),
                                                               kp_ref[hs, ls].astype(BF16)), 0.0)
                for lv in range(1, nlev):
                    _hg_scale_blocks(s_ref, qp_ref, kp_ref, fac_ref, (1 + lv) * nb, ls, upper[d][lv - 1])
                    if lv == 1:
                        cross[hh] = _dot_nt(s_ref[halves[qh], :].astype(BF16),
                                            s_ref[halves[kh], :].astype(BF16)).astype(BF16)
                    else:
                        for hb in range(2):
                            sv = s_ref[halves[hb], :].astype(BF16)
                            diag[hh][hb] = jnp.where(masks[lv - 1], _dot_nt(sv, sv), diag[hh][hb])
                _hg_scale_blocks(s_ref, qp_ref, qp_ref, fac_ref, 0, ls, all_up)
                qe_pair.append(s_ref[...].astype(BF16))
                if d == 0:
                    _hg_scale_blocks(s_ref, kp_ref, kp_ref, fac_ref, nb, ls, all_up)
                    ke_pair.append(s_ref[...].astype(BF16))
            dg = [[diag[hh][hb].astype(BF16) for hb in range(2)] for hh in range(2)]
            vs = jnp.concatenate([vlo, vhi], axis=0)
            if d == 0:
                o_q = _dot(jnp.concatenate([cross[0], dg[0][1], cross[1], dg[1][1]], axis=1), vs)
                o_k = _dot(jnp.concatenate([dg[0][0], dg[1][0]], axis=1),
                           jnp.concatenate([vlo[halves[0]], vhi[halves[0]]], axis=0))
                intra = jnp.concatenate([o_k, o_q], axis=0)
            else:
                o_q = _dot(jnp.concatenate([dg[0][0], cross[0], dg[1][0], cross[1]], axis=1), vs)
                o_k = _dot(jnp.concatenate([dg[0][1], dg[1][1]], axis=1),
                           jnp.concatenate([vlo[halves[1]], vhi[halves[1]]], axis=0))
                intra = jnp.concatenate([o_q, o_k], axis=0)
            state = st_ref[p] if d == 0 else sb_ref[0, 0, p]
            inter = _dot_nt(jnp.concatenate(qe_pair, axis=1), state.astype(BF16))
            osum[p] = osum[p] + intra + inter
            if d == 0:
                ps = slice(256 * p, 256 * p + 256)
                dec = fac_ref[nb - 1:nb, ps] * fac_ref[2 * nb - 1:2 * nb, ps]
                kvt = _dot_tn(v.astype(BF16), jnp.concatenate(ke_pair, axis=1))
                st_ref[p] = dec * st_ref[p] + jnp.where(bmt, kvt, 0.0)
    for p in range(2):
        o = osum[p]
        ms = _dot_split_rhs_const(o * o, bones) * (1.0 / HG_DV)
        g = h_ref[:, 1792 + 128 * p:1920 + 128 * p]
        y = o * lax.rsqrt(ms + RMS_EPS) * nw_ref[:, 128 * p:128 * p + 128] * jax.nn.sigmoid(g)
        o_ref[:, 128 * p:128 * p + 128] = y.astype(o_ref.dtype)


def _hgrn2(h, lb, nw, consts, B, S):
    L = HG_BLOCK
    nb = S // L
    mf, mb, kf, kb, mpre, upper = consts
    nsec = mf.shape[0]
    nfac = nsec - L
    sb = pl.pallas_call(
        _hg_bwd_state_kernel,
        grid=(B, nb),
        in_specs=[pl.BlockSpec((L, 512), lambda b, n: (b * nb + nb - 1 - n, 2)),
                  pl.BlockSpec((L, 256), lambda b, n: (b * nb + nb - 1 - n, 6)),
                  pl.BlockSpec((1, 512), lambda b, n: (0, 0)),
                  pl.BlockSpec((L + 8, L), lambda b, n: (0, 0))],
        out_specs=pl.BlockSpec((1, 1, 2, LANES, 2 * HG_DK), lambda b, n: (b, nb - 1 - n, 0, 0, 0)),
        out_shape=jax.ShapeDtypeStruct((B, nb, 2, LANES, 2 * HG_DK), F32),
        scratch_shapes=[pltpu.VMEM((2, LANES, 2 * HG_DK), F32)],
        compiler_params=_cparams(("parallel", "arbitrary")),
        name="hg_bwd_state",
    )(h, h, lb, mpre)
    return pl.pallas_call(
        functools.partial(_hg_main_kernel, upper=upper),
        grid=(B, nb),
        in_specs=[pl.BlockSpec((L, HG_W), lambda b, n: (b * nb + n, 0)),
                  pl.BlockSpec((1, 512), lambda b, n: (0, 0)),
                  pl.BlockSpec((1, 256), lambda b, n: (0, 0)),
                  pl.BlockSpec((nsec, L), lambda b, n: (0, 0)),
                  pl.BlockSpec((nsec, L), lambda b, n: (0, 0)),
                  pl.BlockSpec(kf.shape, lambda b, n: (0, 0, 0)),
                  pl.BlockSpec(kb.shape, lambda b, n: (0, 0, 0)),
                  pl.BlockSpec((1, 1, 2, LANES, 2 * HG_DK), lambda b, n: (b, n, 0, 0, 0))],
        out_specs=pl.BlockSpec((L, 256), lambda b, n: (b * nb + n, 0)),
        out_shape=jax.ShapeDtypeStruct((B * S, 256), BF16),
        scratch_shapes=[pltpu.VMEM((2, LANES, 2 * HG_DK), F32),
                        pltpu.VMEM((nsec, 512), F32),
                        pltpu.VMEM((L, 512), F32), pltpu.VMEM((L, 512), F32),
                        pltpu.VMEM((nfac, 512), F32), pltpu.VMEM((L, LANES), F32)],
        compiler_params=_cparams(("parallel", "arbitrary")),
        name="hg_main",
    )(h, lb, nw, mf, mb, kf, kb, sb)


def _outproj_kernel(x_ref, r_ref, s_ref, g_ref, wr_ref, ws_ref, wg_ref, lg_ref, lb_ref, o_ref):
    mix = _dot(r_ref[...], wr_ref[...]) + _dot(s_ref[...], ws_ref[...]) + _dot(g_ref[...], wg_ref[...])
    o_ref[...] = _layer_norm(DEEPNORM_ALPHA * x_ref[...] + mix, lg_ref[...], lb_ref[...])


def _outproj_router_kernel(x_ref, r_ref, s_ref, g_ref, wr_ref, ws_ref, wg_ref, lg_ref, lb_ref, wrt_ref,
                           o_ref, ot_ref, route_ref, cnt_ref):
    @pl.when(pl.program_id(0) == 0)
    def _():
        cnt_ref[...] = jnp.zeros_like(cnt_ref)

    mix = _dot(r_ref[...], wr_ref[...]) + _dot(s_ref[...], ws_ref[...]) + _dot(g_ref[...], wg_ref[...])
    y = _layer_norm(DEEPNORM_ALPHA * x_ref[...] + mix, lg_ref[...], lb_ref[...])
    o_ref[...] = y
    ot_ref[...] = _to_token_tiles(y)
    w = wrt_ref[...]
    w_hi = w.astype(BF16)
    w_lo = (w - w_hi.astype(F32)).astype(BF16)
    y_hi = y.astype(BF16)
    y_lo = (y - y_hi.astype(F32)).astype(BF16)
    logits = _dot(y_hi, w_hi) + _dot(y_lo, w_hi) + _dot(y_hi, w_lo)
    tm = logits.shape[0]
    lane = lax.broadcasted_iota(jnp.int32, logits.shape, 1).astype(F32)
    neg = jnp.float32(-jnp.inf)
    l1 = jnp.where(lane < N_EXPERTS, logits, neg)
    m1 = jnp.max(l1, axis=-1, keepdims=True)
    i1 = jnp.min(jnp.where(l1 == m1, lane, float(LANES)), axis=-1, keepdims=True)
    l2 = jnp.where(lane == i1, neg, l1)
    m2 = jnp.max(l2, axis=-1, keepdims=True)
    i2 = jnp.min(jnp.where(l2 == m2, lane, float(LANES)), axis=-1, keepdims=True)
    e2 = jnp.exp(m2 - m1)
    den = 1.0 + e2
    picked = jnp.where((lane == i1) | (lane == i2), 1.0, 0.0)
    rr = lax.broadcasted_iota(jnp.int32, (tm, tm), 0)
    cc = lax.broadcasted_iota(jnp.int32, (tm, tm), 1)
    before = _dot(jnp.where(rr > cc, 1.0, 0.0).astype(BF16), picked.astype(BF16))
    rank = before + cnt_ref[0:1, :]
    r1 = jnp.sum(jnp.where(lane == i1, rank, 0.0), axis=-1, keepdims=True)
    r2 = jnp.sum(jnp.where(lane == i2, rank, 0.0), axis=-1, keepdims=True)
    cnt_ref[...] = cnt_ref[...] + jnp.sum(picked, axis=0, keepdims=True)
    route = jnp.where(lane == 0.0, i1, 0.0) + jnp.where(lane == 1.0, i2, 0.0)
    route = route + jnp.where(lane == 2.0, 1.0 / den, 0.0) + jnp.where(lane == 3.0, e2 / den, 0.0)
    route_ref[...] = route + jnp.where(lane == 4.0, r1, 0.0) + jnp.where(lane == 5.0, r2, 0.0)


def _out_proj(x2d, ret, swa, hgo, w_r, w_s, w_g, ln_g, ln_b, w_router=None):
    T = x2d.shape[0]
    tm = 512
    row = lambda i: (i, 0)
    full = lambda i: (0, 0)
    in_specs = [pl.BlockSpec((tm, D_MODEL), row),
                pl.BlockSpec((tm, 256), row), pl.BlockSpec((tm, 512), row), pl.BlockSpec((tm, 256), row),
                pl.BlockSpec((256, D_MODEL), full), pl.BlockSpec((512, D_MODEL), full),
                pl.BlockSpec((256, D_MODEL), full),
                pl.BlockSpec((1, D_MODEL), full), pl.BlockSpec((1, D_MODEL), full)]
    args = [x2d, ret, swa, hgo, w_r, w_s, w_g, ln_g, ln_b]
    if w_router is None:
        return pl.pallas_call(
            _outproj_kernel, grid=(T // tm,), in_specs=in_specs,
            out_specs=pl.BlockSpec((tm, D_MODEL), row),
            out_shape=jax.ShapeDtypeStruct((T, D_MODEL), F32),
            compiler_params=_cparams(("parallel",)), name="out_proj",
        )(*args)
    return pl.pallas_call(
        _outproj_router_kernel, grid=(T // tm,),
        in_specs=in_specs + [pl.BlockSpec((D_MODEL, LANES), full)],
        out_specs=[pl.BlockSpec((tm, D_MODEL), row), pl.BlockSpec((tm, 8, LANES), lambda i: (i, 0, 0)),
                   pl.BlockSpec((tm, LANES), row), pl.BlockSpec((8, LANES), full)],
        out_shape=[jax.ShapeDtypeStruct((T, D_MODEL), F32), jax.ShapeDtypeStruct((T, 8, LANES), F32),
                   jax.ShapeDtypeStruct((T, LANES), F32), jax.ShapeDtypeStruct((8, LANES), F32)],
        compiler_params=_cparams(("arbitrary",)), name="out_proj_router",
    )(*args, w_router)


FFN_CHUNKS = 2


def _ffn_kernel(x_ref, wg_ref, wu_ref, wd_ref, lg_ref, lb_ref, o_ref):
    x = x_ref[...]
    xb = x.astype(BF16)
    th = FFN_DIM // FFN_CHUNKS
    f = jnp.zeros(x.shape, F32)
    for c in range(FFN_CHUNKS):
        g = _dot(xb, wg_ref[:, c * th:(c + 1) * th])
        u = _dot(xb, wu_ref[:, c * th:(c + 1) * th])
        a = (g * jax.nn.sigmoid(g)) * u
        f = f + _dot(a.astype(BF16), wd_ref[c * th:(c + 1) * th, :])
    o_ref[...] = _layer_norm(DEEPNORM_ALPHA * x + f, lg_ref[...], lb_ref[...])


def _ffn(x2d, wg, wu, wd, ln_g, ln_b):
    T = x2d.shape[0]
    tm = 512
    once = pl.Buffered(1)
    return pl.pallas_call(
        _ffn_kernel, grid=(T // tm,),
        in_specs=[pl.BlockSpec((tm, D_MODEL), lambda i: (i, 0)),
                  pl.BlockSpec((D_MODEL, FFN_DIM), lambda i: (0, 0), pipeline_mode=once),
                  pl.BlockSpec((D_MODEL, FFN_DIM), lambda i: (0, 0), pipeline_mode=once),
                  pl.BlockSpec((FFN_DIM, D_MODEL), lambda i: (0, 0), pipeline_mode=once),
                  pl.BlockSpec((1, D_MODEL), lambda i: (0, 0)),
                  pl.BlockSpec((1, D_MODEL), lambda i: (0, 0))],
        out_specs=pl.BlockSpec((tm, D_MODEL), lambda i: (i, 0)),
        out_shape=jax.ShapeDtypeStruct((T, D_MODEL), F32),
        compiler_params=_cparams(("parallel",)), name="ffn",
    )(x2d, wg, wu, wd, ln_g, ln_b)


MOE_TILE = 512
MOE_TH = 1792
MOE_TOK = 512


def _moe_num_tiles(T):
    return (2 * T) // MOE_TILE + N_EXPERTS + 1


def _moe_tables(route, counts):
    T = route.shape[0]
    tm = MOE_TILE
    cnt = counts[0, :N_EXPERTS].astype(jnp.int32)
    pc = ((cnt + tm - 1) // tm) * tm
    pend = jnp.cumsum(pc)
    pstart = pend - pc
    e = route[:, 0:2].astype(jnp.int32)
    rank = route[:, 4:6].astype(jnp.int32)
    sel = e[:, :, None] == jnp.arange(N_EXPERTS, dtype=jnp.int32)[None, None, :]
    pos = jnp.sum(jnp.where(sel, pstart[None, None, :], 0), axis=-1) + rank
    n_tiles = _moe_num_tiles(T)
    tile_row = jnp.arange(n_tiles, dtype=jnp.int32) * tm
    texp = jnp.minimum(jnp.sum((tile_row[:, None] >= pend[None, :]).astype(jnp.int32), axis=1), N_EXPERTS - 1)
    n_used = pend[-1] // tm
    last = jnp.sum(jnp.where(jnp.arange(n_tiles) == n_used - 1, texp, 0))
    texp = jnp.where(jnp.arange(n_tiles) < n_used, texp, last)
    return pos.reshape(-1), texp, n_used.reshape(1), pstart + cnt


def _to_token_tiles(x):
    cols = jnp.stack([x[:, LANES * c:LANES * c + LANES] for c in range(D_MODEL // LANES)], axis=0)
    return pltpu.einshape("csl->scl", cols)


def _from_token_tiles(xt):
    y = pltpu.einshape("scl->csl", xt)
    return jnp.concatenate([y[c] for c in range(D_MODEL // LANES)], axis=1)


MOE_DMA_UNROLL = 8


def _moe_dispatch_kernel(pos_ref, pad_ref, nused_ref, xt_ref, xs_hbm, zero_ref, sem, zsem):
    i = pl.program_id(0)
    tm = xt_ref.shape[0]
    n_tiles = xs_hbm.shape[0] // MOE_TILE

    def zero_copy(row):
        return pltpu.make_async_copy(zero_ref, xs_hbm.at[pl.ds(row, MOE_TILE)], zsem)

    @pl.when(i == 0)
    def _():
        zero_ref[...] = jnp.zeros_like(zero_ref)
        for e in range(N_EXPERTS):
            zero_copy(pad_ref[e]).start()
        for e in range(N_EXPERTS):
            zero_copy(pad_ref[e]).wait()
        for t in range(n_tiles - N_EXPERTS - 1, n_tiles):
            @pl.when(t >= nused_ref[0])
            def _():
                zero_copy(t * MOE_TILE).start()
                zero_copy(t * MOE_TILE).wait()

    def body(kk, c):
        k0 = kk * MOE_DMA_UNROLL
        t0 = i * tm + k0
        dst = [pos_ref[2 * t0 + u] for u in range(2 * MOE_DMA_UNROLL)]
        for u in range(2 * MOE_DMA_UNROLL):
            pltpu.make_async_copy(xt_ref.at[k0 + u // 2], xs_hbm.at[dst[u]], sem).start()
        return c

    lax.fori_loop(0, tm // MOE_DMA_UNROLL, body, 0)
    for s in range(2):
        pltpu.make_async_copy(xt_ref, xs_hbm.at[pl.ds(0, tm)], sem).wait()


def _moe_dispatch(x1t, pos, padstart, n_used):
    T = x1t.shape[0]
    P = _moe_num_tiles(T) * MOE_TILE
    return pl.pallas_call(
        _moe_dispatch_kernel,
        grid_spec=pltpu.PrefetchScalarGridSpec(
            num_scalar_prefetch=3, grid=(T // MOE_TOK,),
            in_specs=[pl.BlockSpec((MOE_TOK, 8, LANES), lambda i, pos, pad, nused: (i, 0, 0))],
            out_specs=pl.BlockSpec(memory_space=pl.ANY),
            scratch_shapes=[pltpu.VMEM((MOE_TILE, 8, LANES), F32), pltpu.SemaphoreType.DMA,
                            pltpu.SemaphoreType.DMA]),
        out_shape=jax.ShapeDtypeStruct((P, 8, LANES), F32),
        compiler_params=_cparams(("arbitrary",)), name="moe_dispatch",
    )(pos, padstart, n_used, x1t)


def _moe_group_kernel(texp_ref, nused_ref, xs_ref, wg_ref, wu_ref, wd_ref, y_ref, xb_ref, acc_ref):
    i = pl.program_id(0)
    j = pl.program_id(1)

    @pl.when(i < nused_ref[0])
    def _():
        @pl.when(j == 0)
        def _():
            xb_ref[...] = _from_token_tiles(xs_ref[...]).astype(BF16)
            acc_ref[...] = jnp.zeros_like(acc_ref)

        xb = xb_ref[...]
        g = _dot(xb, wg_ref[...])
        u = _dot(xb, wu_ref[...])
        a = (g * jax.nn.sigmoid(g)) * u
        acc_ref[...] += _dot(a.astype(BF16), wd_ref[...])

        @pl.when(j == pl.num_programs(1) - 1)
        def _():
            y_ref[...] = _to_token_tiles(acc_ref[...])

    @pl.when((i >= nused_ref[0]) & (j == 0))
    def _():
        y_ref[...] = jnp.zeros_like(y_ref)


def _moe_group(xs, texp, n_used, wg, wu, wd):
    P = xs.shape[0]
    tm, th = MOE_TILE, MOE_TH
    nh = EXPERT_DIM // th

    def row_map(i, j, texp, nused):
        return (jnp.minimum(i, nused[0] - 1), 0, 0)

    def col(i, j, nused):
        return jnp.where(i < nused[0], j, nh - 1)

    return pl.pallas_call(
        _moe_group_kernel,
        grid_spec=pltpu.PrefetchScalarGridSpec(
            num_scalar_prefetch=2, grid=(P // tm, nh),
            in_specs=[pl.BlockSpec((tm, 8, LANES), row_map),
                      pl.BlockSpec((None, D_MODEL, th), lambda i, j, texp, nused: (texp[i], 0, col(i, j, nused))),
                      pl.BlockSpec((None, D_MODEL, th), lambda i, j, texp, nused: (texp[i], 0, col(i, j, nused))),
                      pl.BlockSpec((None, th, D_MODEL), lambda i, j, texp, nused: (texp[i], col(i, j, nused), 0))],
            out_specs=pl.BlockSpec((tm, 8, LANES), lambda i, j, texp, nused: (i, 0, 0)),
            scratch_shapes=[pltpu.VMEM((tm, D_MODEL), BF16), pltpu.VMEM((tm, D_MODEL), F32)]),
        out_shape=jax.ShapeDtypeStruct((P, 8, LANES), F32),
        compiler_params=_cparams(("arbitrary", "arbitrary")), name="moe_group",
    )(texp, n_used, xs, wg, wu, wd)


def _moe_combine_kernel(pos_ref, x_ref, route_ref, y_hbm, lg_ref, lb_ref, o_ref, buf_ref, sem):
    i = pl.program_id(0)
    tm = x_ref.shape[0]

    def issue(tile, slot):
        def body(kk, c):
            k0 = kk * MOE_DMA_UNROLL
            t0 = tile * tm + k0
            src = [pos_ref[2 * t0 + u] for u in range(2 * MOE_DMA_UNROLL)]
            for u in range(2 * MOE_DMA_UNROLL):
                pltpu.make_async_copy(y_hbm.at[src[u]], buf_ref.at[slot, (u % 2) * tm + k0 + u // 2],
                                      sem.at[slot]).start()
            return c
        lax.fori_loop(0, tm // MOE_DMA_UNROLL, body, 0)

    @pl.when(i == 0)
    def _():
        issue(0, 0)

    slot = i % 2
    pltpu.make_async_copy(y_hbm.at[pl.ds(0, 2 * tm)], buf_ref.at[slot], sem.at[slot]).wait()

    @pl.when(i + 1 < pl.num_programs(0))
    def _():
        issue(i + 1, 1 - slot)

    a = _from_token_tiles(buf_ref[slot, pl.ds(0, tm)])
    b = _from_token_tiles(buf_ref[slot, pl.ds(tm, tm)])
    f = route_ref[:, 2:3] * a + route_ref[:, 3:4] * b
    o_ref[...] = _layer_norm(DEEPNORM_ALPHA * x_ref[...] + f, lg_ref[...], lb_ref[...])


def _moe_combine(x1, route, y, pos, ln_g, ln_b):
    T = x1.shape[0]
    tm = MOE_TOK
    return pl.pallas_call(
        _moe_combine_kernel,
        grid_spec=pltpu.PrefetchScalarGridSpec(
            num_scalar_prefetch=1, grid=(T // tm,),
            in_specs=[pl.BlockSpec((tm, D_MODEL), lambda i, pos: (i, 0)),
                      pl.BlockSpec((tm, LANES), lambda i, pos: (i, 0)),
                      pl.BlockSpec(memory_space=pl.ANY),
                      pl.BlockSpec((1, D_MODEL), lambda i, pos: (0, 0)),
                      pl.BlockSpec((1, D_MODEL), lambda i, pos: (0, 0))],
            out_specs=pl.BlockSpec((tm, D_MODEL), lambda i, pos: (i, 0)),
            scratch_shapes=[pltpu.VMEM((2, 2 * tm, 8, LANES), F32), pltpu.SemaphoreType.DMA((2,))]),
        out_shape=jax.ShapeDtypeStruct((T, D_MODEL), F32),
        compiler_params=_cparams(("arbitrary",)), name="moe_combine",
    )(pos, x1, route, y, ln_g, ln_b)


def _moe(x1, x1t, route, counts, wg, wu, wd, ln_g, ln_b):
    pos, texp, n_used, padstart = _moe_tables(route, counts)
    xs = _moe_dispatch(x1t, pos, padstart, n_used)
    y = _moe_group(xs, texp, n_used, wg, wu, wd)
    return _moe_combine(x1, route, y, pos, ln_g, ln_b)


def _t5_bucket(rel):
    nb = REL_BUCKETS // 2
    ret = (rel > 0).astype(np.int32) * nb
    n = np.abs(rel)
    max_exact = nb // 2
    large = max_exact + (np.log(np.maximum(n, 1) / max_exact) / np.log(REL_MAX_DIST / max_exact)
                         * (nb - max_exact)).astype(np.int32)
    large = np.minimum(large, nb - 1)
    return ret + np.where(n < max_exact, n, large)


def _swa_head_order():
    return [h for c in range(4) for h in (c, 4 + c)]


def _prepare(w_in, w_out, ret_decay, swa_sink, rel_bias, hg_lb, hg_norm_w, moe_router, seq_lens):
    order = _swa_head_order()
    w_in_p = jnp.concatenate(
        [w_in[:, :, 1792:3840], w_in[:, :, 0:1024]]
        + [w_in[:, :, 1024 + 64 * h:1088 + 64 * h] for h in order]
        + [w_in[:, :, 1536:1792]], axis=2).astype(BF16)
    w_r = w_out[:, 0:256, :].astype(BF16)
    w_s = jnp.concatenate([w_out[:, 256 + 64 * h:320 + 64 * h, :] for h in order], axis=1).astype(BF16)
    w_g = w_out[:, 768:1024, :].astype(BF16)

    rd = ret_decay.astype(F32)
    rdp = jnp.repeat(rd.reshape(DEPTH, 4, 2), 64, axis=-1)
    rdp = jnp.concatenate([rdp, rdp], axis=1)
    rdh = jnp.broadcast_to(rd.reshape(DEPTH, 8, 1), (DEPTH, 8, LANES))

    W = SWA_BLOCK
    rel = np.arange(3 * W)[None, :] - W - np.arange(W)[:, None]
    onehot = jnp.asarray(_t5_bucket(rel)[..., None] == np.arange(REL_BUCKETS), F32)
    bias = jnp.einsum("qkb,bh->hqk", onehot, rel_bias.astype(F32), precision=lax.Precision.HIGHEST)
    band = jnp.asarray(np.abs(rel) <= WINDOW)
    bias_kq = jnp.transpose(jnp.where(band[None], bias, MASK_VALUE), (0, 2, 1))
    order = _swa_head_order()
    bias_t = jnp.concatenate([bias_kq[h] for h in order], axis=1)
    sink = swa_sink.astype(F32)
    sink_t = jnp.concatenate([jnp.broadcast_to(sink[:, h, None, None], (DEPTH, 8, W)) for h in order], axis=2)

    p = jax.nn.softmax(hg_lb.astype(F32), axis=0)
    lower = (jnp.cumsum(p, axis=0) - p[0]).reshape(DEPTH, 1, HG_HEADS * HG_DK)
    nw = hg_norm_w.astype(F32).reshape(DEPTH, 1, HG_HEADS * HG_DV)

    mf, mb, kf, kb, upper = _hg_constants(HG_BLOCK)
    L = HG_BLOCK
    mpre = np.concatenate([np.tril(np.ones((L, L), np.float32), -1), np.ones((8, L), np.float32)], axis=0)
    consts = (jnp.asarray(mf, BF16), jnp.asarray(mb, BF16), jnp.asarray(kf, F32), jnp.asarray(kb, F32),
              jnp.asarray(mpre, BF16), upper)

    half = RET_DK // 2
    inv = ROPE_BASE ** (-np.arange(half, dtype=np.float32) / half)
    rope = {}
    for S in set(seq_lens):
        ang = jnp.arange(S, dtype=F32)[:, None] * jnp.asarray(inv)[None, :]
        c32, s32 = jnp.cos(ang), jnp.sin(ang)
        rope[S] = (jnp.tile(c32, (1, 4)), jnp.tile(jnp.concatenate([-s32, s32], axis=1), (1, 2)))

    router = jnp.pad(moe_router.astype(F32), ((0, 0), (0, 0), (0, LANES - N_EXPERTS)))
    return dict(w_in=w_in_p, w_r=w_r, w_s=w_s, w_g=w_g, rdp=rdp, rdh=rdh, bias=bias_t, sink=sink_t,
                lower=lower, nw=nw, consts=consts, rope=rope, router=router)


def _trunk(x, prm, ln1_g, ln1_b, ln2_g, ln2_b, ffn_w, moe_w):
    B, S, _ = x.shape
    x2d = x.reshape(B * S, D_MODEL)
    cos_t, sin_t = prm["rope"][S]
    for l in range(DEPTH):
        h = _in_proj(x2d, prm["w_in"][l])
        ret = _retention(h, cos_t, sin_t, prm["rdp"][l], prm["rdh"][l], B, S)
        swa = _swa(h, prm["bias"], prm["sink"][l], B, S)
        hgo = _hgrn2(h, prm["lower"][l], prm["nw"][l], prm["consts"], B, S)
        g1, b1 = ln1_g[l].reshape(1, D_MODEL), ln1_b[l].reshape(1, D_MODEL)
        g2, b2 = ln2_g[l].reshape(1, D_MODEL), ln2_b[l].reshape(1, D_MODEL)
        i = l // 2
        if l % 2 == 0:
            x1 = _out_proj(x2d, ret, swa, hgo, prm["w_r"][l], prm["w_s"][l], prm["w_g"][l], g1, b1)
            wg, wu, wd = ffn_w
            x2d = _ffn(x1, wg[i], wu[i], wd[i], g2, b2)
        else:
            x1, x1t, route, counts = _out_proj(x2d, ret, swa, hgo, prm["w_r"][l], prm["w_s"][l], prm["w_g"][l],
                                               g1, b1, prm["router"][i])
            wg, wu, wd = moe_w
            x2d = _moe(x1, x1t, route, counts, wg[i], wu[i], wd[i], g2, b2)
    return x2d.reshape(B, S, D_MODEL)


def kernel(x_prompt, x_sample, w_in, w_out, ret_decay, swa_sink, rel_bias, hg_lb, hg_norm_w, ln1_g, ln1_b,
           ln2_g, ln2_b, ffn_w_gate, ffn_w_up, ffn_w_down, moe_router, moe_w_gate, moe_w_up, moe_w_down):
    prm = _prepare(w_in, w_out, ret_decay, swa_sink, rel_bias, hg_lb, hg_norm_w, moe_router,
                   (x_prompt.shape[1], x_sample.shape[1]))
    ffn_w = (ffn_w_gate.astype(BF16), ffn_w_up.astype(BF16), ffn_w_down.astype(BF16))
    moe_w = (moe_w_gate.astype(BF16), moe_w_up.astype(BF16), moe_w_down.astype(BF16))
    outs = []
    for x in (x_prompt, x_sample):
        outs.append(_trunk(x, prm, ln1_g, ln1_b, ln2_g, ln2_b, ffn_w, moe_w))
    return tuple(outs)
```

```python
import functools
import math

import numpy as np
import jax
import jax.numpy as jnp
from jax import lax
from jax.experimental import pallas as pl
from jax.experimental.pallas import tpu as pltpu

F32 = jnp.float32
BF16 = jnp.bfloat16

D_MODEL = 1024
DEPTH = 2
RET_HEADS = 4
RET_DK = 64
RET_CHUNK = 128
ROPE_BASE = 10000.0
SWA_Q_HEADS = 8
SWA_KV_HEADS = 2
SWA_DH = 64
WINDOW = 128
SWA_BLOCK = 128
REL_BUCKETS = 32
REL_MAX_DIST = 128
MASK_VALUE = -1e30
HG_HEADS = 4
HG_DK = 128
HG_DV = 64
FFN_DIM = 2816
N_EXPERTS = 8
EXPERT_DIM = 3584
LN_EPS = 1e-5
RMS_EPS = 1e-6
DEEPNORM_ALPHA = (2.0 * DEPTH) ** 0.25

LANES = 128
VMEM_LIMIT = 56 * 1024 * 1024

HG_W = 2048
RET_W = 1024
SWA_W = 768
IN_COLS = HG_W + RET_W + SWA_W

HG_BLOCK = 256
HG_BASE = 8
HG_PRE_BLOCKS = 4
RET_TILE = 512
SWA_TILE = 512


def _cparams(sem):
    return pltpu.CompilerParams(dimension_semantics=sem, vmem_limit_bytes=VMEM_LIMIT)


def _dot(a, b):
    return jnp.dot(a, b, preferred_element_type=F32)


def _dot_nt(a, b):
    return lax.dot_general(a, b, (((1,), (1,)), ((), ())), preferred_element_type=F32)


def _dot_tn(a, b):
    return lax.dot_general(a, b, (((0,), (0,)), ((), ())), preferred_element_type=F32)


def _dot_split(a_bf16, x):
    hi = x.astype(BF16)
    lo = (x - hi.astype(F32)).astype(BF16)
    return _dot(a_bf16, hi) + _dot(a_bf16, lo)


def _dot_split_rhs_const(x, b_bf16):
    hi = x.astype(BF16)
    lo = (x - hi.astype(F32)).astype(BF16)
    return _dot(hi, b_bf16) + _dot(lo, b_bf16)


def _layer_norm(y, g, b):
    mu = jnp.mean(y, axis=-1, keepdims=True)
    yc = y - mu
    var = jnp.mean(yc * yc, axis=-1, keepdims=True)
    return yc * lax.rsqrt(var + LN_EPS) * g + b


def _lane_lo_mask(shape):
    return lax.broadcasted_iota(jnp.int32, shape, len(shape) - 1) < (LANES // 2)


def _head_block_ones():
    r = lax.broadcasted_iota(jnp.int32, (LANES, LANES), 0) // 64
    c = lax.broadcasted_iota(jnp.int32, (LANES, LANES), 1) // 64
    return r == c


def _inproj_kernel(x_ref, w_ref, o_ref):
    o_ref[...] = _dot(x_ref[...].astype(BF16), w_ref[...])


def _in_proj(x2d, w):
    T = x2d.shape[0]
    tm, tn = 2048, 1280
    return pl.pallas_call(
        _inproj_kernel,
        grid=(T // tm, IN_COLS // tn),
        in_specs=[pl.BlockSpec((tm, D_MODEL), lambda i, j: (i, 0)),
                  pl.BlockSpec((D_MODEL, tn), lambda i, j: (0, j))],
        out_specs=pl.BlockSpec((tm, tn), lambda i, j: (i, j)),
        out_shape=jax.ShapeDtypeStruct((T, IN_COLS), F32),
        compiler_params=_cparams(("parallel", "arbitrary")),
        name="in_proj",
    )(x2d, w)


def _rotary(x, cos, sin_signed, even):
    sw = jnp.where(even, pltpu.roll(x, 96, 1), pltpu.roll(x, 32, 1))
    return x * cos + sw * sin_signed


def _ret_tables(rdp_ref):
    lg = jnp.log1p(-jnp.exp(rdp_ref[...]))
    return lg


def _ret_bwd_state_kernel(h_ref, cos_ref, sin_ref, rdp_ref, sb_ref, st_ref):
    C = RET_CHUNK
    nc = h_ref.shape[0] // C

    @pl.when(pl.program_id(1) == 0)
    def _():
        st_ref[...] = jnp.zeros_like(st_ref)

    lg = _ret_tables(rdp_ref)
    rows = lax.broadcasted_iota(jnp.int32, (C, LANES), 0).astype(F32)
    even = (lax.broadcasted_iota(jnp.int32, (C, LANES), 1) // 32) % 2 == 0
    bm = _head_block_ones()
    for c in range(nc - 1, -1, -1):
        sl = pl.ds(c * C, C)
        cos = cos_ref[sl, :]
        sin = sin_ref[sl, :]
        for p in range(2):
            lgb = lg[2 + p:3 + p, :]
            kwb = jnp.exp(lgb * rows)
            decb = jnp.where(bm, jnp.exp(lgb * float(C)), 0.0)
            k = _rotary(h_ref[sl, 256 + 128 * p:384 + 128 * p], cos, sin, even) * (RET_DK ** -0.5)
            v = h_ref[sl, 512 + 128 * p:640 + 128 * p]
            sb_ref[0, c, p] = st_ref[p]
            kv = _dot_tn((k * kwb).astype(BF16), v.astype(BF16))
            st_ref[p] = decb * st_ref[p] + jnp.where(bm, kv, 0.0)


def _ret_main_kernel(h_ref, cos_ref, sin_ref, rdp_ref, rdh_ref, sb_ref, o_ref, st_ref):
    C = RET_CHUNK
    nc = h_ref.shape[0] // C

    @pl.when(pl.program_id(1) == 0)
    def _():
        st_ref[...] = jnp.zeros_like(st_ref)

    lg = _ret_tables(rdp_ref)
    lgh = jnp.log1p(-jnp.exp(rdh_ref[...]))
    rows = lax.broadcasted_iota(jnp.int32, (C, LANES), 0).astype(F32)
    cols = lax.broadcasted_iota(jnp.int32, (C, LANES), 1).astype(F32)
    rel = rows - cols
    even = (lax.broadcasted_iota(jnp.int32, (C, LANES), 1) // 32) % 2 == 0
    lo = _lane_lo_mask((C, LANES))
    bm = _head_block_ones()
    bones = jnp.where(bm, 1.0, 0.0).astype(BF16)

    dmat = []
    for hd in range(RET_HEADS):
        f = jnp.where(rel >= 0, jnp.exp(lgh[hd:hd + 1, :] * jnp.maximum(rel, 0.0)), 0.0)
        b = jnp.where(rel <= 0, jnp.exp(lgh[4 + hd:5 + hd, :] * jnp.maximum(-rel, 0.0)), 0.0)
        dmat.append(f + b)

    for c in range(nc):
        sl = pl.ds(c * C, C)
        cos = cos_ref[sl, :]
        sin = sin_ref[sl, :]
        for p in range(2):
            lgf = lg[p:p + 1, :]
            lgb = lg[2 + p:3 + p, :]
            q = _rotary(h_ref[sl, 128 * p:128 * p + 128], cos, sin, even)
            k = _rotary(h_ref[sl, 256 + 128 * p:384 + 128 * p], cos, sin, even) * (RET_DK ** -0.5)
            v = h_ref[sl, 512 + 128 * p:640 + 128 * p]
            g = h_ref[sl, 768 + 128 * p:896 + 128 * p]
            kb = k.astype(BF16)
            vb = v.astype(BF16)
            qs = jnp.concatenate([jnp.where(lo, q, 0.0), jnp.where(lo, 0.0, q)], axis=0).astype(BF16)
            s = _dot_nt(qs, kb)
            pm = jnp.concatenate([s[:C] * dmat[2 * p], s[C:] * dmat[2 * p + 1]], axis=1).astype(BF16)
            vs = jnp.concatenate([jnp.where(lo, v, 0.0), jnp.where(lo, 0.0, v)], axis=0).astype(BF16)
            intra = _dot(pm, vs)
            qwf = jnp.exp(lgf * (rows + 1.0))
            qwb = jnp.exp(lgb * (float(C) - rows))
            qi = jnp.concatenate([q * qwf, q * qwb], axis=1).astype(BF16)
            sst = jnp.concatenate([st_ref[p], sb_ref[0, c, p]], axis=0).astype(BF16)
            o = intra + _dot(qi, sst)
            kwf = jnp.exp(lgf * (float(C) - 1.0 - rows))
            decf = jnp.where(bm, jnp.exp(lgf * float(C)), 0.0)
            kv = _dot_tn((k * kwf).astype(BF16), vb)
            st_ref[p] = decf * st_ref[p] + jnp.where(bm, kv, 0.0)
            ms = _dot_split_rhs_const(o * o, bones) * (1.0 / RET_DK)
            y = o * lax.rsqrt(ms + RMS_EPS) * (g * jax.nn.sigmoid(g))
            o_ref[sl, 128 * p:128 * p + 128] = y.astype(o_ref.dtype)


def _retention(h, cos_t, sin_t, rdp, rdh, B, S):
    L = RET_TILE
    nb = S // L
    ncb = L // RET_CHUNK
    n_chunks = S // RET_CHUNK
    colblk = HG_W // RET_W
    sb = pl.pallas_call(
        _ret_bwd_state_kernel,
        grid=(B, nb),
        in_specs=[pl.BlockSpec((L, RET_W), lambda b, n: (b * nb + nb - 1 - n, colblk)),
                  pl.BlockSpec((L, LANES), lambda b, n: (nb - 1 - n, 0)),
                  pl.BlockSpec((L, LANES), lambda b, n: (nb - 1 - n, 0)),
                  pl.BlockSpec((8, LANES), lambda b, n: (0, 0))],
        out_specs=pl.BlockSpec((1, ncb, 2, LANES, LANES), lambda b, n: (b, nb - 1 - n, 0, 0, 0)),
        out_shape=jax.ShapeDtypeStruct((B, n_chunks, 2, LANES, LANES), F32),
        scratch_shapes=[pltpu.VMEM((2, LANES, LANES), F32)],
        compiler_params=_cparams(("parallel", "arbitrary")),
        name="ret_bwd_state",
    )(h, cos_t, sin_t, rdp)
    return pl.pallas_call(
        _ret_main_kernel,
        grid=(B, nb),
        in_specs=[pl.BlockSpec((L, RET_W), lambda b, n: (b * nb + n, colblk)),
                  pl.BlockSpec((L, LANES), lambda b, n: (n, 0)),
                  pl.BlockSpec((L, LANES), lambda b, n: (n, 0)),
                  pl.BlockSpec((8, LANES), lambda b, n: (0, 0)),
                  pl.BlockSpec((8, LANES), lambda b, n: (0, 0)),
                  pl.BlockSpec((1, ncb, 2, LANES, LANES), lambda b, n: (b, n, 0, 0, 0))],
        out_specs=pl.BlockSpec((L, 256), lambda b, n: (b * nb + n, 0)),
        out_shape=jax.ShapeDtypeStruct((B * S, 256), BF16),
        scratch_shapes=[pltpu.VMEM((2, LANES, LANES), F32)],
        compiler_params=_cparams(("parallel", "arbitrary")),
        name="ret_main",
    )(h, cos_t, sin_t, rdp, rdh, sb)


def _swa_kernel(q_ref, kp_ref, kc_ref, kn_ref, vp_ref, vc_ref, vn_ref, bias_ref, sink_ref, o_ref):
    W = SWA_BLOCK
    nq = q_ref.shape[0] // W
    n = pl.program_id(1)
    last = pl.num_programs(1) - 1
    kall = jnp.concatenate([kp_ref[...], kc_ref[...], kn_ref[...]], axis=0).astype(BF16)
    vt = jnp.concatenate([vp_ref[...], vc_ref[...], vn_ref[...]], axis=0).T.astype(BF16)
    lo = _lane_lo_mask((W, LANES))
    ki = lax.broadcasted_iota(jnp.int32, (3 * W, SWA_Q_HEADS * W), 0)
    edge_first = jnp.where(ki >= W, 0.0, jnp.where(n > 0, 0.0, MASK_VALUE))
    edge_last = jnp.where(ki < 2 * W, 0.0, jnp.where(n < last, 0.0, MASK_VALUE))
    sink = sink_ref[0:1, :]
    for r in range(nq):
        qs = []
        for c in range(4):
            q = q_ref[pl.ds(r * W, W), 128 * c:128 * c + 128] * (SWA_DH ** -0.5)
            qs += [jnp.where(lo, q, 0.0), jnp.where(lo, 0.0, q)]
        qs = jnp.concatenate(qs, axis=0).astype(BF16)
        s = _dot_nt(kall[r * W:(r + 3) * W], qs) + bias_ref[...]
        if r == 0:
            s = s + edge_first
        if r == nq - 1:
            s = s + edge_last
        m = jnp.maximum(jnp.max(s, axis=0, keepdims=True), sink)
        e = jnp.exp(s - m)
        den = jnp.sum(e, axis=0, keepdims=True) + jnp.exp(sink - m)
        ot = _dot(vt[:, r * W:(r + 3) * W], e.astype(BF16)) / den
        for c in range(4):
            sel = jnp.concatenate([ot[0:SWA_DH, 2 * c * W:(2 * c + 1) * W],
                                   ot[SWA_DH:2 * SWA_DH, (2 * c + 1) * W:(2 * c + 2) * W]], axis=0)
            o_ref[pl.ds(r * W, W), 128 * c:128 * c + 128] = sel.T.astype(o_ref.dtype)


def _swa(h, bias_t, sink_t, B, S):
    L = SWA_TILE
    W = SWA_BLOCK
    nb = S // L
    r = L // W
    nblk = S // W
    qcol = (HG_W + RET_W) // 512
    kcol = (HG_W + RET_W + 512) // LANES
    vcol = kcol + 1

    def prev_map(col):
        return lambda b, n: (b * nblk + jnp.maximum(n * r - 1, 0), col)

    def cur_map(col):
        return lambda b, n: (b * nb + n, col)

    def next_map(col):
        return lambda b, n: (b * nblk + jnp.minimum(n * r + r, nblk - 1), col)

    return pl.pallas_call(
        _swa_kernel,
        grid=(B, nb),
        in_specs=[pl.BlockSpec((L, 512), cur_map(qcol)),
                  pl.BlockSpec((W, LANES), prev_map(kcol)),
                  pl.BlockSpec((L, LANES), cur_map(kcol)),
                  pl.BlockSpec((W, LANES), next_map(kcol)),
                  pl.BlockSpec((W, LANES), prev_map(vcol)),
                  pl.BlockSpec((L, LANES), cur_map(vcol)),
                  pl.BlockSpec((W, LANES), next_map(vcol)),
                  pl.BlockSpec((3 * W, SWA_Q_HEADS * W), lambda b, n: (0, 0)),
                  pl.BlockSpec((8, SWA_Q_HEADS * W), lambda b, n: (0, 0))],
        out_specs=pl.BlockSpec((L, 512), lambda b, n: (b * nb + n, 0)),
        out_shape=jax.ShapeDtypeStruct((B * S, 512), BF16),
        compiler_params=_cparams(("parallel", "parallel")),
        name="swa",
    )(h, h, h, h, h, h, h, bias_t, sink_t)


def _hg_levels(L):
    g = L
    out = []
    while g >= 2 * HG_BASE:
        out.append(g)
        g //= 2
    return out


def _hg_constants(L):
    nb = L // HG_BASE
    idx = np.arange(L)
    t = idx[None, :]
    i = idx[:, None]
    r = HG_BASE * (i // HG_BASE) + HG_BASE // 2 - 1
    secs = [((t > r) & (t <= i)).astype(np.float32) - ((t > i) & (t <= r)).astype(np.float32)]
    u = np.arange(nb)[:, None]
    ru = HG_BASE * u + HG_BASE // 2 - 1
    secs.append((t <= ru).astype(np.float32))
    secs.append((t > ru).astype(np.float32))
    masks = [((idx[:, None] // HG_BASE) == (idx[None, :] // HG_BASE)) & (idx[:, None] >= idx[None, :])]
    upper = []
    for g in _hg_levels(L):
        gb = g // HG_BASE
        ref = HG_BASE * ((u // gb) * gb + gb // 2) - 1
        up = (u % gb) >= gb // 2
        secs.append(np.where(up, (t > ref) & (t <= ru), (t > ru) & (t <= ref)).astype(np.float32))
        upper.append([bool(v) for v in up[:, 0]])
        ii, jj = idx[:, None], idx[None, :]
        masks.append(((ii // g) == (jj // g)) & ((ii % g) >= g // 2) & ((jj % g) < g // 2))
    mf = np.concatenate(secs, axis=0)
    mb = np.concatenate([s[::-1, ::-1] for s in secs], axis=0)
    H = L // 2
    sub = [masks[0][:H, :H]] + [m[:H, :H] for m in masks[2:]]
    kf = np.stack(sub).astype(np.float32)
    kb = np.stack([m[::-1, ::-1] for m in sub]).astype(np.float32)
    upper_fb = (tuple(tuple(x) for x in upper), tuple(tuple(x[::-1]) for x in upper))
    return mf, mb, kf, kb, upper_fb


def _hg_gate(z, lb):
    key = (1.0 - lb) * jax.nn.sigmoid(-z)
    return key, jnp.log1p(-key)


def _hg_state_mask():
    r = lax.broadcasted_iota(jnp.int32, (LANES, 2 * HG_DK), 0) // HG_DV
    c = lax.broadcasted_iota(jnp.int32, (LANES, 2 * HG_DK), 1) // HG_DK
    return r == c


def _hg_bwd_state_kernel(z_ref, v_ref, lb_ref, m_ref, sb_ref, st_ref):
    L = HG_BLOCK
    nblk = z_ref.shape[0] // L

    @pl.when(pl.program_id(1) == 0)
    def _():
        st_ref[...] = jnp.zeros_like(st_ref)

    bmt = _hg_state_mask()
    for blk in range(nblk - 1, -1, -1):
        rows = slice(blk * L, (blk + 1) * L)
        key, lf = _hg_gate(z_ref[rows, :], lb_ref[...])
        e = _dot_split(m_ref[...], lf)
        ke = key * jnp.exp(e[:L])
        dec = jnp.exp(e[L:L + 1])
        for p in range(2):
            sb_ref[0, blk, p] = st_ref[p]
            kvt = _dot_tn(v_ref[rows, 128 * p:128 * p + 128].astype(BF16),
                          ke[:, 256 * p:256 * p + 256].astype(BF16))
            st_ref[p] = dec[:, 256 * p:256 * p + 256] * st_ref[p] + jnp.where(bmt, kvt, 0.0)


def _hg_scale_blocks(dst_ref, up_ref, low_ref, fac_ref, row0, upper):
    for u, is_up in enumerate(upper):
        rows = slice(HG_BASE * u, HG_BASE * u + HG_BASE)
        src = up_ref if is_up else low_ref
        for hd in range(HG_HEADS):
            ls = slice(HG_DK * hd, HG_DK * hd + HG_DK)
            dst_ref[rows, ls] = src[rows, ls] * fac_ref[row0 + u:row0 + u + 1, ls]


def _hg_main_kernel(h_ref, lb_ref, nw_ref, mf_ref, mb_ref, kf_ref, kb_ref, sb_ref, o_ref,
                    st_ref, e2_ref, qp2_ref, kp2_ref, fac2_ref, s_ref, *, upper):
    L = h_ref.shape[0]
    nb = L // HG_BASE
    nlev = kf_ref.shape[0] + 1
    all_up = (True,) * nb

    @pl.when(pl.program_id(1) == 0)
    def _():
        st_ref[...] = jnp.zeros_like(st_ref)

    lo_v = _lane_lo_mask((L, LANES))
    bmt = _hg_state_mask()
    bones = jnp.where(_head_block_ones(), 1.0, 0.0).astype(BF16)
    osum = [jnp.zeros((L, LANES), F32), jnp.zeros((L, LANES), F32)]
    nbuf = nlev + 1
    for d in range(2):
        m_ref = mf_ref if d == 0 else mb_ref
        k_ref = kf_ref if d == 0 else kb_ref
        e_ref, qp_ref, kp_ref, fac_ref = e2_ref.at[d], qp2_ref.at[d], kp2_ref.at[d], fac2_ref.at[d]
        sbuf = [s_ref.at[d * nbuf + k] for k in range(nbuf)]
        key, lf = _hg_gate(h_ref[:, 512 + 512 * d:1024 + 512 * d], lb_ref[...])
        e_ref[...] = _dot_split(m_ref[...], lf)
        eb = e_ref[pl.ds(0, L), :]
        qp_ref[...] = h_ref[:, 0:512] * jnp.exp(eb)
        kp_ref[...] = key * jnp.exp(-eb)
        fac_ref[...] = jnp.exp(e_ref[pl.ds(L, (nlev + 1) * nb), :])
        for lv in range(1, nlev):
            _hg_scale_blocks(sbuf[lv - 1], qp_ref, kp_ref, fac_ref, (1 + lv) * nb, upper[d][lv - 1])
        _hg_scale_blocks(sbuf[nlev - 1], qp_ref, qp_ref, fac_ref, 0, all_up)
        if d == 0:
            _hg_scale_blocks(sbuf[nlev], kp_ref, kp_ref, fac_ref, nb, all_up)
        H = L // 2
        halves = (slice(0, H), slice(H, L))
        qh, kh = (1, 0) if d == 0 else (0, 1)
        masks = [k_ref[lv] > 0.5 for lv in range(nlev - 1)]
        for p in range(2):
            v = h_ref[:, 1536 + 128 * p:1664 + 128 * p]
            vlo = jnp.where(lo_v, v, 0.0).astype(BF16)
            vhi = jnp.where(lo_v, 0.0, v).astype(BF16)
            diag = [[None, None], [None, None]]
            cross = [None, None]
            qe_pair = []
            ke_pair = []
            for hh in range(2):
                hd = 2 * p + hh
                ls = slice(128 * hd, 128 * hd + 128)
                for hb in range(2):
                    hs = halves[hb]
                    diag[hh][hb] = jnp.where(masks[0], _dot_nt(qp_ref[hs, ls].astype(BF16),
                                                               kp_ref[hs, ls].astype(BF16)), 0.0)
                for lv in range(1, nlev):
                    sl = sbuf[lv - 1]
                    if lv == 1:
                        cross[hh] = _dot_nt(sl[halves[qh], ls].astype(BF16),
                                            sl[halves[kh], ls].astype(BF16)).astype(BF16)
                    else:
                        for hb in range(2):
                            sv = sl[halves[hb], ls].astype(BF16)
                            diag[hh][hb] = jnp.where(masks[lv - 1], _dot_nt(sv, sv), diag[hh][hb])
                qe_pair.append(sbuf[nlev - 1][:, ls].astype(BF16))
                if d == 0:
                    ke_pair.append(sbuf[nlev][:, ls].astype(BF16))
            dg = [[diag[hh][hb].astype(BF16) for hb in range(2)] for hh in range(2)]
            vs = jnp.concatenate([vlo, vhi], axis=0)
            if d == 0:
                o_q = _dot(jnp.concatenate([cross[0], dg[0][1], cross[1], dg[1][1]], axis=1), vs)
                o_k = _dot(jnp.concatenate([dg[0][0], dg[1][0]], axis=1),
                           jnp.concatenate([vlo[halves[0]], vhi[halves[0]]], axis=0))
                intra = jnp.concatenate([o_k, o_q], axis=0)
            else:
                o_q = _dot(jnp.concatenate([dg[0][0], cross[0], dg[1][0], cross[1]], axis=1), vs)
                o_k = _dot(jnp.concatenate([dg[0][1], dg[1][1]], axis=1),
                           jnp.concatenate([vlo[halves[1]], vhi[halves[1]]], axis=0))
                intra = jnp.concatenate([o_q, o_k], axis=0)
            state = st_ref[p] if d == 0 else sb_ref[0, 0, p]
            inter = _dot_nt(jnp.concatenate(qe_pair, axis=1), state.astype(BF16))
            osum[p] = osum[p] + intra + inter
            if d == 0:
                ps = slice(256 * p, 256 * p + 256)
                dec = fac_ref[nb - 1:nb, ps] * fac_ref[2 * nb - 1:2 * nb, ps]
                kvt = _dot_tn(v.astype(BF16), jnp.concatenate(ke_pair, axis=1))
                st_ref[p] = dec * st_ref[p] + jnp.where(bmt, kvt, 0.0)
    for p in range(2):
        o = osum[p]
        ms = _dot_split_rhs_const(o * o, bones) * (1.0 / HG_DV)
        g = h_ref[:, 1792 + 128 * p:1920 + 128 * p]
        y = o * lax.rsqrt(ms + RMS_EPS) * nw_ref[:, 128 * p:128 * p + 128] * jax.nn.sigmoid(g)
        o_ref[:, 128 * p:128 * p + 128] = y.astype(o_ref.dtype)


def _hgrn2(h, lb, nw, consts, B, S):
    L = HG_BLOCK
    nb = S // L
    mf, mb, kf, kb, mpre, upper = consts
    nsec = mf.shape[0]
    nfac = nsec - L
    sb = pl.pallas_call(
        _hg_bwd_state_kernel,
        grid=(B, nb // HG_PRE_BLOCKS),
        in_specs=[pl.BlockSpec((HG_PRE_BLOCKS * L, 512), lambda b, n: ((b * nb + nb) // HG_PRE_BLOCKS - 1 - n, 2)),
                  pl.BlockSpec((HG_PRE_BLOCKS * L, 256), lambda b, n: ((b * nb + nb) // HG_PRE_BLOCKS - 1 - n, 6)),
                  pl.BlockSpec((1, 512), lambda b, n: (0, 0)),
                  pl.BlockSpec((L + 8, L), lambda b, n: (0, 0))],
        out_specs=pl.BlockSpec((1, HG_PRE_BLOCKS, 2, LANES, 2 * HG_DK),
                               lambda b, n: (b, nb // HG_PRE_BLOCKS - 1 - n, 0, 0, 0)),
        out_shape=jax.ShapeDtypeStruct((B, nb, 2, LANES, 2 * HG_DK), F32),
        scratch_shapes=[pltpu.VMEM((2, LANES, 2 * HG_DK), F32)],
        compiler_params=_cparams(("parallel", "arbitrary")),
        name="hg_bwd_state",
    )(h, h, lb, mpre)
    return pl.pallas_call(
        functools.partial(_hg_main_kernel, upper=upper),
        grid=(B, nb),
        in_specs=[pl.BlockSpec((L, HG_W), lambda b, n: (b * nb + n, 0)),
                  pl.BlockSpec((1, 512), lambda b, n: (0, 0)),
                  pl.BlockSpec((1, 256), lambda b, n: (0, 0)),
                  pl.BlockSpec((nsec, L), lambda b, n: (0, 0)),
                  pl.BlockSpec((nsec, L), lambda b, n: (0, 0)),
                  pl.BlockSpec(kf.shape, lambda b, n: (0, 0, 0)),
                  pl.BlockSpec(kb.shape, lambda b, n: (0, 0, 0)),
                  pl.BlockSpec((1, 1, 2, LANES, 2 * HG_DK), lambda b, n: (b, n, 0, 0, 0))],
        out_specs=pl.BlockSpec((L, 256), lambda b, n: (b * nb + n, 0)),
        out_shape=jax.ShapeDtypeStruct((B * S, 256), BF16),
        scratch_shapes=[pltpu.VMEM((2, LANES, 2 * HG_DK), F32),
                        pltpu.VMEM((2, nsec, 512), F32),
                        pltpu.VMEM((2, L, 512), F32), pltpu.VMEM((2, L, 512), F32),
                        pltpu.VMEM((2, nfac, 512), F32),
                        pltpu.VMEM((2 * (kf.shape[0] + 2), L, 512), F32)],
        compiler_params=_cparams(("parallel", "arbitrary")),
        name="hg_main",
    )(h, lb, nw, mf, mb, kf, kb, sb)


def _outproj_kernel(x_ref, r_ref, s_ref, g_ref, wr_ref, ws_ref, wg_ref, lg_ref, lb_ref, o_ref):
    mix = _dot(r_ref[...], wr_ref[...]) + _dot(s_ref[...], ws_ref[...]) + _dot(g_ref[...], wg_ref[...])
    o_ref[...] = _layer_norm(DEEPNORM_ALPHA * x_ref[...] + mix, lg_ref[...], lb_ref[...])


def _outproj_router_kernel(x_ref, r_ref, s_ref, g_ref, wr_ref, ws_ref, wg_ref, lg_ref, lb_ref, wrt_ref,
                           o_ref, ot_ref, route_ref, cnt_ref):
    @pl.when(pl.program_id(0) == 0)
    def _():
        cnt_ref[...] = jnp.zeros_like(cnt_ref)

    mix = _dot(r_ref[...], wr_ref[...]) + _dot(s_ref[...], ws_ref[...]) + _dot(g_ref[...], wg_ref[...])
    y = _layer_norm(DEEPNORM_ALPHA * x_ref[...] + mix, lg_ref[...], lb_ref[...])
    o_ref[...] = y
    ot_ref[...] = _to_token_tiles(y)
    w = wrt_ref[...]
    w_hi = w.astype(BF16)
    w_lo = (w - w_hi.astype(F32)).astype(BF16)
    y_hi = y.astype(BF16)
    y_lo = (y - y_hi.astype(F32)).astype(BF16)
    logits = _dot(y_hi, w_hi) + _dot(y_lo, w_hi) + _dot(y_hi, w_lo)
    tm = logits.shape[0]
    lane = lax.broadcasted_iota(jnp.int32, logits.shape, 1).astype(F32)
    neg = jnp.float32(-jnp.inf)
    l1 = jnp.where(lane < N_EXPERTS, logits, neg)
    m1 = jnp.max(l1, axis=-1, keepdims=True)
    i1 = jnp.min(jnp.where(l1 == m1, lane, float(LANES)), axis=-1, keepdims=True)
    l2 = jnp.where(lane == i1, neg, l1)
    m2 = jnp.max(l2, axis=-1, keepdims=True)
    i2 = jnp.min(jnp.where(l2 == m2, lane, float(LANES)), axis=-1, keepdims=True)
    e2 = jnp.exp(m2 - m1)
    den = 1.0 + e2
    picked = jnp.where((lane == i1) | (lane == i2), 1.0, 0.0)
    rr = lax.broadcasted_iota(jnp.int32, (tm, tm), 0)
    cc = lax.broadcasted_iota(jnp.int32, (tm, tm), 1)
    before = _dot(jnp.where(rr > cc, 1.0, 0.0).astype(BF16), picked.astype(BF16))
    rank = before + cnt_ref[0:1, :]
    r1 = jnp.sum(jnp.where(lane == i1, rank, 0.0), axis=-1, keepdims=True)
    r2 = jnp.sum(jnp.where(lane == i2, rank, 0.0), axis=-1, keepdims=True)
    cnt_ref[...] = cnt_ref[...] + jnp.sum(picked, axis=0, keepdims=True)
    route = jnp.where(lane == 0.0, i1, 0.0) + jnp.where(lane == 1.0, i2, 0.0)
    route = route + jnp.where(lane == 2.0, 1.0 / den, 0.0) + jnp.where(lane == 3.0, e2 / den, 0.0)
    route_ref[...] = route + jnp.where(lane == 4.0, r1, 0.0) + jnp.where(lane == 5.0, r2, 0.0)


def _out_proj(x2d, ret, swa, hgo, w_r, w_s, w_g, ln_g, ln_b, w_router=None):
    T = x2d.shape[0]
    tm = 1024
    row = lambda i: (i, 0)
    full = lambda i: (0, 0)
    in_specs = [pl.BlockSpec((tm, D_MODEL), row),
                pl.BlockSpec((tm, 256), row), pl.BlockSpec((tm, 512), row), pl.BlockSpec((tm, 256), row),
                pl.BlockSpec((256, D_MODEL), full), pl.BlockSpec((512, D_MODEL), full),
                pl.BlockSpec((256, D_MODEL), full),
                pl.BlockSpec((1, D_MODEL), full), pl.BlockSpec((1, D_MODEL), full)]
    args = [x2d, ret, swa, hgo, w_r, w_s, w_g, ln_g, ln_b]
    if w_router is None:
        return pl.pallas_call(
            _outproj_kernel, grid=(T // tm,), in_specs=in_specs,
            out_specs=pl.BlockSpec((tm, D_MODEL), row),
            out_shape=jax.ShapeDtypeStruct((T, D_MODEL), F32),
            compiler_params=_cparams(("parallel",)), name="out_proj",
        )(*args)
    return pl.pallas_call(
        _outproj_router_kernel, grid=(T // tm,),
        in_specs=in_specs + [pl.BlockSpec((D_MODEL, LANES), full)],
        out_specs=[pl.BlockSpec((tm, D_MODEL), row), pl.BlockSpec((tm, 8, LANES), lambda i: (i, 0, 0)),
                   pl.BlockSpec((tm, LANES), row), pl.BlockSpec((8, LANES), full)],
        out_shape=[jax.ShapeDtypeStruct((T, D_MODEL), F32), jax.ShapeDtypeStruct((T, 8, LANES), F32),
                   jax.ShapeDtypeStruct((T, LANES), F32), jax.ShapeDtypeStruct((8, LANES), F32)],
        compiler_params=_cparams(("arbitrary",)), name="out_proj_router",
    )(*args, w_router)


FFN_CHUNKS = 2


def _ffn_kernel(x_ref, wg_ref, wu_ref, wd_ref, lg_ref, lb_ref, o_ref):
    x = x_ref[...]
    xb = x.astype(BF16)
    th = FFN_DIM // FFN_CHUNKS
    f = jnp.zeros(x.shape, F32)
    for c in range(FFN_CHUNKS):
        g = _dot(xb, wg_ref[:, c * th:(c + 1) * th])
        u = _dot(xb, wu_ref[:, c * th:(c + 1) * th])
        a = (g * jax.nn.sigmoid(g)) * u
        f = f + _dot(a.astype(BF16), wd_ref[c * th:(c + 1) * th, :])
    o_ref[...] = _layer_norm(DEEPNORM_ALPHA * x + f, lg_ref[...], lb_ref[...])


def _ffn(x2d, wg, wu, wd, ln_g, ln_b):
    T = x2d.shape[0]
    tm = 512
    once = pl.Buffered(1)
    return pl.pallas_call(
        _ffn_kernel, grid=(T // tm,),
        in_specs=[pl.BlockSpec((tm, D_MODEL), lambda i: (i, 0)),
                  pl.BlockSpec((D_MODEL, FFN_DIM), lambda i: (0, 0), pipeline_mode=once),
                  pl.BlockSpec((D_MODEL, FFN_DIM), lambda i: (0, 0), pipeline_mode=once),
                  pl.BlockSpec((FFN_DIM, D_MODEL), lambda i: (0, 0), pipeline_mode=once),
                  pl.BlockSpec((1, D_MODEL), lambda i: (0, 0)),
                  pl.BlockSpec((1, D_MODEL), lambda i: (0, 0))],
        out_specs=pl.BlockSpec((tm, D_MODEL), lambda i: (i, 0)),
        out_shape=jax.ShapeDtypeStruct((T, D_MODEL), F32),
        compiler_params=_cparams(("parallel",)), name="ffn",
    )(x2d, wg, wu, wd, ln_g, ln_b)


MOE_TILE = 512
MOE_TH = 1792
MOE_TOK = 512


def _moe_num_tiles(T):
    return (2 * T) // MOE_TILE + N_EXPERTS + 1


def _moe_tables(route, counts):
    T = route.shape[0]
    tm = MOE_TILE
    cnt = counts[0, :N_EXPERTS].astype(jnp.int32)
    pc = ((cnt + tm - 1) // tm) * tm
    pend = jnp.cumsum(pc)
    pstart = pend - pc
    e = route[:, 0:2].astype(jnp.int32)
    rank = route[:, 4:6].astype(jnp.int32)
    sel = e[:, :, None] == jnp.arange(N_EXPERTS, dtype=jnp.int32)[None, None, :]
    pos = jnp.sum(jnp.where(sel, pstart[None, None, :], 0), axis=-1) + rank
    n_tiles = _moe_num_tiles(T)
    tile_row = jnp.arange(n_tiles, dtype=jnp.int32) * tm
    texp = jnp.minimum(jnp.sum((tile_row[:, None] >= pend[None, :]).astype(jnp.int32), axis=1), N_EXPERTS - 1)
    n_used = pend[-1] // tm
    last = jnp.sum(jnp.where(jnp.arange(n_tiles) == n_used - 1, texp, 0))
    texp = jnp.where(jnp.arange(n_tiles) < n_used, texp, last)
    return pos.reshape(-1), texp, n_used.reshape(1), pstart + cnt


def _to_token_tiles(x):
    cols = jnp.stack([x[:, LANES * c:LANES * c + LANES] for c in range(D_MODEL // LANES)], axis=0)
    return pltpu.einshape("csl->scl", cols)


def _from_token_tiles(xt):
    y = pltpu.einshape("scl->csl", xt)
    return jnp.concatenate([y[c] for c in range(D_MODEL // LANES)], axis=1)


MOE_DMA_UNROLL = 8


def _moe_dispatch_kernel(pos_ref, pad_ref, nused_ref, xt_ref, xs_hbm, zero_ref, sem, zsem):
    i = pl.program_id(0)
    tm = xt_ref.shape[0]
    n_tiles = xs_hbm.shape[0] // MOE_TILE

    def zero_copy(row):
        return pltpu.make_async_copy(zero_ref, xs_hbm.at[pl.ds(row, MOE_TILE)], zsem)

    @pl.when(i == 0)
    def _():
        zero_ref[...] = jnp.zeros_like(zero_ref)
        for e in range(N_EXPERTS):
            zero_copy(pad_ref[e]).start()
        for e in range(N_EXPERTS):
            zero_copy(pad_ref[e]).wait()
        for t in range(n_tiles - N_EXPERTS - 1, n_tiles):
            @pl.when(t >= nused_ref[0])
            def _():
                zero_copy(t * MOE_TILE).start()
                zero_copy(t * MOE_TILE).wait()

    def body(kk, c):
        k0 = kk * MOE_DMA_UNROLL
        t0 = i * tm + k0
        dst = [pos_ref[2 * t0 + u] for u in range(2 * MOE_DMA_UNROLL)]
        for u in range(2 * MOE_DMA_UNROLL):
            pltpu.make_async_copy(xt_ref.at[k0 + u // 2], xs_hbm.at[dst[u]], sem).start()
        return c

    lax.fori_loop(0, tm // MOE_DMA_UNROLL, body, 0)
    for s in range(2):
        pltpu.make_async_copy(xt_ref, xs_hbm.at[pl.ds(0, tm)], sem).wait()


def _moe_dispatch(x1t, pos, padstart, n_used):
    T = x1t.shape[0]
    P = _moe_num_tiles(T) * MOE_TILE
    return pl.pallas_call(
        _moe_dispatch_kernel,
        grid_spec=pltpu.PrefetchScalarGridSpec(
            num_scalar_prefetch=3, grid=(T // MOE_TOK,),
            in_specs=[pl.BlockSpec((MOE_TOK, 8, LANES), lambda i, pos, pad, nused: (i, 0, 0))],
            out_specs=pl.BlockSpec(memory_space=pl.ANY),
            scratch_shapes=[pltpu.VMEM((MOE_TILE, 8, LANES), F32), pltpu.SemaphoreType.DMA,
                            pltpu.SemaphoreType.DMA]),
        out_shape=jax.ShapeDtypeStruct((P, 8, LANES), F32),
        compiler_params=_cparams(("arbitrary",)), name="moe_dispatch",
    )(pos, padstart, n_used, x1t)


def _moe_group_kernel(texp_ref, nused_ref, xs_ref, wg_ref, wu_ref, wd_ref, y_ref, xb_ref, acc_ref):
    i = pl.program_id(0)
    j = pl.program_id(1)

    @pl.when(i < nused_ref[0])
    def _():
        @pl.when(j == 0)
        def _():
            acc_ref[...] = jnp.zeros_like(acc_ref)

        xb = _from_token_tiles(xs_ref[...]).astype(BF16)
        g = _dot(xb, wg_ref[...])
        u = _dot(xb, wu_ref[...])
        a = (g * jax.nn.sigmoid(g)) * u
        acc_ref[...] += _dot(a.astype(BF16), wd_ref[...])

        @pl.when(j == pl.num_programs(1) - 1)
        def _():
            y_ref[...] = _to_token_tiles(acc_ref[...])

    @pl.when((i >= nused_ref[0]) & (j == 0))
    def _():
        y_ref[...] = jnp.zeros_like(y_ref)


def _moe_group(xs, texp, n_used, wg, wu, wd):
    P = xs.shape[0]
    tm, th = MOE_TILE, MOE_TH
    nh = EXPERT_DIM // th

    def row_map(i, j, texp, nused):
        return (jnp.minimum(i, nused[0] - 1), 0, 0)

    def col(i, j, nused):
        return jnp.where(i < nused[0], j, nh - 1)

    return pl.pallas_call(
        _moe_group_kernel,
        grid_spec=pltpu.PrefetchScalarGridSpec(
            num_scalar_prefetch=2, grid=(P // tm, nh),
            in_specs=[pl.BlockSpec((tm, 8, LANES), row_map),
                      pl.BlockSpec((None, D_MODEL, th), lambda i, j, texp, nused: (texp[i], 0, col(i, j, nused))),
                      pl.BlockSpec((None, D_MODEL, th), lambda i, j, texp, nused: (texp[i], 0, col(i, j, nused))),
                      pl.BlockSpec((None, th, D_MODEL), lambda i, j, texp, nused: (texp[i], col(i, j, nused), 0))],
            out_specs=pl.BlockSpec((tm, 8, LANES), lambda i, j, texp, nused: (i, 0, 0)),
            scratch_shapes=[pltpu.VMEM((tm, D_MODEL), BF16), pltpu.VMEM((tm, D_MODEL), F32)]),
        out_shape=jax.ShapeDtypeStruct((P, 8, LANES), F32),
        compiler_params=_cparams(("arbitrary", "arbitrary")), name="moe_group",
    )(texp, n_used, xs, wg, wu, wd)


def _moe_combine_kernel(pos_ref, x_ref, route_ref, y_hbm, lg_ref, lb_ref, o_ref, buf_ref, sem):
    i = pl.program_id(0)
    tm = x_ref.shape[0]

    def issue(tile, slot):
        def body(kk, c):
            k0 = kk * MOE_DMA_UNROLL
            t0 = tile * tm + k0
            src = [pos_ref[2 * t0 + u] for u in range(2 * MOE_DMA_UNROLL)]
            for u in range(2 * MOE_DMA_UNROLL):
                pltpu.make_async_copy(y_hbm.at[src[u]], buf_ref.at[slot, (u % 2) * tm + k0 + u // 2],
                                      sem.at[slot]).start()
            return c
        lax.fori_loop(0, tm // MOE_DMA_UNROLL, body, 0)

    @pl.when(i == 0)
    def _():
        issue(0, 0)

    slot = i % 2
    pltpu.make_async_copy(y_hbm.at[pl.ds(0, 2 * tm)], buf_ref.at[slot], sem.at[slot]).wait()

    @pl.when(i + 1 < pl.num_programs(0))
    def _():
        issue(i + 1, 1 - slot)

    a = _from_token_tiles(buf_ref[slot, pl.ds(0, tm)])
    b = _from_token_tiles(buf_ref[slot, pl.ds(tm, tm)])
    f = route_ref[:, 2:3] * a + route_ref[:, 3:4] * b
    o_ref[...] = _layer_norm(DEEPNORM_ALPHA * x_ref[...] + f, lg_ref[...], lb_ref[...])


def _moe_combine(x1, route, y, pos, ln_g, ln_b):
    T = x1.shape[0]
    tm = MOE_TOK
    return pl.pallas_call(
        _moe_combine_kernel,
        grid_spec=pltpu.PrefetchScalarGridSpec(
            num_scalar_prefetch=1, grid=(T // tm,),
            in_specs=[pl.BlockSpec((tm, D_MODEL), lambda i, pos: (i, 0)),
                      pl.BlockSpec((tm, LANES), lambda i, pos: (i, 0)),
                      pl.BlockSpec(memory_space=pl.ANY),
                      pl.BlockSpec((1, D_MODEL), lambda i, pos: (0, 0)),
                      pl.BlockSpec((1, D_MODEL), lambda i, pos: (0, 0))],
            out_specs=pl.BlockSpec((tm, D_MODEL), lambda i, pos: (i, 0)),
            scratch_shapes=[pltpu.VMEM((2, 2 * tm, 8, LANES), F32), pltpu.SemaphoreType.DMA((2,))]),
        out_shape=jax.ShapeDtypeStruct((T, D_MODEL), F32),
        compiler_params=_cparams(("arbitrary",)), name="moe_combine",
    )(pos, x1, route, y, ln_g, ln_b)


def _moe(x1, x1t, route, counts, wg, wu, wd, ln_g, ln_b):
    pos, texp, n_used, padstart = _moe_tables(route, counts)
    xs = _moe_dispatch(x1t, pos, padstart, n_used)
    y = _moe_group(xs, texp, n_used, wg, wu, wd)
    return _moe_combine(x1, route, y, pos, ln_g, ln_b)


def _t5_bucket(rel):
    nb = REL_BUCKETS // 2
    ret = (rel > 0).astype(np.int32) * nb
    n = np.abs(rel)
    max_exact = nb // 2
    large = max_exact + (np.log(np.maximum(n, 1) / max_exact) / np.log(REL_MAX_DIST / max_exact)
                         * (nb - max_exact)).astype(np.int32)
    large = np.minimum(large, nb - 1)
    return ret + np.where(n < max_exact, n, large)


def _swa_head_order():
    return [h for c in range(4) for h in (c, 4 + c)]


def _prepare(w_in, w_out, ret_decay, swa_sink, rel_bias, hg_lb, hg_norm_w, moe_router, seq_lens):
    order = _swa_head_order()
    w_in_p = jnp.concatenate(
        [w_in[:, :, 1792:3840], w_in[:, :, 0:1024]]
        + [w_in[:, :, 1024 + 64 * h:1088 + 64 * h] for h in order]
        + [w_in[:, :, 1536:1792]], axis=2).astype(BF16)
    w_r = w_out[:, 0:256, :].astype(BF16)
    w_s = jnp.concatenate([w_out[:, 256 + 64 * h:320 + 64 * h, :] for h in order], axis=1).astype(BF16)
    w_g = w_out[:, 768:1024, :].astype(BF16)

    rd = ret_decay.astype(F32)
    rdp = jnp.repeat(rd.reshape(DEPTH, 4, 2), 64, axis=-1)
    rdp = jnp.concatenate([rdp, rdp], axis=1)
    rdh = jnp.broadcast_to(rd.reshape(DEPTH, 8, 1), (DEPTH, 8, LANES))

    W = SWA_BLOCK
    rel = np.arange(3 * W)[None, :] - W - np.arange(W)[:, None]
    onehot = jnp.asarray(_t5_bucket(rel)[..., None] == np.arange(REL_BUCKETS), F32)
    bias = jnp.einsum("qkb,bh->hqk", onehot, rel_bias.astype(F32), precision=lax.Precision.HIGHEST)
    band = jnp.asarray(np.abs(rel) <= WINDOW)
    bias_kq = jnp.transpose(jnp.where(band[None], bias, MASK_VALUE), (0, 2, 1))
    order = _swa_head_order()
    bias_t = jnp.concatenate([bias_kq[h] for h in order], axis=1)
    sink = swa_sink.astype(F32)
    sink_t = jnp.concatenate([jnp.broadcast_to(sink[:, h, None, None], (DEPTH, 8, W)) for h in order], axis=2)

    p = jax.nn.softmax(hg_lb.astype(F32), axis=0)
    lower = (jnp.cumsum(p, axis=0) - p[0]).reshape(DEPTH, 1, HG_HEADS * HG_DK)
    nw = hg_norm_w.astype(F32).reshape(DEPTH, 1, HG_HEADS * HG_DV)

    mf, mb, kf, kb, upper = _hg_constants(HG_BLOCK)
    L = HG_BLOCK
    mpre = np.concatenate([np.tril(np.ones((L, L), np.float32), -1), np.ones((8, L), np.float32)], axis=0)
    consts = (jnp.asarray(mf, BF16), jnp.asarray(mb, BF16), jnp.asarray(kf, F32), jnp.asarray(kb, F32),
              jnp.asarray(mpre, BF16), upper)

    half = RET_DK // 2
    inv = ROPE_BASE ** (-np.arange(half, dtype=np.float32) / half)
    rope = {}
    for S in set(seq_lens):
        ang = jnp.arange(S, dtype=F32)[:, None] * jnp.asarray(inv)[None, :]
        c32, s32 = jnp.cos(ang), jnp.sin(ang)
        rope[S] = (jnp.tile(c32, (1, 4)), jnp.tile(jnp.concatenate([-s32, s32], axis=1), (1, 2)))

    router = jnp.pad(moe_router.astype(F32), ((0, 0), (0, 0), (0, LANES - N_EXPERTS)))
    return dict(w_in=w_in_p, w_r=w_r, w_s=w_s, w_g=w_g, rdp=rdp, rdh=rdh, bias=bias_t, sink=sink_t,
                lower=lower, nw=nw, consts=consts, rope=rope, router=router)


def _trunk(x, prm, ln1_g, ln1_b, ln2_g, ln2_b, ffn_w, moe_w):
    B, S, _ = x.shape
    x2d = x.reshape(B * S, D_MODEL)
    cos_t, sin_t = prm["rope"][S]
    for l in range(DEPTH):
        h = _in_proj(x2d, prm["w_in"][l])
        ret = _retention(h, cos_t, sin_t, prm["rdp"][l], prm["rdh"][l], B, S)
        swa = _swa(h, prm["bias"], prm["sink"][l], B, S)
        hgo = _hgrn2(h, prm["lower"][l], prm["nw"][l], prm["consts"], B, S)
        g1, b1 = ln1_g[l].reshape(1, D_MODEL), ln1_b[l].reshape(1, D_MODEL)
        g2, b2 = ln2_g[l].reshape(1, D_MODEL), ln2_b[l].reshape(1, D_MODEL)
        i = l // 2
        if l % 2 == 0:
            x1 = _out_proj(x2d, ret, swa, hgo, prm["w_r"][l], prm["w_s"][l], prm["w_g"][l], g1, b1)
            wg, wu, wd = ffn_w
            x2d = _ffn(x1, wg[i], wu[i], wd[i], g2, b2)
        else:
            x1, x1t, route, counts = _out_proj(x2d, ret, swa, hgo, prm["w_r"][l], prm["w_s"][l], prm["w_g"][l],
                                               g1, b1, prm["router"][i])
            wg, wu, wd = moe_w
            x2d = _moe(x1, x1t, route, counts, wg[i], wu[i], wd[i], g2, b2)
    return x2d.reshape(B, S, D_MODEL)


def kernel(x_prompt, x_sample, w_in, w_out, ret_decay, swa_sink, rel_bias, hg_lb, hg_norm_w, ln1_g, ln1_b,
           ln2_g, ln2_b, ffn_w_gate, ffn_w_up, ffn_w_down, moe_router, moe_w_gate, moe_w_up, moe_w_down):
    prm = _prepare(w_in, w_out, ret_decay, swa_sink, rel_bias, hg_lb, hg_norm_w, moe_router,
                   (x_prompt.shape[1], x_sample.shape[1]))
    ffn_w = (ffn_w_gate.astype(BF16), ffn_w_up.astype(BF16), ffn_w_down.astype(BF16))
    moe_w = (moe_w_gate.astype(BF16), moe_w_up.astype(BF16), moe_w_down.astype(BF16))
    outs = []
    for x in (x_prompt, x_sample):
        outs.append(_trunk(x, prm, ln1_g, ln1_b, ln2_g, ln2_b, ffn_w, moe_w))
    return tuple(outs)
```

```python
import functools
import math

import numpy as np
import jax
import jax.numpy as jnp
from jax import lax
from jax.experimental import pallas as pl
from jax.experimental.pallas import tpu as pltpu

F32 = jnp.float32
BF16 = jnp.bfloat16

D_MODEL = 1024
DEPTH = 2
RET_HEADS = 4
RET_DK = 64
RET_CHUNK = 128
ROPE_BASE = 10000.0
SWA_Q_HEADS = 8
SWA_KV_HEADS = 2
SWA_DH = 64
WINDOW = 128
SWA_BLOCK = 128
REL_BUCKETS = 32
REL_MAX_DIST = 128
MASK_VALUE = -1e30
HG_HEADS = 4
HG_DK = 128
HG_DV = 64
FFN_DIM = 2816
N_EXPERTS = 8
EXPERT_DIM = 3584
LN_EPS = 1e-5
RMS_EPS = 1e-6
DEEPNORM_ALPHA = (2.0 * DEPTH) ** 0.25

LANES = 128
VMEM_LIMIT = 56 * 1024 * 1024

HG_W = 2048
RET_W = 1024
SWA_W = 768
IN_COLS = HG_W + RET_W + SWA_W

HG_BLOCK = 256
HG_BASE = 8
HG_PRE_BLOCKS = 4
RET_TILE = 512
SWA_TILE = 512


def _cparams(sem):
    return pltpu.CompilerParams(dimension_semantics=sem, vmem_limit_bytes=VMEM_LIMIT)


def _dot(a, b):
    return jnp.dot(a, b, preferred_element_type=F32)


def _dot_nt(a, b):
    return lax.dot_general(a, b, (((1,), (1,)), ((), ())), preferred_element_type=F32)


def _dot_tn(a, b):
    return lax.dot_general(a, b, (((0,), (0,)), ((), ())), preferred_element_type=F32)


def _dot_split(a_bf16, x):
    hi = x.astype(BF16)
    lo = (x - hi.astype(F32)).astype(BF16)
    return _dot(a_bf16, hi) + _dot(a_bf16, lo)


def _dot_split_rhs_const(x, b_bf16):
    hi = x.astype(BF16)
    lo = (x - hi.astype(F32)).astype(BF16)
    return _dot(hi, b_bf16) + _dot(lo, b_bf16)


def _layer_norm(y, g, b):
    mu = jnp.mean(y, axis=-1, keepdims=True)
    yc = y - mu
    var = jnp.mean(yc * yc, axis=-1, keepdims=True)
    return yc * lax.rsqrt(var + LN_EPS) * g + b


def _lane_lo_mask(shape):
    return lax.broadcasted_iota(jnp.int32, shape, len(shape) - 1) < (LANES // 2)


def _head_block_ones():
    r = lax.broadcasted_iota(jnp.int32, (LANES, LANES), 0) // 64
    c = lax.broadcasted_iota(jnp.int32, (LANES, LANES), 1) // 64
    return r == c


def _inproj_kernel(x_ref, w_ref, o_ref):
    o_ref[...] = _dot(x_ref[...].astype(BF16), w_ref[...]).astype(o_ref.dtype)


def _in_proj(x2d, w):
    T = x2d.shape[0]
    tm, tn = 2048, 1280
    return pl.pallas_call(
        _inproj_kernel,
        grid=(T // tm, IN_COLS // tn),
        in_specs=[pl.BlockSpec((tm, D_MODEL), lambda i, j: (i, 0)),
                  pl.BlockSpec((D_MODEL, tn), lambda i, j: (0, j))],
        out_specs=pl.BlockSpec((tm, tn), lambda i, j: (i, j)),
        out_shape=jax.ShapeDtypeStruct((T, IN_COLS), BF16),
        compiler_params=_cparams(("parallel", "arbitrary")),
        name="in_proj",
    )(x2d, w)


def _rotary(x, cos, sin_signed, even):
    sw = jnp.where(even, pltpu.roll(x, 96, 1), pltpu.roll(x, 32, 1))
    return x * cos + sw * sin_signed


def _ret_tables(rdp_ref):
    lg = jnp.log1p(-jnp.exp(rdp_ref[...]))
    return lg


def _ret_bwd_state_kernel(h_ref, cos_ref, sin_ref, rdp_ref, sb_ref, st_ref):
    C = RET_CHUNK
    nc = h_ref.shape[0] // C

    @pl.when(pl.program_id(1) == 0)
    def _():
        st_ref[...] = jnp.zeros_like(st_ref)

    lg = _ret_tables(rdp_ref)
    rows = lax.broadcasted_iota(jnp.int32, (C, LANES), 0).astype(F32)
    even = (lax.broadcasted_iota(jnp.int32, (C, LANES), 1) // 32) % 2 == 0
    bm = _head_block_ones()
    for c in range(nc - 1, -1, -1):
        sl = pl.ds(c * C, C)
        cos = cos_ref[sl, :]
        sin = sin_ref[sl, :]
        for p in range(2):
            lgb = lg[2 + p:3 + p, :]
            kwb = jnp.exp(lgb * rows)
            decb = jnp.where(bm, jnp.exp(lgb * float(C)), 0.0)
            k = _rotary(h_ref[sl, 256 + 128 * p:384 + 128 * p].astype(F32), cos, sin, even) * (RET_DK ** -0.5)
            v = h_ref[sl, 512 + 128 * p:640 + 128 * p].astype(F32)
            sb_ref[0, c, p] = st_ref[p]
            kv = _dot_tn((k * kwb).astype(BF16), v.astype(BF16))
            st_ref[p] = decb * st_ref[p] + jnp.where(bm, kv, 0.0)


def _ret_main_kernel(h_ref, cos_ref, sin_ref, rdp_ref, rdh_ref, sb_ref, o_ref, st_ref):
    C = RET_CHUNK
    nc = h_ref.shape[0] // C

    @pl.when(pl.program_id(1) == 0)
    def _():
        st_ref[...] = jnp.zeros_like(st_ref)

    lg = _ret_tables(rdp_ref)
    lgh = jnp.log1p(-jnp.exp(rdh_ref[...]))
    rows = lax.broadcasted_iota(jnp.int32, (C, LANES), 0).astype(F32)
    cols = lax.broadcasted_iota(jnp.int32, (C, LANES), 1).astype(F32)
    rel = rows - cols
    even = (lax.broadcasted_iota(jnp.int32, (C, LANES), 1) // 32) % 2 == 0
    lo = _lane_lo_mask((C, LANES))
    bm = _head_block_ones()
    bones = jnp.where(bm, 1.0, 0.0).astype(BF16)

    dmat = []
    for hd in range(RET_HEADS):
        f = jnp.where(rel >= 0, jnp.exp(lgh[hd:hd + 1, :] * jnp.maximum(rel, 0.0)), 0.0)
        b = jnp.where(rel <= 0, jnp.exp(lgh[4 + hd:5 + hd, :] * jnp.maximum(-rel, 0.0)), 0.0)
        dmat.append(f + b)

    for c in range(nc):
        sl = pl.ds(c * C, C)
        cos = cos_ref[sl, :]
        sin = sin_ref[sl, :]
        for p in range(2):
            lgf = lg[p:p + 1, :]
            lgb = lg[2 + p:3 + p, :]
            q = _rotary(h_ref[sl, 128 * p:128 * p + 128].astype(F32), cos, sin, even)
            k = _rotary(h_ref[sl, 256 + 128 * p:384 + 128 * p].astype(F32), cos, sin, even) * (RET_DK ** -0.5)
            v = h_ref[sl, 512 + 128 * p:640 + 128 * p].astype(F32)
            g = h_ref[sl, 768 + 128 * p:896 + 128 * p].astype(F32)
            kb = k.astype(BF16)
            vb = v.astype(BF16)
            qs = jnp.concatenate([jnp.where(lo, q, 0.0), jnp.where(lo, 0.0, q)], axis=0).astype(BF16)
            s = _dot_nt(qs, kb)
            pm = jnp.concatenate([s[:C] * dmat[2 * p], s[C:] * dmat[2 * p + 1]], axis=1).astype(BF16)
            vs = jnp.concatenate([jnp.where(lo, v, 0.0), jnp.where(lo, 0.0, v)], axis=0).astype(BF16)
            intra = _dot(pm, vs)
            qwf = jnp.exp(lgf * (rows + 1.0))
            qwb = jnp.exp(lgb * (float(C) - rows))
            qi = jnp.concatenate([q * qwf, q * qwb], axis=1).astype(BF16)
            sst = jnp.concatenate([st_ref[p], sb_ref[0, c, p]], axis=0).astype(BF16)
            o = intra + _dot(qi, sst)
            kwf = jnp.exp(lgf * (float(C) - 1.0 - rows))
            decf = jnp.where(bm, jnp.exp(lgf * float(C)), 0.0)
            kv = _dot_tn((k * kwf).astype(BF16), vb)
            st_ref[p] = decf * st_ref[p] + jnp.where(bm, kv, 0.0)
            ms = _dot_split_rhs_const(o * o, bones) * (1.0 / RET_DK)
            y = o * lax.rsqrt(ms + RMS_EPS) * (g * jax.nn.sigmoid(g))
            o_ref[sl, 128 * p:128 * p + 128] = y.astype(o_ref.dtype)


def _retention(h, cos_t, sin_t, rdp, rdh, B, S):
    L = RET_TILE
    nb = S // L
    ncb = L // RET_CHUNK
    n_chunks = S // RET_CHUNK
    colblk = HG_W // RET_W
    sb = pl.pallas_call(
        _ret_bwd_state_kernel,
        grid=(B, nb),
        in_specs=[pl.BlockSpec((L, RET_W), lambda b, n: (b * nb + nb - 1 - n, colblk)),
                  pl.BlockSpec((L, LANES), lambda b, n: (nb - 1 - n, 0)),
                  pl.BlockSpec((L, LANES), lambda b, n: (nb - 1 - n, 0)),
                  pl.BlockSpec((8, LANES), lambda b, n: (0, 0))],
        out_specs=pl.BlockSpec((1, ncb, 2, LANES, LANES), lambda b, n: (b, nb - 1 - n, 0, 0, 0)),
        out_shape=jax.ShapeDtypeStruct((B, n_chunks, 2, LANES, LANES), F32),
        scratch_shapes=[pltpu.VMEM((2, LANES, LANES), F32)],
        compiler_params=_cparams(("parallel", "arbitrary")),
        name="ret_bwd_state",
    )(h, cos_t, sin_t, rdp)
    return pl.pallas_call(
        _ret_main_kernel,
        grid=(B, nb),
        in_specs=[pl.BlockSpec((L, RET_W), lambda b, n: (b * nb + n, colblk)),
                  pl.BlockSpec((L, LANES), lambda b, n: (n, 0)),
                  pl.BlockSpec((L, LANES), lambda b, n: (n, 0)),
                  pl.BlockSpec((8, LANES), lambda b, n: (0, 0)),
                  pl.BlockSpec((8, LANES), lambda b, n: (0, 0)),
                  pl.BlockSpec((1, ncb, 2, LANES, LANES), lambda b, n: (b, n, 0, 0, 0))],
        out_specs=pl.BlockSpec((L, 256), lambda b, n: (b * nb + n, 0)),
        out_shape=jax.ShapeDtypeStruct((B * S, 256), BF16),
        scratch_shapes=[pltpu.VMEM((2, LANES, LANES), F32)],
        compiler_params=_cparams(("parallel", "arbitrary")),
        name="ret_main",
    )(h, cos_t, sin_t, rdp, rdh, sb)


def _swa_kernel(q_ref, kp_ref, kc_ref, kn_ref, vp_ref, vc_ref, vn_ref, bias_ref, sink_ref, o_ref):
    W = SWA_BLOCK
    nq = q_ref.shape[0] // W
    n = pl.program_id(1)
    last = pl.num_programs(1) - 1
    kall = jnp.concatenate([kp_ref[...], kc_ref[...], kn_ref[...]], axis=0).astype(BF16)
    vt = jnp.concatenate([vp_ref[...], vc_ref[...], vn_ref[...]], axis=0).astype(F32).T.astype(BF16)
    lo = _lane_lo_mask((W, LANES))
    ki = lax.broadcasted_iota(jnp.int32, (3 * W, SWA_Q_HEADS * W), 0)
    edge_first = jnp.where(ki >= W, 0.0, jnp.where(n > 0, 0.0, MASK_VALUE))
    edge_last = jnp.where(ki < 2 * W, 0.0, jnp.where(n < last, 0.0, MASK_VALUE))
    sink = sink_ref[0:1, :]
    for r in range(nq):
        qs = []
        for c in range(4):
            q = q_ref[pl.ds(r * W, W), 128 * c:128 * c + 128].astype(F32) * (SWA_DH ** -0.5)
            qs += [jnp.where(lo, q, 0.0), jnp.where(lo, 0.0, q)]
        qs = jnp.concatenate(qs, axis=0).astype(BF16)
        s = _dot_nt(kall[r * W:(r + 3) * W], qs) + bias_ref[...]
        if r == 0:
            s = s + edge_first
        if r == nq - 1:
            s = s + edge_last
        m = jnp.maximum(jnp.max(s, axis=0, keepdims=True), sink)
        e = jnp.exp(s - m)
        den = jnp.sum(e, axis=0, keepdims=True) + jnp.exp(sink - m)
        ot = _dot(vt[:, r * W:(r + 3) * W], e.astype(BF16)) / den
        for c in range(4):
            sel = jnp.concatenate([ot[0:SWA_DH, 2 * c * W:(2 * c + 1) * W],
                                   ot[SWA_DH:2 * SWA_DH, (2 * c + 1) * W:(2 * c + 2) * W]], axis=0)
            o_ref[pl.ds(r * W, W), 128 * c:128 * c + 128] = sel.T.astype(o_ref.dtype)


def _swa(h, bias_t, sink_t, B, S):
    L = SWA_TILE
    W = SWA_BLOCK
    nb = S // L
    r = L // W
    nblk = S // W
    qcol = (HG_W + RET_W) // 512
    kcol = (HG_W + RET_W + 512) // LANES
    vcol = kcol + 1

    def prev_map(col):
        return lambda b, n: (b * nblk + jnp.maximum(n * r - 1, 0), col)

    def cur_map(col):
        return lambda b, n: (b * nb + n, col)

    def next_map(col):
        return lambda b, n: (b * nblk + jnp.minimum(n * r + r, nblk - 1), col)

    return pl.pallas_call(
        _swa_kernel,
        grid=(B, nb),
        in_specs=[pl.BlockSpec((L, 512), cur_map(qcol)),
                  pl.BlockSpec((W, LANES), prev_map(kcol)),
                  pl.BlockSpec((L, LANES), cur_map(kcol)),
                  pl.BlockSpec((W, LANES), next_map(kcol)),
                  pl.BlockSpec((W, LANES), prev_map(vcol)),
                  pl.BlockSpec((L, LANES), cur_map(vcol)),
                  pl.BlockSpec((W, LANES), next_map(vcol)),
                  pl.BlockSpec((3 * W, SWA_Q_HEADS * W), lambda b, n: (0, 0)),
                  pl.BlockSpec((8, SWA_Q_HEADS * W), lambda b, n: (0, 0))],
        out_specs=pl.BlockSpec((L, 512), lambda b, n: (b * nb + n, 0)),
        out_shape=jax.ShapeDtypeStruct((B * S, 512), BF16),
        compiler_params=_cparams(("parallel", "parallel")),
        name="swa",
    )(h, h, h, h, h, h, h, bias_t, sink_t)


def _hg_levels(L):
    g = L
    out = []
    while g >= 2 * HG_BASE:
        out.append(g)
        g //= 2
    return out


def _hg_constants(L):
    nb = L // HG_BASE
    idx = np.arange(L)
    t = idx[None, :]
    i = idx[:, None]
    r = HG_BASE * (i // HG_BASE) + HG_BASE // 2 - 1
    secs = [((t > r) & (t <= i)).astype(np.float32) - ((t > i) & (t <= r)).astype(np.float32)]
    u = np.arange(nb)[:, None]
    ru = HG_BASE * u + HG_BASE // 2 - 1
    secs.append((t <= ru).astype(np.float32))
    secs.append((t > ru).astype(np.float32))
    masks = [((idx[:, None] // HG_BASE) == (idx[None, :] // HG_BASE)) & (idx[:, None] >= idx[None, :])]
    upper = []
    for g in _hg_levels(L):
        gb = g // HG_BASE
        ref = HG_BASE * ((u // gb) * gb + gb // 2) - 1
        up = (u % gb) >= gb // 2
        secs.append(np.where(up, (t > ref) & (t <= ru), (t > ru) & (t <= ref)).astype(np.float32))
        upper.append([bool(v) for v in up[:, 0]])
        ii, jj = idx[:, None], idx[None, :]
        masks.append(((ii // g) == (jj // g)) & ((ii % g) >= g // 2) & ((jj % g) < g // 2))
    mf = np.concatenate(secs, axis=0)
    mb = np.concatenate([s[::-1, ::-1] for s in secs], axis=0)
    H = L // 2
    sub = [masks[0][:H, :H]] + [m[:H, :H] for m in masks[2:]]
    kf = np.stack(sub).astype(np.float32)
    kb = np.stack([m[::-1, ::-1] for m in sub]).astype(np.float32)
    upper_fb = (tuple(tuple(x) for x in upper), tuple(tuple(x[::-1]) for x in upper))
    return mf, mb, kf, kb, upper_fb


def _hg_gate(z, lb):
    key = (1.0 - lb) * jax.nn.sigmoid(-z)
    return key, jnp.log(1.0 - key)


def _hg_state_mask():
    r = lax.broadcasted_iota(jnp.int32, (LANES, 2 * HG_DK), 0) // HG_DV
    c = lax.broadcasted_iota(jnp.int32, (LANES, 2 * HG_DK), 1) // HG_DK
    return r == c


def _hg_bwd_state_kernel(z_ref, v_ref, lb_ref, m_ref, sb_ref, st_ref):
    L = HG_BLOCK
    nblk = z_ref.shape[0] // L

    @pl.when(pl.program_id(1) == 0)
    def _():
        st_ref[...] = jnp.zeros_like(st_ref)

    bmt = _hg_state_mask()
    for blk in range(nblk - 1, -1, -1):
        rows = slice(blk * L, (blk + 1) * L)
        key, lf = _hg_gate(z_ref[rows, :].astype(F32), lb_ref[...])
        e = _dot_split(m_ref[...], lf)
        ke = key * jnp.exp(e[:L])
        dec = jnp.exp(e[L:L + 1])
        for p in range(2):
            sb_ref[0, blk, p] = st_ref[p]
            kvt = _dot_tn(v_ref[rows, 128 * p:128 * p + 128].astype(BF16),
                          ke[:, 256 * p:256 * p + 256].astype(BF16))
            st_ref[p] = dec[:, 256 * p:256 * p + 256] * st_ref[p] + jnp.where(bmt, kvt, 0.0)


def _hg_scale_blocks(jobs, qp_ref, kp_ref, fac_ref, nblocks):
    for u in range(0, nblocks, 2):
        r0 = slice(HG_BASE * u, HG_BASE * u + HG_BASE)
        r1 = slice(HG_BASE * (u + 1), HG_BASE * (u + 2))
        for hd in range(HG_HEADS):
            ls = slice(HG_DK * hd, HG_DK * hd + HG_DK)
            q0, k0, q1, k1 = qp_ref[r0, ls], kp_ref[r0, ls], qp_ref[r1, ls], kp_ref[r1, ls]
            for dst_ref, row0, flags in jobs:
                a = (q0 if flags[u] else k0) * fac_ref[row0 + u:row0 + u + 1, ls]
                b = (q1 if flags[u + 1] else k1) * fac_ref[row0 + u + 1:row0 + u + 2, ls]
                dst_ref[HG_BASE * u:HG_BASE * (u + 2), ls] = jnp.concatenate([a, b], axis=0).astype(dst_ref.dtype)


def _hg_main_kernel(h_ref, lb_ref, nw_ref, mf_ref, mb_ref, kf_ref, kb_ref, sb_ref, o_ref,
                    st_ref, e2_ref, qp2_ref, kp2_ref, fac2_ref, s_ref, *, upper):
    L = h_ref.shape[0]
    nb = L // HG_BASE
    nlev = kf_ref.shape[0] + 1
    all_up = (True,) * nb

    @pl.when(pl.program_id(1) == 0)
    def _():
        st_ref[...] = jnp.zeros_like(st_ref)

    lo_v = _lane_lo_mask((L, LANES))
    bmt = _hg_state_mask()
    bones = jnp.where(_head_block_ones(), 1.0, 0.0).astype(BF16)
    osum = [jnp.zeros((L, LANES), F32), jnp.zeros((L, LANES), F32)]
    nbuf = nlev + 1
    for d in range(2):
        m_ref = mf_ref if d == 0 else mb_ref
        k_ref = kf_ref if d == 0 else kb_ref
        e_ref, qp_ref, kp_ref, fac_ref = e2_ref.at[d], qp2_ref.at[d], kp2_ref.at[d], fac2_ref.at[d]
        sbuf = [s_ref.at[d * nbuf + k] for k in range(nbuf)]
        key, lf = _hg_gate(h_ref[:, 512 + 512 * d:1024 + 512 * d].astype(F32), lb_ref[...])
        e_ref[...] = _dot_split(m_ref[...], lf)
        eb = e_ref[pl.ds(0, L), :]
        qp_ref[...] = h_ref[:, 0:512].astype(F32) * jnp.exp(eb)
        kp_ref[...] = key * jnp.exp(-eb)
        fac_ref[...] = jnp.exp(e_ref[pl.ds(L, (nlev + 1) * nb), :])
        jobs = [(sbuf[lv - 1], (1 + lv) * nb, upper[d][lv - 1]) for lv in range(1, nlev)]
        jobs.append((sbuf[nlev - 1], 0, all_up))
        if d == 0:
            jobs.append((sbuf[nlev], nb, (False,) * nb))
        _hg_scale_blocks(jobs, qp_ref, kp_ref, fac_ref, nb)
        H = L // 2
        halves = (slice(0, H), slice(H, L))
        qh, kh = (1, 0) if d == 0 else (0, 1)
        masks = [k_ref[lv] > 0.5 for lv in range(nlev - 1)]

        def tiles(ref, hbs):
            return jnp.stack([ref[halves[hb], HG_DK * hd:HG_DK * hd + HG_DK]
                              for hd in range(HG_HEADS) for hb in hbs], axis=0).astype(BF16)

        def bdot(a, b):
            return lax.dot_general(a, b, (((2,), (2,)), ((0,), (0,))), preferred_element_type=F32)

        diag_all = jnp.where(masks[0][None], bdot(tiles(qp_ref, (0, 1)), tiles(kp_ref, (0, 1))), 0.0)
        for lv in range(2, nlev):
            t = tiles(sbuf[lv - 1], (0, 1))
            diag_all = jnp.where(masks[lv - 1][None], bdot(t, t), diag_all)
        diag_all = diag_all.astype(BF16)
        cross_all = bdot(tiles(sbuf[0], (qh,)), tiles(sbuf[0], (kh,))).astype(BF16)
        for p in range(2):
            v = h_ref[:, 1536 + 128 * p:1664 + 128 * p].astype(F32)
            vlo = jnp.where(lo_v, v, 0.0).astype(BF16)
            vhi = jnp.where(lo_v, 0.0, v).astype(BF16)
            qe_pair = []
            ke_pair = []
            for hh in range(2):
                ls = slice(HG_DK * (2 * p + hh), HG_DK * (2 * p + hh) + HG_DK)
                qe_pair.append(sbuf[nlev - 1][:, ls])
                if d == 0:
                    ke_pair.append(sbuf[nlev][:, ls])
            dg = [[diag_all[2 * (2 * p + hh) + hb] for hb in range(2)] for hh in range(2)]
            cross = [cross_all[2 * p + hh] for hh in range(2)]
            vs = jnp.concatenate([vlo, vhi], axis=0)
            if d == 0:
                o_q = _dot(jnp.concatenate([cross[0], dg[0][1], cross[1], dg[1][1]], axis=1), vs)
                o_k = _dot(jnp.concatenate([dg[0][0], dg[1][0]], axis=1),
                           jnp.concatenate([vlo[halves[0]], vhi[halves[0]]], axis=0))
                intra = jnp.concatenate([o_k, o_q], axis=0)
            else:
                o_q = _dot(jnp.concatenate([dg[0][0], cross[0], dg[1][0], cross[1]], axis=1), vs)
                o_k = _dot(jnp.concatenate([dg[0][1], dg[1][1]], axis=1),
                           jnp.concatenate([vlo[halves[1]], vhi[halves[1]]], axis=0))
                intra = jnp.concatenate([o_q, o_k], axis=0)
            state = st_ref[p] if d == 0 else sb_ref[0, 0, p]
            inter = _dot_nt(jnp.concatenate(qe_pair, axis=1), state.astype(BF16))
            osum[p] = osum[p] + intra + inter
            if d == 0:
                ps = slice(256 * p, 256 * p + 256)
                dec = fac_ref[nb - 1:nb, ps] * fac_ref[2 * nb - 1:2 * nb, ps]
                kvt = _dot_tn(v.astype(BF16), jnp.concatenate(ke_pair, axis=1))
                st_ref[p] = dec * st_ref[p] + jnp.where(bmt, kvt, 0.0)
    for p in range(2):
        o = osum[p]
        ms = _dot_split_rhs_const(o * o, bones) * (1.0 / HG_DV)
        g = h_ref[:, 1792 + 128 * p:1920 + 128 * p].astype(F32)
        y = o * lax.rsqrt(ms + RMS_EPS) * nw_ref[:, 128 * p:128 * p + 128] * jax.nn.sigmoid(g)
        o_ref[:, 128 * p:128 * p + 128] = y.astype(o_ref.dtype)


def _hgrn2(h, lb, nw, consts, B, S):
    L = HG_BLOCK
    nb = S // L
    mf, mb, kf, kb, mpre, upper = consts
    nsec = mf.shape[0]
    nfac = nsec - L
    sb = pl.pallas_call(
        _hg_bwd_state_kernel,
        grid=(B, nb // HG_PRE_BLOCKS),
        in_specs=[pl.BlockSpec((HG_PRE_BLOCKS * L, 512), lambda b, n: ((b * nb + nb) // HG_PRE_BLOCKS - 1 - n, 2)),
                  pl.BlockSpec((HG_PRE_BLOCKS * L, 256), lambda b, n: ((b * nb + nb) // HG_PRE_BLOCKS - 1 - n, 6)),
                  pl.BlockSpec((1, 512), lambda b, n: (0, 0)),
                  pl.BlockSpec((L + 8, L), lambda b, n: (0, 0))],
        out_specs=pl.BlockSpec((1, HG_PRE_BLOCKS, 2, LANES, 2 * HG_DK),
                               lambda b, n: (b, nb // HG_PRE_BLOCKS - 1 - n, 0, 0, 0)),
        out_shape=jax.ShapeDtypeStruct((B, nb, 2, LANES, 2 * HG_DK), F32),
        scratch_shapes=[pltpu.VMEM((2, LANES, 2 * HG_DK), F32)],
        compiler_params=_cparams(("parallel", "arbitrary")),
        name="hg_bwd_state",
    )(h, h, lb, mpre)
    return pl.pallas_call(
        functools.partial(_hg_main_kernel, upper=upper),
        grid=(B, nb),
        in_specs=[pl.BlockSpec((L, HG_W), lambda b, n: (b * nb + n, 0)),
                  pl.BlockSpec((1, 512), lambda b, n: (0, 0)),
                  pl.BlockSpec((1, 256), lambda b, n: (0, 0)),
                  pl.BlockSpec((nsec, L), lambda b, n: (0, 0)),
                  pl.BlockSpec((nsec, L), lambda b, n: (0, 0)),
                  pl.BlockSpec(kf.shape, lambda b, n: (0, 0, 0)),
                  pl.BlockSpec(kb.shape, lambda b, n: (0, 0, 0)),
                  pl.BlockSpec((1, 1, 2, LANES, 2 * HG_DK), lambda b, n: (b, n, 0, 0, 0))],
        out_specs=pl.BlockSpec((L, 256), lambda b, n: (b * nb + n, 0)),
        out_shape=jax.ShapeDtypeStruct((B * S, 256), BF16),
        scratch_shapes=[pltpu.VMEM((2, LANES, 2 * HG_DK), F32),
                        pltpu.VMEM((2, nsec, 512), F32),
                        pltpu.VMEM((2, L, 512), F32), pltpu.VMEM((2, L, 512), F32),
                        pltpu.VMEM((2, nfac, 512), F32),
                        pltpu.VMEM((2 * (kf.shape[0] + 2), L, 512), BF16)],
        compiler_params=_cparams(("parallel", "arbitrary")),
        name="hg_main",
    )(h, lb, nw, mf, mb, kf, kb, sb)


def _outproj_kernel(x_ref, r_ref, s_ref, g_ref, wr_ref, ws_ref, wg_ref, lg_ref, lb_ref, o_ref):
    mix = _dot(r_ref[...], wr_ref[...]) + _dot(s_ref[...], ws_ref[...]) + _dot(g_ref[...], wg_ref[...])
    o_ref[...] = _layer_norm(DEEPNORM_ALPHA * x_ref[...] + mix, lg_ref[...], lb_ref[...])


def _outproj_router_kernel(x_ref, r_ref, s_ref, g_ref, wr_ref, ws_ref, wg_ref, lg_ref, lb_ref, wrt_ref,
                           o_ref, ot_ref, route_ref, cnt_ref):
    @pl.when(pl.program_id(0) == 0)
    def _():
        cnt_ref[...] = jnp.zeros_like(cnt_ref)

    mix = _dot(r_ref[...], wr_ref[...]) + _dot(s_ref[...], ws_ref[...]) + _dot(g_ref[...], wg_ref[...])
    y = _layer_norm(DEEPNORM_ALPHA * x_ref[...] + mix, lg_ref[...], lb_ref[...])
    o_ref[...] = y
    ot_ref[...] = _to_token_tiles(y)
    w = wrt_ref[...]
    w_hi = w.astype(BF16)
    w_lo = (w - w_hi.astype(F32)).astype(BF16)
    y_hi = y.astype(BF16)
    y_lo = (y - y_hi.astype(F32)).astype(BF16)
    logits = _dot(y_hi, w_hi) + _dot(y_lo, w_hi) + _dot(y_hi, w_lo)
    tm = logits.shape[0]
    lane = lax.broadcasted_iota(jnp.int32, logits.shape, 1).astype(F32)
    neg = jnp.float32(-jnp.inf)
    l1 = jnp.where(lane < N_EXPERTS, logits, neg)
    m1 = jnp.max(l1, axis=-1, keepdims=True)
    i1 = jnp.min(jnp.where(l1 == m1, lane, float(LANES)), axis=-1, keepdims=True)
    l2 = jnp.where(lane == i1, neg, l1)
    m2 = jnp.max(l2, axis=-1, keepdims=True)
    i2 = jnp.min(jnp.where(l2 == m2, lane, float(LANES)), axis=-1, keepdims=True)
    e2 = jnp.exp(m2 - m1)
    den = 1.0 + e2
    picked = jnp.where((lane == i1) | (lane == i2), 1.0, 0.0)
    rr = lax.broadcasted_iota(jnp.int32, (tm, tm), 0)
    cc = lax.broadcasted_iota(jnp.int32, (tm, tm), 1)
    before = _dot(jnp.where(rr > cc, 1.0, 0.0).astype(BF16), picked.astype(BF16))
    rank = before + cnt_ref[0:1, :]
    r1 = jnp.sum(jnp.where(lane == i1, rank, 0.0), axis=-1, keepdims=True)
    r2 = jnp.sum(jnp.where(lane == i2, rank, 0.0), axis=-1, keepdims=True)
    cnt_ref[...] = cnt_ref[...] + jnp.sum(picked, axis=0, keepdims=True)
    route = jnp.where(lane == 0.0, i1, 0.0) + jnp.where(lane == 1.0, i2, 0.0)
    route = route + jnp.where(lane == 2.0, 1.0 / den, 0.0) + jnp.where(lane == 3.0, e2 / den, 0.0)
    route_ref[...] = route + jnp.where(lane == 4.0, r1, 0.0) + jnp.where(lane == 5.0, r2, 0.0)


def _out_proj(x2d, ret, swa, hgo, w_r, w_s, w_g, ln_g, ln_b, w_router=None):
    T = x2d.shape[0]
    tm = 1024
    row = lambda i: (i, 0)
    full = lambda i: (0, 0)
    in_specs = [pl.BlockSpec((tm, D_MODEL), row),
                pl.BlockSpec((tm, 256), row), pl.BlockSpec((tm, 512), row), pl.BlockSpec((tm, 256), row),
                pl.BlockSpec((256, D_MODEL), full), pl.BlockSpec((512, D_MODEL), full),
                pl.BlockSpec((256, D_MODEL), full),
                pl.BlockSpec((1, D_MODEL), full), pl.BlockSpec((1, D_MODEL), full)]
    args = [x2d, ret, swa, hgo, w_r, w_s, w_g, ln_g, ln_b]
    if w_router is None:
        return pl.pallas_call(
            _outproj_kernel, grid=(T // tm,), in_specs=in_specs,
            out_specs=pl.BlockSpec((tm, D_MODEL), row),
            out_shape=jax.ShapeDtypeStruct((T, D_MODEL), F32),
            compiler_params=_cparams(("parallel",)), name="out_proj",
        )(*args)
    return pl.pallas_call(
        _outproj_router_kernel, grid=(T // tm,),
        in_specs=in_specs + [pl.BlockSpec((D_MODEL, LANES), full)],
        out_specs=[pl.BlockSpec((tm, D_MODEL), row), pl.BlockSpec((tm, 8, LANES), lambda i: (i, 0, 0)),
                   pl.BlockSpec((tm, LANES), row), pl.BlockSpec((8, LANES), full)],
        out_shape=[jax.ShapeDtypeStruct((T, D_MODEL), F32), jax.ShapeDtypeStruct((T, 8, LANES), F32),
                   jax.ShapeDtypeStruct((T, LANES), F32), jax.ShapeDtypeStruct((8, LANES), F32)],
        compiler_params=_cparams(("arbitrary",)), name="out_proj_router",
    )(*args, w_router)


FFN_CHUNKS = 2


def _ffn_kernel(x_ref, wg_ref, wu_ref, wd_ref, lg_ref, lb_ref, o_ref):
    x = x_ref[...]
    xb = x.astype(BF16)
    th = FFN_DIM // FFN_CHUNKS
    f = jnp.zeros(x.shape, F32)
    for c in range(FFN_CHUNKS):
        g = _dot(xb, wg_ref[:, c * th:(c + 1) * th])
        u = _dot(xb, wu_ref[:, c * th:(c + 1) * th])
        a = (g * jax.nn.sigmoid(g)) * u
        f = f + _dot(a.astype(BF16), wd_ref[c * th:(c + 1) * th, :])
    o_ref[...] = _layer_norm(DEEPNORM_ALPHA * x + f, lg_ref[...], lb_ref[...])


def _ffn(x2d, wg, wu, wd, ln_g, ln_b):
    T = x2d.shape[0]
    tm = 512
    once = pl.Buffered(1)
    return pl.pallas_call(
        _ffn_kernel, grid=(T // tm,),
        in_specs=[pl.BlockSpec((tm, D_MODEL), lambda i: (i, 0)),
                  pl.BlockSpec((D_MODEL, FFN_DIM), lambda i: (0, 0), pipeline_mode=once),
                  pl.BlockSpec((D_MODEL, FFN_DIM), lambda i: (0, 0), pipeline_mode=once),
                  pl.BlockSpec((FFN_DIM, D_MODEL), lambda i: (0, 0), pipeline_mode=once),
                  pl.BlockSpec((1, D_MODEL), lambda i: (0, 0)),
                  pl.BlockSpec((1, D_MODEL), lambda i: (0, 0))],
        out_specs=pl.BlockSpec((tm, D_MODEL), lambda i: (i, 0)),
        out_shape=jax.ShapeDtypeStruct((T, D_MODEL), F32),
        compiler_params=_cparams(("parallel",)), name="ffn",
    )(x2d, wg, wu, wd, ln_g, ln_b)


MOE_TILE = 512
MOE_TH = 1792
MOE_TOK = 512


def _moe_num_tiles(T):
    return (2 * T) // MOE_TILE + N_EXPERTS + 1


def _moe_tables(route, counts):
    T = route.shape[0]
    tm = MOE_TILE
    cnt = counts[0, :N_EXPERTS].astype(jnp.int32)
    pc = ((cnt + tm - 1) // tm) * tm
    pend = jnp.cumsum(pc)
    pstart = pend - pc
    e = route[:, 0:2].astype(jnp.int32)
    rank = route[:, 4:6].astype(jnp.int32)
    sel = e[:, :, None] == jnp.arange(N_EXPERTS, dtype=jnp.int32)[None, None, :]
    pos = jnp.sum(jnp.where(sel, pstart[None, None, :], 0), axis=-1) + rank
    n_tiles = _moe_num_tiles(T)
    tile_row = jnp.arange(n_tiles, dtype=jnp.int32) * tm
    texp = jnp.minimum(jnp.sum((tile_row[:, None] >= pend[None, :]).astype(jnp.int32), axis=1), N_EXPERTS - 1)
    n_used = pend[-1] // tm
    last = jnp.sum(jnp.where(jnp.arange(n_tiles) == n_used - 1, texp, 0))
    texp = jnp.where(jnp.arange(n_tiles) < n_used, texp, last)
    return pos.reshape(-1), texp, n_used.reshape(1), pstart + cnt


def _to_token_tiles(x):
    cols = jnp.stack([x[:, LANES * c:LANES * c + LANES] for c in range(D_MODEL // LANES)], axis=0)
    return pltpu.einshape("csl->scl", cols)


def _from_token_tiles(xt):
    y = pltpu.einshape("scl->csl", xt)
    return jnp.concatenate([y[c] for c in range(D_MODEL // LANES)], axis=1)


MOE_DMA_UNROLL = 8


def _moe_dispatch_kernel(pos_ref, pad_ref, nused_ref, xt_ref, xs_hbm, zero_ref, sem, zsem):
    i = pl.program_id(0)
    tm = xt_ref.shape[0]
    n_tiles = xs_hbm.shape[0] // MOE_TILE

    def zero_copy(row):
        return pltpu.make_async_copy(zero_ref, xs_hbm.at[pl.ds(row, MOE_TILE)], zsem)

    @pl.when(i == 0)
    def _():
        zero_ref[...] = jnp.zeros_like(zero_ref)
        for e in range(N_EXPERTS):
            zero_copy(pad_ref[e]).start()
        for e in range(N_EXPERTS):
            zero_copy(pad_ref[e]).wait()
        for t in range(n_tiles - N_EXPERTS - 1, n_tiles):
            @pl.when(t >= nused_ref[0])
            def _():
                zero_copy(t * MOE_TILE).start()
                zero_copy(t * MOE_TILE).wait()

    def body(kk, c):
        k0 = kk * MOE_DMA_UNROLL
        t0 = i * tm + k0
        dst = [pos_ref[2 * t0 + u] for u in range(2 * MOE_DMA_UNROLL)]
        for u in range(2 * MOE_DMA_UNROLL):
            pltpu.make_async_copy(xt_ref.at[k0 + u // 2], xs_hbm.at[dst[u]], sem).start()
        return c

    lax.fori_loop(0, tm // MOE_DMA_UNROLL, body, 0)
    for s in range(2):
        pltpu.make_async_copy(xt_ref, xs_hbm.at[pl.ds(0, tm)], sem).wait()


def _moe_dispatch(x1t, pos, padstart, n_used):
    T = x1t.shape[0]
    P = _moe_num_tiles(T) * MOE_TILE
    return pl.pallas_call(
        _moe_dispatch_kernel,
        grid_spec=pltpu.PrefetchScalarGridSpec(
            num_scalar_prefetch=3, grid=(T // MOE_TOK,),
            in_specs=[pl.BlockSpec((MOE_TOK, 8, LANES), lambda i, pos, pad, nused: (i, 0, 0))],
            out_specs=pl.BlockSpec(memory_space=pl.ANY),
            scratch_shapes=[pltpu.VMEM((MOE_TILE, 8, LANES), F32), pltpu.SemaphoreType.DMA,
                            pltpu.SemaphoreType.DMA]),
        out_shape=jax.ShapeDtypeStruct((P, 8, LANES), F32),
        compiler_params=_cparams(("arbitrary",)), name="moe_dispatch",
    )(pos, padstart, n_used, x1t)


def _moe_group_kernel(texp_ref, nused_ref, xs_ref, wg_ref, wu_ref, wd_ref, y_ref, xb_ref, acc_ref):
    i = pl.program_id(0)
    j = pl.program_id(1)

    @pl.when(i < nused_ref[0])
    def _():
        @pl.when(j == 0)
        def _():
            acc_ref[...] = jnp.zeros_like(acc_ref)

        xb = _from_token_tiles(xs_ref[...]).astype(BF16)
        g = _dot(xb, wg_ref[...])
        u = _dot(xb, wu_ref[...])
        a = (g * jax.nn.sigmoid(g)) * u
        acc_ref[...] += _dot(a.astype(BF16), wd_ref[...])

        @pl.when(j == pl.num_programs(1) - 1)
        def _():
            y_ref[...] = _to_token_tiles(acc_ref[...])

    @pl.when((i >= nused_ref[0]) & (j == 0))
    def _():
        y_ref[...] = jnp.zeros_like(y_ref)


def _moe_group(xs, texp, n_used, wg, wu, wd):
    P = xs.shape[0]
    tm, th = MOE_TILE, MOE_TH
    nh = EXPERT_DIM // th

    def row_map(i, j, texp, nused):
        return (jnp.minimum(i, nused[0] - 1), 0, 0)

    def col(i, j, nused):
        return jnp.where(i < nused[0], j, nh - 1)

    return pl.pallas_call(
        _moe_group_kernel,
        grid_spec=pltpu.PrefetchScalarGridSpec(
            num_scalar_prefetch=2, grid=(P // tm, nh),
            in_specs=[pl.BlockSpec((tm, 8, LANES), row_map),
                      pl.BlockSpec((None, D_MODEL, th), lambda i, j, texp, nused: (texp[i], 0, col(i, j, nused))),
                      pl.BlockSpec((None, D_MODEL, th), lambda i, j, texp, nused: (texp[i], 0, col(i, j, nused))),
                      pl.BlockSpec((None, th, D_MODEL), lambda i, j, texp, nused: (texp[i], col(i, j, nused), 0))],
            out_specs=pl.BlockSpec((tm, 8, LANES), lambda i, j, texp, nused: (i, 0, 0)),
            scratch_shapes=[pltpu.VMEM((tm, D_MODEL), BF16), pltpu.VMEM((tm, D_MODEL), F32)]),
        out_shape=jax.ShapeDtypeStruct((P, 8, LANES), F32),
        compiler_params=_cparams(("arbitrary", "arbitrary")), name="moe_group",
    )(texp, n_used, xs, wg, wu, wd)


def _moe_combine_kernel(pos_ref, x_ref, route_ref, y_hbm, lg_ref, lb_ref, o_ref, buf_ref, sem):
    i = pl.program_id(0)
    tm = x_ref.shape[0]

    def issue(tile, slot):
        def body(kk, c):
            k0 = kk * MOE_DMA_UNROLL
            t0 = tile * tm + k0
            src = [pos_ref[2 * t0 + u] for u in range(2 * MOE_DMA_UNROLL)]
            for u in range(2 * MOE_DMA_UNROLL):
                pltpu.make_async_copy(y_hbm.at[src[u]], buf_ref.at[slot, (u % 2) * tm + k0 + u // 2],
                                      sem.at[slot]).start()
            return c
        lax.fori_loop(0, tm // MOE_DMA_UNROLL, body, 0)

    @pl.when(i == 0)
    def _():
        issue(0, 0)

    slot = i % 2
    pltpu.make_async_copy(y_hbm.at[pl.ds(0, 2 * tm)], buf_ref.at[slot], sem.at[slot]).wait()

    @pl.when(i + 1 < pl.num_programs(0))
    def _():
        issue(i + 1, 1 - slot)

    a = _from_token_tiles(buf_ref[slot, pl.ds(0, tm)])
    b = _from_token_tiles(buf_ref[slot, pl.ds(tm, tm)])
    f = route_ref[:, 2:3] * a + route_ref[:, 3:4] * b
    o_ref[...] = _layer_norm(DEEPNORM_ALPHA * x_ref[...] + f, lg_ref[...], lb_ref[...])


def _moe_combine(x1, route, y, pos, ln_g, ln_b):
    T = x1.shape[0]
    tm = MOE_TOK
    return pl.pallas_call(
        _moe_combine_kernel,
        grid_spec=pltpu.PrefetchScalarGridSpec(
            num_scalar_prefetch=1, grid=(T // tm,),
            in_specs=[pl.BlockSpec((tm, D_MODEL), lambda i, pos: (i, 0)),
                      pl.BlockSpec((tm, LANES), lambda i, pos: (i, 0)),
                      pl.BlockSpec(memory_space=pl.ANY),
                      pl.BlockSpec((1, D_MODEL), lambda i, pos: (0, 0)),
                      pl.BlockSpec((1, D_MODEL), lambda i, pos: (0, 0))],
            out_specs=pl.BlockSpec((tm, D_MODEL), lambda i, pos: (i, 0)),
            scratch_shapes=[pltpu.VMEM((2, 2 * tm, 8, LANES), F32), pltpu.SemaphoreType.DMA((2,))]),
        out_shape=jax.ShapeDtypeStruct((T, D_MODEL), F32),
        compiler_params=_cparams(("arbitrary",)), name="moe_combine",
    )(pos, x1, route, y, ln_g, ln_b)


def _moe(x1, x1t, route, counts, wg, wu, wd, ln_g, ln_b):
    pos, texp, n_used, padstart = _moe_tables(route, counts)
    xs = _moe_dispatch(x1t, pos, padstart, n_used)
    y = _moe_group(xs, texp, n_used, wg, wu, wd)
    return _moe_combine(x1, route, y, pos, ln_g, ln_b)


def _t5_bucket(rel):
    nb = REL_BUCKETS // 2
    ret = (rel > 0).astype(np.int32) * nb
    n = np.abs(rel)
    max_exact = nb // 2
    large = max_exact + (np.log(np.maximum(n, 1) / max_exact) / np.log(REL_MAX_DIST / max_exact)
                         * (nb - max_exact)).astype(np.int32)
    large = np.minimum(large, nb - 1)
    return ret + np.where(n < max_exact, n, large)


def _swa_head_order():
    return [h for c in range(4) for h in (c, 4 + c)]


def _prepare(w_in, w_out, ret_decay, swa_sink, rel_bias, hg_lb, hg_norm_w, moe_router, seq_lens):
    order = _swa_head_order()
    w_in_p = jnp.concatenate(
        [w_in[:, :, 1792:3840], w_in[:, :, 0:1024]]
        + [w_in[:, :, 1024 + 64 * h:1088 + 64 * h] for h in order]
        + [w_in[:, :, 1536:1792]], axis=2).astype(BF16)
    w_r = w_out[:, 0:256, :].astype(BF16)
    w_s = jnp.concatenate([w_out[:, 256 + 64 * h:320 + 64 * h, :] for h in order], axis=1).astype(BF16)
    w_g = w_out[:, 768:1024, :].astype(BF16)

    rd = ret_decay.astype(F32)
    rdp = jnp.repeat(rd.reshape(DEPTH, 4, 2), 64, axis=-1)
    rdp = jnp.concatenate([rdp, rdp], axis=1)
    rdh = jnp.broadcast_to(rd.reshape(DEPTH, 8, 1), (DEPTH, 8, LANES))

    W = SWA_BLOCK
    rel = np.arange(3 * W)[None, :] - W - np.arange(W)[:, None]
    onehot = jnp.asarray(_t5_bucket(rel)[..., None] == np.arange(REL_BUCKETS), F32)
    bias = jnp.einsum("qkb,bh->hqk", onehot, rel_bias.astype(F32), precision=lax.Precision.HIGHEST)
    band = jnp.asarray(np.abs(rel) <= WINDOW)
    bias_kq = jnp.transpose(jnp.where(band[None], bias, MASK_VALUE), (0, 2, 1))
    order = _swa_head_order()
    bias_t = jnp.concatenate([bias_kq[h] for h in order], axis=1)
    sink = swa_sink.astype(F32)
    sink_t = jnp.concatenate([jnp.broadcast_to(sink[:, h, None, None], (DEPTH, 8, W)) for h in order], axis=2)

    p = jax.nn.softmax(hg_lb.astype(F32), axis=0)
    lower = (jnp.cumsum(p, axis=0) - p[0]).reshape(DEPTH, 1, HG_HEADS * HG_DK)
    nw = hg_norm_w.astype(F32).reshape(DEPTH, 1, HG_HEADS * HG_DV)

    mf, mb, kf, kb, upper = _hg_constants(HG_BLOCK)
    L = HG_BLOCK
    mpre = np.concatenate([np.tril(np.ones((L, L), np.float32), -1), np.ones((8, L), np.float32)], axis=0)
    consts = (jnp.asarray(mf, BF16), jnp.asarray(mb, BF16), jnp.asarray(kf, F32), jnp.asarray(kb, F32),
              jnp.asarray(mpre, BF16), upper)

    half = RET_DK // 2
    inv = ROPE_BASE ** (-np.arange(half, dtype=np.float32) / half)
    rope = {}
    for S in set(seq_lens):
        ang = jnp.arange(S, dtype=F32)[:, None] * jnp.asarray(inv)[None, :]
        c32, s32 = jnp.cos(ang), jnp.sin(ang)
        rope[S] = (jnp.tile(c32, (1, 4)), jnp.tile(jnp.concatenate([-s32, s32], axis=1), (1, 2)))

    router = jnp.pad(moe_router.astype(F32), ((0, 0), (0, 0), (0, LANES - N_EXPERTS)))
    return dict(w_in=w_in_p, w_r=w_r, w_s=w_s, w_g=w_g, rdp=rdp, rdh=rdh, bias=bias_t, sink=sink_t,
                lower=lower, nw=nw, consts=consts, rope=rope, router=router)


def _trunk(x, prm, ln1_g, ln1_b, ln2_g, ln2_b, ffn_w, moe_w):
    B, S, _ = x.shape
    x2d = x.reshape(B * S, D_MODEL)
    cos_t, sin_t = prm["rope"][S]
    for l in range(DEPTH):
        h = _in_proj(x2d, prm["w_in"][l])
        ret = _retention(h, cos_t, sin_t, prm["rdp"][l], prm["rdh"][l], B, S)
        swa = _swa(h, prm["bias"], prm["sink"][l], B, S)
        hgo = _hgrn2(h, prm["lower"][l], prm["nw"][l], prm["consts"], B, S)
        g1, b1 = ln1_g[l].reshape(1, D_MODEL), ln1_b[l].reshape(1, D_MODEL)
        g2, b2 = ln2_g[l].reshape(1, D_MODEL), ln2_b[l].reshape(1, D_MODEL)
        i = l // 2
        if l % 2 == 0:
            x1 = _out_proj(x2d, ret, swa, hgo, prm["w_r"][l], prm["w_s"][l], prm["w_g"][l], g1, b1)
            wg, wu, wd = ffn_w
            x2d = _ffn(x1, wg[i], wu[i], wd[i], g2, b2)
        else:
            x1, x1t, route, counts = _out_proj(x2d, ret, swa, hgo, prm["w_r"][l], prm["w_s"][l], prm["w_g"][l],
                                               g1, b1, prm["router"][i])
            wg, wu, wd = moe_w
            x2d = _moe(x1, x1t, route, counts, wg[i], wu[i], wd[i], g2, b2)
    return x2d.reshape(B, S, D_MODEL)


def kernel(x_prompt, x_sample, w_in, w_out, ret_decay, swa_sink, rel_bias, hg_lb, hg_norm_w, ln1_g, ln1_b,
           ln2_g, ln2_b, ffn_w_gate, ffn_w_up, ffn_w_down, moe_router, moe_w_gate, moe_w_up, moe_w_down):
    prm = _prepare(w_in, w_out, ret_decay, swa_sink, rel_bias, hg_lb, hg_norm_w, moe_router,
                   (x_prompt.shape[1], x_sample.shape[1]))
    ffn_w = (ffn_w_gate.astype(BF16), ffn_w_up.astype(BF16), ffn_w_down.astype(BF16))
    moe_w = (moe_w_gate.astype(BF16), moe_w_up.astype(BF16), moe_w_down.astype(BF16))
    outs = []
    for x in (x_prompt, x_sample):
        outs.append(_trunk(x, prm, ln1_g, ln1_b, ln2_g, ln2_b, ffn_w, moe_w))
    return tuple(outs)
```

```python
import functools
import math

import numpy as np
import jax
import jax.numpy as jnp
from jax import lax
from jax.experimental import pallas as pl
from jax.experimental.pallas import tpu as pltpu

F32 = jnp.float32
BF16 = jnp.bfloat16

D_MODEL = 1024
DEPTH = 2
RET_HEADS = 4
RET_DK = 64
RET_CHUNK = 128
ROPE_BASE = 10000.0
SWA_Q_HEADS = 8
SWA_KV_HEADS = 2
SWA_DH = 64
WINDOW = 128
SWA_BLOCK = 128
REL_BUCKETS = 32
REL_MAX_DIST = 128
MASK_VALUE = -1e30
HG_HEADS = 4
HG_DK = 128
HG_DV = 64
FFN_DIM = 2816
N_EXPERTS = 8
EXPERT_DIM = 3584
LN_EPS = 1e-5
RMS_EPS = 1e-6
DEEPNORM_ALPHA = (2.0 * DEPTH) ** 0.25

LANES = 128
VMEM_LIMIT = 56 * 1024 * 1024

HG_W = 2048
RET_W = 1024
SWA_W = 768
IN_COLS = HG_W + RET_W + SWA_W

HG_BLOCK = 256
HG_BASE = 8
HG_PRE_BLOCKS = 4
RET_TILE = 1024
SWA_TILE = 1024


def _cparams(sem):
    return pltpu.CompilerParams(dimension_semantics=sem, vmem_limit_bytes=VMEM_LIMIT)


def _dot(a, b):
    return jnp.dot(a, b, preferred_element_type=F32)


def _dot_nt(a, b):
    return lax.dot_general(a, b, (((1,), (1,)), ((), ())), preferred_element_type=F32)


def _dot_tn(a, b):
    return lax.dot_general(a, b, (((0,), (0,)), ((), ())), preferred_element_type=F32)


def _dot_split(a_bf16, x):
    hi = x.astype(BF16)
    lo = (x - hi.astype(F32)).astype(BF16)
    return _dot(a_bf16, hi) + _dot(a_bf16, lo)


def _dot_split_rhs_const(x, b_bf16):
    hi = x.astype(BF16)
    lo = (x - hi.astype(F32)).astype(BF16)
    return _dot(hi, b_bf16) + _dot(lo, b_bf16)


def _layer_norm(y, g, b):
    mu = jnp.mean(y, axis=-1, keepdims=True)
    yc = y - mu
    var = jnp.mean(yc * yc, axis=-1, keepdims=True)
    return yc * lax.rsqrt(var + LN_EPS) * g + b


def _lane_lo_mask(shape):
    return lax.broadcasted_iota(jnp.int32, shape, len(shape) - 1) < (LANES // 2)


def _head_block_ones():
    r = lax.broadcasted_iota(jnp.int32, (LANES, LANES), 0) // 64
    c = lax.broadcasted_iota(jnp.int32, (LANES, LANES), 1) // 64
    return r == c


def _inproj_kernel(x_ref, w_ref, o_ref):
    o_ref[...] = _dot(x_ref[...].astype(BF16), w_ref[...]).astype(o_ref.dtype)


def _in_proj(x2d, w):
    T = x2d.shape[0]
    tm, tn = 2048, 1280
    return pl.pallas_call(
        _inproj_kernel,
        grid=(T // tm, IN_COLS // tn),
        in_specs=[pl.BlockSpec((tm, D_MODEL), lambda i, j: (i, 0)),
                  pl.BlockSpec((D_MODEL, tn), lambda i, j: (0, j))],
        out_specs=pl.BlockSpec((tm, tn), lambda i, j: (i, j)),
        out_shape=jax.ShapeDtypeStruct((T, IN_COLS), BF16),
        compiler_params=_cparams(("parallel", "arbitrary")),
        name="in_proj",
    )(x2d, w)


def _rotary(x, cos, sin_signed, even):
    sw = jnp.where(even, pltpu.roll(x, 96, 1), pltpu.roll(x, 32, 1))
    return x * cos + sw * sin_signed


def _ret_tables(rdp_ref):
    lg = jnp.log1p(-jnp.exp(rdp_ref[...]))
    return lg


def _ret_bwd_state_kernel(h_ref, cos_ref, sin_ref, rdp_ref, sb_ref, st_ref):
    C = RET_CHUNK
    nc = h_ref.shape[0] // C

    @pl.when(pl.program_id(1) == 0)
    def _():
        st_ref[...] = jnp.zeros_like(st_ref)

    lg = _ret_tables(rdp_ref)
    rows = lax.broadcasted_iota(jnp.int32, (C, LANES), 0).astype(F32)
    even = (lax.broadcasted_iota(jnp.int32, (C, LANES), 1) // 32) % 2 == 0
    bm = _head_block_ones()
    for c in range(nc - 1, -1, -1):
        sl = pl.ds(c * C, C)
        cos = cos_ref[sl, :]
        sin = sin_ref[sl, :]
        for p in range(2):
            lgb = lg[2 + p:3 + p, :]
            kwb = jnp.exp(lgb * rows)
            decb = jnp.where(bm, jnp.exp(lgb * float(C)), 0.0)
            k = _rotary(h_ref[sl, 256 + 128 * p:384 + 128 * p].astype(F32), cos, sin, even) * (RET_DK ** -0.5)
            v = h_ref[sl, 512 + 128 * p:640 + 128 * p].astype(F32)
            sb_ref[0, c, p] = st_ref[p]
            kv = _dot_tn((k * kwb).astype(BF16), v.astype(BF16))
            st_ref[p] = decb * st_ref[p] + jnp.where(bm, kv, 0.0)


def _ret_main_kernel(h_ref, cos_ref, sin_ref, rdp_ref, rdh_ref, sb_ref, o_ref, st_ref):
    C = RET_CHUNK
    nc = h_ref.shape[0] // C

    @pl.when(pl.program_id(1) == 0)
    def _():
        st_ref[...] = jnp.zeros_like(st_ref)

    lg = _ret_tables(rdp_ref)
    lgh = jnp.log1p(-jnp.exp(rdh_ref[...]))
    rows = lax.broadcasted_iota(jnp.int32, (C, LANES), 0).astype(F32)
    cols = lax.broadcasted_iota(jnp.int32, (C, LANES), 1).astype(F32)
    rel = rows - cols
    even = (lax.broadcasted_iota(jnp.int32, (C, LANES), 1) // 32) % 2 == 0
    lo = _lane_lo_mask((C, LANES))
    bm = _head_block_ones()
    bones = jnp.where(bm, 1.0, 0.0).astype(BF16)

    dmat = []
    for hd in range(RET_HEADS):
        f = jnp.where(rel >= 0, jnp.exp(lgh[hd:hd + 1, :] * jnp.maximum(rel, 0.0)), 0.0)
        b = jnp.where(rel <= 0, jnp.exp(lgh[4 + hd:5 + hd, :] * jnp.maximum(-rel, 0.0)), 0.0)
        dmat.append(f + b)

    for c in range(nc):
        sl = pl.ds(c * C, C)
        cos = cos_ref[sl, :]
        sin = sin_ref[sl, :]
        for p in range(2):
            lgf = lg[p:p + 1, :]
            lgb = lg[2 + p:3 + p, :]
            q = _rotary(h_ref[sl, 128 * p:128 * p + 128].astype(F32), cos, sin, even)
            k = _rotary(h_ref[sl, 256 + 128 * p:384 + 128 * p].astype(F32), cos, sin, even) * (RET_DK ** -0.5)
            v = h_ref[sl, 512 + 128 * p:640 + 128 * p].astype(F32)
            g = h_ref[sl, 768 + 128 * p:896 + 128 * p].astype(F32)
            kb = k.astype(BF16)
            vb = v.astype(BF16)
            qs = jnp.concatenate([jnp.where(lo, q, 0.0), jnp.where(lo, 0.0, q)], axis=0).astype(BF16)
            s = _dot_nt(qs, kb)
            pm = jnp.concatenate([s[:C] * dmat[2 * p], s[C:] * dmat[2 * p + 1]], axis=1).astype(BF16)
            vs = jnp.concatenate([jnp.where(lo, v, 0.0), jnp.where(lo, 0.0, v)], axis=0).astype(BF16)
            intra = _dot(pm, vs)
            qwf = jnp.exp(lgf * (rows + 1.0))
            qwb = jnp.exp(lgb * (float(C) - rows))
            qi = jnp.concatenate([q * qwf, q * qwb], axis=1).astype(BF16)
            sst = jnp.concatenate([st_ref[p], sb_ref[0, c, p]], axis=0).astype(BF16)
            o = intra + _dot(qi, sst)
            kwf = jnp.exp(lgf * (float(C) - 1.0 - rows))
            decf = jnp.where(bm, jnp.exp(lgf * float(C)), 0.0)
            kv = _dot_tn((k * kwf).astype(BF16), vb)
            st_ref[p] = decf * st_ref[p] + jnp.where(bm, kv, 0.0)
            ms = _dot_split_rhs_const(o * o, bones) * (1.0 / RET_DK)
            y = o * lax.rsqrt(ms + RMS_EPS) * (g * jax.nn.sigmoid(g))
            o_ref[sl, 128 * p:128 * p + 128] = y.astype(o_ref.dtype)


def _retention(h, cos_t, sin_t, rdp, rdh, B, S):
    L = RET_TILE
    nb = S // L
    ncb = L // RET_CHUNK
    n_chunks = S // RET_CHUNK
    colblk = HG_W // RET_W
    sb = pl.pallas_call(
        _ret_bwd_state_kernel,
        grid=(B, nb),
        in_specs=[pl.BlockSpec((L, RET_W), lambda b, n: (b * nb + nb - 1 - n, colblk)),
                  pl.BlockSpec((L, LANES), lambda b, n: (nb - 1 - n, 0)),
                  pl.BlockSpec((L, LANES), lambda b, n: (nb - 1 - n, 0)),
                  pl.BlockSpec((8, LANES), lambda b, n: (0, 0))],
        out_specs=pl.BlockSpec((1, ncb, 2, LANES, LANES), lambda b, n: (b, nb - 1 - n, 0, 0, 0)),
        out_shape=jax.ShapeDtypeStruct((B, n_chunks, 2, LANES, LANES), F32),
        scratch_shapes=[pltpu.VMEM((2, LANES, LANES), F32)],
        compiler_params=_cparams(("parallel", "arbitrary")),
        name="ret_bwd_state",
    )(h, cos_t, sin_t, rdp)
    return pl.pallas_call(
        _ret_main_kernel,
        grid=(B, nb),
        in_specs=[pl.BlockSpec((L, RET_W), lambda b, n: (b * nb + n, colblk)),
                  pl.BlockSpec((L, LANES), lambda b, n: (n, 0)),
                  pl.BlockSpec((L, LANES), lambda b, n: (n, 0)),
                  pl.BlockSpec((8, LANES), lambda b, n: (0, 0)),
                  pl.BlockSpec((8, LANES), lambda b, n: (0, 0)),
                  pl.BlockSpec((1, ncb, 2, LANES, LANES), lambda b, n: (b, n, 0, 0, 0))],
        out_specs=pl.BlockSpec((L, 256), lambda b, n: (b * nb + n, 0)),
        out_shape=jax.ShapeDtypeStruct((B * S, 256), BF16),
        scratch_shapes=[pltpu.VMEM((2, LANES, LANES), F32)],
        compiler_params=_cparams(("parallel", "arbitrary")),
        name="ret_main",
    )(h, cos_t, sin_t, rdp, rdh, sb)


def _swa_kernel(q_ref, kp_ref, kc_ref, kn_ref, vp_ref, vc_ref, vn_ref, bias_ref, sink_ref, o_ref):
    W = SWA_BLOCK
    nq = q_ref.shape[0] // W
    n = pl.program_id(1)
    last = pl.num_programs(1) - 1
    kall = jnp.concatenate([kp_ref[...], kc_ref[...], kn_ref[...]], axis=0).astype(BF16)
    vt = jnp.concatenate([vp_ref[...], vc_ref[...], vn_ref[...]], axis=0).astype(F32).T.astype(BF16)
    lo = _lane_lo_mask((W, LANES))
    ki = lax.broadcasted_iota(jnp.int32, (3 * W, SWA_Q_HEADS * W), 0)
    edge_first = jnp.where(ki >= W, 0.0, jnp.where(n > 0, 0.0, MASK_VALUE))
    edge_last = jnp.where(ki < 2 * W, 0.0, jnp.where(n < last, 0.0, MASK_VALUE))
    sink = sink_ref[0:1, :]
    for r in range(nq):
        qs = []
        for c in range(4):
            q = q_ref[pl.ds(r * W, W), 128 * c:128 * c + 128].astype(F32) * (SWA_DH ** -0.5)
            qs += [jnp.where(lo, q, 0.0), jnp.where(lo, 0.0, q)]
        qs = jnp.concatenate(qs, axis=0).astype(BF16)
        s = _dot_nt(kall[r * W:(r + 3) * W], qs) + bias_ref[...]
        if r == 0:
            s = s + edge_first
        if r == nq - 1:
            s = s + edge_last
        m = jnp.maximum(jnp.max(s, axis=0, keepdims=True), sink)
        e = jnp.exp(s - m)
        den = jnp.sum(e, axis=0, keepdims=True) + jnp.exp(sink - m)
        ot = _dot(vt[:, r * W:(r + 3) * W], e.astype(BF16)) / den
        for c in range(4):
            sel = jnp.concatenate([ot[0:SWA_DH, 2 * c * W:(2 * c + 1) * W],
                                   ot[SWA_DH:2 * SWA_DH, (2 * c + 1) * W:(2 * c + 2) * W]], axis=0)
            o_ref[pl.ds(r * W, W), 128 * c:128 * c + 128] = sel.T.astype(o_ref.dtype)


def _swa(h, bias_t, sink_t, B, S):
    L = SWA_TILE
    W = SWA_BLOCK
    nb = S // L
    r = L // W
    nblk = S // W
    qcol = (HG_W + RET_W) // 512
    kcol = (HG_W + RET_W + 512) // LANES
    vcol = kcol + 1

    def prev_map(col):
        return lambda b, n: (b * nblk + jnp.maximum(n * r - 1, 0), col)

    def cur_map(col):
        return lambda b, n: (b * nb + n, col)

    def next_map(col):
        return lambda b, n: (b * nblk + jnp.minimum(n * r + r, nblk - 1), col)

    return pl.pallas_call(
        _swa_kernel,
        grid=(B, nb),
        in_specs=[pl.BlockSpec((L, 512), cur_map(qcol)),
                  pl.BlockSpec((W, LANES), prev_map(kcol)),
                  pl.BlockSpec((L, LANES), cur_map(kcol)),
                  pl.BlockSpec((W, LANES), next_map(kcol)),
                  pl.BlockSpec((W, LANES), prev_map(vcol)),
                  pl.BlockSpec((L, LANES), cur_map(vcol)),
                  pl.BlockSpec((W, LANES), next_map(vcol)),
                  pl.BlockSpec((3 * W, SWA_Q_HEADS * W), lambda b, n: (0, 0)),
                  pl.BlockSpec((8, SWA_Q_HEADS * W), lambda b, n: (0, 0))],
        out_specs=pl.BlockSpec((L, 512), lambda b, n: (b * nb + n, 0)),
        out_shape=jax.ShapeDtypeStruct((B * S, 512), BF16),
        compiler_params=_cparams(("parallel", "parallel")),
        name="swa",
    )(h, h, h, h, h, h, h, bias_t, sink_t)


def _hg_levels(L):
    g = L
    out = []
    while g >= 2 * HG_BASE:
        out.append(g)
        g //= 2
    return out


def _hg_constants(L):
    nb = L // HG_BASE
    idx = np.arange(L)
    t = idx[None, :]
    i = idx[:, None]
    r = HG_BASE * (i // HG_BASE) + HG_BASE // 2 - 1
    secs = [((t > r) & (t <= i)).astype(np.float32) - ((t > i) & (t <= r)).astype(np.float32)]
    u = np.arange(nb)[:, None]
    ru = HG_BASE * u + HG_BASE // 2 - 1
    secs.append((t <= ru).astype(np.float32))
    secs.append((t > ru).astype(np.float32))
    masks = [((idx[:, None] // HG_BASE) == (idx[None, :] // HG_BASE)) & (idx[:, None] >= idx[None, :])]
    upper = []
    for g in _hg_levels(L):
        gb = g // HG_BASE
        ref = HG_BASE * ((u // gb) * gb + gb // 2) - 1
        up = (u % gb) >= gb // 2
        secs.append(np.where(up, (t > ref) & (t <= ru), (t > ru) & (t <= ref)).astype(np.float32))
        upper.append([bool(v) for v in up[:, 0]])
        ii, jj = idx[:, None], idx[None, :]
        masks.append(((ii // g) == (jj // g)) & ((ii % g) >= g // 2) & ((jj % g) < g // 2))
    mf = np.concatenate(secs, axis=0)
    mb = np.concatenate([s[::-1, ::-1] for s in secs], axis=0)
    H = L // 2
    sub = [masks[0][:H, :H]] + [m[:H, :H] for m in masks[2:]]
    kf = np.stack(sub).astype(np.float32)
    kb = np.stack([m[::-1, ::-1] for m in sub]).astype(np.float32)
    upper_fb = (tuple(tuple(x) for x in upper), tuple(tuple(x[::-1]) for x in upper))
    return mf, mb, kf, kb, upper_fb


def _hg_gate(z, lb):
    key = (1.0 - lb) * jax.nn.sigmoid(-z)
    return key, jnp.log(1.0 - key)


def _hg_state_mask():
    r = lax.broadcasted_iota(jnp.int32, (LANES, 2 * HG_DK), 0) // HG_DV
    c = lax.broadcasted_iota(jnp.int32, (LANES, 2 * HG_DK), 1) // HG_DK
    return r == c


def _hg_bwd_state_kernel(z_ref, v_ref, lb_ref, m_ref, sb_ref, st_ref):
    L = HG_BLOCK
    nblk = z_ref.shape[0] // L

    @pl.when(pl.program_id(1) == 0)
    def _():
        st_ref[...] = jnp.zeros_like(st_ref)

    bmt = _hg_state_mask()
    for blk in range(nblk - 1, -1, -1):
        rows = slice(blk * L, (blk + 1) * L)
        key, lf = _hg_gate(z_ref[rows, :].astype(F32), lb_ref[...])
        e = _dot_split(m_ref[...], lf)
        ke = key * jnp.exp(e[:L])
        dec = jnp.exp(e[L:L + 1])
        for p in range(2):
            sb_ref[0, blk, p] = st_ref[p]
            kvt = _dot_tn(v_ref[rows, 128 * p:128 * p + 128].astype(BF16),
                          ke[:, 256 * p:256 * p + 256].astype(BF16))
            st_ref[p] = dec[:, 256 * p:256 * p + 256] * st_ref[p] + jnp.where(bmt, kvt, 0.0)


def _hg_scale_blocks(jobs, qp_ref, kp_ref, fac_ref, nblocks):
    for u in range(0, nblocks, 2):
        r0 = slice(HG_BASE * u, HG_BASE * u + HG_BASE)
        r1 = slice(HG_BASE * (u + 1), HG_BASE * (u + 2))
        for hd in range(HG_HEADS):
            ls = slice(HG_DK * hd, HG_DK * hd + HG_DK)
            q0, k0, q1, k1 = qp_ref[r0, ls], kp_ref[r0, ls], qp_ref[r1, ls], kp_ref[r1, ls]
            for dst_ref, row0, flags in jobs:
                a = (q0 if flags[u] else k0) * fac_ref[row0 + u:row0 + u + 1, ls]
                b = (q1 if flags[u + 1] else k1) * fac_ref[row0 + u + 1:row0 + u + 2, ls]
                dst_ref[HG_BASE * u:HG_BASE * (u + 2), ls] = jnp.concatenate([a, b], axis=0).astype(dst_ref.dtype)


def _hg_main_kernel(h_ref, lb_ref, nw_ref, mf_ref, mb_ref, kf_ref, kb_ref, sb_ref, o_ref,
                    st_ref, e2_ref, qp2_ref, kp2_ref, fac2_ref, s_ref, *, upper):
    L = h_ref.shape[0]
    nb = L // HG_BASE
    nlev = kf_ref.shape[0] + 1
    all_up = (True,) * nb

    @pl.when(pl.program_id(1) == 0)
    def _():
        st_ref[...] = jnp.zeros_like(st_ref)

    lo_v = _lane_lo_mask((L, LANES))
    bmt = _hg_state_mask()
    bones = jnp.where(_head_block_ones(), 1.0, 0.0).astype(BF16)
    osum = [jnp.zeros((L, LANES), F32), jnp.zeros((L, LANES), F32)]
    nbuf = nlev + 1
    for d in range(2):
        m_ref = mf_ref if d == 0 else mb_ref
        k_ref = kf_ref if d == 0 else kb_ref
        e_ref, qp_ref, kp_ref, fac_ref = e2_ref.at[d], qp2_ref.at[d], kp2_ref.at[d], fac2_ref.at[d]
        sbuf = [s_ref.at[d * nbuf + k] for k in range(nbuf)]
        key, lf = _hg_gate(h_ref[:, 512 + 512 * d:1024 + 512 * d].astype(F32), lb_ref[...])
        e_ref[...] = _dot_split(m_ref[...], lf)
        eb = e_ref[pl.ds(0, L), :]
        qp_ref[...] = h_ref[:, 0:512].astype(F32) * jnp.exp(eb)
        kp_ref[...] = key * jnp.exp(-eb)
        fac_ref[...] = jnp.exp(e_ref[pl.ds(L, (nlev + 1) * nb), :])
        jobs = [(sbuf[lv - 1], (1 + lv) * nb, upper[d][lv - 1]) for lv in range(1, nlev)]
        jobs.append((sbuf[nlev - 1], 0, all_up))
        if d == 0:
            jobs.append((sbuf[nlev], nb, (False,) * nb))
        _hg_scale_blocks(jobs, qp_ref, kp_ref, fac_ref, nb)
        H = L // 2
        halves = (slice(0, H), slice(H, L))
        qh, kh = (1, 0) if d == 0 else (0, 1)
        masks = [k_ref[lv] > 0.5 for lv in range(nlev - 1)]

        def tiles(ref, hbs):
            return jnp.stack([ref[halves[hb], HG_DK * hd:HG_DK * hd + HG_DK]
                              for hd in range(HG_HEADS) for hb in hbs], axis=0).astype(BF16)

        def bdot(a, b):
            return lax.dot_general(a, b, (((2,), (2,)), ((0,), (0,))), preferred_element_type=F32)

        diag_all = jnp.where(masks[0][None], bdot(tiles(qp_ref, (0, 1)), tiles(kp_ref, (0, 1))), 0.0)
        for lv in range(2, nlev):
            t = tiles(sbuf[lv - 1], (0, 1))
            diag_all = jnp.where(masks[lv - 1][None], bdot(t, t), diag_all)
        diag_all = diag_all.astype(BF16)
        cross_all = bdot(tiles(sbuf[0], (qh,)), tiles(sbuf[0], (kh,))).astype(BF16)
        for p in range(2):
            v = h_ref[:, 1536 + 128 * p:1664 + 128 * p].astype(F32)
            vlo = jnp.where(lo_v, v, 0.0).astype(BF16)
            vhi = jnp.where(lo_v, 0.0, v).astype(BF16)
            qe_pair = []
            ke_pair = []
            for hh in range(2):
                ls = slice(HG_DK * (2 * p + hh), HG_DK * (2 * p + hh) + HG_DK)
                qe_pair.append(sbuf[nlev - 1][:, ls])
                if d == 0:
                    ke_pair.append(sbuf[nlev][:, ls])
            dg = [[diag_all[2 * (2 * p + hh) + hb] for hb in range(2)] for hh in range(2)]
            cross = [cross_all[2 * p + hh] for hh in range(2)]
            vs = jnp.concatenate([vlo, vhi], axis=0)
            if d == 0:
                o_q = _dot(jnp.concatenate([cross[0], dg[0][1], cross[1], dg[1][1]], axis=1), vs)
                o_k = _dot(jnp.concatenate([dg[0][0], dg[1][0]], axis=1),
                           jnp.concatenate([vlo[halves[0]], vhi[halves[0]]], axis=0))
                intra = jnp.concatenate([o_k, o_q], axis=0)
            else:
                o_q = _dot(jnp.concatenate([dg[0][0], cross[0], dg[1][0], cross[1]], axis=1), vs)
                o_k = _dot(jnp.concatenate([dg[0][1], dg[1][1]], axis=1),
                           jnp.concatenate([vlo[halves[1]], vhi[halves[1]]], axis=0))
                intra = jnp.concatenate([o_q, o_k], axis=0)
            state = st_ref[p] if d == 0 else sb_ref[0, 0, p]
            inter = _dot_nt(jnp.concatenate(qe_pair, axis=1), state.astype(BF16))
            osum[p] = osum[p] + intra + inter
            if d == 0:
                ps = slice(256 * p, 256 * p + 256)
                dec = fac_ref[nb - 1:nb, ps] * fac_ref[2 * nb - 1:2 * nb, ps]
                kvt = _dot_tn(v.astype(BF16), jnp.concatenate(ke_pair, axis=1))
                st_ref[p] = dec * st_ref[p] + jnp.where(bmt, kvt, 0.0)
    for p in range(2):
        o = osum[p]
        ms = _dot_split_rhs_const(o * o, bones) * (1.0 / HG_DV)
        g = h_ref[:, 1792 + 128 * p:1920 + 128 * p].astype(F32)
        y = o * lax.rsqrt(ms + RMS_EPS) * nw_ref[:, 128 * p:128 * p + 128] * jax.nn.sigmoid(g)
        o_ref[:, 128 * p:128 * p + 128] = y.astype(o_ref.dtype)


def _hgrn2(h, lb, nw, consts, B, S):
    L = HG_BLOCK
    nb = S // L
    mf, mb, kf, kb, mpre, upper = consts
    nsec = mf.shape[0]
    nfac = nsec - L
    sb = pl.pallas_call(
        _hg_bwd_state_kernel,
        grid=(B, nb // HG_PRE_BLOCKS),
        in_specs=[pl.BlockSpec((HG_PRE_BLOCKS * L, 512), lambda b, n: ((b * nb + nb) // HG_PRE_BLOCKS - 1 - n, 2)),
                  pl.BlockSpec((HG_PRE_BLOCKS * L, 256), lambda b, n: ((b * nb + nb) // HG_PRE_BLOCKS - 1 - n, 6)),
                  pl.BlockSpec((1, 512), lambda b, n: (0, 0)),
                  pl.BlockSpec((L + 8, L), lambda b, n: (0, 0))],
        out_specs=pl.BlockSpec((1, HG_PRE_BLOCKS, 2, LANES, 2 * HG_DK),
                               lambda b, n: (b, nb // HG_PRE_BLOCKS - 1 - n, 0, 0, 0)),
        out_shape=jax.ShapeDtypeStruct((B, nb, 2, LANES, 2 * HG_DK), F32),
        scratch_shapes=[pltpu.VMEM((2, LANES, 2 * HG_DK), F32)],
        compiler_params=_cparams(("parallel", "arbitrary")),
        name="hg_bwd_state",
    )(h, h, lb, mpre)
    return pl.pallas_call(
        functools.partial(_hg_main_kernel, upper=upper),
        grid=(B, nb),
        in_specs=[pl.BlockSpec((L, HG_W), lambda b, n: (b * nb + n, 0)),
                  pl.BlockSpec((1, 512), lambda b, n: (0, 0)),
                  pl.BlockSpec((1, 256), lambda b, n: (0, 0)),
                  pl.BlockSpec((nsec, L), lambda b, n: (0, 0)),
                  pl.BlockSpec((nsec, L), lambda b, n: (0, 0)),
                  pl.BlockSpec(kf.shape, lambda b, n: (0, 0, 0)),
                  pl.BlockSpec(kb.shape, lambda b, n: (0, 0, 0)),
                  pl.BlockSpec((1, 1, 2, LANES, 2 * HG_DK), lambda b, n: (b, n, 0, 0, 0))],
        out_specs=pl.BlockSpec((L, 256), lambda b, n: (b * nb + n, 0)),
        out_shape=jax.ShapeDtypeStruct((B * S, 256), BF16),
        scratch_shapes=[pltpu.VMEM((2, LANES, 2 * HG_DK), F32),
                        pltpu.VMEM((2, nsec, 512), F32),
                        pltpu.VMEM((2, L, 512), F32), pltpu.VMEM((2, L, 512), F32),
                        pltpu.VMEM((2, nfac, 512), F32),
                        pltpu.VMEM((2 * (kf.shape[0] + 2), L, 512), BF16)],
        compiler_params=_cparams(("parallel", "arbitrary")),
        name="hg_main",
    )(h, lb, nw, mf, mb, kf, kb, sb)


def _outproj_kernel(x_ref, r_ref, s_ref, g_ref, wr_ref, ws_ref, wg_ref, lg_ref, lb_ref, o_ref):
    mix = _dot(r_ref[...], wr_ref[...]) + _dot(s_ref[...], ws_ref[...]) + _dot(g_ref[...], wg_ref[...])
    o_ref[...] = _layer_norm(DEEPNORM_ALPHA * x_ref[...] + mix, lg_ref[...], lb_ref[...])


ROUTER_ROWS = 16


def _outproj_router_kernel(x_ref, r_ref, s_ref, g_ref, wr_ref, ws_ref, wg_ref, lg_ref, lb_ref, wrt_ref, tri_ref,
                           o_ref, ot_ref, route_ref, cnt_ref):
    @pl.when(pl.program_id(0) == 0)
    def _():
        cnt_ref[...] = jnp.zeros_like(cnt_ref)

    mix = _dot(r_ref[...], wr_ref[...]) + _dot(s_ref[...], ws_ref[...]) + _dot(g_ref[...], wg_ref[...])
    y = _layer_norm(DEEPNORM_ALPHA * x_ref[...] + mix, lg_ref[...], lb_ref[...])
    o_ref[...] = y
    ot_ref[...] = _to_token_tiles(y)
    tm = y.shape[0]
    w = wrt_ref[...]
    w_hi = w.astype(BF16)
    w_lo = (w - w_hi.astype(F32)).astype(BF16)
    y_hi = y.astype(BF16)
    y_lo = (y - y_hi.astype(F32)).astype(BF16)
    lt = _dot_nt(jnp.concatenate([w_hi, w_lo], axis=0), y_hi)
    logits = lt[:ROUTER_ROWS] + lt[ROUTER_ROWS:] + _dot_nt(w_hi, y_lo)
    row = lax.broadcasted_iota(jnp.int32, logits.shape, 0).astype(F32)
    neg = jnp.float32(-jnp.inf)
    l1 = jnp.where(row < N_EXPERTS, logits, neg)
    m1 = jnp.max(l1, axis=0, keepdims=True)
    i1 = jnp.min(jnp.where(l1 == m1, row, float(ROUTER_ROWS)), axis=0, keepdims=True)
    l2 = jnp.where(row == i1, neg, l1)
    m2 = jnp.max(l2, axis=0, keepdims=True)
    i2 = jnp.min(jnp.where(l2 == m2, row, float(ROUTER_ROWS)), axis=0, keepdims=True)
    e2 = jnp.exp(m2 - m1)
    den = 1.0 + e2
    picked = jnp.where((row == i1) | (row == i2), 1.0, 0.0)
    before = _dot(picked.astype(BF16), tri_ref[...])
    rank = before + cnt_ref[:, 0:1]
    r1 = jnp.sum(jnp.where(row == i1, rank, 0.0), axis=0, keepdims=True)
    r2 = jnp.sum(jnp.where(row == i2, rank, 0.0), axis=0, keepdims=True)
    cnt_ref[...] = cnt_ref[...] + jnp.sum(picked, axis=1, keepdims=True)
    lrow = lax.broadcasted_iota(jnp.int32, (8, tm), 0)
    route8 = (jnp.where(lrow == 0, i1, 0.0) + jnp.where(lrow == 1, i2, 0.0)
              + jnp.where(lrow == 2, 1.0 / den, 0.0) + jnp.where(lrow == 3, e2 / den, 0.0)
              + jnp.where(lrow == 4, r1, 0.0) + jnp.where(lrow == 5, r2, 0.0))
    route_ref[...] = jnp.concatenate([route8, jnp.zeros((LANES - 8, tm), F32)], axis=0).T


def _out_proj(x2d, ret, swa, hgo, w_r, w_s, w_g, ln_g, ln_b, w_router=None):
    T = x2d.shape[0]
    tm = 1024
    row = lambda i: (i, 0)
    full = lambda i: (0, 0)
    in_specs = [pl.BlockSpec((tm, D_MODEL), row),
                pl.BlockSpec((tm, 256), row), pl.BlockSpec((tm, 512), row), pl.BlockSpec((tm, 256), row),
                pl.BlockSpec((256, D_MODEL), full), pl.BlockSpec((512, D_MODEL), full),
                pl.BlockSpec((256, D_MODEL), full),
                pl.BlockSpec((1, D_MODEL), full), pl.BlockSpec((1, D_MODEL), full)]
    args = [x2d, ret, swa, hgo, w_r, w_s, w_g, ln_g, ln_b]
    if w_router is None:
        return pl.pallas_call(
            _outproj_kernel, grid=(T // tm,), in_specs=in_specs,
            out_specs=pl.BlockSpec((tm, D_MODEL), row),
            out_shape=jax.ShapeDtypeStruct((T, D_MODEL), F32),
            compiler_params=_cparams(("parallel",)), name="out_proj",
        )(*args)
    return pl.pallas_call(
        _outproj_router_kernel, grid=(T // tm,),
        in_specs=in_specs + [pl.BlockSpec((ROUTER_ROWS, D_MODEL), full),
                             pl.BlockSpec((tm, tm), full, pipeline_mode=pl.Buffered(1))],
        out_specs=[pl.BlockSpec((tm, D_MODEL), row), pl.BlockSpec((tm, 8, LANES), lambda i: (i, 0, 0)),
                   pl.BlockSpec((tm, LANES), row), pl.BlockSpec((ROUTER_ROWS, LANES), full)],
        out_shape=[jax.ShapeDtypeStruct((T, D_MODEL), F32), jax.ShapeDtypeStruct((T, 8, LANES), F32),
                   jax.ShapeDtypeStruct((T, LANES), F32), jax.ShapeDtypeStruct((ROUTER_ROWS, LANES), F32)],
        compiler_params=_cparams(("arbitrary",)), name="out_proj_router",
    )(*args, w_router, jnp.asarray(np.triu(np.ones((tm, tm), np.float32), 1), BF16))


FFN_CHUNKS = 2


def _ffn_kernel(x_ref, wg_ref, wu_ref, wd_ref, lg_ref, lb_ref, o_ref):
    x = x_ref[...]
    xb = x.astype(BF16)
    th = FFN_DIM // FFN_CHUNKS
    f = jnp.zeros(x.shape, F32)
    for c in range(FFN_CHUNKS):
        g = _dot(xb, wg_ref[:, c * th:(c + 1) * th])
        u = _dot(xb, wu_ref[:, c * th:(c + 1) * th])
        a = (g * jax.nn.sigmoid(g)) * u
        f = f + _dot(a.astype(BF16), wd_ref[c * th:(c + 1) * th, :])
    o_ref[...] = _layer_norm(DEEPNORM_ALPHA * x + f, lg_ref[...], lb_ref[...])


def _ffn(x2d, wg, wu, wd, ln_g, ln_b):
    T = x2d.shape[0]
    tm = 512
    once = pl.Buffered(1)
    return pl.pallas_call(
        _ffn_kernel, grid=(T // tm,),
        in_specs=[pl.BlockSpec((tm, D_MODEL), lambda i: (i, 0)),
                  pl.BlockSpec((D_MODEL, FFN_DIM), lambda i: (0, 0), pipeline_mode=once),
                  pl.BlockSpec((D_MODEL, FFN_DIM), lambda i: (0, 0), pipeline_mode=once),
                  pl.BlockSpec((FFN_DIM, D_MODEL), lambda i: (0, 0), pipeline_mode=once),
                  pl.BlockSpec((1, D_MODEL), lambda i: (0, 0)),
                  pl.BlockSpec((1, D_MODEL), lambda i: (0, 0))],
        out_specs=pl.BlockSpec((tm, D_MODEL), lambda i: (i, 0)),
        out_shape=jax.ShapeDtypeStruct((T, D_MODEL), F32),
        compiler_params=_cparams(("parallel",)), name="ffn",
    )(x2d, wg, wu, wd, ln_g, ln_b)


MOE_TILE = 512
MOE_TH = 1792
MOE_TOK = 512


def _moe_num_tiles(T):
    return (2 * T) // MOE_TILE + N_EXPERTS + 1


def _moe_tables(route, counts):
    T = route.shape[0]
    tm = MOE_TILE
    cnt = counts[:N_EXPERTS, 0].astype(jnp.int32)
    pc = ((cnt + tm - 1) // tm) * tm
    pend = jnp.cumsum(pc)
    pstart = pend - pc
    e = route[:, 0:2].astype(jnp.int32)
    rank = route[:, 4:6].astype(jnp.int32)
    sel = e[:, :, None] == jnp.arange(N_EXPERTS, dtype=jnp.int32)[None, None, :]
    pos = jnp.sum(jnp.where(sel, pstart[None, None, :], 0), axis=-1) + rank
    n_tiles = _moe_num_tiles(T)
    tile_row = jnp.arange(n_tiles, dtype=jnp.int32) * tm
    texp = jnp.minimum(jnp.sum((tile_row[:, None] >= pend[None, :]).astype(jnp.int32), axis=1), N_EXPERTS - 1)
    n_used = pend[-1] // tm
    last = jnp.sum(jnp.where(jnp.arange(n_tiles) == n_used - 1, texp, 0))
    texp = jnp.where(jnp.arange(n_tiles) < n_used, texp, last)
    return pos.reshape(-1), texp, n_used.reshape(1), pstart + cnt


def _to_token_tiles(x):
    cols = jnp.stack([x[:, LANES * c:LANES * c + LANES] for c in range(D_MODEL // LANES)], axis=0)
    return pltpu.einshape("csl->scl", cols)


def _from_token_tiles(xt):
    y = pltpu.einshape("scl->csl", xt)
    return jnp.concatenate([y[c] for c in range(D_MODEL // LANES)], axis=1)


MOE_DMA_UNROLL = 8


def _moe_dispatch_kernel(pos_ref, pad_ref, nused_ref, xt_ref, xs_hbm, zero_ref, sem, zsem):
    i = pl.program_id(0)
    tm = xt_ref.shape[0]
    n_tiles = xs_hbm.shape[0] // MOE_TILE

    def zero_copy(row):
        return pltpu.make_async_copy(zero_ref, xs_hbm.at[pl.ds(row, MOE_TILE)], zsem)

    @pl.when(i == 0)
    def _():
        zero_ref[...] = jnp.zeros_like(zero_ref)
        for e in range(N_EXPERTS):
            zero_copy(pad_ref[e]).start()
        for e in range(N_EXPERTS):
            zero_copy(pad_ref[e]).wait()
        for t in range(n_tiles - N_EXPERTS - 1, n_tiles):
            @pl.when(t >= nused_ref[0])
            def _():
                zero_copy(t * MOE_TILE).start()
                zero_copy(t * MOE_TILE).wait()

    def body(kk, c):
        k0 = kk * MOE_DMA_UNROLL
        t0 = i * tm + k0
        dst = [pos_ref[2 * t0 + u] for u in range(2 * MOE_DMA_UNROLL)]
        for u in range(2 * MOE_DMA_UNROLL):
            pltpu.make_async_copy(xt_ref.at[k0 + u // 2], xs_hbm.at[dst[u]], sem).start()
        return c

    lax.fori_loop(0, tm // MOE_DMA_UNROLL, body, 0)
    for s in range(2):
        pltpu.make_async_copy(xt_ref, xs_hbm.at[pl.ds(0, tm)], sem).wait()


def _moe_dispatch(x1t, pos, padstart, n_used):
    T = x1t.shape[0]
    P = _moe_num_tiles(T) * MOE_TILE
    return pl.pallas_call(
        _moe_dispatch_kernel,
        grid_spec=pltpu.PrefetchScalarGridSpec(
            num_scalar_prefetch=3, grid=(T // MOE_TOK,),
            in_specs=[pl.BlockSpec((MOE_TOK, 8, LANES), lambda i, pos, pad, nused: (i, 0, 0))],
            out_specs=pl.BlockSpec(memory_space=pl.ANY),
            scratch_shapes=[pltpu.VMEM((MOE_TILE, 8, LANES), F32), pltpu.SemaphoreType.DMA,
                            pltpu.SemaphoreType.DMA]),
        out_shape=jax.ShapeDtypeStruct((P, 8, LANES), F32),
        compiler_params=_cparams(("arbitrary",)), name="moe_dispatch",
    )(pos, padstart, n_used, x1t)


def _moe_group_kernel(texp_ref, nused_ref, xs_ref, wg_ref, wu_ref, wd_ref, y_ref, xb_ref, acc_ref):
    i = pl.program_id(0)
    j = pl.program_id(1)

    @pl.when(i < nused_ref[0])
    def _():
        @pl.when(j == 0)
        def _():
            acc_ref[...] = jnp.zeros_like(acc_ref)

        xb = _from_token_tiles(xs_ref[...]).astype(BF16)
        g = _dot(xb, wg_ref[...])
        u = _dot(xb, wu_ref[...])
        a = (g * jax.nn.sigmoid(g)) * u
        acc_ref[...] += _dot(a.astype(BF16), wd_ref[...])

        @pl.when(j == pl.num_programs(1) - 1)
        def _():
            y_ref[...] = _to_token_tiles(acc_ref[...])

    @pl.when((i >= nused_ref[0]) & (j == 0))
    def _():
        y_ref[...] = jnp.zeros_like(y_ref)


def _moe_group(xs, texp, n_used, wg, wu, wd):
    P = xs.shape[0]
    tm, th = MOE_TILE, MOE_TH
    nh = EXPERT_DIM // th

    def row_map(i, j, texp, nused):
        return (jnp.minimum(i, nused[0] - 1), 0, 0)

    def col(i, j, nused):
        return jnp.where(i < nused[0], j, nh - 1)

    return pl.pallas_call(
        _moe_group_kernel,
        grid_spec=pltpu.PrefetchScalarGridSpec(
            num_scalar_prefetch=2, grid=(P // tm, nh),
            in_specs=[pl.BlockSpec((tm, 8, LANES), row_map),
                      pl.BlockSpec((None, D_MODEL, th), lambda i, j, texp, nused: (texp[i], 0, col(i, j, nused))),
                      pl.BlockSpec((None, D_MODEL, th), lambda i, j, texp, nused: (texp[i], 0, col(i, j, nused))),
                      pl.BlockSpec((None, th, D_MODEL), lambda i, j, texp, nused: (texp[i], col(i, j, nused), 0))],
            out_specs=pl.BlockSpec((tm, 8, LANES), lambda i, j, texp, nused: (i, 0, 0)),
            scratch_shapes=[pltpu.VMEM((tm, D_MODEL), BF16), pltpu.VMEM((tm, D_MODEL), F32)]),
        out_shape=jax.ShapeDtypeStruct((P, 8, LANES), F32),
        compiler_params=_cparams(("arbitrary", "arbitrary")), name="moe_group",
    )(texp, n_used, xs, wg, wu, wd)


def _moe_combine_kernel(pos_ref, x_ref, route_ref, y_hbm, lg_ref, lb_ref, o_ref, buf_ref, sem):
    i = pl.program_id(0)
    tm = x_ref.shape[0]

    def issue(tile, slot):
        def body(kk, c):
            k0 = kk * MOE_DMA_UNROLL
            t0 = tile * tm + k0
            src = [pos_ref[2 * t0 + u] for u in range(2 * MOE_DMA_UNROLL)]
            for u in range(2 * MOE_DMA_UNROLL):
                pltpu.make_async_copy(y_hbm.at[src[u]], buf_ref.at[slot, (u % 2) * tm + k0 + u // 2],
                                      sem.at[slot]).start()
            return c
        lax.fori_loop(0, tm // MOE_DMA_UNROLL, body, 0)

    @pl.when(i == 0)
    def _():
        issue(0, 0)

    slot = i % 2
    pltpu.make_async_copy(y_hbm.at[pl.ds(0, 2 * tm)], buf_ref.at[slot], sem.at[slot]).wait()

    @pl.when(i + 1 < pl.num_programs(0))
    def _():
        issue(i + 1, 1 - slot)

    a = _from_token_tiles(buf_ref[slot, pl.ds(0, tm)])
    b = _from_token_tiles(buf_ref[slot, pl.ds(tm, tm)])
    f = route_ref[:, 2:3] * a + route_ref[:, 3:4] * b
    o_ref[...] = _layer_norm(DEEPNORM_ALPHA * x_ref[...] + f, lg_ref[...], lb_ref[...])


def _moe_combine(x1, route, y, pos, ln_g, ln_b):
    T = x1.shape[0]
    tm = MOE_TOK
    return pl.pallas_call(
        _moe_combine_kernel,
        grid_spec=pltpu.PrefetchScalarGridSpec(
            num_scalar_prefetch=1, grid=(T // tm,),
            in_specs=[pl.BlockSpec((tm, D_MODEL), lambda i, pos: (i, 0)),
                      pl.BlockSpec((tm, LANES), lambda i, pos: (i, 0)),
                      pl.BlockSpec(memory_space=pl.ANY),
                      pl.BlockSpec((1, D_MODEL), lambda i, pos: (0, 0)),
                      pl.BlockSpec((1, D_MODEL), lambda i, pos: (0, 0))],
            out_specs=pl.BlockSpec((tm, D_MODEL), lambda i, pos: (i, 0)),
            scratch_shapes=[pltpu.VMEM((2, 2 * tm, 8, LANES), F32), pltpu.SemaphoreType.DMA((2,))]),
        out_shape=jax.ShapeDtypeStruct((T, D_MODEL), F32),
        compiler_params=_cparams(("arbitrary",)), name="moe_combine",
    )(pos, x1, route, y, ln_g, ln_b)


def _moe(x1, x1t, route, counts, wg, wu, wd, ln_g, ln_b):
    pos, texp, n_used, padstart = _moe_tables(route, counts)
    xs = _moe_dispatch(x1t, pos, padstart, n_used)
    y = _moe_group(xs, texp, n_used, wg, wu, wd)
    return _moe_combine(x1, route, y, pos, ln_g, ln_b)


def _t5_bucket(rel):
    nb = REL_BUCKETS // 2
    ret = (rel > 0).astype(np.int32) * nb
    n = np.abs(rel)
    max_exact = nb // 2
    large = max_exact + (np.log(np.maximum(n, 1) / max_exact) / np.log(REL_MAX_DIST / max_exact)
                         * (nb - max_exact)).astype(np.int32)
    large = np.minimum(large, nb - 1)
    return ret + np.where(n < max_exact, n, large)


def _swa_head_order():
    return [h for c in range(4) for h in (c, 4 + c)]


def _prepare(w_in, w_out, ret_decay, swa_sink, rel_bias, hg_lb, hg_norm_w, moe_router, seq_lens):
    order = _swa_head_order()
    w_in_p = jnp.concatenate(
        [w_in[:, :, 1792:3840], w_in[:, :, 0:1024]]
        + [w_in[:, :, 1024 + 64 * h:1088 + 64 * h] for h in order]
        + [w_in[:, :, 1536:1792]], axis=2).astype(BF16)
    w_r = w_out[:, 0:256, :].astype(BF16)
    w_s = jnp.concatenate([w_out[:, 256 + 64 * h:320 + 64 * h, :] for h in order], axis=1).astype(BF16)
    w_g = w_out[:, 768:1024, :].astype(BF16)

    rd = ret_decay.astype(F32)
    rdp = jnp.repeat(rd.reshape(DEPTH, 4, 2), 64, axis=-1)
    rdp = jnp.concatenate([rdp, rdp], axis=1)
    rdh = jnp.broadcast_to(rd.reshape(DEPTH, 8, 1), (DEPTH, 8, LANES))

    W = SWA_BLOCK
    rel = np.arange(3 * W)[None, :] - W - np.arange(W)[:, None]
    onehot = jnp.asarray(_t5_bucket(rel)[..., None] == np.arange(REL_BUCKETS), F32)
    bias = jnp.einsum("qkb,bh->hqk", onehot, rel_bias.astype(F32), precision=lax.Precision.HIGHEST)
    band = jnp.asarray(np.abs(rel) <= WINDOW)
    bias_kq = jnp.transpose(jnp.where(band[None], bias, MASK_VALUE), (0, 2, 1))
    order = _swa_head_order()
    bias_t = jnp.concatenate([bias_kq[h] for h in order], axis=1)
    sink = swa_sink.astype(F32)
    sink_t = jnp.concatenate([jnp.broadcast_to(sink[:, h, None, None], (DEPTH, 8, W)) for h in order], axis=2)

    p = jax.nn.softmax(hg_lb.astype(F32), axis=0)
    lower = (jnp.cumsum(p, axis=0) - p[0]).reshape(DEPTH, 1, HG_HEADS * HG_DK)
    nw = hg_norm_w.astype(F32).reshape(DEPTH, 1, HG_HEADS * HG_DV)

    mf, mb, kf, kb, upper = _hg_constants(HG_BLOCK)
    L = HG_BLOCK
    mpre = np.concatenate([np.tril(np.ones((L, L), np.float32), -1), np.ones((8, L), np.float32)], axis=0)
    consts = (jnp.asarray(mf, BF16), jnp.asarray(mb, BF16), jnp.asarray(kf, F32), jnp.asarray(kb, F32),
              jnp.asarray(mpre, BF16), upper)

    half = RET_DK // 2
    inv = ROPE_BASE ** (-np.arange(half, dtype=np.float32) / half)
    rope = {}
    for S in set(seq_lens):
        ang = jnp.arange(S, dtype=F32)[:, None] * jnp.asarray(inv)[None, :]
        c32, s32 = jnp.cos(ang), jnp.sin(ang)
        rope[S] = (jnp.tile(c32, (1, 4)), jnp.tile(jnp.concatenate([-s32, s32], axis=1), (1, 2)))

    router = jnp.pad(jnp.transpose(moe_router.astype(F32), (0, 2, 1)),
                     ((0, 0), (0, ROUTER_ROWS - N_EXPERTS), (0, 0)))
    return dict(w_in=w_in_p, w_r=w_r, w_s=w_s, w_g=w_g, rdp=rdp, rdh=rdh, bias=bias_t, sink=sink_t,
                lower=lower, nw=nw, consts=consts, rope=rope, router=router)


def _trunk(x, prm, ln1_g, ln1_b, ln2_g, ln2_b, ffn_w, moe_w):
    B, S, _ = x.shape
    x2d = x.reshape(B * S, D_MODEL)
    cos_t, sin_t = prm["rope"][S]
    for l in range(DEPTH):
        h = _in_proj(x2d, prm["w_in"][l])
        ret = _retention(h, cos_t, sin_t, prm["rdp"][l], prm["rdh"][l], B, S)
        swa = _swa(h, prm["bias"], prm["sink"][l], B, S)
        hgo = _hgrn2(h, prm["lower"][l], prm["nw"][l], prm["consts"], B, S)
        g1, b1 = ln1_g[l].reshape(1, D_MODEL), ln1_b[l].reshape(1, D_MODEL)
        g2, b2 = ln2_g[l].reshape(1, D_MODEL), ln2_b[l].reshape(1, D_MODEL)
        i = l // 2
        if l % 2 == 0:
            x1 = _out_proj(x2d, ret, swa, hgo, prm["w_r"][l], prm["w_s"][l], prm["w_g"][l], g1, b1)
            wg, wu, wd = ffn_w
            x2d = _ffn(x1, wg[i], wu[i], wd[i], g2, b2)
        else:
            x1, x1t, route, counts = _out_proj(x2d, ret, swa, hgo, prm["w_r"][l], prm["w_s"][l], prm["w_g"][l],
                                               g1, b1, prm["router"][i])
            wg, wu, wd = moe_w
            x2d = _moe(x1, x1t, route, counts, wg[i], wu[i], wd[i], g2, b2)
    return x2d.reshape(B, S, D_MODEL)


def kernel(x_prompt, x_sample, w_in, w_out, ret_decay, swa_sink, rel_bias, hg_lb, hg_norm_w, ln1_g, ln1_b,
           ln2_g, ln2_b, ffn_w_gate, ffn_w_up, ffn_w_down, moe_router, moe_w_gate, moe_w_up, moe_w_down):
    prm = _prepare(w_in, w_out, ret_decay, swa_sink, rel_bias, hg_lb, hg_norm_w, moe_router,
                   (x_prompt.shape[1], x_sample.shape[1]))
    ffn_w = (ffn_w_gate.astype(BF16), ffn_w_up.astype(BF16), ffn_w_down.astype(BF16))
    moe_w = (moe_w_gate.astype(BF16), moe_w_up.astype(BF16), moe_w_down.astype(BF16))
    outs = []
    for x in (x_prompt, x_sample):
        outs.append(_trunk(x, prm, ln1_g, ln1_b, ln2_g, ln2_b, ffn_w, moe_w))
    return tuple(outs)
```

```python
import functools
import math

import numpy as np
import jax
import jax.numpy as jnp
from jax import lax
from jax.experimental import pallas as pl
from jax.experimental.pallas import tpu as pltpu

F32 = jnp.float32
BF16 = jnp.bfloat16

D_MODEL = 1024
DEPTH = 2
RET_HEADS = 4
RET_DK = 64
RET_CHUNK = 128
ROPE_BASE = 10000.0
SWA_Q_HEADS = 8
SWA_KV_HEADS = 2
SWA_DH = 64
WINDOW = 128
SWA_BLOCK = 128
REL_BUCKETS = 32
REL_MAX_DIST = 128
MASK_VALUE = -1e30
HG_HEADS = 4
HG_DK = 128
HG_DV = 64
FFN_DIM = 2816
N_EXPERTS = 8
EXPERT_DIM = 3584
LN_EPS = 1e-5
RMS_EPS = 1e-6
DEEPNORM_ALPHA = (2.0 * DEPTH) ** 0.25

LANES = 128
VMEM_LIMIT = 56 * 1024 * 1024

HG_W = 2048
RET_W = 1024
SWA_W = 768
IN_COLS = HG_W + RET_W + SWA_W

HG_BLOCK = 256
HG_BASE = 8
HG_PRE_BLOCKS = 4
RET_TILE = 1024
SWA_TILE = 1024


def _cparams(sem):
    return pltpu.CompilerParams(dimension_semantics=sem, vmem_limit_bytes=VMEM_LIMIT)


def _dot(a, b):
    return jnp.dot(a, b, preferred_element_type=F32)


def _dot_nt(a, b):
    return lax.dot_general(a, b, (((1,), (1,)), ((), ())), preferred_element_type=F32)


def _dot_tn(a, b):
    return lax.dot_general(a, b, (((0,), (0,)), ((), ())), preferred_element_type=F32)


def _dot_split(a_bf16, x):
    hi = x.astype(BF16)
    lo = (x - hi.astype(F32)).astype(BF16)
    return _dot(a_bf16, hi) + _dot(a_bf16, lo)


def _dot_split_rhs_const(x, b_bf16):
    hi = x.astype(BF16)
    lo = (x - hi.astype(F32)).astype(BF16)
    return _dot(hi, b_bf16) + _dot(lo, b_bf16)


def _layer_norm(y, g, b):
    mu = jnp.mean(y, axis=-1, keepdims=True)
    yc = y - mu
    var = jnp.mean(yc * yc, axis=-1, keepdims=True)
    return yc * lax.rsqrt(var + LN_EPS) * g + b


def _lane_lo_mask(shape):
    return lax.broadcasted_iota(jnp.int32, shape, len(shape) - 1) < (LANES // 2)


def _head_block_ones():
    r = lax.broadcasted_iota(jnp.int32, (LANES, LANES), 0) // 64
    c = lax.broadcasted_iota(jnp.int32, (LANES, LANES), 1) // 64
    return r == c


def _inproj_kernel(x_ref, w_ref, o_ref):
    o_ref[...] = _dot(x_ref[...].astype(BF16), w_ref[...]).astype(o_ref.dtype)


def _in_proj(x2d, w):
    T = x2d.shape[0]
    tm, tn = 2048, 1280
    return pl.pallas_call(
        _inproj_kernel,
        grid=(T // tm, IN_COLS // tn),
        in_specs=[pl.BlockSpec((tm, D_MODEL), lambda i, j: (i, 0)),
                  pl.BlockSpec((D_MODEL, tn), lambda i, j: (0, j))],
        out_specs=pl.BlockSpec((tm, tn), lambda i, j: (i, j)),
        out_shape=jax.ShapeDtypeStruct((T, IN_COLS), BF16),
        compiler_params=_cparams(("parallel", "arbitrary")),
        name="in_proj",
    )(x2d, w)


def _rotary(x, cos, sin_signed, even):
    sw = jnp.where(even, pltpu.roll(x, 96, 1), pltpu.roll(x, 32, 1))
    return x * cos + sw * sin_signed


def _ret_tables(rdp_ref):
    lg = jnp.log1p(-jnp.exp(rdp_ref[...]))
    return lg


def _ret_bwd_state_kernel(h_ref, cos_ref, sin_ref, rdp_ref, sb_ref, st_ref):
    C = RET_CHUNK
    nc = h_ref.shape[0] // C

    @pl.when(pl.program_id(1) == 0)
    def _():
        st_ref[...] = jnp.zeros_like(st_ref)

    lg = _ret_tables(rdp_ref)
    rows = lax.broadcasted_iota(jnp.int32, (C, LANES), 0).astype(F32)
    even = (lax.broadcasted_iota(jnp.int32, (C, LANES), 1) // 32) % 2 == 0
    bm = _head_block_ones()
    for c in range(nc - 1, -1, -1):
        sl = pl.ds(c * C, C)
        cos = cos_ref[sl, :]
        sin = sin_ref[sl, :]
        for p in range(2):
            lgb = lg[2 + p:3 + p, :]
            kwb = jnp.exp(lgb * rows)
            decb = jnp.where(bm, jnp.exp(lgb * float(C)), 0.0)
            k = _rotary(h_ref[sl, 256 + 128 * p:384 + 128 * p].astype(F32), cos, sin, even) * (RET_DK ** -0.5)
            v = h_ref[sl, 512 + 128 * p:640 + 128 * p].astype(F32)
            sb_ref[0, c, p] = st_ref[p]
            kv = _dot_tn((k * kwb).astype(BF16), v.astype(BF16))
            st_ref[p] = decb * st_ref[p] + jnp.where(bm, kv, 0.0)


def _ret_main_kernel(h_ref, cos_ref, sin_ref, rdp_ref, rdh_ref, sb_ref, o_ref, st_ref):
    C = RET_CHUNK
    nc = h_ref.shape[0] // C

    @pl.when(pl.program_id(1) == 0)
    def _():
        st_ref[...] = jnp.zeros_like(st_ref)

    lg = _ret_tables(rdp_ref)
    lgh = jnp.log1p(-jnp.exp(rdh_ref[...]))
    rows = lax.broadcasted_iota(jnp.int32, (C, LANES), 0).astype(F32)
    cols = lax.broadcasted_iota(jnp.int32, (C, LANES), 1).astype(F32)
    rel = rows - cols
    even = (lax.broadcasted_iota(jnp.int32, (C, LANES), 1) // 32) % 2 == 0
    lo = _lane_lo_mask((C, LANES))
    bm = _head_block_ones()
    bones = jnp.where(bm, 1.0, 0.0).astype(BF16)

    dmat = []
    for hd in range(RET_HEADS):
        f = jnp.where(rel >= 0, jnp.exp(lgh[hd:hd + 1, :] * jnp.maximum(rel, 0.0)), 0.0)
        b = jnp.where(rel <= 0, jnp.exp(lgh[4 + hd:5 + hd, :] * jnp.maximum(-rel, 0.0)), 0.0)
        dmat.append(f + b)

    for c in range(nc):
        sl = pl.ds(c * C, C)
        cos = cos_ref[sl, :]
        sin = sin_ref[sl, :]
        for p in range(2):
            lgf = lg[p:p + 1, :]
            lgb = lg[2 + p:3 + p, :]
            q = _rotary(h_ref[sl, 128 * p:128 * p + 128].astype(F32), cos, sin, even)
            k = _rotary(h_ref[sl, 256 + 128 * p:384 + 128 * p].astype(F32), cos, sin, even) * (RET_DK ** -0.5)
            v = h_ref[sl, 512 + 128 * p:640 + 128 * p].astype(F32)
            g = h_ref[sl, 768 + 128 * p:896 + 128 * p].astype(F32)
            kb = k.astype(BF16)
            vb = v.astype(BF16)
            qs = jnp.concatenate([jnp.where(lo, q, 0.0), jnp.where(lo, 0.0, q)], axis=0).astype(BF16)
            s = _dot_nt(qs, kb)
            pm = jnp.concatenate([s[:C] * dmat[2 * p], s[C:] * dmat[2 * p + 1]], axis=1).astype(BF16)
            vs = jnp.concatenate([jnp.where(lo, v, 0.0), jnp.where(lo, 0.0, v)], axis=0).astype(BF16)
            intra = _dot(pm, vs)
            qwf = jnp.exp(lgf * (rows + 1.0))
            qwb = jnp.exp(lgb * (float(C) - rows))
            qi = jnp.concatenate([q * qwf, q * qwb], axis=1).astype(BF16)
            sst = jnp.concatenate([st_ref[p], sb_ref[0, c, p]], axis=0).astype(BF16)
            o = intra + _dot(qi, sst)
            kwf = jnp.exp(lgf * (float(C) - 1.0 - rows))
            decf = jnp.where(bm, jnp.exp(lgf * float(C)), 0.0)
            kv = _dot_tn((k * kwf).astype(BF16), vb)
            st_ref[p] = decf * st_ref[p] + jnp.where(bm, kv, 0.0)
            ms = _dot_split_rhs_const(o * o, bones) * (1.0 / RET_DK)
            y = o * lax.rsqrt(ms + RMS_EPS) * (g * jax.nn.sigmoid(g))
            o_ref[sl, 128 * p:128 * p + 128] = y.astype(o_ref.dtype)


def _retention(h, cos_t, sin_t, rdp, rdh, B, S):
    L = RET_TILE
    nb = S // L
    ncb = L // RET_CHUNK
    n_chunks = S // RET_CHUNK
    colblk = HG_W // RET_W
    sb = pl.pallas_call(
        _ret_bwd_state_kernel,
        grid=(B, nb),
        in_specs=[pl.BlockSpec((L, RET_W), lambda b, n: (b * nb + nb - 1 - n, colblk)),
                  pl.BlockSpec((L, LANES), lambda b, n: (nb - 1 - n, 0)),
                  pl.BlockSpec((L, LANES), lambda b, n: (nb - 1 - n, 0)),
                  pl.BlockSpec((8, LANES), lambda b, n: (0, 0))],
        out_specs=pl.BlockSpec((1, ncb, 2, LANES, LANES), lambda b, n: (b, nb - 1 - n, 0, 0, 0)),
        out_shape=jax.ShapeDtypeStruct((B, n_chunks, 2, LANES, LANES), F32),
        scratch_shapes=[pltpu.VMEM((2, LANES, LANES), F32)],
        compiler_params=_cparams(("parallel", "arbitrary")),
        name="ret_bwd_state",
    )(h, cos_t, sin_t, rdp)
    return pl.pallas_call(
        _ret_main_kernel,
        grid=(B, nb),
        in_specs=[pl.BlockSpec((L, RET_W), lambda b, n: (b * nb + n, colblk)),
                  pl.BlockSpec((L, LANES), lambda b, n: (n, 0)),
                  pl.BlockSpec((L, LANES), lambda b, n: (n, 0)),
                  pl.BlockSpec((8, LANES), lambda b, n: (0, 0)),
                  pl.BlockSpec((8, LANES), lambda b, n: (0, 0)),
                  pl.BlockSpec((1, ncb, 2, LANES, LANES), lambda b, n: (b, n, 0, 0, 0))],
        out_specs=pl.BlockSpec((L, 256), lambda b, n: (b * nb + n, 0)),
        out_shape=jax.ShapeDtypeStruct((B * S, 256), BF16),
        scratch_shapes=[pltpu.VMEM((2, LANES, LANES), F32)],
        compiler_params=_cparams(("parallel", "arbitrary")),
        name="ret_main",
    )(h, cos_t, sin_t, rdp, rdh, sb)


def _swa_kernel(q_ref, kp_ref, kc_ref, kn_ref, vp_ref, vc_ref, vn_ref, bias_ref, sink_ref, o_ref):
    W = SWA_BLOCK
    nq = q_ref.shape[0] // W
    n = pl.program_id(1)
    last = pl.num_programs(1) - 1
    kall = jnp.concatenate([kp_ref[...], kc_ref[...], kn_ref[...]], axis=0).astype(BF16)
    vt = jnp.concatenate([vp_ref[...], vc_ref[...], vn_ref[...]], axis=0).astype(F32).T.astype(BF16)
    lo = _lane_lo_mask((W, LANES))
    ki = lax.broadcasted_iota(jnp.int32, (3 * W, SWA_Q_HEADS * W), 0)
    edge_first = jnp.where(ki >= W, 0.0, jnp.where(n > 0, 0.0, MASK_VALUE))
    edge_last = jnp.where(ki < 2 * W, 0.0, jnp.where(n < last, 0.0, MASK_VALUE))
    sink = sink_ref[0:1, :]
    for r in range(nq):
        qs = []
        for c in range(4):
            q = q_ref[pl.ds(r * W, W), 128 * c:128 * c + 128].astype(F32) * (SWA_DH ** -0.5)
            qs += [jnp.where(lo, q, 0.0), jnp.where(lo, 0.0, q)]
        qs = jnp.concatenate(qs, axis=0).astype(BF16)
        s = _dot_nt(kall[r * W:(r + 3) * W], qs) + bias_ref[...]
        if r == 0:
            s = s + edge_first
        if r == nq - 1:
            s = s + edge_last
        m = jnp.maximum(jnp.max(s, axis=0, keepdims=True), sink)
        e = jnp.exp(s - m)
        den = jnp.sum(e, axis=0, keepdims=True) + jnp.exp(sink - m)
        ot = _dot(vt[:, r * W:(r + 3) * W], e.astype(BF16)) / den
        for c in range(4):
            sel = jnp.concatenate([ot[0:SWA_DH, 2 * c * W:(2 * c + 1) * W],
                                   ot[SWA_DH:2 * SWA_DH, (2 * c + 1) * W:(2 * c + 2) * W]], axis=0)
            o_ref[pl.ds(r * W, W), 128 * c:128 * c + 128] = sel.T.astype(o_ref.dtype)


def _swa(h, bias_t, sink_t, B, S):
    L = SWA_TILE
    W = SWA_BLOCK
    nb = S // L
    r = L // W
    nblk = S // W
    qcol = (HG_W + RET_W) // 512
    kcol = (HG_W + RET_W + 512) // LANES
    vcol = kcol + 1

    def prev_map(col):
        return lambda b, n: (b * nblk + jnp.maximum(n * r - 1, 0), col)

    def cur_map(col):
        return lambda b, n: (b * nb + n, col)

    def next_map(col):
        return lambda b, n: (b * nblk + jnp.minimum(n * r + r, nblk - 1), col)

    return pl.pallas_call(
        _swa_kernel,
        grid=(B, nb),
        in_specs=[pl.BlockSpec((L, 512), cur_map(qcol)),
                  pl.BlockSpec((W, LANES), prev_map(kcol)),
                  pl.BlockSpec((L, LANES), cur_map(kcol)),
                  pl.BlockSpec((W, LANES), next_map(kcol)),
                  pl.BlockSpec((W, LANES), prev_map(vcol)),
                  pl.BlockSpec((L, LANES), cur_map(vcol)),
                  pl.BlockSpec((W, LANES), next_map(vcol)),
                  pl.BlockSpec((3 * W, SWA_Q_HEADS * W), lambda b, n: (0, 0)),
                  pl.BlockSpec((8, SWA_Q_HEADS * W), lambda b, n: (0, 0))],
        out_specs=pl.BlockSpec((L, 512), lambda b, n: (b * nb + n, 0)),
        out_shape=jax.ShapeDtypeStruct((B * S, 512), BF16),
        compiler_params=_cparams(("parallel", "parallel")),
        name="swa",
    )(h, h, h, h, h, h, h, bias_t, sink_t)


def _hg_levels(L):
    g = L
    out = []
    while g >= 2 * HG_BASE:
        out.append(g)
        g //= 2
    return out


def _hg_constants(L):
    nb = L // HG_BASE
    idx = np.arange(L)
    t = idx[None, :]
    i = idx[:, None]
    r = HG_BASE * (i // HG_BASE) + HG_BASE // 2 - 1
    secs = [((t > r) & (t <= i)).astype(np.float32) - ((t > i) & (t <= r)).astype(np.float32)]
    u = np.arange(nb)[:, None]
    ru = HG_BASE * u + HG_BASE // 2 - 1
    secs.append((t <= ru).astype(np.float32))
    secs.append((t > ru).astype(np.float32))
    masks = [((idx[:, None] // HG_BASE) == (idx[None, :] // HG_BASE)) & (idx[:, None] >= idx[None, :])]
    upper = []
    for g in _hg_levels(L):
        gb = g // HG_BASE
        ref = HG_BASE * ((u // gb) * gb + gb // 2) - 1
        up = (u % gb) >= gb // 2
        secs.append(np.where(up, (t > ref) & (t <= ru), (t > ru) & (t <= ref)).astype(np.float32))
        upper.append([bool(v) for v in up[:, 0]])
        ii, jj = idx[:, None], idx[None, :]
        masks.append(((ii // g) == (jj // g)) & ((ii % g) >= g // 2) & ((jj % g) < g // 2))
    mf = np.concatenate(secs, axis=0)
    mb = np.concatenate([s[::-1, ::-1] for s in secs], axis=0)
    H = L // 2
    sub = [masks[0][:H, :H]] + [m[:H, :H] for m in masks[2:]]
    kf = np.stack(sub).astype(np.float32)
    kb = np.stack([m[::-1, ::-1] for m in sub]).astype(np.float32)
    upper_fb = (tuple(tuple(x) for x in upper), tuple(tuple(x[::-1]) for x in upper))
    return mf, mb, kf, kb, upper_fb


def _hg_gate(z, lb):
    key = (1.0 - lb) * jax.nn.sigmoid(-z)
    return key, jnp.log(1.0 - key)


def _hg_state_mask():
    r = lax.broadcasted_iota(jnp.int32, (LANES, 2 * HG_DK), 0) // HG_DV
    c = lax.broadcasted_iota(jnp.int32, (LANES, 2 * HG_DK), 1) // HG_DK
    return r == c


def _hg_bwd_state_kernel(z_ref, v_ref, lb_ref, m_ref, sb_ref, st_ref):
    L = HG_BLOCK
    nblk = z_ref.shape[0] // L

    @pl.when(pl.program_id(1) == 0)
    def _():
        st_ref[...] = jnp.zeros_like(st_ref)

    bmt = _hg_state_mask()
    for blk in range(nblk - 1, -1, -1):
        rows = slice(blk * L, (blk + 1) * L)
        key, lf = _hg_gate(z_ref[rows, :].astype(F32), lb_ref[...])
        e = _dot_split(m_ref[...], lf)
        ke = key * jnp.exp(e[:L])
        dec = jnp.exp(e[L:L + 1])
        for p in range(2):
            sb_ref[0, blk, p] = st_ref[p]
            kvt = _dot_tn(v_ref[rows, 128 * p:128 * p + 128].astype(BF16),
                          ke[:, 256 * p:256 * p + 256].astype(BF16))
            st_ref[p] = dec[:, 256 * p:256 * p + 256] * st_ref[p] + jnp.where(bmt, kvt, 0.0)


def _hg_scale_blocks(jobs, qp_ref, kp_ref, fac_ref, nblocks):
    for u in range(0, nblocks, 2):
        r0 = slice(HG_BASE * u, HG_BASE * u + HG_BASE)
        r1 = slice(HG_BASE * (u + 1), HG_BASE * (u + 2))
        for hd in range(HG_HEADS):
            ls = slice(HG_DK * hd, HG_DK * hd + HG_DK)
            q0, k0, q1, k1 = qp_ref[r0, ls], kp_ref[r0, ls], qp_ref[r1, ls], kp_ref[r1, ls]
            for dst_ref, row0, flags in jobs:
                a = (q0 if flags[u] else k0) * fac_ref[row0 + u:row0 + u + 1, ls]
                b = (q1 if flags[u + 1] else k1) * fac_ref[row0 + u + 1:row0 + u + 2, ls]
                dst_ref[HG_BASE * u:HG_BASE * (u + 2), ls] = jnp.concatenate([a, b], axis=0).astype(dst_ref.dtype)


def _hg_main_kernel(h_ref, lb_ref, nw_ref, mf_ref, mb_ref, kf_ref, kb_ref, sb_ref, o_ref,
                    st_ref, e2_ref, qp2_ref, kp2_ref, fac2_ref, s_ref, *, upper):
    L = h_ref.shape[0]
    nb = L // HG_BASE
    nlev = kf_ref.shape[0] + 1
    all_up = (True,) * nb

    @pl.when(pl.program_id(1) == 0)
    def _():
        st_ref[...] = jnp.zeros_like(st_ref)

    lo_v = _lane_lo_mask((L, LANES))
    bmt = _hg_state_mask()
    bones = jnp.where(_head_block_ones(), 1.0, 0.0).astype(BF16)
    osum = [jnp.zeros((L, LANES), F32), jnp.zeros((L, LANES), F32)]
    nbuf = nlev + 1
    for d in range(2):
        m_ref = mf_ref if d == 0 else mb_ref
        k_ref = kf_ref if d == 0 else kb_ref
        e_ref, qp_ref, kp_ref, fac_ref = e2_ref.at[d], qp2_ref.at[d], kp2_ref.at[d], fac2_ref.at[d]
        sbuf = [s_ref.at[d * nbuf + k] for k in range(nbuf)]
        key, lf = _hg_gate(h_ref[:, 512 + 512 * d:1024 + 512 * d].astype(F32), lb_ref[...])
        e_ref[...] = _dot_split(m_ref[...], lf)
        eb = e_ref[pl.ds(0, L), :]
        qp_ref[...] = h_ref[:, 0:512].astype(F32) * jnp.exp(eb)
        kp_ref[...] = key * jnp.exp(-eb)
        fac_ref[...] = jnp.exp(e_ref[pl.ds(L, (nlev + 1) * nb), :])
        jobs = [(sbuf[lv - 1], (1 + lv) * nb, upper[d][lv - 1]) for lv in range(1, nlev)]
        jobs.append((sbuf[nlev - 1], 0, all_up))
        if d == 0:
            jobs.append((sbuf[nlev], nb, (False,) * nb))
        _hg_scale_blocks(jobs, qp_ref, kp_ref, fac_ref, nb)
        H = L // 2
        halves = (slice(0, H), slice(H, L))
        qh, kh = (1, 0) if d == 0 else (0, 1)
        masks = [k_ref[lv] > 0.5 for lv in range(nlev - 1)]

        def tiles(ref, hbs):
            return jnp.stack([ref[halves[hb], HG_DK * hd:HG_DK * hd + HG_DK]
                              for hd in range(HG_HEADS) for hb in hbs], axis=0).astype(BF16)

        def bdot(a, b):
            return lax.dot_general(a, b, (((2,), (2,)), ((0,), (0,))), preferred_element_type=F32)

        diag_all = jnp.where(masks[0][None], bdot(tiles(qp_ref, (0, 1)), tiles(kp_ref, (0, 1))), 0.0)
        for lv in range(2, nlev):
            t = tiles(sbuf[lv - 1], (0, 1))
            diag_all = jnp.where(masks[lv - 1][None], bdot(t, t), diag_all)
        diag_all = diag_all.astype(BF16)
        cross_all = bdot(tiles(sbuf[0], (qh,)), tiles(sbuf[0], (kh,))).astype(BF16)
        for p in range(2):
            v = h_ref[:, 1536 + 128 * p:1664 + 128 * p].astype(F32)
            vlo = jnp.where(lo_v, v, 0.0).astype(BF16)
            vhi = jnp.where(lo_v, 0.0, v).astype(BF16)
            qe_pair = []
            ke_pair = []
            for hh in range(2):
                ls = slice(HG_DK * (2 * p + hh), HG_DK * (2 * p + hh) + HG_DK)
                qe_pair.append(sbuf[nlev - 1][:, ls])
                if d == 0:
                    ke_pair.append(sbuf[nlev][:, ls])
            dg = [[diag_all[2 * (2 * p + hh) + hb] for hb in range(2)] for hh in range(2)]
            cross = [cross_all[2 * p + hh] for hh in range(2)]
            vs = jnp.concatenate([vlo, vhi], axis=0)
            if d == 0:
                o_q = _dot(jnp.concatenate([cross[0], dg[0][1], cross[1], dg[1][1]], axis=1), vs)
                o_k = _dot(jnp.concatenate([dg[0][0], dg[1][0]], axis=1),
                           jnp.concatenate([vlo[halves[0]], vhi[halves[0]]], axis=0))
                intra = jnp.concatenate([o_k, o_q], axis=0)
            else:
                o_q = _dot(jnp.concatenate([dg[0][0], cross[0], dg[1][0], cross[1]], axis=1), vs)
                o_k = _dot(jnp.concatenate([dg[0][1], dg[1][1]], axis=1),
                           jnp.concatenate([vlo[halves[1]], vhi[halves[1]]], axis=0))
                intra = jnp.concatenate([o_q, o_k], axis=0)
            state = st_ref[p] if d == 0 else sb_ref[0, 0, p]
            inter = _dot_nt(jnp.concatenate(qe_pair, axis=1), state.astype(BF16))
            osum[p] = osum[p] + intra + inter
            if d == 0:
                ps = slice(256 * p, 256 * p + 256)
                dec = fac_ref[nb - 1:nb, ps] * fac_ref[2 * nb - 1:2 * nb, ps]
                kvt = _dot_tn(v.astype(BF16), jnp.concatenate(ke_pair, axis=1))
                st_ref[p] = dec * st_ref[p] + jnp.where(bmt, kvt, 0.0)
    for p in range(2):
        o = osum[p]
        ms = _dot_split_rhs_const(o * o, bones) * (1.0 / HG_DV)
        g = h_ref[:, 1792 + 128 * p:1920 + 128 * p].astype(F32)
        y = o * lax.rsqrt(ms + RMS_EPS) * nw_ref[:, 128 * p:128 * p + 128] * jax.nn.sigmoid(g)
        o_ref[:, 128 * p:128 * p + 128] = y.astype(o_ref.dtype)


def _hgrn2(h, lb, nw, consts, B, S):
    L = HG_BLOCK
    nb = S // L
    mf, mb, kf, kb, mpre, upper = consts
    nsec = mf.shape[0]
    nfac = nsec - L
    sb = pl.pallas_call(
        _hg_bwd_state_kernel,
        grid=(B, nb // HG_PRE_BLOCKS),
        in_specs=[pl.BlockSpec((HG_PRE_BLOCKS * L, 512), lambda b, n: ((b * nb + nb) // HG_PRE_BLOCKS - 1 - n, 2)),
                  pl.BlockSpec((HG_PRE_BLOCKS * L, 256), lambda b, n: ((b * nb + nb) // HG_PRE_BLOCKS - 1 - n, 6)),
                  pl.BlockSpec((1, 512), lambda b, n: (0, 0)),
                  pl.BlockSpec((L + 8, L), lambda b, n: (0, 0))],
        out_specs=pl.BlockSpec((1, HG_PRE_BLOCKS, 2, LANES, 2 * HG_DK),
                               lambda b, n: (b, nb // HG_PRE_BLOCKS - 1 - n, 0, 0, 0)),
        out_shape=jax.ShapeDtypeStruct((B, nb, 2, LANES, 2 * HG_DK), F32),
        scratch_shapes=[pltpu.VMEM((2, LANES, 2 * HG_DK), F32)],
        compiler_params=_cparams(("parallel", "arbitrary")),
        name="hg_bwd_state",
    )(h, h, lb, mpre)
    return pl.pallas_call(
        functools.partial(_hg_main_kernel, upper=upper),
        grid=(B, nb),
        in_specs=[pl.BlockSpec((L, HG_W), lambda b, n: (b * nb + n, 0)),
                  pl.BlockSpec((1, 512), lambda b, n: (0, 0)),
                  pl.BlockSpec((1, 256), lambda b, n: (0, 0)),
                  pl.BlockSpec((nsec, L), lambda b, n: (0, 0)),
                  pl.BlockSpec((nsec, L), lambda b, n: (0, 0)),
                  pl.BlockSpec(kf.shape, lambda b, n: (0, 0, 0)),
                  pl.BlockSpec(kb.shape, lambda b, n: (0, 0, 0)),
                  pl.BlockSpec((1, 1, 2, LANES, 2 * HG_DK), lambda b, n: (b, n, 0, 0, 0))],
        out_specs=pl.BlockSpec((L, 256), lambda b, n: (b * nb + n, 0)),
        out_shape=jax.ShapeDtypeStruct((B * S, 256), BF16),
        scratch_shapes=[pltpu.VMEM((2, LANES, 2 * HG_DK), F32),
                        pltpu.VMEM((2, nsec, 512), F32),
                        pltpu.VMEM((2, L, 512), F32), pltpu.VMEM((2, L, 512), F32),
                        pltpu.VMEM((2, nfac, 512), F32),
                        pltpu.VMEM((2 * (kf.shape[0] + 2), L, 512), BF16)],
        compiler_params=_cparams(("parallel", "arbitrary")),
        name="hg_main",
    )(h, lb, nw, mf, mb, kf, kb, sb)


def _outproj_kernel(x_ref, r_ref, s_ref, g_ref, wr_ref, ws_ref, wg_ref, lg_ref, lb_ref, o_ref):
    mix = _dot(r_ref[...], wr_ref[...]) + _dot(s_ref[...], ws_ref[...]) + _dot(g_ref[...], wg_ref[...])
    o_ref[...] = _layer_norm(DEEPNORM_ALPHA * x_ref[...] + mix, lg_ref[...], lb_ref[...])


ROUTER_ROWS = 16


def _outproj_router_kernel(x_ref, r_ref, s_ref, g_ref, wr_ref, ws_ref, wg_ref, lg_ref, lb_ref, wrt_ref, tri_ref,
                           o_ref, ot_ref, route_ref, cnt_ref):
    @pl.when(pl.program_id(0) == 0)
    def _():
        cnt_ref[...] = jnp.zeros_like(cnt_ref)

    mix = _dot(r_ref[...], wr_ref[...]) + _dot(s_ref[...], ws_ref[...]) + _dot(g_ref[...], wg_ref[...])
    y = _layer_norm(DEEPNORM_ALPHA * x_ref[...] + mix, lg_ref[...], lb_ref[...])
    o_ref[...] = y
    ot_ref[...] = _to_token_tiles(y)
    tm = y.shape[0]
    w = wrt_ref[...]
    w_hi = w.astype(BF16)
    w_lo = (w - w_hi.astype(F32)).astype(BF16)
    y_hi = y.astype(BF16)
    y_lo = (y - y_hi.astype(F32)).astype(BF16)
    lt = _dot_nt(jnp.concatenate([w_hi, w_lo], axis=0), y_hi)
    logits = lt[:ROUTER_ROWS] + lt[ROUTER_ROWS:] + _dot_nt(w_hi, y_lo)
    row = lax.broadcasted_iota(jnp.int32, logits.shape, 0).astype(F32)
    neg = jnp.float32(-jnp.inf)
    l1 = jnp.where(row < N_EXPERTS, logits, neg)
    m1 = jnp.max(l1, axis=0, keepdims=True)
    i1 = jnp.min(jnp.where(l1 == m1, row, float(ROUTER_ROWS)), axis=0, keepdims=True)
    l2 = jnp.where(row == i1, neg, l1)
    m2 = jnp.max(l2, axis=0, keepdims=True)
    i2 = jnp.min(jnp.where(l2 == m2, row, float(ROUTER_ROWS)), axis=0, keepdims=True)
    e2 = jnp.exp(m2 - m1)
    den = 1.0 + e2
    picked = jnp.where((row == i1) | (row == i2), 1.0, 0.0)
    before = _dot(picked.astype(BF16), tri_ref[...])
    rank = before + cnt_ref[:, 0:1]
    r1 = jnp.sum(jnp.where(row == i1, rank, 0.0), axis=0, keepdims=True)
    r2 = jnp.sum(jnp.where(row == i2, rank, 0.0), axis=0, keepdims=True)
    cnt_ref[...] = cnt_ref[...] + jnp.sum(picked, axis=1, keepdims=True)
    lrow = lax.broadcasted_iota(jnp.int32, (8, tm), 0)
    route8 = (jnp.where(lrow == 0, i1, 0.0) + jnp.where(lrow == 1, i2, 0.0)
              + jnp.where(lrow == 2, 1.0 / den, 0.0) + jnp.where(lrow == 3, e2 / den, 0.0)
              + jnp.where(lrow == 4, r1, 0.0) + jnp.where(lrow == 5, r2, 0.0))
    route_ref[...] = jnp.concatenate([route8, jnp.zeros((LANES - 8, tm), F32)], axis=0).T


def _out_proj(x2d, ret, swa, hgo, w_r, w_s, w_g, ln_g, ln_b, w_router=None):
    T = x2d.shape[0]
    tm = 1024
    row = lambda i: (i, 0)
    full = lambda i: (0, 0)
    in_specs = [pl.BlockSpec((tm, D_MODEL), row),
                pl.BlockSpec((tm, 256), row), pl.BlockSpec((tm, 512), row), pl.BlockSpec((tm, 256), row),
                pl.BlockSpec((256, D_MODEL), full), pl.BlockSpec((512, D_MODEL), full),
                pl.BlockSpec((256, D_MODEL), full),
                pl.BlockSpec((1, D_MODEL), full), pl.BlockSpec((1, D_MODEL), full)]
    args = [x2d, ret, swa, hgo, w_r, w_s, w_g, ln_g, ln_b]
    if w_router is None:
        return pl.pallas_call(
            _outproj_kernel, grid=(T // tm,), in_specs=in_specs,
            out_specs=pl.BlockSpec((tm, D_MODEL), row),
            out_shape=jax.ShapeDtypeStruct((T, D_MODEL), F32),
            compiler_params=_cparams(("parallel",)), name="out_proj",
        )(*args)
    return pl.pallas_call(
        _outproj_router_kernel, grid=(T // tm,),
        in_specs=in_specs + [pl.BlockSpec((ROUTER_ROWS, D_MODEL), full),
                             pl.BlockSpec((tm, tm), full, pipeline_mode=pl.Buffered(1))],
        out_specs=[pl.BlockSpec((tm, D_MODEL), row), pl.BlockSpec((tm, 8, LANES), lambda i: (i, 0, 0)),
                   pl.BlockSpec((tm, LANES), row), pl.BlockSpec((ROUTER_ROWS, LANES), full)],
        out_shape=[jax.ShapeDtypeStruct((T, D_MODEL), F32), jax.ShapeDtypeStruct((T, 8, LANES), F32),
                   jax.ShapeDtypeStruct((T, LANES), F32), jax.ShapeDtypeStruct((ROUTER_ROWS, LANES), F32)],
        compiler_params=_cparams(("arbitrary",)), name="out_proj_router",
    )(*args, w_router, jnp.asarray(np.triu(np.ones((tm, tm), np.float32), 1), BF16))


FFN_CHUNKS = 1


def _ffn_kernel(x_ref, wg_ref, wu_ref, wd_ref, lg_ref, lb_ref, o_ref):
    x = x_ref[...]
    xb = x.astype(BF16)
    th = FFN_DIM // FFN_CHUNKS
    f = jnp.zeros(x.shape, F32)
    for c in range(FFN_CHUNKS):
        g = _dot(xb, wg_ref[:, c * th:(c + 1) * th])
        u = _dot(xb, wu_ref[:, c * th:(c + 1) * th])
        a = (g * jax.nn.sigmoid(g)) * u
        f = f + _dot(a.astype(BF16), wd_ref[c * th:(c + 1) * th, :])
    o_ref[...] = _layer_norm(DEEPNORM_ALPHA * x + f, lg_ref[...], lb_ref[...])


def _ffn(x2d, wg, wu, wd, ln_g, ln_b):
    T = x2d.shape[0]
    tm = 512
    once = pl.Buffered(1)
    return pl.pallas_call(
        _ffn_kernel, grid=(T // tm,),
        in_specs=[pl.BlockSpec((tm, D_MODEL), lambda i: (i, 0)),
                  pl.BlockSpec((D_MODEL, FFN_DIM), lambda i: (0, 0), pipeline_mode=once),
                  pl.BlockSpec((D_MODEL, FFN_DIM), lambda i: (0, 0), pipeline_mode=once),
                  pl.BlockSpec((FFN_DIM, D_MODEL), lambda i: (0, 0), pipeline_mode=once),
                  pl.BlockSpec((1, D_MODEL), lambda i: (0, 0)),
                  pl.BlockSpec((1, D_MODEL), lambda i: (0, 0))],
        out_specs=pl.BlockSpec((tm, D_MODEL), lambda i: (i, 0)),
        out_shape=jax.ShapeDtypeStruct((T, D_MODEL), F32),
        compiler_params=_cparams(("parallel",)), name="ffn",
    )(x2d, wg, wu, wd, ln_g, ln_b)


MOE_TILE = 512
MOE_TH = 1792
MOE_TOK = 512


def _moe_num_tiles(T):
    return (2 * T) // MOE_TILE + N_EXPERTS + 1


def _moe_tables(route, counts):
    T = route.shape[0]
    tm = MOE_TILE
    cnt = counts[:N_EXPERTS, 0].astype(jnp.int32)
    pc = ((cnt + tm - 1) // tm) * tm
    pend = jnp.cumsum(pc)
    pstart = pend - pc
    e = route[:, 0:2].astype(jnp.int32)
    rank = route[:, 4:6].astype(jnp.int32)
    sel = e[:, :, None] == jnp.arange(N_EXPERTS, dtype=jnp.int32)[None, None, :]
    pos = jnp.sum(jnp.where(sel, pstart[None, None, :], 0), axis=-1) + rank
    n_tiles = _moe_num_tiles(T)
    tile_row = jnp.arange(n_tiles, dtype=jnp.int32) * tm
    texp = jnp.minimum(jnp.sum((tile_row[:, None] >= pend[None, :]).astype(jnp.int32), axis=1), N_EXPERTS - 1)
    n_used = pend[-1] // tm
    last = jnp.sum(jnp.where(jnp.arange(n_tiles) == n_used - 1, texp, 0))
    texp = jnp.where(jnp.arange(n_tiles) < n_used, texp, last)
    return pos.reshape(-1), texp, n_used.reshape(1), pstart + cnt


def _to_token_tiles(x):
    cols = jnp.stack([x[:, LANES * c:LANES * c + LANES] for c in range(D_MODEL // LANES)], axis=0)
    return pltpu.einshape("csl->scl", cols)


def _from_token_tiles(xt):
    y = pltpu.einshape("scl->csl", xt)
    return jnp.concatenate([y[c] for c in range(D_MODEL // LANES)], axis=1)


MOE_DMA_UNROLL = 8


def _moe_dispatch_kernel(pos_ref, pad_ref, nused_ref, xt_hbm, xs_hbm, zero_ref, buf_ref, fsem, ssem, zsem):
    i = pl.program_id(0)
    n = pl.num_programs(0)
    tm = MOE_TOK
    n_tiles = xs_hbm.shape[0] // MOE_TILE
    slot = i % 3

    def zero_copy(row):
        return pltpu.make_async_copy(zero_ref, xs_hbm.at[pl.ds(row, MOE_TILE)], zsem)

    def fetch(step, s):
        return pltpu.make_async_copy(xt_hbm.at[pl.ds(step * tm, tm)], buf_ref.at[s], fsem.at[s])

    def drain_copies(s):
        for _ in range(2):
            pltpu.make_async_copy(buf_ref.at[s], xs_hbm.at[pl.ds(0, tm)], ssem.at[s]).wait()

    @pl.when(i == 0)
    def _():
        fetch(0, 0).start()
        zero_ref[...] = jnp.zeros_like(zero_ref)
        for e in range(N_EXPERTS):
            zero_copy(pad_ref[e]).start()
        for e in range(N_EXPERTS):
            zero_copy(pad_ref[e]).wait()
        for t in range(n_tiles - N_EXPERTS - 1, n_tiles):
            @pl.when(t >= nused_ref[0])
            def _():
                zero_copy(t * MOE_TILE).start()
                zero_copy(t * MOE_TILE).wait()

    @pl.when(i >= 2)
    def _():
        drain_copies((i + 1) % 3)

    @pl.when(i + 1 < n)
    def _():
        fetch(i + 1, (i + 1) % 3).start()

    fetch(i, slot).wait()

    def body(kk, c):
        k0 = kk * MOE_DMA_UNROLL
        t0 = i * tm + k0
        dst = [pos_ref[2 * t0 + u] for u in range(2 * MOE_DMA_UNROLL)]
        for u in range(2 * MOE_DMA_UNROLL):
            pltpu.make_async_copy(buf_ref.at[slot, k0 + u // 2], xs_hbm.at[dst[u]], ssem.at[slot]).start()
        return c

    lax.fori_loop(0, tm // MOE_DMA_UNROLL, body, 0)

    @pl.when(i == n - 1)
    def _():
        @pl.when(i >= 1)
        def _():
            drain_copies((i - 1) % 3)
        drain_copies(slot)


def _moe_dispatch(x1t, pos, padstart, n_used):
    T = x1t.shape[0]
    P = _moe_num_tiles(T) * MOE_TILE
    return pl.pallas_call(
        _moe_dispatch_kernel,
        grid_spec=pltpu.PrefetchScalarGridSpec(
            num_scalar_prefetch=3, grid=(T // MOE_TOK,),
            in_specs=[pl.BlockSpec(memory_space=pl.ANY)],
            out_specs=pl.BlockSpec(memory_space=pl.ANY),
            scratch_shapes=[pltpu.VMEM((MOE_TILE, 8, LANES), F32), pltpu.VMEM((3, MOE_TOK, 8, LANES), F32),
                            pltpu.SemaphoreType.DMA((3,)), pltpu.SemaphoreType.DMA((3,)),
                            pltpu.SemaphoreType.DMA]),
        out_shape=jax.ShapeDtypeStruct((P, 8, LANES), F32),
        compiler_params=_cparams(("arbitrary",)), name="moe_dispatch",
    )(pos, padstart, n_used, x1t)


def _moe_group_kernel(texp_ref, nused_ref, xs_ref, wg_ref, wu_ref, wd_ref, y_ref, xb_ref, acc_ref):
    i = pl.program_id(0)
    j = pl.program_id(1)

    @pl.when(i < nused_ref[0])
    def _():
        @pl.when(j == 0)
        def _():
            acc_ref[...] = jnp.zeros_like(acc_ref)

        xb = _from_token_tiles(xs_ref[...]).astype(BF16)
        g = _dot(xb, wg_ref[...])
        u = _dot(xb, wu_ref[...])
        a = (g * jax.nn.sigmoid(g)) * u
        acc_ref[...] += _dot(a.astype(BF16), wd_ref[...])

        @pl.when(j == pl.num_programs(1) - 1)
        def _():
            y_ref[...] = _to_token_tiles(acc_ref[...])

    @pl.when((i >= nused_ref[0]) & (j == 0))
    def _():
        y_ref[...] = jnp.zeros_like(y_ref)


def _moe_group(xs, texp, n_used, wg, wu, wd):
    P = xs.shape[0]
    tm, th = MOE_TILE, MOE_TH
    nh = EXPERT_DIM // th

    def row_map(i, j, texp, nused):
        return (jnp.minimum(i, nused[0] - 1), 0, 0)

    def col(i, j, nused):
        return jnp.where(i < nused[0], j, nh - 1)

    return pl.pallas_call(
        _moe_group_kernel,
        grid_spec=pltpu.PrefetchScalarGridSpec(
            num_scalar_prefetch=2, grid=(P // tm, nh),
            in_specs=[pl.BlockSpec((tm, 8, LANES), row_map),
                      pl.BlockSpec((None, D_MODEL, th), lambda i, j, texp, nused: (texp[i], 0, col(i, j, nused))),
                      pl.BlockSpec((None, D_MODEL, th), lambda i, j, texp, nused: (texp[i], 0, col(i, j, nused))),
                      pl.BlockSpec((None, th, D_MODEL), lambda i, j, texp, nused: (texp[i], col(i, j, nused), 0))],
            out_specs=pl.BlockSpec((tm, 8, LANES), lambda i, j, texp, nused: (i, 0, 0)),
            scratch_shapes=[pltpu.VMEM((tm, D_MODEL), BF16), pltpu.VMEM((tm, D_MODEL), F32)]),
        out_shape=jax.ShapeDtypeStruct((P, 8, LANES), F32),
        compiler_params=_cparams(("arbitrary", "arbitrary")), name="moe_group",
    )(texp, n_used, xs, wg, wu, wd)


def _moe_combine_kernel(pos_ref, x_ref, route_ref, y_hbm, lg_ref, lb_ref, o_ref, buf_ref, sem):
    i = pl.program_id(0)
    tm = x_ref.shape[0]

    def issue(tile, slot):
        def body(kk, c):
            k0 = kk * MOE_DMA_UNROLL
            t0 = tile * tm + k0
            src = [pos_ref[2 * t0 + u] for u in range(2 * MOE_DMA_UNROLL)]
            for u in range(2 * MOE_DMA_UNROLL):
                pltpu.make_async_copy(y_hbm.at[src[u]], buf_ref.at[slot, (u % 2) * tm + k0 + u // 2],
                                      sem.at[slot]).start()
            return c
        lax.fori_loop(0, tm // MOE_DMA_UNROLL, body, 0)

    @pl.when(i == 0)
    def _():
        issue(0, 0)

    slot = i % 2
    pltpu.make_async_copy(y_hbm.at[pl.ds(0, 2 * tm)], buf_ref.at[slot], sem.at[slot]).wait()

    @pl.when(i + 1 < pl.num_programs(0))
    def _():
        issue(i + 1, 1 - slot)

    a = _from_token_tiles(buf_ref[slot, pl.ds(0, tm)])
    b = _from_token_tiles(buf_ref[slot, pl.ds(tm, tm)])
    f = route_ref[:, 2:3] * a + route_ref[:, 3:4] * b
    o_ref[...] = _layer_norm(DEEPNORM_ALPHA * x_ref[...] + f, lg_ref[...], lb_ref[...])


def _moe_combine(x1, route, y, pos, ln_g, ln_b):
    T = x1.shape[0]
    tm = MOE_TOK
    return pl.pallas_call(
        _moe_combine_kernel,
        grid_spec=pltpu.PrefetchScalarGridSpec(
            num_scalar_prefetch=1, grid=(T // tm,),
            in_specs=[pl.BlockSpec((tm, D_MODEL), lambda i, pos: (i, 0)),
                      pl.BlockSpec((tm, LANES), lambda i, pos: (i, 0)),
                      pl.BlockSpec(memory_space=pl.ANY),
                      pl.BlockSpec((1, D_MODEL), lambda i, pos: (0, 0)),
                      pl.BlockSpec((1, D_MODEL), lambda i, pos: (0, 0))],
            out_specs=pl.BlockSpec((tm, D_MODEL), lambda i, pos: (i, 0)),
            scratch_shapes=[pltpu.VMEM((2, 2 * tm, 8, LANES), F32), pltpu.SemaphoreType.DMA((2,))]),
        out_shape=jax.ShapeDtypeStruct((T, D_MODEL), F32),
        compiler_params=_cparams(("arbitrary",)), name="moe_combine",
    )(pos, x1, route, y, ln_g, ln_b)


def _moe(x1, x1t, route, counts, wg, wu, wd, ln_g, ln_b):
    pos, texp, n_used, padstart = _moe_tables(route, counts)
    xs = _moe_dispatch(x1t, pos, padstart, n_used)
    y = _moe_group(xs, texp, n_used, wg, wu, wd)
    return _moe_combine(x1, route, y, pos, ln_g, ln_b)


def _t5_bucket(rel):
    nb = REL_BUCKETS // 2
    ret = (rel > 0).astype(np.int32) * nb
    n = np.abs(rel)
    max_exact = nb // 2
    large = max_exact + (np.log(np.maximum(n, 1) / max_exact) / np.log(REL_MAX_DIST / max_exact)
                         * (nb - max_exact)).astype(np.int32)
    large = np.minimum(large, nb - 1)
    return ret + np.where(n < max_exact, n, large)


def _swa_head_order():
    return [h for c in range(4) for h in (c, 4 + c)]


def _prepare(w_in, w_out, ret_decay, swa_sink, rel_bias, hg_lb, hg_norm_w, moe_router, seq_lens):
    order = _swa_head_order()
    w_in_p = jnp.concatenate(
        [w_in[:, :, 1792:3840], w_in[:, :, 0:1024]]
        + [w_in[:, :, 1024 + 64 * h:1088 + 64 * h] for h in order]
        + [w_in[:, :, 1536:1792]], axis=2).astype(BF16)
    w_r = w_out[:, 0:256, :].astype(BF16)
    w_s = jnp.concatenate([w_out[:, 256 + 64 * h:320 + 64 * h, :] for h in order], axis=1).astype(BF16)
    w_g = w_out[:, 768:1024, :].astype(BF16)

    rd = ret_decay.astype(F32)
    rdp = jnp.repeat(rd.reshape(DEPTH, 4, 2), 64, axis=-1)
    rdp = jnp.concatenate([rdp, rdp], axis=1)
    rdh = jnp.broadcast_to(rd.reshape(DEPTH, 8, 1), (DEPTH, 8, LANES))

    W = SWA_BLOCK
    rel = np.arange(3 * W)[None, :] - W - np.arange(W)[:, None]
    onehot = jnp.asarray(_t5_bucket(rel)[..., None] == np.arange(REL_BUCKETS), F32)
    bias = jnp.einsum("qkb,bh->hqk", onehot, rel_bias.astype(F32), precision=lax.Precision.HIGHEST)
    band = jnp.asarray(np.abs(rel) <= WINDOW)
    bias_kq = jnp.transpose(jnp.where(band[None], bias, MASK_VALUE), (0, 2, 1))
    order = _swa_head_order()
    bias_t = jnp.concatenate([bias_kq[h] for h in order], axis=1)
    sink = swa_sink.astype(F32)
    sink_t = jnp.concatenate([jnp.broadcast_to(sink[:, h, None, None], (DEPTH, 8, W)) for h in order], axis=2)

    p = jax.nn.softmax(hg_lb.astype(F32), axis=0)
    lower = (jnp.cumsum(p, axis=0) - p[0]).reshape(DEPTH, 1, HG_HEADS * HG_DK)
    nw = hg_norm_w.astype(F32).reshape(DEPTH, 1, HG_HEADS * HG_DV)

    mf, mb, kf, kb, upper = _hg_constants(HG_BLOCK)
    L = HG_BLOCK
    mpre = np.concatenate([np.tril(np.ones((L, L), np.float32), -1), np.ones((8, L), np.float32)], axis=0)
    consts = (jnp.asarray(mf, BF16), jnp.asarray(mb, BF16), jnp.asarray(kf, F32), jnp.asarray(kb, F32),
              jnp.asarray(mpre, BF16), upper)

    half = RET_DK // 2
    inv = ROPE_BASE ** (-np.arange(half, dtype=np.float32) / half)
    rope = {}
    for S in set(seq_lens):
        ang = jnp.arange(S, dtype=F32)[:, None] * jnp.asarray(inv)[None, :]
        c32, s32 = jnp.cos(ang), jnp.sin(ang)
        rope[S] = (jnp.tile(c32, (1, 4)), jnp.tile(jnp.concatenate([-s32, s32], axis=1), (1, 2)))

    router = jnp.pad(jnp.transpose(moe_router.astype(F32), (0, 2, 1)),
                     ((0, 0), (0, ROUTER_ROWS - N_EXPERTS), (0, 0)))
    return dict(w_in=w_in_p, w_r=w_r, w_s=w_s, w_g=w_g, rdp=rdp, rdh=rdh, bias=bias_t, sink=sink_t,
                lower=lower, nw=nw, consts=consts, rope=rope, router=router)


def _trunk(x, prm, ln1_g, ln1_b, ln2_g, ln2_b, ffn_w, moe_w):
    B, S, _ = x.shape
    x2d = x.reshape(B * S, D_MODEL)
    cos_t, sin_t = prm["rope"][S]
    for l in range(DEPTH):
        h = _in_proj(x2d, prm["w_in"][l])
        ret = _retention(h, cos_t, sin_t, prm["rdp"][l], prm["rdh"][l], B, S)
        swa = _swa(h, prm["bias"], prm["sink"][l], B, S)
        hgo = _hgrn2(h, prm["lower"][l], prm["nw"][l], prm["consts"], B, S)
        g1, b1 = ln1_g[l].reshape(1, D_MODEL), ln1_b[l].reshape(1, D_MODEL)
        g2, b2 = ln2_g[l].reshape(1, D_MODEL), ln2_b[l].reshape(1, D_MODEL)
        i = l // 2
        if l % 2 == 0:
            x1 = _out_proj(x2d, ret, swa, hgo, prm["w_r"][l], prm["w_s"][l], prm["w_g"][l], g1, b1)
            wg, wu, wd = ffn_w
            x2d = _ffn(x1, wg[i], wu[i], wd[i], g2, b2)
        else:
            x1, x1t, route, counts = _out_proj(x2d, ret, swa, hgo, prm["w_r"][l], prm["w_s"][l], prm["w_g"][l],
                                               g1, b1, prm["router"][i])
            wg, wu, wd = moe_w
            x2d = _moe(x1, x1t, route, counts, wg[i], wu[i], wd[i], g2, b2)
    return x2d.reshape(B, S, D_MODEL)


def kernel(x_prompt, x_sample, w_in, w_out, ret_decay, swa_sink, rel_bias, hg_lb, hg_norm_w, ln1_g, ln1_b,
           ln2_g, ln2_b, ffn_w_gate, ffn_w_up, ffn_w_down, moe_router, moe_w_gate, moe_w_up, moe_w_down):
    prm = _prepare(w_in, w_out, ret_decay, swa_sink, rel_bias, hg_lb, hg_norm_w, moe_router,
                   (x_prompt.shape[1], x_sample.shape[1]))
    ffn_w = (ffn_w_gate.astype(BF16), ffn_w_up.astype(BF16), ffn_w_down.astype(BF16))
    moe_w = (moe_w_gate.astype(BF16), moe_w_up.astype(BF16), moe_w_down.astype(BF16))
    outs = []
    for x in (x_prompt, x_sample):
        outs.append(_trunk(x, prm, ln1_g, ln1_b, ln2_g, ln2_b, ffn_w, moe_w))
    return tuple(outs)
```

```python
import functools

import numpy as np
import jax
import jax.numpy as jnp
from jax import lax
from jax.experimental import pallas as pl
from jax.experimental.pallas import tpu as pltpu

F32 = jnp.float32
BF16 = jnp.bfloat16

D_MODEL = 1024
DEPTH = 2
RET_HEADS = 4
RET_DK = 64
RET_CHUNK = 128
ROPE_BASE = 10000.0
SWA_Q_HEADS = 8
SWA_KV_HEADS = 2
SWA_DH = 64
WINDOW = 128
SWA_BLOCK = 128
REL_BUCKETS = 32
REL_MAX_DIST = 128
MASK_VALUE = -1e30
LOG2_E = 1.4426950408889634
HG_HEADS = 4
HG_DK = 128
HG_DV = 64
FFN_DIM = 2816
N_EXPERTS = 8
EXPERT_DIM = 3584
LN_EPS = 1e-5
RMS_EPS = 1e-6
DEEPNORM_ALPHA = (2.0 * DEPTH) ** 0.25

LANES = 128
VMEM_LIMIT = 56 * 1024 * 1024

HG_W = 2048
RET_W = 1024
SWA_W = 768
IN_COLS = HG_W + RET_W + SWA_W

HG_BLOCK = 256
HG_BASE = 8
HG_PRE_BLOCKS = 4
RET_TILE = 1024
SWA_TILE = 1024


def _cparams(sem):
    return pltpu.CompilerParams(dimension_semantics=sem, vmem_limit_bytes=VMEM_LIMIT)


def _dot(a, b):
    return jnp.dot(a, b, preferred_element_type=F32)


def _dot_nt(a, b):
    return lax.dot_general(a, b, (((1,), (1,)), ((), ())), preferred_element_type=F32)


def _dot_tn(a, b):
    return lax.dot_general(a, b, (((0,), (0,)), ((), ())), preferred_element_type=F32)


def _dot_split(a_bf16, x):
    hi = x.astype(BF16)
    lo = (x - hi.astype(F32)).astype(BF16)
    return _dot(a_bf16, hi) + _dot(a_bf16, lo)


def _dot_split_rhs_const(x, b_bf16):
    hi = x.astype(BF16)
    lo = (x - hi.astype(F32)).astype(BF16)
    return _dot(hi, b_bf16) + _dot(lo, b_bf16)


def _layer_norm(y, g, b):
    mu = jnp.mean(y, axis=-1, keepdims=True)
    yc = y - mu
    var = jnp.mean(yc * yc, axis=-1, keepdims=True)
    return yc * lax.rsqrt(var + LN_EPS) * g + b


def _lane_lo_mask(shape):
    return lax.broadcasted_iota(jnp.int32, shape, len(shape) - 1) < (LANES // 2)


def _head_block_ones():
    r = lax.broadcasted_iota(jnp.int32, (LANES, LANES), 0) // 64
    c = lax.broadcasted_iota(jnp.int32, (LANES, LANES), 1) // 64
    return r == c


def _inproj_kernel(x_ref, w_ref, o_ref):
    o_ref[...] = _dot(x_ref[...].astype(BF16), w_ref[...]).astype(o_ref.dtype)


def _in_proj(x2d, w):
    T = x2d.shape[0]
    tm = 1024
    return pl.pallas_call(
        _inproj_kernel,
        grid=(T // tm,),
        in_specs=[pl.BlockSpec((tm, D_MODEL), lambda i: (i, 0)),
                  pl.BlockSpec((D_MODEL, IN_COLS), lambda i: (0, 0), pipeline_mode=pl.Buffered(1))],
        out_specs=pl.BlockSpec((tm, IN_COLS), lambda i: (i, 0)),
        out_shape=jax.ShapeDtypeStruct((T, IN_COLS), BF16),
        compiler_params=_cparams(("parallel",)),
        name="in_proj",
    )(x2d, w)


def _rotary(x, cos, sin_signed, even):
    sw = jnp.where(even, pltpu.roll(x, 96, 1), pltpu.roll(x, 32, 1))
    return x * cos + sw * sin_signed


def _ret_tables(rdp_ref):
    lg = jnp.log1p(-jnp.exp(rdp_ref[...]))
    return lg


def _ret_bwd_state_kernel(h_ref, cos_ref, sin_ref, rdp_ref, sb_ref, st_ref):
    C = RET_CHUNK
    nc = h_ref.shape[0] // C

    @pl.when(pl.program_id(1) == 0)
    def _():
        st_ref[...] = jnp.zeros_like(st_ref)

    lg = _ret_tables(rdp_ref)
    rows = lax.broadcasted_iota(jnp.int32, (C, LANES), 0).astype(F32)
    even = (lax.broadcasted_iota(jnp.int32, (C, LANES), 1) // 32) % 2 == 0
    bm = _head_block_ones()
    for c in range(nc - 1, -1, -1):
        sl = pl.ds(c * C, C)
        cos = cos_ref[sl, :]
        sin = sin_ref[sl, :]
        for p in range(2):
            lgb = lg[2 + p:3 + p, :]
            kwb = jnp.exp(lgb * rows)
            decb = jnp.where(bm, jnp.exp(lgb * float(C)), 0.0)
            k = _rotary(h_ref[sl, 256 + 128 * p:384 + 128 * p].astype(F32), cos, sin, even) * (RET_DK ** -0.5)
            v = h_ref[sl, 512 + 128 * p:640 + 128 * p].astype(F32)
            sb_ref[0, c, p] = st_ref[p]
            kv = _dot_tn((k * kwb).astype(BF16), v.astype(BF16))
            st_ref[p] = decb * st_ref[p] + jnp.where(bm, kv, 0.0)


def _ret_main_kernel(h_ref, cos_ref, sin_ref, rdp_ref, rdh_ref, sb_ref, o_ref, st_ref):
    C = RET_CHUNK
    nc = h_ref.shape[0] // C

    @pl.when(pl.program_id(1) == 0)
    def _():
        st_ref[...] = jnp.zeros_like(st_ref)

    lg = _ret_tables(rdp_ref)
    lgh = jnp.log1p(-jnp.exp(rdh_ref[...]))
    rows = lax.broadcasted_iota(jnp.int32, (C, LANES), 0).astype(F32)
    cols = lax.broadcasted_iota(jnp.int32, (C, LANES), 1).astype(F32)
    rel = rows - cols
    even = (lax.broadcasted_iota(jnp.int32, (C, LANES), 1) // 32) % 2 == 0
    lo = _lane_lo_mask((C, LANES))
    bm = _head_block_ones()
    bones = jnp.where(bm, 1.0, 0.0).astype(BF16)

    dmat = []
    for hd in range(RET_HEADS):
        f = jnp.where(rel >= 0, jnp.exp(lgh[hd:hd + 1, :] * jnp.maximum(rel, 0.0)), 0.0)
        b = jnp.where(rel <= 0, jnp.exp(lgh[4 + hd:5 + hd, :] * jnp.maximum(-rel, 0.0)), 0.0)
        dmat.append(f + b)

    for c in range(nc):
        sl = pl.ds(c * C, C)
        cos = cos_ref[sl, :]
        sin = sin_ref[sl, :]
        for p in range(2):
            lgf = lg[p:p + 1, :]
            lgb = lg[2 + p:3 + p, :]
            q = _rotary(h_ref[sl, 128 * p:128 * p + 128].astype(F32), cos, sin, even)
            k = _rotary(h_ref[sl, 256 + 128 * p:384 + 128 * p].astype(F32), cos, sin, even) * (RET_DK ** -0.5)
            v = h_ref[sl, 512 + 128 * p:640 + 128 * p].astype(F32)
            g = h_ref[sl, 768 + 128 * p:896 + 128 * p].astype(F32)
            kb = k.astype(BF16)
            vb = v.astype(BF16)
            qs = jnp.concatenate([jnp.where(lo, q, 0.0), jnp.where(lo, 0.0, q)], axis=0).astype(BF16)
            s = _dot_nt(qs, kb)
            pm = jnp.concatenate([s[:C] * dmat[2 * p], s[C:] * dmat[2 * p + 1]], axis=1).astype(BF16)
            vs = jnp.concatenate([jnp.where(lo, v, 0.0), jnp.where(lo, 0.0, v)], axis=0).astype(BF16)
            intra = _dot(pm, vs)
            qwf = jnp.exp(lgf * (rows + 1.0))
            qwb = jnp.exp(lgb * (float(C) - rows))
            qi = jnp.concatenate([q * qwf, q * qwb], axis=1).astype(BF16)
            sst = jnp.concatenate([st_ref[p], sb_ref[0, c, p]], axis=0).astype(BF16)
            o = intra + _dot(qi, sst)
            kwf = jnp.exp(lgf * (float(C) - 1.0 - rows))
            decf = jnp.where(bm, jnp.exp(lgf * float(C)), 0.0)
            kv = _dot_tn((k * kwf).astype(BF16), vb)
            st_ref[p] = decf * st_ref[p] + jnp.where(bm, kv, 0.0)
            ms = _dot_split_rhs_const(o * o, bones) * (1.0 / RET_DK)
            y = o * lax.rsqrt(ms + RMS_EPS) * (g * jax.nn.sigmoid(g))
            o_ref[sl, 128 * p:128 * p + 128] = y.astype(o_ref.dtype)


def _retention(h, cos_t, sin_t, rdp, rdh, B, S):
    L = RET_TILE
    nb = S // L
    ncb = L // RET_CHUNK
    n_chunks = S // RET_CHUNK
    colblk = HG_W // RET_W
    sb = pl.pallas_call(
        _ret_bwd_state_kernel,
        grid=(B, nb),
        in_specs=[pl.BlockSpec((L, RET_W), lambda b, n: (b * nb + nb - 1 - n, colblk)),
                  pl.BlockSpec((L, LANES), lambda b, n: (nb - 1 - n, 0)),
                  pl.BlockSpec((L, LANES), lambda b, n: (nb - 1 - n, 0)),
                  pl.BlockSpec((8, LANES), lambda b, n: (0, 0))],
        out_specs=pl.BlockSpec((1, ncb, 2, LANES, LANES), lambda b, n: (b, nb - 1 - n, 0, 0, 0)),
        out_shape=jax.ShapeDtypeStruct((B, n_chunks, 2, LANES, LANES), F32),
        scratch_shapes=[pltpu.VMEM((2, LANES, LANES), F32)],
        compiler_params=_cparams(("parallel", "arbitrary")),
        name="ret_bwd_state",
    )(h, cos_t, sin_t, rdp)
    return pl.pallas_call(
        _ret_main_kernel,
        grid=(B, nb),
        in_specs=[pl.BlockSpec((L, RET_W), lambda b, n: (b * nb + n, colblk)),
                  pl.BlockSpec((L, LANES), lambda b, n: (n, 0)),
                  pl.BlockSpec((L, LANES), lambda b, n: (n, 0)),
                  pl.BlockSpec((8, LANES), lambda b, n: (0, 0)),
                  pl.BlockSpec((8, LANES), lambda b, n: (0, 0)),
                  pl.BlockSpec((1, ncb, 2, LANES, LANES), lambda b, n: (b, n, 0, 0, 0))],
        out_specs=pl.BlockSpec((L, 256), lambda b, n: (b * nb + n, 0)),
        out_shape=jax.ShapeDtypeStruct((B * S, 256), BF16),
        scratch_shapes=[pltpu.VMEM((2, LANES, LANES), F32)],
        compiler_params=_cparams(("parallel", "arbitrary")),
        name="ret_main",
    )(h, cos_t, sin_t, rdp, rdh, sb)


def _swa_kernel(q_ref, kp_ref, kc_ref, kn_ref, vp_ref, vc_ref, vn_ref, bias_ref, sink_ref, o_ref):
    W = SWA_BLOCK
    nq = q_ref.shape[0] // W
    n = pl.program_id(1)
    last = pl.num_programs(1) - 1
    kall = jnp.concatenate([kp_ref[...], kc_ref[...], kn_ref[...]], axis=0).astype(BF16)
    vt = jnp.concatenate([vp_ref[...], vc_ref[...], vn_ref[...]], axis=0).astype(F32).T.astype(BF16)
    lo = _lane_lo_mask((W, LANES))
    edge_first = jnp.where(n > 0, 0.0, MASK_VALUE)
    edge_last = jnp.where(n < last, 0.0, MASK_VALUE)
    sink = sink_ref[0:1, :]
    for r in range(nq):
        qs = []
        for c in range(4):
            q = q_ref[pl.ds(r * W, W), 128 * c:128 * c + 128].astype(F32) * (SWA_DH ** -0.5 * LOG2_E)
            qs += [jnp.where(lo, q, 0.0), jnp.where(lo, 0.0, q)]
        qs = jnp.concatenate(qs, axis=0).astype(BF16)
        s = _dot_nt(kall[r * W:(r + 3) * W], qs) + bias_ref[...]
        if r == 0:
            s = jnp.concatenate([s[:W] + edge_first, s[W:]], axis=0)
        if r == nq - 1:
            s = jnp.concatenate([s[:2 * W], s[2 * W:] + edge_last], axis=0)
        m = jnp.maximum(jnp.max(s, axis=0, keepdims=True), sink)
        e = jnp.exp2(s - m)
        den = jnp.sum(e, axis=0, keepdims=True) + jnp.exp2(sink - m)
        ot = _dot(vt[:, r * W:(r + 3) * W], e.astype(BF16)) / den
        for c in range(4):
            sel = jnp.concatenate([ot[0:SWA_DH, 2 * c * W:(2 * c + 1) * W],
                                   ot[SWA_DH:2 * SWA_DH, (2 * c + 1) * W:(2 * c + 2) * W]], axis=0)
            o_ref[pl.ds(r * W, W), 128 * c:128 * c + 128] = sel.T.astype(o_ref.dtype)


def _swa(h, bias_t, sink_t, B, S):
    L = SWA_TILE
    W = SWA_BLOCK
    nb = S // L
    r = L // W
    nblk = S // W
    qcol = (HG_W + RET_W) // 512
    kcol = (HG_W + RET_W + 512) // LANES
    vcol = kcol + 1

    def prev_map(col):
        return lambda b, n: (b * nblk + jnp.maximum(n * r - 1, 0), col)

    def cur_map(col):
        return lambda b, n: (b * nb + n, col)

    def next_map(col):
        return lambda b, n: (b * nblk + jnp.minimum(n * r + r, nblk - 1), col)

    return pl.pallas_call(
        _swa_kernel,
        grid=(B, nb),
        in_specs=[pl.BlockSpec((L, 512), cur_map(qcol)),
                  pl.BlockSpec((W, LANES), prev_map(kcol)),
                  pl.BlockSpec((L, LANES), cur_map(kcol)),
                  pl.BlockSpec((W, LANES), next_map(kcol)),
                  pl.BlockSpec((W, LANES), prev_map(vcol)),
                  pl.BlockSpec((L, LANES), cur_map(vcol)),
                  pl.BlockSpec((W, LANES), next_map(vcol)),
                  pl.BlockSpec((3 * W, SWA_Q_HEADS * W), lambda b, n: (0, 0)),
                  pl.BlockSpec((8, SWA_Q_HEADS * W), lambda b, n: (0, 0))],
        out_specs=pl.BlockSpec((L, 512), lambda b, n: (b * nb + n, 0)),
        out_shape=jax.ShapeDtypeStruct((B * S, 512), BF16),
        compiler_params=_cparams(("parallel", "parallel")),
        name="swa",
    )(h, h, h, h, h, h, h, bias_t, sink_t)


def _hg_levels(L):
    g = L
    out = []
    while g >= 2 * HG_BASE:
        out.append(g)
        g //= 2
    return out


def _hg_constants(L):
    nb = L // HG_BASE
    idx = np.arange(L)
    t = idx[None, :]
    i = idx[:, None]
    r = HG_BASE * (i // HG_BASE) + HG_BASE // 2 - 1
    secs = [((t > r) & (t <= i)).astype(np.float32) - ((t > i) & (t <= r)).astype(np.float32)]
    u = np.arange(nb)[:, None]
    ru = HG_BASE * u + HG_BASE // 2 - 1
    secs.append((t <= ru).astype(np.float32))
    secs.append((t > ru).astype(np.float32))
    masks = [((idx[:, None] // HG_BASE) == (idx[None, :] // HG_BASE)) & (idx[:, None] >= idx[None, :])]
    upper = []
    for g in _hg_levels(L):
        gb = g // HG_BASE
        ref = HG_BASE * ((u // gb) * gb + gb // 2) - 1
        up = (u % gb) >= gb // 2
        secs.append(np.where(up, (t > ref) & (t <= ru), (t > ru) & (t <= ref)).astype(np.float32))
        upper.append([bool(v) for v in up[:, 0]])
        ii, jj = idx[:, None], idx[None, :]
        masks.append(((ii // g) == (jj // g)) & ((ii % g) >= g // 2) & ((jj % g) < g // 2))
    mf = np.concatenate(secs, axis=0)
    mb = np.concatenate([s[::-1, ::-1] for s in secs], axis=0)
    H = L // 2
    sub = [masks[0][:H, :H]] + [m[:H, :H] for m in masks[2:]]
    kf = np.stack(sub).astype(np.float32)
    kb = np.stack([m[::-1, ::-1] for m in sub]).astype(np.float32)
    upper_fb = (tuple(tuple(x) for x in upper), tuple(tuple(x[::-1]) for x in upper))
    return mf, mb, kf, kb, upper_fb


def _hg_gate(z, lb):
    key = (1.0 - lb) * jax.nn.sigmoid(-z)
    return key, jnp.log2(1.0 - key)


def _hg_state_mask():
    r = lax.broadcasted_iota(jnp.int32, (LANES, 2 * HG_DK), 0) // HG_DV
    c = lax.broadcasted_iota(jnp.int32, (LANES, 2 * HG_DK), 1) // HG_DK
    return r == c


def _hg_bwd_state_kernel(z_ref, v_ref, lb_ref, m_ref, sb_ref, st_ref):
    L = HG_BLOCK
    nblk = z_ref.shape[0] // L

    @pl.when(pl.program_id(1) == 0)
    def _():
        st_ref[...] = jnp.zeros_like(st_ref)

    bmt = _hg_state_mask()
    for blk in range(nblk - 1, -1, -1):
        rows = slice(blk * L, (blk + 1) * L)
        key, lf = _hg_gate(z_ref[rows, :].astype(F32), lb_ref[...])
        e = _dot_split(m_ref[...], lf)
        ke = key * jnp.exp2(e[:L])
        dec = jnp.exp2(e[L:L + 1])
        for p in range(2):
            sb_ref[0, blk, p] = st_ref[p]
            kvt = _dot_tn(v_ref[rows, 128 * p:128 * p + 128].astype(BF16),
                          ke[:, 256 * p:256 * p + 256].astype(BF16))
            st_ref[p] = dec[:, 256 * p:256 * p + 256] * st_ref[p] + jnp.where(bmt, kvt, 0.0)


def _hg_scale_blocks(jobs, qp_ref, kp_ref, fac_ref, nblocks):
    for u in range(0, nblocks, 2):
        r0 = slice(HG_BASE * u, HG_BASE * u + HG_BASE)
        r1 = slice(HG_BASE * (u + 1), HG_BASE * (u + 2))
        for hd in range(HG_HEADS):
            ls = slice(HG_DK * hd, HG_DK * hd + HG_DK)
            q0, k0, q1, k1 = qp_ref[r0, ls], kp_ref[r0, ls], qp_ref[r1, ls], kp_ref[r1, ls]
            for dst_ref, row0, flags in jobs:
                a = (q0 if flags[u] else k0) * fac_ref[row0 + u:row0 + u + 1, ls]
                b = (q1 if flags[u + 1] else k1) * fac_ref[row0 + u + 1:row0 + u + 2, ls]
                dst_ref[HG_BASE * u:HG_BASE * (u + 2), ls] = jnp.concatenate([a, b], axis=0).astype(dst_ref.dtype)


def _hg_main_kernel(h_ref, lb_ref, nw_ref, mf_ref, mb_ref, kf_ref, kb_ref, sb_ref, o_ref,
                    st_ref, e2_ref, qp2_ref, kp2_ref, fac2_ref, s_ref, *, upper):
    L = h_ref.shape[0]
    nb = L // HG_BASE
    nlev = kf_ref.shape[0] + 1
    all_up = (True,) * nb

    @pl.when(pl.program_id(1) == 0)
    def _():
        st_ref[...] = jnp.zeros_like(st_ref)

    lo_v = _lane_lo_mask((L, LANES))
    bmt = _hg_state_mask()
    bones = jnp.where(_head_block_ones(), 1.0, 0.0).astype(BF16)
    osum = [jnp.zeros((L, LANES), F32), jnp.zeros((L, LANES), F32)]
    nbuf = nlev + 1
    for d in range(2):
        m_ref = mf_ref if d == 0 else mb_ref
        k_ref = kf_ref if d == 0 else kb_ref
        e_ref, qp_ref, kp_ref, fac_ref = e2_ref.at[d], qp2_ref.at[d], kp2_ref.at[d], fac2_ref.at[d]
        sbuf = [s_ref.at[d * nbuf + k] for k in range(nbuf)]
        key, lf = _hg_gate(h_ref[:, 512 + 512 * d:1024 + 512 * d].astype(F32), lb_ref[...])
        e_ref[...] = _dot_split(m_ref[...], lf)
        eb = e_ref[pl.ds(0, L), :]
        qp_ref[...] = h_ref[:, 0:512].astype(F32) * jnp.exp2(eb)
        kp_ref[...] = key * jnp.exp2(-eb)
        fac_ref[...] = jnp.exp2(e_ref[pl.ds(L, (nlev + 1) * nb), :])
        jobs = [(sbuf[lv - 1], (1 + lv) * nb, upper[d][lv - 1]) for lv in range(1, nlev)]
        jobs.append((sbuf[nlev - 1], 0, all_up))
        if d == 0:
            jobs.append((sbuf[nlev], nb, (False,) * nb))
        _hg_scale_blocks(jobs, qp_ref, kp_ref, fac_ref, nb)
        H = L // 2
        halves = (slice(0, H), slice(H, L))
        qh, kh = (1, 0) if d == 0 else (0, 1)
        masks = [k_ref[lv] > 0.5 for lv in range(nlev - 1)]

        def tiles(ref, hbs):
            return jnp.stack([ref[halves[hb], HG_DK * hd:HG_DK * hd + HG_DK]
                              for hd in range(HG_HEADS) for hb in hbs], axis=0).astype(BF16)

        def bdot(a, b):
            return lax.dot_general(a, b, (((2,), (2,)), ((0,), (0,))), preferred_element_type=F32)

        diag_all = jnp.where(masks[0][None], bdot(tiles(qp_ref, (0, 1)), tiles(kp_ref, (0, 1))), 0.0)
        for lv in range(2, nlev):
            t = tiles(sbuf[lv - 1], (0, 1))
            diag_all = jnp.where(masks[lv - 1][None], bdot(t, t), diag_all)
        diag_all = diag_all.astype(BF16)
        cross_all = bdot(tiles(sbuf[0], (qh,)), tiles(sbuf[0], (kh,))).astype(BF16)
        for p in range(2):
            v = h_ref[:, 1536 + 128 * p:1664 + 128 * p].astype(F32)
            vlo = jnp.where(lo_v, v, 0.0).astype(BF16)
            vhi = jnp.where(lo_v, 0.0, v).astype(BF16)
            qe_pair = []
            ke_pair = []
            for hh in range(2):
                ls = slice(HG_DK * (2 * p + hh), HG_DK * (2 * p + hh) + HG_DK)
                qe_pair.append(sbuf[nlev - 1][:, ls])
                if d == 0:
                    ke_pair.append(sbuf[nlev][:, ls])
            dg = [[diag_all[2 * (2 * p + hh) + hb] for hb in range(2)] for hh in range(2)]
            cross = [cross_all[2 * p + hh] for hh in range(2)]
            vs = jnp.concatenate([vlo, vhi], axis=0)
            if d == 0:
                o_q = _dot(jnp.concatenate([cross[0], dg[0][1], cross[1], dg[1][1]], axis=1), vs)
                o_k = _dot(jnp.concatenate([dg[0][0], dg[1][0]], axis=1),
                           jnp.concatenate([vlo[halves[0]], vhi[halves[0]]], axis=0))
                intra = jnp.concatenate([o_k, o_q], axis=0)
            else:
                o_q = _dot(jnp.concatenate([dg[0][0], cross[0], dg[1][0], cross[1]], axis=1), vs)
                o_k = _dot(jnp.concatenate([dg[0][1], dg[1][1]], axis=1),
                           jnp.concatenate([vlo[halves[1]], vhi[halves[1]]], axis=0))
                intra = jnp.concatenate([o_q, o_k], axis=0)
            state = st_ref[p] if d == 0 else sb_ref[0, 0, p]
            inter = _dot_nt(jnp.concatenate(qe_pair, axis=1), state.astype(BF16))
            osum[p] = osum[p] + intra + inter
            if d == 0:
                ps = slice(256 * p, 256 * p + 256)
                dec = fac_ref[nb - 1:nb, ps] * fac_ref[2 * nb - 1:2 * nb, ps]
                kvt = _dot_tn(v.astype(BF16), jnp.concatenate(ke_pair, axis=1))
                st_ref[p] = dec * st_ref[p] + jnp.where(bmt, kvt, 0.0)
    for p in range(2):
        o = osum[p]
        ms = _dot_split_rhs_const(o * o, bones) * (1.0 / HG_DV)
        g = h_ref[:, 1792 + 128 * p:1920 + 128 * p].astype(F32)
        y = o * lax.rsqrt(ms + RMS_EPS) * nw_ref[:, 128 * p:128 * p + 128] * jax.nn.sigmoid(g)
        o_ref[:, 128 * p:128 * p + 128] = y.astype(o_ref.dtype)


def _hgrn2(h, lb, nw, consts, B, S):
    L = HG_BLOCK
    nb = S // L
    mf, mb, kf, kb, mpre, upper = consts
    nsec = mf.shape[0]
    nfac = nsec - L
    sb = pl.pallas_call(
        _hg_bwd_state_kernel,
        grid=(B, nb // HG_PRE_BLOCKS),
        in_specs=[pl.BlockSpec((HG_PRE_BLOCKS * L, 512), lambda b, n: ((b * nb + nb) // HG_PRE_BLOCKS - 1 - n, 2)),
                  pl.BlockSpec((HG_PRE_BLOCKS * L, 256), lambda b, n: ((b * nb + nb) // HG_PRE_BLOCKS - 1 - n, 6)),
                  pl.BlockSpec((1, 512), lambda b, n: (0, 0)),
                  pl.BlockSpec((L + 8, L), lambda b, n: (0, 0))],
        out_specs=pl.BlockSpec((1, HG_PRE_BLOCKS, 2, LANES, 2 * HG_DK),
                               lambda b, n: (b, nb // HG_PRE_BLOCKS - 1 - n, 0, 0, 0)),
        out_shape=jax.ShapeDtypeStruct((B, nb, 2, LANES, 2 * HG_DK), F32),
        scratch_shapes=[pltpu.VMEM((2, LANES, 2 * HG_DK), F32)],
        compiler_params=_cparams(("parallel", "arbitrary")),
        name="hg_bwd_state",
    )(h, h, lb, mpre)
    return pl.pallas_call(
        functools.partial(_hg_main_kernel, upper=upper),
        grid=(B, nb),
        in_specs=[pl.BlockSpec((L, HG_W), lambda b, n: (b * nb + n, 0)),
                  pl.BlockSpec((1, 512), lambda b, n: (0, 0)),
                  pl.BlockSpec((1, 256), lambda b, n: (0, 0)),
                  pl.BlockSpec((nsec, L), lambda b, n: (0, 0)),
                  pl.BlockSpec((nsec, L), lambda b, n: (0, 0)),
                  pl.BlockSpec(kf.shape, lambda b, n: (0, 0, 0)),
                  pl.BlockSpec(kb.shape, lambda b, n: (0, 0, 0)),
                  pl.BlockSpec((1, 1, 2, LANES, 2 * HG_DK), lambda b, n: (b, n, 0, 0, 0))],
        out_specs=pl.BlockSpec((L, 256), lambda b, n: (b * nb + n, 0)),
        out_shape=jax.ShapeDtypeStruct((B * S, 256), BF16),
        scratch_shapes=[pltpu.VMEM((2, LANES, 2 * HG_DK), F32),
                        pltpu.VMEM((2, nsec, 512), F32),
                        pltpu.VMEM((2, L, 512), F32), pltpu.VMEM((2, L, 512), F32),
                        pltpu.VMEM((2, nfac, 512), F32),
                        pltpu.VMEM((2 * (kf.shape[0] + 2), L, 512), BF16)],
        compiler_params=_cparams(("parallel", "arbitrary")),
        name="hg_main",
    )(h, lb, nw, mf, mb, kf, kb, sb)


def _outproj_kernel(x_ref, r_ref, s_ref, g_ref, wr_ref, ws_ref, wg_ref, lg_ref, lb_ref, o_ref):
    mix = _dot(r_ref[...], wr_ref[...]) + _dot(s_ref[...], ws_ref[...]) + _dot(g_ref[...], wg_ref[...])
    o_ref[...] = _layer_norm(DEEPNORM_ALPHA * x_ref[...] + mix, lg_ref[...], lb_ref[...])


ROUTER_ROWS = 16


def _outproj_router_kernel(x_ref, r_ref, s_ref, g_ref, wr_ref, ws_ref, wg_ref, lg_ref, lb_ref, wrt_ref, tri_ref,
                           o_ref, ot_ref, route_ref, cnt_ref):
    @pl.when(pl.program_id(0) == 0)
    def _():
        cnt_ref[...] = jnp.zeros_like(cnt_ref)

    mix = _dot(r_ref[...], wr_ref[...]) + _dot(s_ref[...], ws_ref[...]) + _dot(g_ref[...], wg_ref[...])
    y = _layer_norm(DEEPNORM_ALPHA * x_ref[...] + mix, lg_ref[...], lb_ref[...])
    o_ref[...] = y
    ot_ref[...] = _to_token_tiles(y)
    tm = y.shape[0]
    w = wrt_ref[...]
    w_hi = w.astype(BF16)
    w_lo = (w - w_hi.astype(F32)).astype(BF16)
    y_hi = y.astype(BF16)
    y_lo = (y - y_hi.astype(F32)).astype(BF16)
    lt = _dot_nt(jnp.concatenate([w_hi, w_lo], axis=0), y_hi)
    logits = lt[:ROUTER_ROWS] + lt[ROUTER_ROWS:] + _dot_nt(w_hi, y_lo)
    row = lax.broadcasted_iota(jnp.int32, logits.shape, 0).astype(F32)
    neg = jnp.float32(-jnp.inf)
    l1 = jnp.where(row < N_EXPERTS, logits, neg)
    m1 = jnp.max(l1, axis=0, keepdims=True)
    i1 = jnp.min(jnp.where(l1 == m1, row, float(ROUTER_ROWS)), axis=0, keepdims=True)
    l2 = jnp.where(row == i1, neg, l1)
    m2 = jnp.max(l2, axis=0, keepdims=True)
    i2 = jnp.min(jnp.where(l2 == m2, row, float(ROUTER_ROWS)), axis=0, keepdims=True)
    e2 = jnp.exp(m2 - m1)
    den = 1.0 + e2
    picked = jnp.where((row == i1) | (row == i2), 1.0, 0.0)
    before = _dot(picked.astype(BF16), tri_ref[...])
    rank = before + cnt_ref[:, 0:1]
    r1 = jnp.sum(jnp.where(row == i1, rank, 0.0), axis=0, keepdims=True)
    r2 = jnp.sum(jnp.where(row == i2, rank, 0.0), axis=0, keepdims=True)
    cnt_ref[...] = cnt_ref[...] + jnp.sum(picked, axis=1, keepdims=True)
    lrow = lax.broadcasted_iota(jnp.int32, (8, tm), 0)
    route8 = (jnp.where(lrow == 0, i1, 0.0) + jnp.where(lrow == 1, i2, 0.0)
              + jnp.where(lrow == 2, 1.0 / den, 0.0) + jnp.where(lrow == 3, e2 / den, 0.0)
              + jnp.where(lrow == 4, r1, 0.0) + jnp.where(lrow == 5, r2, 0.0))
    route_ref[...] = jnp.concatenate([route8, jnp.zeros((LANES - 8, tm), F32)], axis=0).T


def _out_proj(x2d, ret, swa, hgo, w_r, w_s, w_g, ln_g, ln_b, w_router=None):
    T = x2d.shape[0]
    tm = 1024
    row = lambda i: (i, 0)
    full = lambda i: (0, 0)
    in_specs = [pl.BlockSpec((tm, D_MODEL), row),
                pl.BlockSpec((tm, 256), row), pl.BlockSpec((tm, 512), row), pl.BlockSpec((tm, 256), row),
                pl.BlockSpec((256, D_MODEL), full), pl.BlockSpec((512, D_MODEL), full),
                pl.BlockSpec((256, D_MODEL), full),
                pl.BlockSpec((1, D_MODEL), full), pl.BlockSpec((1, D_MODEL), full)]
    args = [x2d, ret, swa, hgo, w_r, w_s, w_g, ln_g, ln_b]
    if w_router is None:
        return pl.pallas_call(
            _outproj_kernel, grid=(T // tm,), in_specs=in_specs,
            out_specs=pl.BlockSpec((tm, D_MODEL), row),
            out_shape=jax.ShapeDtypeStruct((T, D_MODEL), F32),
            compiler_params=_cparams(("parallel",)), name="out_proj",
        )(*args)
    return pl.pallas_call(
        _outproj_router_kernel, grid=(T // tm,),
        in_specs=in_specs + [pl.BlockSpec((ROUTER_ROWS, D_MODEL), full),
                             pl.BlockSpec((tm, tm), full, pipeline_mode=pl.Buffered(1))],
        out_specs=[pl.BlockSpec((tm, D_MODEL), row), pl.BlockSpec((tm, 8, LANES), lambda i: (i, 0, 0)),
                   pl.BlockSpec((tm, LANES), row), pl.BlockSpec((ROUTER_ROWS, LANES), full)],
        out_shape=[jax.ShapeDtypeStruct((T, D_MODEL), F32), jax.ShapeDtypeStruct((T, 8, LANES), F32),
                   jax.ShapeDtypeStruct((T, LANES), F32), jax.ShapeDtypeStruct((ROUTER_ROWS, LANES), F32)],
        compiler_params=_cparams(("arbitrary",)), name="out_proj_router",
    )(*args, w_router, jnp.asarray(np.triu(np.ones((tm, tm), np.float32), 1), BF16))


FFN_CHUNKS = 1


def _ffn_kernel(x_ref, wg_ref, wu_ref, wd_ref, lg_ref, lb_ref, o_ref):
    x = x_ref[...]
    xb = x.astype(BF16)
    th = FFN_DIM // FFN_CHUNKS
    f = jnp.zeros(x.shape, F32)
    for c in range(FFN_CHUNKS):
        g = _dot(xb, wg_ref[:, c * th:(c + 1) * th])
        u = _dot(xb, wu_ref[:, c * th:(c + 1) * th])
        a = (g * jax.nn.sigmoid(g)) * u
        f = f + _dot(a.astype(BF16), wd_ref[c * th:(c + 1) * th, :])
    o_ref[...] = _layer_norm(DEEPNORM_ALPHA * x + f, lg_ref[...], lb_ref[...])


def _ffn(x2d, wg, wu, wd, ln_g, ln_b):
    T = x2d.shape[0]
    tm = 512
    once = pl.Buffered(1)
    return pl.pallas_call(
        _ffn_kernel, grid=(T // tm,),
        in_specs=[pl.BlockSpec((tm, D_MODEL), lambda i: (i, 0)),
                  pl.BlockSpec((D_MODEL, FFN_DIM), lambda i: (0, 0), pipeline_mode=once),
                  pl.BlockSpec((D_MODEL, FFN_DIM), lambda i: (0, 0), pipeline_mode=once),
                  pl.BlockSpec((FFN_DIM, D_MODEL), lambda i: (0, 0), pipeline_mode=once),
                  pl.BlockSpec((1, D_MODEL), lambda i: (0, 0)),
                  pl.BlockSpec((1, D_MODEL), lambda i: (0, 0))],
        out_specs=pl.BlockSpec((tm, D_MODEL), lambda i: (i, 0)),
        out_shape=jax.ShapeDtypeStruct((T, D_MODEL), F32),
        compiler_params=_cparams(("parallel",)), name="ffn",
    )(x2d, wg, wu, wd, ln_g, ln_b)


MOE_TILE = 512
MOE_TH = 1792
MOE_TOK = 512


def _moe_num_tiles(T):
    return (2 * T) // MOE_TILE + N_EXPERTS + 1


def _moe_tables(route, counts):
    T = route.shape[0]
    tm = MOE_TILE
    cnt = counts[:N_EXPERTS, 0].astype(jnp.int32)
    pc = ((cnt + tm - 1) // tm) * tm
    pend = jnp.cumsum(pc)
    pstart = pend - pc
    e = route[:, 0:2].astype(jnp.int32)
    rank = route[:, 4:6].astype(jnp.int32)
    sel = e[:, :, None] == jnp.arange(N_EXPERTS, dtype=jnp.int32)[None, None, :]
    pos = jnp.sum(jnp.where(sel, pstart[None, None, :], 0), axis=-1) + rank
    n_tiles = _moe_num_tiles(T)
    tile_row = jnp.arange(n_tiles, dtype=jnp.int32) * tm
    texp = jnp.minimum(jnp.sum((tile_row[:, None] >= pend[None, :]).astype(jnp.int32), axis=1), N_EXPERTS - 1)
    n_used = pend[-1] // tm
    last = jnp.sum(jnp.where(jnp.arange(n_tiles) == n_used - 1, texp, 0))
    texp = jnp.where(jnp.arange(n_tiles) < n_used, texp, last)
    return pos.reshape(-1), texp, n_used.reshape(1), pstart + cnt


def _to_token_tiles(x):
    cols = jnp.stack([x[:, LANES * c:LANES * c + LANES] for c in range(D_MODEL // LANES)], axis=0)
    return pltpu.einshape("csl->scl", cols)


def _from_token_tiles(xt):
    y = pltpu.einshape("scl->csl", xt)
    return jnp.concatenate([y[c] for c in range(D_MODEL // LANES)], axis=1)


MOE_DMA_UNROLL = 8


def _moe_dispatch_kernel(pos_ref, pad_ref, nused_ref, xt_hbm, xs_hbm, zero_ref, buf_ref, fsem, ssem, zsem):
    i = pl.program_id(0)
    n = pl.num_programs(0)
    tm = MOE_TOK
    n_tiles = xs_hbm.shape[0] // MOE_TILE
    slot = i % 3

    def zero_copy(row):
        return pltpu.make_async_copy(zero_ref, xs_hbm.at[pl.ds(row, MOE_TILE)], zsem)

    def fetch(step, s):
        return pltpu.make_async_copy(xt_hbm.at[pl.ds(step * tm, tm)], buf_ref.at[s], fsem.at[s])

    def drain_copies(s):
        for _ in range(2):
            pltpu.make_async_copy(buf_ref.at[s], xs_hbm.at[pl.ds(0, tm)], ssem.at[s]).wait()

    @pl.when(i == 0)
    def _():
        fetch(0, 0).start()
        zero_ref[...] = jnp.zeros_like(zero_ref)
        for e in range(N_EXPERTS):
            zero_copy(pad_ref[e]).start()
        for e in range(N_EXPERTS):
            zero_copy(pad_ref[e]).wait()
        for t in range(n_tiles - N_EXPERTS - 1, n_tiles):
            @pl.when(t >= nused_ref[0])
            def _():
                zero_copy(t * MOE_TILE).start()
                zero_copy(t * MOE_TILE).wait()

    @pl.when(i >= 2)
    def _():
        drain_copies((i + 1) % 3)

    @pl.when(i + 1 < n)
    def _():
        fetch(i + 1, (i + 1) % 3).start()

    fetch(i, slot).wait()

    def body(kk, c):
        k0 = kk * MOE_DMA_UNROLL
        t0 = i * tm + k0
        dst = [pos_ref[2 * t0 + u] for u in range(2 * MOE_DMA_UNROLL)]
        for u in range(2 * MOE_DMA_UNROLL):
            pltpu.make_async_copy(buf_ref.at[slot, k0 + u // 2], xs_hbm.at[dst[u]], ssem.at[slot]).start()
        return c

    lax.fori_loop(0, tm // MOE_DMA_UNROLL, body, 0)

    @pl.when(i == n - 1)
    def _():
        @pl.when(i >= 1)
        def _():
            drain_copies((i - 1) % 3)
        drain_copies(slot)


def _moe_dispatch(x1t, pos, padstart, n_used):
    T = x1t.shape[0]
    P = _moe_num_tiles(T) * MOE_TILE
    return pl.pallas_call(
        _moe_dispatch_kernel,
        grid_spec=pltpu.PrefetchScalarGridSpec(
            num_scalar_prefetch=3, grid=(T // MOE_TOK,),
            in_specs=[pl.BlockSpec(memory_space=pl.ANY)],
            out_specs=pl.BlockSpec(memory_space=pl.ANY),
            scratch_shapes=[pltpu.VMEM((MOE_TILE, 8, LANES), F32), pltpu.VMEM((3, MOE_TOK, 8, LANES), F32),
                            pltpu.SemaphoreType.DMA((3,)), pltpu.SemaphoreType.DMA((3,)),
                            pltpu.SemaphoreType.DMA]),
        out_shape=jax.ShapeDtypeStruct((P, 8, LANES), F32),
        compiler_params=_cparams(("arbitrary",)), name="moe_dispatch",
    )(pos, padstart, n_used, x1t)


def _moe_group_kernel(texp_ref, nused_ref, xs_ref, wg_ref, wu_ref, wd_ref, y_ref, xb_ref, acc_ref):
    i = pl.program_id(0)
    j = pl.program_id(1)

    @pl.when(i < nused_ref[0])
    def _():
        @pl.when(j == 0)
        def _():
            acc_ref[...] = jnp.zeros_like(acc_ref)

        xb = _from_token_tiles(xs_ref[...]).astype(BF16)
        g = _dot(xb, wg_ref[...])
        u = _dot(xb, wu_ref[...])
        a = (g * jax.nn.sigmoid(g)) * u
        acc_ref[...] += _dot(a.astype(BF16), wd_ref[...])

        @pl.when(j == pl.num_programs(1) - 1)
        def _():
            y_ref[...] = _to_token_tiles(acc_ref[...])

    @pl.when((i >= nused_ref[0]) & (j == 0))
    def _():
        y_ref[...] = jnp.zeros_like(y_ref)


def _moe_group(xs, texp, n_used, wg, wu, wd):
    P = xs.shape[0]
    tm, th = MOE_TILE, MOE_TH
    nh = EXPERT_DIM // th

    def row_map(i, j, texp, nused):
        return (jnp.minimum(i, nused[0] - 1), 0, 0)

    def col(i, j, nused):
        return jnp.where(i < nused[0], j, nh - 1)

    return pl.pallas_call(
        _moe_group_kernel,
        grid_spec=pltpu.PrefetchScalarGridSpec(
            num_scalar_prefetch=2, grid=(P // tm, nh),
            in_specs=[pl.BlockSpec((tm, 8, LANES), row_map),
                      pl.BlockSpec((None, D_MODEL, th), lambda i, j, texp, nused: (texp[i], 0, col(i, j, nused))),
                      pl.BlockSpec((None, D_MODEL, th), lambda i, j, texp, nused: (texp[i], 0, col(i, j, nused))),
                      pl.BlockSpec((None, th, D_MODEL), lambda i, j, texp, nused: (texp[i], col(i, j, nused), 0))],
            out_specs=pl.BlockSpec((tm, 8, LANES), lambda i, j, texp, nused: (i, 0, 0)),
            scratch_shapes=[pltpu.VMEM((tm, D_MODEL), BF16), pltpu.VMEM((tm, D_MODEL), F32)]),
        out_shape=jax.ShapeDtypeStruct((P, 8, LANES), F32),
        compiler_params=_cparams(("arbitrary", "arbitrary")), name="moe_group",
    )(texp, n_used, xs, wg, wu, wd)


def _moe_combine_kernel(pos_ref, x_ref, route_ref, y_hbm, lg_ref, lb_ref, o_ref, buf_ref, sem):
    i = pl.program_id(0)
    tm = x_ref.shape[0]

    def issue(tile, slot):
        def body(kk, c):
            k0 = kk * MOE_DMA_UNROLL
            t0 = tile * tm + k0
            src = [pos_ref[2 * t0 + u] for u in range(2 * MOE_DMA_UNROLL)]
            for u in range(2 * MOE_DMA_UNROLL):
                pltpu.make_async_copy(y_hbm.at[src[u]], buf_ref.at[slot, (u % 2) * tm + k0 + u // 2],
                                      sem.at[slot]).start()
            return c
        lax.fori_loop(0, tm // MOE_DMA_UNROLL, body, 0)

    @pl.when(i == 0)
    def _():
        issue(0, 0)

    slot = i % 2
    pltpu.make_async_copy(y_hbm.at[pl.ds(0, 2 * tm)], buf_ref.at[slot], sem.at[slot]).wait()

    @pl.when(i + 1 < pl.num_programs(0))
    def _():
        issue(i + 1, 1 - slot)

    a = _from_token_tiles(buf_ref[slot, pl.ds(0, tm)])
    b = _from_token_tiles(buf_ref[slot, pl.ds(tm, tm)])
    f = route_ref[:, 2:3] * a + route_ref[:, 3:4] * b
    o_ref[...] = _layer_norm(DEEPNORM_ALPHA * x_ref[...] + f, lg_ref[...], lb_ref[...])


def _moe_combine(x1, route, y, pos, ln_g, ln_b):
    T = x1.shape[0]
    tm = MOE_TOK
    return pl.pallas_call(
        _moe_combine_kernel,
        grid_spec=pltpu.PrefetchScalarGridSpec(
            num_scalar_prefetch=1, grid=(T // tm,),
            in_specs=[pl.BlockSpec((tm, D_MODEL), lambda i, pos: (i, 0)),
                      pl.BlockSpec((tm, LANES), lambda i, pos: (i, 0)),
                      pl.BlockSpec(memory_space=pl.ANY),
                      pl.BlockSpec((1, D_MODEL), lambda i, pos: (0, 0)),
                      pl.BlockSpec((1, D_MODEL), lambda i, pos: (0, 0))],
            out_specs=pl.BlockSpec((tm, D_MODEL), lambda i, pos: (i, 0)),
            scratch_shapes=[pltpu.VMEM((2, 2 * tm, 8, LANES), F32), pltpu.SemaphoreType.DMA((2,))]),
        out_shape=jax.ShapeDtypeStruct((T, D_MODEL), F32),
        compiler_params=_cparams(("arbitrary",)), name="moe_combine",
    )(pos, x1, route, y, ln_g, ln_b)


def _moe(x1, x1t, route, counts, wg, wu, wd, ln_g, ln_b):
    pos, texp, n_used, padstart = _moe_tables(route, counts)
    xs = _moe_dispatch(x1t, pos, padstart, n_used)
    y = _moe_group(xs, texp, n_used, wg, wu, wd)
    return _moe_combine(x1, route, y, pos, ln_g, ln_b)


def _t5_bucket(rel):
    nb = REL_BUCKETS // 2
    ret = (rel > 0).astype(np.int32) * nb
    n = np.abs(rel)
    max_exact = nb // 2
    large = max_exact + (np.log(np.maximum(n, 1) / max_exact) / np.log(REL_MAX_DIST / max_exact)
                         * (nb - max_exact)).astype(np.int32)
    large = np.minimum(large, nb - 1)
    return ret + np.where(n < max_exact, n, large)


def _swa_head_order():
    return [h for c in range(4) for h in (c, 4 + c)]


def _prepare(w_in, w_out, ret_decay, swa_sink, rel_bias, hg_lb, hg_norm_w, moe_router, seq_lens):
    order = _swa_head_order()
    w_in_p = jnp.concatenate(
        [w_in[:, :, 1792:3840], w_in[:, :, 0:1024]]
        + [w_in[:, :, 1024 + 64 * h:1088 + 64 * h] for h in order]
        + [w_in[:, :, 1536:1792]], axis=2).astype(BF16)
    w_r = w_out[:, 0:256, :].astype(BF16)
    w_s = jnp.concatenate([w_out[:, 256 + 64 * h:320 + 64 * h, :] for h in order], axis=1).astype(BF16)
    w_g = w_out[:, 768:1024, :].astype(BF16)

    rd = ret_decay.astype(F32)
    rdp = jnp.repeat(rd.reshape(DEPTH, 4, 2), 64, axis=-1)
    rdp = jnp.concatenate([rdp, rdp], axis=1)
    rdh = jnp.broadcast_to(rd.reshape(DEPTH, 8, 1), (DEPTH, 8, LANES))

    W = SWA_BLOCK
    rel = np.arange(3 * W)[None, :] - W - np.arange(W)[:, None]
    onehot = jnp.asarray(_t5_bucket(rel)[..., None] == np.arange(REL_BUCKETS), F32)
    bias = jnp.einsum("qkb,bh->hqk", onehot, rel_bias.astype(F32), precision=lax.Precision.HIGHEST)
    band = jnp.asarray(np.abs(rel) <= WINDOW)
    bias_kq = jnp.transpose(jnp.where(band[None], bias, MASK_VALUE), (0, 2, 1))
    order = _swa_head_order()
    bias_t = jnp.concatenate([bias_kq[h] for h in order], axis=1) * LOG2_E
    sink = swa_sink.astype(F32) * LOG2_E
    sink_t = jnp.concatenate([jnp.broadcast_to(sink[:, h, None, None], (DEPTH, 8, W)) for h in order], axis=2)

    p = jax.nn.softmax(hg_lb.astype(F32), axis=0)
    lower = (jnp.cumsum(p, axis=0) - p[0]).reshape(DEPTH, 1, HG_HEADS * HG_DK)
    nw = hg_norm_w.astype(F32).reshape(DEPTH, 1, HG_HEADS * HG_DV)

    mf, mb, kf, kb, upper = _hg_constants(HG_BLOCK)
    L = HG_BLOCK
    mpre = np.concatenate([np.tril(np.ones((L, L), np.float32), -1), np.ones((8, L), np.float32)], axis=0)
    consts = (jnp.asarray(mf, BF16), jnp.asarray(mb, BF16), jnp.asarray(kf, F32), jnp.asarray(kb, F32),
              jnp.asarray(mpre, BF16), upper)

    half = RET_DK // 2
    inv = ROPE_BASE ** (-np.arange(half, dtype=np.float32) / half)
    rope = {}
    for S in set(seq_lens):
        ang = jnp.arange(S, dtype=F32)[:, None] * jnp.asarray(inv)[None, :]
        c32, s32 = jnp.cos(ang), jnp.sin(ang)
        rope[S] = (jnp.tile(c32, (1, 4)), jnp.tile(jnp.concatenate([-s32, s32], axis=1), (1, 2)))

    router = jnp.pad(jnp.transpose(moe_router.astype(F32), (0, 2, 1)),
                     ((0, 0), (0, ROUTER_ROWS - N_EXPERTS), (0, 0)))
    return dict(w_in=w_in_p, w_r=w_r, w_s=w_s, w_g=w_g, rdp=rdp, rdh=rdh, bias=bias_t, sink=sink_t,
                lower=lower, nw=nw, consts=consts, rope=rope, router=router)


def _trunk(x, prm, ln1_g, ln1_b, ln2_g, ln2_b, ffn_w, moe_w):
    B, S, _ = x.shape
    x2d = x.reshape(B * S, D_MODEL)
    cos_t, sin_t = prm["rope"][S]
    for l in range(DEPTH):
        h = _in_proj(x2d, prm["w_in"][l])
        ret = _retention(h, cos_t, sin_t, prm["rdp"][l], prm["rdh"][l], B, S)
        swa = _swa(h, prm["bias"], prm["sink"][l], B, S)
        hgo = _hgrn2(h, prm["lower"][l], prm["nw"][l], prm["consts"], B, S)
        g1, b1 = ln1_g[l].reshape(1, D_MODEL), ln1_b[l].reshape(1, D_MODEL)
        g2, b2 = ln2_g[l].reshape(1, D_MODEL), ln2_b[l].reshape(1, D_MODEL)
        i = l // 2
        if l % 2 == 0:
            x1 = _out_proj(x2d, ret, swa, hgo, prm["w_r"][l], prm["w_s"][l], prm["w_g"][l], g1, b1)
            wg, wu, wd = ffn_w
            x2d = _ffn(x1, wg[i], wu[i], wd[i], g2, b2)
        else:
            x1, x1t, route, counts = _out_proj(x2d, ret, swa, hgo, prm["w_r"][l], prm["w_s"][l], prm["w_g"][l],
                                               g1, b1, prm["router"][i])
            wg, wu, wd = moe_w
            x2d = _moe(x1, x1t, route, counts, wg[i], wu[i], wd[i], g2, b2)
    return x2d.reshape(B, S, D_MODEL)


def kernel(x_prompt, x_sample, w_in, w_out, ret_decay, swa_sink, rel_bias, hg_lb, hg_norm_w, ln1_g, ln1_b,
           ln2_g, ln2_b, ffn_w_gate, ffn_w_up, ffn_w_down, moe_router, moe_w_gate, moe_w_up, moe_w_down):
    prm = _prepare(w_in, w_out, ret_decay, swa_sink, rel_bias, hg_lb, hg_norm_w, moe_router,
                   (x_prompt.shape[1], x_sample.shape[1]))
    ffn_w = (ffn_w_gate.astype(BF16), ffn_w_up.astype(BF16), ffn_w_down.astype(BF16))
    moe_w = (moe_w_gate.astype(BF16), moe_w_up.astype(BF16), moe_w_down.astype(BF16))
    outs = []
    for x in (x_prompt, x_sample):
        outs.append(_trunk(x, prm, ln1_g, ln1_b, ln2_g, ln2_b, ffn_w, moe_w))
    return tuple(outs)
```

```python
import functools

import numpy as np
import jax
import jax.numpy as jnp
from jax import lax
from jax.experimental import pallas as pl
from jax.experimental.pallas import tpu as pltpu

F32 = jnp.float32
BF16 = jnp.bfloat16

D_MODEL = 1024
DEPTH = 2
RET_HEADS = 4
RET_DK = 64
RET_CHUNK = 128
ROPE_BASE = 10000.0
SWA_Q_HEADS = 8
SWA_KV_HEADS = 2
SWA_DH = 64
WINDOW = 128
SWA_BLOCK = 128
REL_BUCKETS = 32
REL_MAX_DIST = 128
MASK_VALUE = -1e30
LOG2_E = 1.4426950408889634
HG_HEADS = 4
HG_DK = 128
HG_DV = 64
FFN_DIM = 2816
N_EXPERTS = 8
EXPERT_DIM = 3584
LN_EPS = 1e-5
RMS_EPS = 1e-6
DEEPNORM_ALPHA = (2.0 * DEPTH) ** 0.25

LANES = 128
VMEM_LIMIT = 56 * 1024 * 1024

HG_W = 2048
RET_W = 1024
SWA_W = 768
IN_COLS = HG_W + RET_W + SWA_W

HG_BLOCK = 256
HG_BASE = 8
HG_PRE_BLOCKS = 8
RET_TILE = 2048
SWA_TILE = 2048


def _cparams(sem):
    return pltpu.CompilerParams(dimension_semantics=sem, vmem_limit_bytes=VMEM_LIMIT)


def _dot(a, b):
    return jnp.dot(a, b, preferred_element_type=F32)


def _dot_nt(a, b):
    return lax.dot_general(a, b, (((1,), (1,)), ((), ())), preferred_element_type=F32)


def _dot_tn(a, b):
    return lax.dot_general(a, b, (((0,), (0,)), ((), ())), preferred_element_type=F32)


def _dot_split(a_bf16, x):
    hi = x.astype(BF16)
    lo = (x - hi.astype(F32)).astype(BF16)
    return _dot(a_bf16, hi) + _dot(a_bf16, lo)


def _dot_split_rhs_const(x, b_bf16):
    hi = x.astype(BF16)
    lo = (x - hi.astype(F32)).astype(BF16)
    return _dot(hi, b_bf16) + _dot(lo, b_bf16)


def _layer_norm(y, g, b):
    mu = jnp.mean(y, axis=-1, keepdims=True)
    yc = y - mu
    var = jnp.mean(yc * yc, axis=-1, keepdims=True)
    return yc * lax.rsqrt(var + LN_EPS) * g + b


def _lane_lo_mask(shape):
    return lax.broadcasted_iota(jnp.int32, shape, len(shape) - 1) < (LANES // 2)


def _head_block_ones():
    r = lax.broadcasted_iota(jnp.int32, (LANES, LANES), 0) // 64
    c = lax.broadcasted_iota(jnp.int32, (LANES, LANES), 1) // 64
    return r == c


def _inproj_kernel(x_ref, w_ref, o_ref):
    o_ref[...] = _dot(x_ref[...].astype(BF16), w_ref[...]).astype(o_ref.dtype)


def _in_proj(x2d, w):
    T = x2d.shape[0]
    tm = 1024
    return pl.pallas_call(
        _inproj_kernel,
        grid=(T // tm,),
        in_specs=[pl.BlockSpec((tm, D_MODEL), lambda i: (i, 0)),
                  pl.BlockSpec((D_MODEL, IN_COLS), lambda i: (0, 0), pipeline_mode=pl.Buffered(1))],
        out_specs=pl.BlockSpec((tm, IN_COLS), lambda i: (i, 0)),
        out_shape=jax.ShapeDtypeStruct((T, IN_COLS), BF16),
        compiler_params=_cparams(("parallel",)),
        name="in_proj",
    )(x2d, w)


def _rotary(x, cos, sin_signed, even):
    sw = jnp.where(even, pltpu.roll(x, 96, 1), pltpu.roll(x, 32, 1))
    return x * cos + sw * sin_signed


def _ret_tables(rdp_ref):
    lg = jnp.log1p(-jnp.exp(rdp_ref[...]))
    return lg


def _ret_bwd_state_kernel(h_ref, cos_ref, sin_ref, rdp_ref, sb_ref, st_ref):
    C = RET_CHUNK
    nc = h_ref.shape[0] // C

    @pl.when(pl.program_id(1) == 0)
    def _():
        st_ref[...] = jnp.zeros_like(st_ref)

    lg = _ret_tables(rdp_ref)
    rows = lax.broadcasted_iota(jnp.int32, (C, LANES), 0).astype(F32)
    even = (lax.broadcasted_iota(jnp.int32, (C, LANES), 1) // 32) % 2 == 0
    bm = _head_block_ones()
    for c in range(nc - 1, -1, -1):
        sl = pl.ds(c * C, C)
        cos = cos_ref[sl, :]
        sin = sin_ref[sl, :]
        for p in range(2):
            lgb = lg[2 + p:3 + p, :]
            kwb = jnp.exp(lgb * rows)
            decb = jnp.where(bm, jnp.exp(lgb * float(C)), 0.0)
            k = _rotary(h_ref[sl, 256 + 128 * p:384 + 128 * p].astype(F32), cos, sin, even) * (RET_DK ** -0.5)
            v = h_ref[sl, 512 + 128 * p:640 + 128 * p].astype(F32)
            sb_ref[0, c, p] = st_ref[p]
            kv = _dot_tn((k * kwb).astype(BF16), v.astype(BF16))
            st_ref[p] = decb * st_ref[p] + jnp.where(bm, kv, 0.0)


def _ret_main_kernel(h_ref, cos_ref, sin_ref, rdp_ref, rdh_ref, sb_ref, o_ref, st_ref):
    C = RET_CHUNK
    nc = h_ref.shape[0] // C

    @pl.when(pl.program_id(1) == 0)
    def _():
        st_ref[...] = jnp.zeros_like(st_ref)

    lg = _ret_tables(rdp_ref)
    lgh = jnp.log1p(-jnp.exp(rdh_ref[...]))
    rows = lax.broadcasted_iota(jnp.int32, (C, LANES), 0).astype(F32)
    cols = lax.broadcasted_iota(jnp.int32, (C, LANES), 1).astype(F32)
    rel = rows - cols
    even = (lax.broadcasted_iota(jnp.int32, (C, LANES), 1) // 32) % 2 == 0
    lo = _lane_lo_mask((C, LANES))
    bm = _head_block_ones()
    bones = jnp.where(bm, 1.0, 0.0).astype(BF16)

    dmat = []
    for hd in range(RET_HEADS):
        f = jnp.where(rel >= 0, jnp.exp(lgh[hd:hd + 1, :] * jnp.maximum(rel, 0.0)), 0.0)
        b = jnp.where(rel <= 0, jnp.exp(lgh[4 + hd:5 + hd, :] * jnp.maximum(-rel, 0.0)), 0.0)
        dmat.append(f + b)

    for c in range(nc):
        sl = pl.ds(c * C, C)
        cos = cos_ref[sl, :]
        sin = sin_ref[sl, :]
        for p in range(2):
            lgf = lg[p:p + 1, :]
            lgb = lg[2 + p:3 + p, :]
            q = _rotary(h_ref[sl, 128 * p:128 * p + 128].astype(F32), cos, sin, even)
            k = _rotary(h_ref[sl, 256 + 128 * p:384 + 128 * p].astype(F32), cos, sin, even) * (RET_DK ** -0.5)
            v = h_ref[sl, 512 + 128 * p:640 + 128 * p].astype(F32)
            g = h_ref[sl, 768 + 128 * p:896 + 128 * p].astype(F32)
            kb = k.astype(BF16)
            vb = v.astype(BF16)
            qs = jnp.concatenate([jnp.where(lo, q, 0.0), jnp.where(lo, 0.0, q)], axis=0).astype(BF16)
            s = _dot_nt(qs, kb)
            pm = jnp.concatenate([s[:C] * dmat[2 * p], s[C:] * dmat[2 * p + 1]], axis=1).astype(BF16)
            vs = jnp.concatenate([jnp.where(lo, v, 0.0), jnp.where(lo, 0.0, v)], axis=0).astype(BF16)
            intra = _dot(pm, vs)
            qwf = jnp.exp(lgf * (rows + 1.0))
            qwb = jnp.exp(lgb * (float(C) - rows))
            qi = jnp.concatenate([q * qwf, q * qwb], axis=1).astype(BF16)
            sst = jnp.concatenate([st_ref[p], sb_ref[0, c, p]], axis=0).astype(BF16)
            o = intra + _dot(qi, sst)
            kwf = jnp.exp(lgf * (float(C) - 1.0 - rows))
            decf = jnp.where(bm, jnp.exp(lgf * float(C)), 0.0)
            kv = _dot_tn((k * kwf).astype(BF16), vb)
            st_ref[p] = decf * st_ref[p] + jnp.where(bm, kv, 0.0)
            ms = _dot_split_rhs_const(o * o, bones) * (1.0 / RET_DK)
            y = o * lax.rsqrt(ms + RMS_EPS) * (g * jax.nn.sigmoid(g))
            o_ref[sl, 128 * p:128 * p + 128] = y.astype(o_ref.dtype)


def _retention(h, cos_t, sin_t, rdp, rdh, B, S):
    L = RET_TILE
    nb = S // L
    ncb = L // RET_CHUNK
    n_chunks = S // RET_CHUNK
    colblk = HG_W // RET_W
    sb = pl.pallas_call(
        _ret_bwd_state_kernel,
        grid=(B, nb),
        in_specs=[pl.BlockSpec((L, RET_W), lambda b, n: (b * nb + nb - 1 - n, colblk)),
                  pl.BlockSpec((L, LANES), lambda b, n: (nb - 1 - n, 0)),
                  pl.BlockSpec((L, LANES), lambda b, n: (nb - 1 - n, 0)),
                  pl.BlockSpec((8, LANES), lambda b, n: (0, 0))],
        out_specs=pl.BlockSpec((1, ncb, 2, LANES, LANES), lambda b, n: (b, nb - 1 - n, 0, 0, 0)),
        out_shape=jax.ShapeDtypeStruct((B, n_chunks, 2, LANES, LANES), F32),
        scratch_shapes=[pltpu.VMEM((2, LANES, LANES), F32)],
        compiler_params=_cparams(("parallel", "arbitrary")),
        name="ret_bwd_state",
    )(h, cos_t, sin_t, rdp)
    return pl.pallas_call(
        _ret_main_kernel,
        grid=(B, nb),
        in_specs=[pl.BlockSpec((L, RET_W), lambda b, n: (b * nb + n, colblk)),
                  pl.BlockSpec((L, LANES), lambda b, n: (n, 0)),
                  pl.BlockSpec((L, LANES), lambda b, n: (n, 0)),
                  pl.BlockSpec((8, LANES), lambda b, n: (0, 0)),
                  pl.BlockSpec((8, LANES), lambda b, n: (0, 0)),
                  pl.BlockSpec((1, ncb, 2, LANES, LANES), lambda b, n: (b, n, 0, 0, 0))],
        out_specs=pl.BlockSpec((L, 256), lambda b, n: (b * nb + n, 0)),
        out_shape=jax.ShapeDtypeStruct((B * S, 256), BF16),
        scratch_shapes=[pltpu.VMEM((2, LANES, LANES), F32)],
        compiler_params=_cparams(("parallel", "arbitrary")),
        name="ret_main",
    )(h, cos_t, sin_t, rdp, rdh, sb)


def _swa_kernel(q_ref, kp_ref, kc_ref, kn_ref, vp_ref, vc_ref, vn_ref, bias_ref, sink_ref, o_ref):
    W = SWA_BLOCK
    nq = q_ref.shape[0] // W
    n = pl.program_id(1)
    last = pl.num_programs(1) - 1
    kall = jnp.concatenate([kp_ref[...], kc_ref[...], kn_ref[...]], axis=0).astype(BF16)
    vt = jnp.concatenate([vp_ref[...], vc_ref[...], vn_ref[...]], axis=0).astype(F32).T.astype(BF16)
    lo = _lane_lo_mask((W, LANES))
    edge_first = jnp.where(n > 0, 0.0, MASK_VALUE)
    edge_last = jnp.where(n < last, 0.0, MASK_VALUE)
    sink = sink_ref[0:1, :]
    for r in range(nq):
        qs = []
        for c in range(4):
            q = q_ref[pl.ds(r * W, W), 128 * c:128 * c + 128].astype(F32) * (SWA_DH ** -0.5 * LOG2_E)
            qs += [jnp.where(lo, q, 0.0), jnp.where(lo, 0.0, q)]
        qs = jnp.concatenate(qs, axis=0).astype(BF16)
        s = _dot_nt(kall[r * W:(r + 3) * W], qs) + bias_ref[...]
        if r == 0:
            s = jnp.concatenate([s[:W] + edge_first, s[W:]], axis=0)
        if r == nq - 1:
            s = jnp.concatenate([s[:2 * W], s[2 * W:] + edge_last], axis=0)
        m = jnp.maximum(jnp.max(s, axis=0, keepdims=True), sink)
        e = jnp.exp2(s - m)
        den = jnp.sum(e, axis=0, keepdims=True) + jnp.exp2(sink - m)
        ot = _dot(vt[:, r * W:(r + 3) * W], e.astype(BF16)) / den
        for c in range(4):
            sel = jnp.concatenate([ot[0:SWA_DH, 2 * c * W:(2 * c + 1) * W],
                                   ot[SWA_DH:2 * SWA_DH, (2 * c + 1) * W:(2 * c + 2) * W]], axis=0)
            o_ref[pl.ds(r * W, W), 128 * c:128 * c + 128] = sel.T.astype(o_ref.dtype)


def _swa(h, bias_t, sink_t, B, S):
    L = SWA_TILE
    W = SWA_BLOCK
    nb = S // L
    r = L // W
    nblk = S // W
    qcol = (HG_W + RET_W) // 512
    kcol = (HG_W + RET_W + 512) // LANES
    vcol = kcol + 1

    def prev_map(col):
        return lambda b, n: (b * nblk + jnp.maximum(n * r - 1, 0), col)

    def cur_map(col):
        return lambda b, n: (b * nb + n, col)

    def next_map(col):
        return lambda b, n: (b * nblk + jnp.minimum(n * r + r, nblk - 1), col)

    return pl.pallas_call(
        _swa_kernel,
        grid=(B, nb),
        in_specs=[pl.BlockSpec((L, 512), cur_map(qcol)),
                  pl.BlockSpec((W, LANES), prev_map(kcol)),
                  pl.BlockSpec((L, LANES), cur_map(kcol)),
                  pl.BlockSpec((W, LANES), next_map(kcol)),
                  pl.BlockSpec((W, LANES), prev_map(vcol)),
                  pl.BlockSpec((L, LANES), cur_map(vcol)),
                  pl.BlockSpec((W, LANES), next_map(vcol)),
                  pl.BlockSpec((3 * W, SWA_Q_HEADS * W), lambda b, n: (0, 0)),
                  pl.BlockSpec((8, SWA_Q_HEADS * W), lambda b, n: (0, 0))],
        out_specs=pl.BlockSpec((L, 512), lambda b, n: (b * nb + n, 0)),
        out_shape=jax.ShapeDtypeStruct((B * S, 512), BF16),
        compiler_params=_cparams(("parallel", "parallel")),
        name="swa",
    )(h, h, h, h, h, h, h, bias_t, sink_t)


def _hg_levels(L):
    g = L
    out = []
    while g >= 2 * HG_BASE:
        out.append(g)
        g //= 2
    return out


def _hg_constants(L):
    nb = L // HG_BASE
    idx = np.arange(L)
    t = idx[None, :]
    i = idx[:, None]
    r = HG_BASE * (i // HG_BASE) + HG_BASE // 2 - 1
    secs = [((t > r) & (t <= i)).astype(np.float32) - ((t > i) & (t <= r)).astype(np.float32)]
    u = np.arange(nb)[:, None]
    ru = HG_BASE * u + HG_BASE // 2 - 1
    secs.append((t <= ru).astype(np.float32))
    secs.append((t > ru).astype(np.float32))
    masks = [((idx[:, None] // HG_BASE) == (idx[None, :] // HG_BASE)) & (idx[:, None] >= idx[None, :])]
    upper = []
    for g in _hg_levels(L):
        gb = g // HG_BASE
        ref = HG_BASE * ((u // gb) * gb + gb // 2) - 1
        up = (u % gb) >= gb // 2
        secs.append(np.where(up, (t > ref) & (t <= ru), (t > ru) & (t <= ref)).astype(np.float32))
        upper.append([bool(v) for v in up[:, 0]])
        ii, jj = idx[:, None], idx[None, :]
        masks.append(((ii // g) == (jj // g)) & ((ii % g) >= g // 2) & ((jj % g) < g // 2))
    mf = np.concatenate(secs, axis=0)
    mb = np.concatenate([s[::-1, ::-1] for s in secs], axis=0)
    H = L // 2
    sub = [masks[0][:H, :H]] + [m[:H, :H] for m in masks[2:]]
    kf = np.stack(sub).astype(np.float32)
    kb = np.stack([m[::-1, ::-1] for m in sub]).astype(np.float32)
    upper_fb = (tuple(tuple(x) for x in upper), tuple(tuple(x[::-1]) for x in upper))
    return mf, mb, kf, kb, upper_fb


def _hg_gate(z, lb):
    key = (1.0 - lb) * jax.nn.sigmoid(-z)
    return key, jnp.log2(1.0 - key)


def _hg_state_mask():
    r = lax.broadcasted_iota(jnp.int32, (LANES, 2 * HG_DK), 0) // HG_DV
    c = lax.broadcasted_iota(jnp.int32, (LANES, 2 * HG_DK), 1) // HG_DK
    return r == c


def _hg_bwd_state_kernel(z_ref, v_ref, lb_ref, m_ref, sb_ref, st_ref):
    L = HG_BLOCK
    nblk = z_ref.shape[0] // L

    @pl.when(pl.program_id(1) == 0)
    def _():
        st_ref[...] = jnp.zeros_like(st_ref)

    bmt = _hg_state_mask()
    for blk in range(nblk - 1, -1, -1):
        rows = slice(blk * L, (blk + 1) * L)
        key, lf = _hg_gate(z_ref[rows, :].astype(F32), lb_ref[...])
        e = _dot_split(m_ref[...], lf)
        ke = key * jnp.exp2(e[:L])
        dec = jnp.exp2(e[L:L + 1])
        for p in range(2):
            sb_ref[0, blk, p] = st_ref[p]
            kvt = _dot_tn(v_ref[rows, 128 * p:128 * p + 128].astype(BF16),
                          ke[:, 256 * p:256 * p + 256].astype(BF16))
            st_ref[p] = dec[:, 256 * p:256 * p + 256] * st_ref[p] + jnp.where(bmt, kvt, 0.0)


def _hg_scale_blocks(jobs, qp_ref, kp_ref, fac_ref, nblocks):
    for u in range(0, nblocks, 2):
        r0 = slice(HG_BASE * u, HG_BASE * u + HG_BASE)
        r1 = slice(HG_BASE * (u + 1), HG_BASE * (u + 2))
        for hd in range(HG_HEADS):
            ls = slice(HG_DK * hd, HG_DK * hd + HG_DK)
            q0, k0, q1, k1 = qp_ref[r0, ls], kp_ref[r0, ls], qp_ref[r1, ls], kp_ref[r1, ls]
            for dst_ref, row0, flags in jobs:
                a = (q0 if flags[u] else k0) * fac_ref[row0 + u:row0 + u + 1, ls]
                b = (q1 if flags[u + 1] else k1) * fac_ref[row0 + u + 1:row0 + u + 2, ls]
                dst_ref[HG_BASE * u:HG_BASE * (u + 2), ls] = jnp.concatenate([a, b], axis=0).astype(dst_ref.dtype)


def _hg_main_kernel(h_ref, lb_ref, nw_ref, mf_ref, mb_ref, kf_ref, kb_ref, sb_ref, o_ref,
                    st_ref, e2_ref, qp2_ref, kp2_ref, fac2_ref, s_ref, *, upper):
    L = h_ref.shape[0]
    nb = L // HG_BASE
    nlev = kf_ref.shape[0] + 1
    all_up = (True,) * nb

    @pl.when(pl.program_id(1) == 0)
    def _():
        st_ref[...] = jnp.zeros_like(st_ref)

    lo_v = _lane_lo_mask((L, LANES))
    bmt = _hg_state_mask()
    bones = jnp.where(_head_block_ones(), 1.0, 0.0).astype(BF16)
    osum = [jnp.zeros((L, LANES), F32), jnp.zeros((L, LANES), F32)]
    nbuf = nlev + 1
    for d in range(2):
        m_ref = mf_ref if d == 0 else mb_ref
        k_ref = kf_ref if d == 0 else kb_ref
        e_ref, qp_ref, kp_ref, fac_ref = e2_ref.at[d], qp2_ref.at[d], kp2_ref.at[d], fac2_ref.at[d]
        sbuf = [s_ref.at[d * nbuf + k] for k in range(nbuf)]
        key, lf = _hg_gate(h_ref[:, 512 + 512 * d:1024 + 512 * d].astype(F32), lb_ref[...])
        e_ref[...] = _dot_split(m_ref[...], lf)
        eb = e_ref[pl.ds(0, L), :]
        qp_ref[...] = h_ref[:, 0:512].astype(F32) * jnp.exp2(eb)
        kp_ref[...] = key * jnp.exp2(-eb)
        fac_ref[...] = jnp.exp2(e_ref[pl.ds(L, (nlev + 1) * nb), :])
        jobs = [(sbuf[lv - 1], (1 + lv) * nb, upper[d][lv - 1]) for lv in range(1, nlev)]
        jobs.append((sbuf[nlev - 1], 0, all_up))
        if d == 0:
            jobs.append((sbuf[nlev], nb, (False,) * nb))
        _hg_scale_blocks(jobs, qp_ref, kp_ref, fac_ref, nb)
        H = L // 2
        halves = (slice(0, H), slice(H, L))
        qh, kh = (1, 0) if d == 0 else (0, 1)
        masks = [k_ref[lv] > 0.5 for lv in range(nlev - 1)]

        def tiles(ref, hbs):
            return jnp.stack([ref[halves[hb], HG_DK * hd:HG_DK * hd + HG_DK]
                              for hd in range(HG_HEADS) for hb in hbs], axis=0).astype(BF16)

        def bdot(a, b):
            return lax.dot_general(a, b, (((2,), (2,)), ((0,), (0,))), preferred_element_type=F32)

        diag_all = jnp.where(masks[0][None], bdot(tiles(qp_ref, (0, 1)), tiles(kp_ref, (0, 1))), 0.0)
        for lv in range(2, nlev):
            t = tiles(sbuf[lv - 1], (0, 1))
            diag_all = jnp.where(masks[lv - 1][None], bdot(t, t), diag_all)
        diag_all = diag_all.astype(BF16)
        cross_all = bdot(tiles(sbuf[0], (qh,)), tiles(sbuf[0], (kh,))).astype(BF16)
        for p in range(2):
            v = h_ref[:, 1536 + 128 * p:1664 + 128 * p].astype(F32)
            vlo = jnp.where(lo_v, v, 0.0).astype(BF16)
            vhi = jnp.where(lo_v, 0.0, v).astype(BF16)
            qe_pair = []
            ke_pair = []
            for hh in range(2):
                ls = slice(HG_DK * (2 * p + hh), HG_DK * (2 * p + hh) + HG_DK)
                qe_pair.append(sbuf[nlev - 1][:, ls])
                if d == 0:
                    ke_pair.append(sbuf[nlev][:, ls])
            dg = [[diag_all[2 * (2 * p + hh) + hb] for hb in range(2)] for hh in range(2)]
            cross = [cross_all[2 * p + hh] for hh in range(2)]
            vs = jnp.concatenate([vlo, vhi], axis=0)
            if d == 0:
                o_q = _dot(jnp.concatenate([cross[0], dg[0][1], cross[1], dg[1][1]], axis=1), vs)
                o_k = _dot(jnp.concatenate([dg[0][0], dg[1][0]], axis=1),
                           jnp.concatenate([vlo[halves[0]], vhi[halves[0]]], axis=0))
                intra = jnp.concatenate([o_k, o_q], axis=0)
            else:
                o_q = _dot(jnp.concatenate([dg[0][0], cross[0], dg[1][0], cross[1]], axis=1), vs)
                o_k = _dot(jnp.concatenate([dg[0][1], dg[1][1]], axis=1),
                           jnp.concatenate([vlo[halves[1]], vhi[halves[1]]], axis=0))
                intra = jnp.concatenate([o_q, o_k], axis=0)
            state = st_ref[p] if d == 0 else sb_ref[0, 0, p]
            inter = _dot_nt(jnp.concatenate(qe_pair, axis=1), state.astype(BF16))
            osum[p] = osum[p] + intra + inter
            if d == 0:
                ps = slice(256 * p, 256 * p + 256)
                dec = fac_ref[nb - 1:nb, ps] * fac_ref[2 * nb - 1:2 * nb, ps]
                kvt = _dot_tn(v.astype(BF16), jnp.concatenate(ke_pair, axis=1))
                st_ref[p] = dec * st_ref[p] + jnp.where(bmt, kvt, 0.0)
    for p in range(2):
        o = osum[p]
        ms = _dot_split_rhs_const(o * o, bones) * (1.0 / HG_DV)
        g = h_ref[:, 1792 + 128 * p:1920 + 128 * p].astype(F32)
        y = o * lax.rsqrt(ms + RMS_EPS) * nw_ref[:, 128 * p:128 * p + 128] * jax.nn.sigmoid(g)
        o_ref[:, 128 * p:128 * p + 128] = y.astype(o_ref.dtype)


def _hgrn2(h, lb, nw, consts, B, S):
    L = HG_BLOCK
    nb = S // L
    mf, mb, kf, kb, mpre, upper = consts
    nsec = mf.shape[0]
    nfac = nsec - L
    sb = pl.pallas_call(
        _hg_bwd_state_kernel,
        grid=(B, nb // HG_PRE_BLOCKS),
        in_specs=[pl.BlockSpec((HG_PRE_BLOCKS * L, 512), lambda b, n: ((b * nb + nb) // HG_PRE_BLOCKS - 1 - n, 2)),
                  pl.BlockSpec((HG_PRE_BLOCKS * L, 256), lambda b, n: ((b * nb + nb) // HG_PRE_BLOCKS - 1 - n, 6)),
                  pl.BlockSpec((1, 512), lambda b, n: (0, 0)),
                  pl.BlockSpec((L + 8, L), lambda b, n: (0, 0))],
        out_specs=pl.BlockSpec((1, HG_PRE_BLOCKS, 2, LANES, 2 * HG_DK),
                               lambda b, n: (b, nb // HG_PRE_BLOCKS - 1 - n, 0, 0, 0)),
        out_shape=jax.ShapeDtypeStruct((B, nb, 2, LANES, 2 * HG_DK), F32),
        scratch_shapes=[pltpu.VMEM((2, LANES, 2 * HG_DK), F32)],
        compiler_params=_cparams(("parallel", "arbitrary")),
        name="hg_bwd_state",
    )(h, h, lb, mpre)
    return pl.pallas_call(
        functools.partial(_hg_main_kernel, upper=upper),
        grid=(B, nb),
        in_specs=[pl.BlockSpec((L, HG_W), lambda b, n: (b * nb + n, 0)),
                  pl.BlockSpec((1, 512), lambda b, n: (0, 0)),
                  pl.BlockSpec((1, 256), lambda b, n: (0, 0)),
                  pl.BlockSpec((nsec, L), lambda b, n: (0, 0)),
                  pl.BlockSpec((nsec, L), lambda b, n: (0, 0)),
                  pl.BlockSpec(kf.shape, lambda b, n: (0, 0, 0)),
                  pl.BlockSpec(kb.shape, lambda b, n: (0, 0, 0)),
                  pl.BlockSpec((1, 1, 2, LANES, 2 * HG_DK), lambda b, n: (b, n, 0, 0, 0))],
        out_specs=pl.BlockSpec((L, 256), lambda b, n: (b * nb + n, 0)),
        out_shape=jax.ShapeDtypeStruct((B * S, 256), BF16),
        scratch_shapes=[pltpu.VMEM((2, LANES, 2 * HG_DK), F32),
                        pltpu.VMEM((2, nsec, 512), F32),
                        pltpu.VMEM((2, L, 512), F32), pltpu.VMEM((2, L, 512), F32),
                        pltpu.VMEM((2, nfac, 512), F32),
                        pltpu.VMEM((2 * (kf.shape[0] + 2), L, 512), BF16)],
        compiler_params=_cparams(("parallel", "arbitrary")),
        name="hg_main",
    )(h, lb, nw, mf, mb, kf, kb, sb)


def _outproj_kernel(x_ref, r_ref, s_ref, g_ref, wr_ref, ws_ref, wg_ref, lg_ref, lb_ref, o_ref):
    mix = _dot(r_ref[...], wr_ref[...]) + _dot(s_ref[...], ws_ref[...]) + _dot(g_ref[...], wg_ref[...])
    o_ref[...] = _layer_norm(DEEPNORM_ALPHA * x_ref[...] + mix, lg_ref[...], lb_ref[...])


ROUTER_ROWS = 16


def _outproj_router_kernel(x_ref, r_ref, s_ref, g_ref, wr_ref, ws_ref, wg_ref, lg_ref, lb_ref, wrt_ref, tri_ref,
                           o_ref, ot_ref, route_ref, cnt_ref):
    @pl.when(pl.program_id(0) == 0)
    def _():
        cnt_ref[...] = jnp.zeros_like(cnt_ref)

    mix = _dot(r_ref[...], wr_ref[...]) + _dot(s_ref[...], ws_ref[...]) + _dot(g_ref[...], wg_ref[...])
    y = _layer_norm(DEEPNORM_ALPHA * x_ref[...] + mix, lg_ref[...], lb_ref[...])
    o_ref[...] = y
    ot_ref[...] = _to_token_tiles(y)
    tm = y.shape[0]
    w = wrt_ref[...]
    w_hi = w.astype(BF16)
    w_lo = (w - w_hi.astype(F32)).astype(BF16)
    y_hi = y.astype(BF16)
    y_lo = (y - y_hi.astype(F32)).astype(BF16)
    lt = _dot_nt(jnp.concatenate([w_hi, w_lo], axis=0), y_hi)
    logits = lt[:ROUTER_ROWS] + lt[ROUTER_ROWS:] + _dot_nt(w_hi, y_lo)
    row = lax.broadcasted_iota(jnp.int32, logits.shape, 0).astype(F32)
    neg = jnp.float32(-jnp.inf)
    l1 = jnp.where(row < N_EXPERTS, logits, neg)
    m1 = jnp.max(l1, axis=0, keepdims=True)
    i1 = jnp.min(jnp.where(l1 == m1, row, float(ROUTER_ROWS)), axis=0, keepdims=True)
    l2 = jnp.where(row == i1, neg, l1)
    m2 = jnp.max(l2, axis=0, keepdims=True)
    i2 = jnp.min(jnp.where(l2 == m2, row, float(ROUTER_ROWS)), axis=0, keepdims=True)
    e2 = jnp.exp(m2 - m1)
    den = 1.0 + e2
    picked = jnp.where((row == i1) | (row == i2), 1.0, 0.0)
    before = _dot(picked.astype(BF16), tri_ref[...])
    rank = before + cnt_ref[:, 0:1]
    r1 = jnp.sum(jnp.where(row == i1, rank, 0.0), axis=0, keepdims=True)
    r2 = jnp.sum(jnp.where(row == i2, rank, 0.0), axis=0, keepdims=True)
    cnt_ref[...] = cnt_ref[...] + jnp.sum(picked, axis=1, keepdims=True)
    lrow = lax.broadcasted_iota(jnp.int32, (8, tm), 0)
    route8 = (jnp.where(lrow == 0, i1, 0.0) + jnp.where(lrow == 1, i2, 0.0)
              + jnp.where(lrow == 2, 1.0 / den, 0.0) + jnp.where(lrow == 3, e2 / den, 0.0)
              + jnp.where(lrow == 4, r1, 0.0) + jnp.where(lrow == 5, r2, 0.0))
    route_ref[...] = jnp.concatenate([route8, jnp.zeros((LANES - 8, tm), F32)], axis=0).T


def _out_proj(x2d, ret, swa, hgo, w_r, w_s, w_g, ln_g, ln_b, w_router=None):
    T = x2d.shape[0]
    tm = 1024
    row = lambda i: (i, 0)
    full = lambda i: (0, 0)
    in_specs = [pl.BlockSpec((tm, D_MODEL), row),
                pl.BlockSpec((tm, 256), row), pl.BlockSpec((tm, 512), row), pl.BlockSpec((tm, 256), row),
                pl.BlockSpec((256, D_MODEL), full), pl.BlockSpec((512, D_MODEL), full),
                pl.BlockSpec((256, D_MODEL), full),
                pl.BlockSpec((1, D_MODEL), full), pl.BlockSpec((1, D_MODEL), full)]
    args = [x2d, ret, swa, hgo, w_r, w_s, w_g, ln_g, ln_b]
    if w_router is None:
        return pl.pallas_call(
            _outproj_kernel, grid=(T // tm,), in_specs=in_specs,
            out_specs=pl.BlockSpec((tm, D_MODEL), row),
            out_shape=jax.ShapeDtypeStruct((T, D_MODEL), F32),
            compiler_params=_cparams(("parallel",)), name="out_proj",
        )(*args)
    return pl.pallas_call(
        _outproj_router_kernel, grid=(T // tm,),
        in_specs=in_specs + [pl.BlockSpec((ROUTER_ROWS, D_MODEL), full),
                             pl.BlockSpec((tm, tm), full, pipeline_mode=pl.Buffered(1))],
        out_specs=[pl.BlockSpec((tm, D_MODEL), row), pl.BlockSpec((tm, 8, LANES), lambda i: (i, 0, 0)),
                   pl.BlockSpec((tm, LANES), row), pl.BlockSpec((ROUTER_ROWS, LANES), full)],
        out_shape=[jax.ShapeDtypeStruct((T, D_MODEL), F32), jax.ShapeDtypeStruct((T, 8, LANES), F32),
                   jax.ShapeDtypeStruct((T, LANES), F32), jax.ShapeDtypeStruct((ROUTER_ROWS, LANES), F32)],
        compiler_params=_cparams(("arbitrary",)), name="out_proj_router",
    )(*args, w_router, jnp.asarray(np.triu(np.ones((tm, tm), np.float32), 1), BF16))


FFN_CHUNKS = 1


def _ffn_kernel(x_ref, wg_ref, wu_ref, wd_ref, lg_ref, lb_ref, o_ref):
    x = x_ref[...]
    xb = x.astype(BF16)
    th = FFN_DIM // FFN_CHUNKS
    f = jnp.zeros(x.shape, F32)
    for c in range(FFN_CHUNKS):
        g = _dot(xb, wg_ref[:, c * th:(c + 1) * th])
        u = _dot(xb, wu_ref[:, c * th:(c + 1) * th])
        a = (g * jax.nn.sigmoid(g)) * u
        f = f + _dot(a.astype(BF16), wd_ref[c * th:(c + 1) * th, :])
    o_ref[...] = _layer_norm(DEEPNORM_ALPHA * x + f, lg_ref[...], lb_ref[...])


def _ffn(x2d, wg, wu, wd, ln_g, ln_b):
    T = x2d.shape[0]
    tm = 512
    once = pl.Buffered(1)
    return pl.pallas_call(
        _ffn_kernel, grid=(T // tm,),
        in_specs=[pl.BlockSpec((tm, D_MODEL), lambda i: (i, 0)),
                  pl.BlockSpec((D_MODEL, FFN_DIM), lambda i: (0, 0), pipeline_mode=once),
                  pl.BlockSpec((D_MODEL, FFN_DIM), lambda i: (0, 0), pipeline_mode=once),
                  pl.BlockSpec((FFN_DIM, D_MODEL), lambda i: (0, 0), pipeline_mode=once),
                  pl.BlockSpec((1, D_MODEL), lambda i: (0, 0)),
                  pl.BlockSpec((1, D_MODEL), lambda i: (0, 0))],
        out_specs=pl.BlockSpec((tm, D_MODEL), lambda i: (i, 0)),
        out_shape=jax.ShapeDtypeStruct((T, D_MODEL), F32),
        compiler_params=_cparams(("parallel",)), name="ffn",
    )(x2d, wg, wu, wd, ln_g, ln_b)


MOE_TILE = 512
MOE_TH = 1792
MOE_TOK = 512


def _moe_num_tiles(T):
    return (2 * T) // MOE_TILE + N_EXPERTS + 1


def _moe_tables(route, counts):
    T = route.shape[0]
    tm = MOE_TILE
    cnt = counts[:N_EXPERTS, 0].astype(jnp.int32)
    pc = ((cnt + tm - 1) // tm) * tm
    pend = jnp.cumsum(pc)
    pstart = pend - pc
    e = route[:, 0:2].astype(jnp.int32)
    rank = route[:, 4:6].astype(jnp.int32)
    sel = e[:, :, None] == jnp.arange(N_EXPERTS, dtype=jnp.int32)[None, None, :]
    pos = jnp.sum(jnp.where(sel, pstart[None, None, :], 0), axis=-1) + rank
    n_tiles = _moe_num_tiles(T)
    tile_row = jnp.arange(n_tiles, dtype=jnp.int32) * tm
    texp = jnp.minimum(jnp.sum((tile_row[:, None] >= pend[None, :]).astype(jnp.int32), axis=1), N_EXPERTS - 1)
    n_used = pend[-1] // tm
    last = jnp.sum(jnp.where(jnp.arange(n_tiles) == n_used - 1, texp, 0))
    texp = jnp.where(jnp.arange(n_tiles) < n_used, texp, last)
    return pos.reshape(-1), texp, n_used.reshape(1), pstart + cnt


def _to_token_tiles(x):
    cols = jnp.stack([x[:, LANES * c:LANES * c + LANES] for c in range(D_MODEL // LANES)], axis=0)
    return pltpu.einshape("csl->scl", cols)


def _from_token_tiles(xt):
    y = pltpu.einshape("scl->csl", xt)
    return jnp.concatenate([y[c] for c in range(D_MODEL // LANES)], axis=1)


MOE_DMA_UNROLL = 8


def _moe_dispatch_kernel(pos_ref, pad_ref, nused_ref, xt_hbm, xs_hbm, zero_ref, buf_ref, fsem, ssem, zsem):
    i = pl.program_id(0)
    n = pl.num_programs(0)
    tm = MOE_TOK
    n_tiles = xs_hbm.shape[0] // MOE_TILE
    slot = i % 3

    def zero_copy(row):
        return pltpu.make_async_copy(zero_ref, xs_hbm.at[pl.ds(row, MOE_TILE)], zsem)

    def fetch(step, s):
        return pltpu.make_async_copy(xt_hbm.at[pl.ds(step * tm, tm)], buf_ref.at[s], fsem.at[s])

    def drain_copies(s):
        for _ in range(2):
            pltpu.make_async_copy(buf_ref.at[s], xs_hbm.at[pl.ds(0, tm)], ssem.at[s]).wait()

    @pl.when(i == 0)
    def _():
        fetch(0, 0).start()
        zero_ref[...] = jnp.zeros_like(zero_ref)
        for e in range(N_EXPERTS):
            zero_copy(pad_ref[e]).start()
        for e in range(N_EXPERTS):
            zero_copy(pad_ref[e]).wait()
        for t in range(n_tiles - N_EXPERTS - 1, n_tiles):
            @pl.when(t >= nused_ref[0])
            def _():
                zero_copy(t * MOE_TILE).start()
                zero_copy(t * MOE_TILE).wait()

    @pl.when(i >= 2)
    def _():
        drain_copies((i + 1) % 3)

    @pl.when(i + 1 < n)
    def _():
        fetch(i + 1, (i + 1) % 3).start()

    fetch(i, slot).wait()

    def body(kk, c):
        k0 = kk * MOE_DMA_UNROLL
        t0 = i * tm + k0
        dst = [pos_ref[2 * t0 + u] for u in range(2 * MOE_DMA_UNROLL)]
        for u in range(2 * MOE_DMA_UNROLL):
            pltpu.make_async_copy(buf_ref.at[slot, k0 + u // 2], xs_hbm.at[dst[u]], ssem.at[slot]).start()
        return c

    lax.fori_loop(0, tm // MOE_DMA_UNROLL, body, 0)

    @pl.when(i == n - 1)
    def _():
        @pl.when(i >= 1)
        def _():
            drain_copies((i - 1) % 3)
        drain_copies(slot)


def _moe_dispatch(x1t, pos, padstart, n_used):
    T = x1t.shape[0]
    P = _moe_num_tiles(T) * MOE_TILE
    return pl.pallas_call(
        _moe_dispatch_kernel,
        grid_spec=pltpu.PrefetchScalarGridSpec(
            num_scalar_prefetch=3, grid=(T // MOE_TOK,),
            in_specs=[pl.BlockSpec(memory_space=pl.ANY)],
            out_specs=pl.BlockSpec(memory_space=pl.ANY),
            scratch_shapes=[pltpu.VMEM((MOE_TILE, 8, LANES), F32), pltpu.VMEM((3, MOE_TOK, 8, LANES), F32),
                            pltpu.SemaphoreType.DMA((3,)), pltpu.SemaphoreType.DMA((3,)),
                            pltpu.SemaphoreType.DMA]),
        out_shape=jax.ShapeDtypeStruct((P, 8, LANES), F32),
        compiler_params=_cparams(("arbitrary",)), name="moe_dispatch",
    )(pos, padstart, n_used, x1t)


def _moe_group_kernel(texp_ref, nused_ref, xs_ref, wg_ref, wu_ref, wd_ref, y_ref, xb_ref, acc_ref):
    i = pl.program_id(0)
    j = pl.program_id(1)

    @pl.when(i < nused_ref[0])
    def _():
        @pl.when(j == 0)
        def _():
            acc_ref[...] = jnp.zeros_like(acc_ref)

        xb = _from_token_tiles(xs_ref[...]).astype(BF16)
        g = _dot(xb, wg_ref[...])
        u = _dot(xb, wu_ref[...])
        a = (g * jax.nn.sigmoid(g)) * u
        acc_ref[...] += _dot(a.astype(BF16), wd_ref[...])

        @pl.when(j == pl.num_programs(1) - 1)
        def _():
            y_ref[...] = _to_token_tiles(acc_ref[...])

    @pl.when((i >= nused_ref[0]) & (j == 0))
    def _():
        y_ref[...] = jnp.zeros_like(y_ref)


def _moe_group(xs, texp, n_used, wg, wu, wd):
    P = xs.shape[0]
    tm, th = MOE_TILE, MOE_TH
    nh = EXPERT_DIM // th

    def row_map(i, j, texp, nused):
        return (jnp.minimum(i, nused[0] - 1), 0, 0)

    def col(i, j, nused):
        return jnp.where(i < nused[0], j, nh - 1)

    return pl.pallas_call(
        _moe_group_kernel,
        grid_spec=pltpu.PrefetchScalarGridSpec(
            num_scalar_prefetch=2, grid=(P // tm, nh),
            in_specs=[pl.BlockSpec((tm, 8, LANES), row_map),
                      pl.BlockSpec((None, D_MODEL, th), lambda i, j, texp, nused: (texp[i], 0, col(i, j, nused))),
                      pl.BlockSpec((None, D_MODEL, th), lambda i, j, texp, nused: (texp[i], 0, col(i, j, nused))),
                      pl.BlockSpec((None, th, D_MODEL), lambda i, j, texp, nused: (texp[i], col(i, j, nused), 0))],
            out_specs=pl.BlockSpec((tm, 8, LANES), lambda i, j, texp, nused: (i, 0, 0)),
            scratch_shapes=[pltpu.VMEM((tm, D_MODEL), BF16), pltpu.VMEM((tm, D_MODEL), F32)]),
        out_shape=jax.ShapeDtypeStruct((P, 8, LANES), F32),
        compiler_params=_cparams(("arbitrary", "arbitrary")), name="moe_group",
    )(texp, n_used, xs, wg, wu, wd)


def _moe_combine_kernel(pos_ref, x_ref, route_ref, y_hbm, lg_ref, lb_ref, o_ref, buf_ref, sem):
    i = pl.program_id(0)
    tm = x_ref.shape[0]

    def issue(tile, slot):
        def body(kk, c):
            k0 = kk * MOE_DMA_UNROLL
            t0 = tile * tm + k0
            src = [pos_ref[2 * t0 + u] for u in range(2 * MOE_DMA_UNROLL)]
            for u in range(2 * MOE_DMA_UNROLL):
                pltpu.make_async_copy(y_hbm.at[src[u]], buf_ref.at[slot, (u % 2) * tm + k0 + u // 2],
                                      sem.at[slot]).start()
            return c
        lax.fori_loop(0, tm // MOE_DMA_UNROLL, body, 0)

    @pl.when(i == 0)
    def _():
        issue(0, 0)

    slot = i % 2
    pltpu.make_async_copy(y_hbm.at[pl.ds(0, 2 * tm)], buf_ref.at[slot], sem.at[slot]).wait()

    @pl.when(i + 1 < pl.num_programs(0))
    def _():
        issue(i + 1, 1 - slot)

    a = _from_token_tiles(buf_ref[slot, pl.ds(0, tm)])
    b = _from_token_tiles(buf_ref[slot, pl.ds(tm, tm)])
    f = route_ref[:, 2:3] * a + route_ref[:, 3:4] * b
    o_ref[...] = _layer_norm(DEEPNORM_ALPHA * x_ref[...] + f, lg_ref[...], lb_ref[...])


def _moe_combine(x1, route, y, pos, ln_g, ln_b):
    T = x1.shape[0]
    tm = MOE_TOK
    return pl.pallas_call(
        _moe_combine_kernel,
        grid_spec=pltpu.PrefetchScalarGridSpec(
            num_scalar_prefetch=1, grid=(T // tm,),
            in_specs=[pl.BlockSpec((tm, D_MODEL), lambda i, pos: (i, 0)),
                      pl.BlockSpec((tm, LANES), lambda i, pos: (i, 0)),
                      pl.BlockSpec(memory_space=pl.ANY),
                      pl.BlockSpec((1, D_MODEL), lambda i, pos: (0, 0)),
                      pl.BlockSpec((1, D_MODEL), lambda i, pos: (0, 0))],
            out_specs=pl.BlockSpec((tm, D_MODEL), lambda i, pos: (i, 0)),
            scratch_shapes=[pltpu.VMEM((2, 2 * tm, 8, LANES), F32), pltpu.SemaphoreType.DMA((2,))]),
        out_shape=jax.ShapeDtypeStruct((T, D_MODEL), F32),
        compiler_params=_cparams(("arbitrary",)), name="moe_combine",
    )(pos, x1, route, y, ln_g, ln_b)


def _moe(x1, x1t, route, counts, wg, wu, wd, ln_g, ln_b):
    pos, texp, n_used, padstart = _moe_tables(route, counts)
    xs = _moe_dispatch(x1t, pos, padstart, n_used)
    y = _moe_group(xs, texp, n_used, wg, wu, wd)
    return _moe_combine(x1, route, y, pos, ln_g, ln_b)


def _t5_bucket(rel):
    nb = REL_BUCKETS // 2
    ret = (rel > 0).astype(np.int32) * nb
    n = np.abs(rel)
    max_exact = nb // 2
    large = max_exact + (np.log(np.maximum(n, 1) / max_exact) / np.log(REL_MAX_DIST / max_exact)
                         * (nb - max_exact)).astype(np.int32)
    large = np.minimum(large, nb - 1)
    return ret + np.where(n < max_exact, n, large)


def _swa_head_order():
    return [h for c in range(4) for h in (c, 4 + c)]


def _prepare(w_in, w_out, ret_decay, swa_sink, rel_bias, hg_lb, hg_norm_w, moe_router, seq_lens):
    order = _swa_head_order()
    w_in_p = jnp.concatenate(
        [w_in[:, :, 1792:3840], w_in[:, :, 0:1024]]
        + [w_in[:, :, 1024 + 64 * h:1088 + 64 * h] for h in order]
        + [w_in[:, :, 1536:1792]], axis=2).astype(BF16)
    w_r = w_out[:, 0:256, :].astype(BF16)
    w_s = jnp.concatenate([w_out[:, 256 + 64 * h:320 + 64 * h, :] for h in order], axis=1).astype(BF16)
    w_g = w_out[:, 768:1024, :].astype(BF16)

    rd = ret_decay.astype(F32)
    rdp = jnp.repeat(rd.reshape(DEPTH, 4, 2), 64, axis=-1)
    rdp = jnp.concatenate([rdp, rdp], axis=1)
    rdh = jnp.broadcast_to(rd.reshape(DEPTH, 8, 1), (DEPTH, 8, LANES))

    W = SWA_BLOCK
    rel = np.arange(3 * W)[None, :] - W - np.arange(W)[:, None]
    onehot = jnp.asarray(_t5_bucket(rel)[..., None] == np.arange(REL_BUCKETS), F32)
    bias = jnp.einsum("qkb,bh->hqk", onehot, rel_bias.astype(F32), precision=lax.Precision.HIGHEST)
    band = jnp.asarray(np.abs(rel) <= WINDOW)
    bias_kq = jnp.transpose(jnp.where(band[None], bias, MASK_VALUE), (0, 2, 1))
    order = _swa_head_order()
    bias_t = jnp.concatenate([bias_kq[h] for h in order], axis=1) * LOG2_E
    sink = swa_sink.astype(F32) * LOG2_E
    sink_t = jnp.concatenate([jnp.broadcast_to(sink[:, h, None, None], (DEPTH, 8, W)) for h in order], axis=2)

    p = jax.nn.softmax(hg_lb.astype(F32), axis=0)
    lower = (jnp.cumsum(p, axis=0) - p[0]).reshape(DEPTH, 1, HG_HEADS * HG_DK)
    nw = hg_norm_w.astype(F32).reshape(DEPTH, 1, HG_HEADS * HG_DV)

    mf, mb, kf, kb, upper = _hg_constants(HG_BLOCK)
    L = HG_BLOCK
    mpre = np.concatenate([np.tril(np.ones((L, L), np.float32), -1), np.ones((8, L), np.float32)], axis=0)
    consts = (jnp.asarray(mf, BF16), jnp.asarray(mb, BF16), jnp.asarray(kf, F32), jnp.asarray(kb, F32),
              jnp.asarray(mpre, BF16), upper)

    half = RET_DK // 2
    inv = ROPE_BASE ** (-np.arange(half, dtype=np.float32) / half)
    rope = {}
    for S in set(seq_lens):
        ang = jnp.arange(S, dtype=F32)[:, None] * jnp.asarray(inv)[None, :]
        c32, s32 = jnp.cos(ang), jnp.sin(ang)
        rope[S] = (jnp.tile(c32, (1, 4)), jnp.tile(jnp.concatenate([-s32, s32], axis=1), (1, 2)))

    router = jnp.pad(jnp.transpose(moe_router.astype(F32), (0, 2, 1)),
                     ((0, 0), (0, ROUTER_ROWS - N_EXPERTS), (0, 0)))
    return dict(w_in=w_in_p, w_r=w_r, w_s=w_s, w_g=w_g, rdp=rdp, rdh=rdh, bias=bias_t, sink=sink_t,
                lower=lower, nw=nw, consts=consts, rope=rope, router=router)


def _trunk(x, prm, ln1_g, ln1_b, ln2_g, ln2_b, ffn_w, moe_w):
    B, S, _ = x.shape
    x2d = x.reshape(B * S, D_MODEL)
    cos_t, sin_t = prm["rope"][S]
    for l in range(DEPTH):
        h = _in_proj(x2d, prm["w_in"][l])
        ret = _retention(h, cos_t, sin_t, prm["rdp"][l], prm["rdh"][l], B, S)
        swa = _swa(h, prm["bias"], prm["sink"][l], B, S)
        hgo = _hgrn2(h, prm["lower"][l], prm["nw"][l], prm["consts"], B, S)
        g1, b1 = ln1_g[l].reshape(1, D_MODEL), ln1_b[l].reshape(1, D_MODEL)
        g2, b2 = ln2_g[l].reshape(1, D_MODEL), ln2_b[l].reshape(1, D_MODEL)
        i = l // 2
        if l % 2 == 0:
            x1 = _out_proj(x2d, ret, swa, hgo, prm["w_r"][l], prm["w_s"][l], prm["w_g"][l], g1, b1)
            wg, wu, wd = ffn_w
            x2d = _ffn(x1, wg[i], wu[i], wd[i], g2, b2)
        else:
            x1, x1t, route, counts = _out_proj(x2d, ret, swa, hgo, prm["w_r"][l], prm["w_s"][l], prm["w_g"][l],
                                               g1, b1, prm["router"][i])
            wg, wu, wd = moe_w
            x2d = _moe(x1, x1t, route, counts, wg[i], wu[i], wd[i], g2, b2)
    return x2d.reshape(B, S, D_MODEL)


def kernel(x_prompt, x_sample, w_in, w_out, ret_decay, swa_sink, rel_bias, hg_lb, hg_norm_w, ln1_g, ln1_b,
           ln2_g, ln2_b, ffn_w_gate, ffn_w_up, ffn_w_down, moe_router, moe_w_gate, moe_w_up, moe_w_down):
    prm = _prepare(w_in, w_out, ret_decay, swa_sink, rel_bias, hg_lb, hg_norm_w, moe_router,
                   (x_prompt.shape[1], x_sample.shape[1]))
    ffn_w = (ffn_w_gate.astype(BF16), ffn_w_up.astype(BF16), ffn_w_down.astype(BF16))
    moe_w = (moe_w_gate.astype(BF16), moe_w_up.astype(BF16), moe_w_down.astype(BF16))
    outs = []
    for x in (x_prompt, x_sample):
        outs.append(_trunk(x, prm, ln1_g, ln1_b, ln2_g, ln2_b, ffn_w, moe_w))
    return tuple(outs)
```

```python
import functools

import numpy as np
import jax
import jax.numpy as jnp
from jax import lax
from jax.experimental import pallas as pl
from jax.experimental.pallas import tpu as pltpu

F32 = jnp.float32
BF16 = jnp.bfloat16

D_MODEL = 1024
DEPTH = 2
RET_HEADS = 4
RET_DK = 64
RET_CHUNK = 128
ROPE_BASE = 10000.0
SWA_Q_HEADS = 8
SWA_KV_HEADS = 2
SWA_DH = 64
WINDOW = 128
SWA_BLOCK = 128
REL_BUCKETS = 32
REL_MAX_DIST = 128
MASK_VALUE = -1e30
LOG2_E = 1.4426950408889634
HG_HEADS = 4
HG_DK = 128
HG_DV = 64
FFN_DIM = 2816
N_EXPERTS = 8
EXPERT_DIM = 3584
LN_EPS = 1e-5
RMS_EPS = 1e-6
DEEPNORM_ALPHA = (2.0 * DEPTH) ** 0.25

LANES = 128
VMEM_LIMIT = 56 * 1024 * 1024

HG_W = 2048
RET_W = 1024
SWA_W = 768
IN_COLS = HG_W + RET_W + SWA_W

HG_BLOCK = 256
HG_BASE = 8
HG_PRE_BLOCKS = 8
RET_TILE = 2048
SWA_TILE = 2048


def _cparams(sem):
    return pltpu.CompilerParams(dimension_semantics=sem, vmem_limit_bytes=VMEM_LIMIT)


def _dot(a, b):
    return jnp.dot(a, b, preferred_element_type=F32)


def _dot_nt(a, b):
    return lax.dot_general(a, b, (((1,), (1,)), ((), ())), preferred_element_type=F32)


def _dot_tn(a, b):
    return lax.dot_general(a, b, (((0,), (0,)), ((), ())), preferred_element_type=F32)


def _dot_split(a_bf16, x):
    hi = x.astype(BF16)
    lo = (x - hi.astype(F32)).astype(BF16)
    return _dot(a_bf16, hi) + _dot(a_bf16, lo)


def _dot_split_rhs_const(x, b_bf16):
    hi = x.astype(BF16)
    lo = (x - hi.astype(F32)).astype(BF16)
    return _dot(hi, b_bf16) + _dot(lo, b_bf16)


def _layer_norm(y, g, b):
    mu = jnp.mean(y, axis=-1, keepdims=True)
    yc = y - mu
    var = jnp.mean(yc * yc, axis=-1, keepdims=True)
    return yc * lax.rsqrt(var + LN_EPS) * g + b


def _lane_lo_mask(shape):
    return lax.broadcasted_iota(jnp.int32, shape, len(shape) - 1) < (LANES // 2)


def _head_block_ones():
    r = lax.broadcasted_iota(jnp.int32, (LANES, LANES), 0) // 64
    c = lax.broadcasted_iota(jnp.int32, (LANES, LANES), 1) // 64
    return r == c


def _inproj_kernel(x_ref, w_ref, o_ref):
    o_ref[...] = _dot(x_ref[...].astype(BF16), w_ref[...]).astype(o_ref.dtype)


def _in_proj(x2d, w):
    T = x2d.shape[0]
    tm = 1024
    return pl.pallas_call(
        _inproj_kernel,
        grid=(T // tm,),
        in_specs=[pl.BlockSpec((tm, D_MODEL), lambda i: (i, 0)),
                  pl.BlockSpec((D_MODEL, IN_COLS), lambda i: (0, 0), pipeline_mode=pl.Buffered(1))],
        out_specs=pl.BlockSpec((tm, IN_COLS), lambda i: (i, 0)),
        out_shape=jax.ShapeDtypeStruct((T, IN_COLS), BF16),
        compiler_params=_cparams(("parallel",)),
        name="in_proj",
    )(x2d, w)


def _rotary(x, cos, sin_signed, even):
    sw = jnp.where(even, pltpu.roll(x, 96, 1), pltpu.roll(x, 32, 1))
    return x * cos + sw * sin_signed


def _ret_tables(rdp_ref):
    lg = jnp.log1p(-jnp.exp(rdp_ref[...]))
    return lg


def _ret_bwd_state_kernel(h_ref, cos_ref, sin_ref, rdp_ref, sb_ref, st_ref):
    C = RET_CHUNK
    nc = h_ref.shape[0] // C

    @pl.when(pl.program_id(1) == 0)
    def _():
        st_ref[...] = jnp.zeros_like(st_ref)

    lg = _ret_tables(rdp_ref)
    rows = lax.broadcasted_iota(jnp.int32, (C, LANES), 0).astype(F32)
    even = (lax.broadcasted_iota(jnp.int32, (C, LANES), 1) // 32) % 2 == 0
    bm = _head_block_ones()
    for c in range(nc - 1, -1, -1):
        sl = pl.ds(c * C, C)
        cos = cos_ref[sl, :]
        sin = sin_ref[sl, :]
        for p in range(2):
            lgb = lg[2 + p:3 + p, :]
            kwb = jnp.exp(lgb * rows)
            decb = jnp.where(bm, jnp.exp(lgb * float(C)), 0.0)
            k = _rotary(h_ref[sl, 256 + 128 * p:384 + 128 * p].astype(F32), cos, sin, even) * (RET_DK ** -0.5)
            v = h_ref[sl, 512 + 128 * p:640 + 128 * p].astype(F32)
            sb_ref[0, c, p] = st_ref[p]
            kv = _dot_tn((k * kwb).astype(BF16), v.astype(BF16))
            st_ref[p] = decb * st_ref[p] + jnp.where(bm, kv, 0.0)


def _ret_main_kernel(h_ref, cos_ref, sin_ref, rdp_ref, rdh_ref, sb_ref, o_ref, st_ref):
    C = RET_CHUNK
    nc = h_ref.shape[0] // C

    @pl.when(pl.program_id(1) == 0)
    def _():
        st_ref[...] = jnp.zeros_like(st_ref)

    lg = _ret_tables(rdp_ref)
    lgh = jnp.log1p(-jnp.exp(rdh_ref[...]))
    rows = lax.broadcasted_iota(jnp.int32, (C, LANES), 0).astype(F32)
    cols = lax.broadcasted_iota(jnp.int32, (C, LANES), 1).astype(F32)
    rel = rows - cols
    even = (lax.broadcasted_iota(jnp.int32, (C, LANES), 1) // 32) % 2 == 0
    lo = _lane_lo_mask((C, LANES))
    bm = _head_block_ones()
    bones = jnp.where(bm, 1.0, 0.0).astype(BF16)

    dmat = []
    for hd in range(RET_HEADS):
        f = jnp.where(rel >= 0, jnp.exp(lgh[hd:hd + 1, :] * jnp.maximum(rel, 0.0)), 0.0)
        b = jnp.where(rel <= 0, jnp.exp(lgh[4 + hd:5 + hd, :] * jnp.maximum(-rel, 0.0)), 0.0)
        dmat.append(f + b)

    for c in range(nc):
        sl = pl.ds(c * C, C)
        cos = cos_ref[sl, :]
        sin = sin_ref[sl, :]
        for p in range(2):
            lgf = lg[p:p + 1, :]
            lgb = lg[2 + p:3 + p, :]
            q = _rotary(h_ref[sl, 128 * p:128 * p + 128].astype(F32), cos, sin, even)
            k = _rotary(h_ref[sl, 256 + 128 * p:384 + 128 * p].astype(F32), cos, sin, even) * (RET_DK ** -0.5)
            v = h_ref[sl, 512 + 128 * p:640 + 128 * p].astype(F32)
            g = h_ref[sl, 768 + 128 * p:896 + 128 * p].astype(F32)
            kb = k.astype(BF16)
            vb = v.astype(BF16)
            qs = jnp.concatenate([jnp.where(lo, q, 0.0), jnp.where(lo, 0.0, q)], axis=0).astype(BF16)
            s = _dot_nt(qs, kb)
            pm = jnp.concatenate([s[:C] * dmat[2 * p], s[C:] * dmat[2 * p + 1]], axis=1).astype(BF16)
            vs = jnp.concatenate([jnp.where(lo, v, 0.0), jnp.where(lo, 0.0, v)], axis=0).astype(BF16)
            intra = _dot(pm, vs)
            qwf = jnp.exp(lgf * (rows + 1.0))
            qwb = jnp.exp(lgb * (float(C) - rows))
            qi = jnp.concatenate([q * qwf, q * qwb], axis=1).astype(BF16)
            sst = jnp.concatenate([st_ref[p], sb_ref[0, c, p]], axis=0).astype(BF16)
            o = intra + _dot(qi, sst)
            kwf = jnp.exp(lgf * (float(C) - 1.0 - rows))
            decf = jnp.where(bm, jnp.exp(lgf * float(C)), 0.0)
            kv = _dot_tn((k * kwf).astype(BF16), vb)
            st_ref[p] = decf * st_ref[p] + jnp.where(bm, kv, 0.0)
            ms = _dot_split_rhs_const(o * o, bones) * (1.0 / RET_DK)
            y = o * lax.rsqrt(ms + RMS_EPS) * (g * jax.nn.sigmoid(g))
            o_ref[sl, 128 * p:128 * p + 128] = y.astype(o_ref.dtype)


def _retention(h, cos_t, sin_t, rdp, rdh, B, S):
    L = RET_TILE
    nb = S // L
    ncb = L // RET_CHUNK
    n_chunks = S // RET_CHUNK
    colblk = HG_W // RET_W
    sb = pl.pallas_call(
        _ret_bwd_state_kernel,
        grid=(B, nb),
        in_specs=[pl.BlockSpec((L, RET_W), lambda b, n: (b * nb + nb - 1 - n, colblk)),
                  pl.BlockSpec((L, LANES), lambda b, n: (nb - 1 - n, 0)),
                  pl.BlockSpec((L, LANES), lambda b, n: (nb - 1 - n, 0)),
                  pl.BlockSpec((8, LANES), lambda b, n: (0, 0))],
        out_specs=pl.BlockSpec((1, ncb, 2, LANES, LANES), lambda b, n: (b, nb - 1 - n, 0, 0, 0)),
        out_shape=jax.ShapeDtypeStruct((B, n_chunks, 2, LANES, LANES), F32),
        scratch_shapes=[pltpu.VMEM((2, LANES, LANES), F32)],
        compiler_params=_cparams(("parallel", "arbitrary")),
        name="ret_bwd_state",
    )(h, cos_t, sin_t, rdp)
    return pl.pallas_call(
        _ret_main_kernel,
        grid=(B, nb),
        in_specs=[pl.BlockSpec((L, RET_W), lambda b, n: (b * nb + n, colblk)),
                  pl.BlockSpec((L, LANES), lambda b, n: (n, 0)),
                  pl.BlockSpec((L, LANES), lambda b, n: (n, 0)),
                  pl.BlockSpec((8, LANES), lambda b, n: (0, 0)),
                  pl.BlockSpec((8, LANES), lambda b, n: (0, 0)),
                  pl.BlockSpec((1, ncb, 2, LANES, LANES), lambda b, n: (b, n, 0, 0, 0))],
        out_specs=pl.BlockSpec((L, 256), lambda b, n: (b * nb + n, 0)),
        out_shape=jax.ShapeDtypeStruct((B * S, 256), BF16),
        scratch_shapes=[pltpu.VMEM((2, LANES, LANES), F32)],
        compiler_params=_cparams(("parallel", "arbitrary")),
        name="ret_main",
    )(h, cos_t, sin_t, rdp, rdh, sb)


def _swa_kernel(q_ref, kp_ref, kc_ref, kn_ref, vp_ref, vc_ref, vn_ref, bias_ref, sink_ref, o_ref):
    W = SWA_BLOCK
    nq = q_ref.shape[0] // W
    n = pl.program_id(1)
    last = pl.num_programs(1) - 1
    kall = jnp.concatenate([kp_ref[...], kc_ref[...], kn_ref[...]], axis=0).astype(BF16)
    vt = jnp.concatenate([vp_ref[...], vc_ref[...], vn_ref[...]], axis=0).astype(F32).T.astype(BF16)
    lo = _lane_lo_mask((W, LANES))
    edge_first = jnp.where(n > 0, 0.0, MASK_VALUE)
    edge_last = jnp.where(n < last, 0.0, MASK_VALUE)
    sink = sink_ref[0:1, :]
    for r in range(nq):
        qs = []
        for c in range(4):
            q = q_ref[pl.ds(r * W, W), 128 * c:128 * c + 128].astype(F32) * (SWA_DH ** -0.5 * LOG2_E)
            qs += [jnp.where(lo, q, 0.0), jnp.where(lo, 0.0, q)]
        qs = jnp.concatenate(qs, axis=0).astype(BF16)
        s = _dot_nt(kall[r * W:(r + 3) * W], qs) + bias_ref[...]
        if r == 0:
            s = jnp.concatenate([s[:W] + edge_first, s[W:]], axis=0)
        if r == nq - 1:
            s = jnp.concatenate([s[:2 * W], s[2 * W:] + edge_last], axis=0)
        m = jnp.maximum(jnp.max(s, axis=0, keepdims=True), sink)
        e = jnp.exp2(s - m)
        den = jnp.sum(e, axis=0, keepdims=True) + jnp.exp2(sink - m)
        ot = _dot(vt[:, r * W:(r + 3) * W], e.astype(BF16)) / den
        for c in range(4):
            sel = jnp.concatenate([ot[0:SWA_DH, 2 * c * W:(2 * c + 1) * W],
                                   ot[SWA_DH:2 * SWA_DH, (2 * c + 1) * W:(2 * c + 2) * W]], axis=0)
            o_ref[pl.ds(r * W, W), 128 * c:128 * c + 128] = sel.T.astype(o_ref.dtype)


def _swa(h, bias_t, sink_t, B, S):
    L = SWA_TILE
    W = SWA_BLOCK
    nb = S // L
    r = L // W
    nblk = S // W
    qcol = (HG_W + RET_W) // 512
    kcol = (HG_W + RET_W + 512) // LANES
    vcol = kcol + 1

    def prev_map(col):
        return lambda b, n: (b * nblk + jnp.maximum(n * r - 1, 0), col)

    def cur_map(col):
        return lambda b, n: (b * nb + n, col)

    def next_map(col):
        return lambda b, n: (b * nblk + jnp.minimum(n * r + r, nblk - 1), col)

    return pl.pallas_call(
        _swa_kernel,
        grid=(B, nb),
        in_specs=[pl.BlockSpec((L, 512), cur_map(qcol)),
                  pl.BlockSpec((W, LANES), prev_map(kcol)),
                  pl.BlockSpec((L, LANES), cur_map(kcol)),
                  pl.BlockSpec((W, LANES), next_map(kcol)),
                  pl.BlockSpec((W, LANES), prev_map(vcol)),
                  pl.BlockSpec((L, LANES), cur_map(vcol)),
                  pl.BlockSpec((W, LANES), next_map(vcol)),
                  pl.BlockSpec((3 * W, SWA_Q_HEADS * W), lambda b, n: (0, 0)),
                  pl.BlockSpec((8, SWA_Q_HEADS * W), lambda b, n: (0, 0))],
        out_specs=pl.BlockSpec((L, 512), lambda b, n: (b * nb + n, 0)),
        out_shape=jax.ShapeDtypeStruct((B * S, 512), BF16),
        compiler_params=_cparams(("parallel", "parallel")),
        name="swa",
    )(h, h, h, h, h, h, h, bias_t, sink_t)


def _hg_levels(L):
    g = L
    out = []
    while g >= 2 * HG_BASE:
        out.append(g)
        g //= 2
    return out


def _hg_constants(L):
    nb = L // HG_BASE
    idx = np.arange(L)
    t = idx[None, :]
    i = idx[:, None]
    r = HG_BASE * (i // HG_BASE) + HG_BASE // 2 - 1
    secs = [((t > r) & (t <= i)).astype(np.float32) - ((t > i) & (t <= r)).astype(np.float32)]
    u = np.arange(nb)[:, None]
    ru = HG_BASE * u + HG_BASE // 2 - 1
    secs.append((t <= ru).astype(np.float32))
    secs.append((t > ru).astype(np.float32))
    masks = [((idx[:, None] // HG_BASE) == (idx[None, :] // HG_BASE)) & (idx[:, None] >= idx[None, :])]
    upper = []
    for g in _hg_levels(L):
        gb = g // HG_BASE
        ref = HG_BASE * ((u // gb) * gb + gb // 2) - 1
        up = (u % gb) >= gb // 2
        secs.append(np.where(up, (t > ref) & (t <= ru), (t > ru) & (t <= ref)).astype(np.float32))
        upper.append([bool(v) for v in up[:, 0]])
        ii, jj = idx[:, None], idx[None, :]
        masks.append(((ii // g) == (jj // g)) & ((ii % g) >= g // 2) & ((jj % g) < g // 2))
    mf = np.concatenate(secs, axis=0)
    mb = np.concatenate([s[::-1, ::-1] for s in secs], axis=0)
    H = L // 2
    sub = [masks[0][:H, :H]] + [m[:H, :H] for m in masks[2:]]
    kf = np.stack(sub).astype(np.float32)
    kb = np.stack([m[::-1, ::-1] for m in sub]).astype(np.float32)
    upper_fb = (tuple(tuple(x) for x in upper), tuple(tuple(x[::-1]) for x in upper))
    return mf, mb, kf, kb, upper_fb


def _hg_gate(z, lb):
    key = (1.0 - lb) * jax.nn.sigmoid(-z)
    return key, jnp.log2(1.0 - key)


def _hg_state_mask():
    r = lax.broadcasted_iota(jnp.int32, (LANES, 2 * HG_DK), 0) // HG_DV
    c = lax.broadcasted_iota(jnp.int32, (LANES, 2 * HG_DK), 1) // HG_DK
    return r == c


def _hg_bwd_state_kernel(z_ref, v_ref, lb_ref, m_ref, sb_ref, st_ref):
    L = HG_BLOCK
    nblk = z_ref.shape[0] // L

    @pl.when(pl.program_id(1) == 0)
    def _():
        st_ref[...] = jnp.zeros_like(st_ref)

    bmt = _hg_state_mask()
    for blk in range(nblk - 1, -1, -1):
        rows = slice(blk * L, (blk + 1) * L)
        key, lf = _hg_gate(z_ref[rows, :].astype(F32), lb_ref[...])
        e = _dot_split(m_ref[...], lf)
        ke = key * jnp.exp2(e[:L])
        dec = jnp.exp2(e[L:L + 1])
        for p in range(2):
            sb_ref[0, blk, p] = st_ref[p]
            kvt = _dot_tn(v_ref[rows, 128 * p:128 * p + 128].astype(BF16),
                          ke[:, 256 * p:256 * p + 256].astype(BF16))
            st_ref[p] = dec[:, 256 * p:256 * p + 256] * st_ref[p] + jnp.where(bmt, kvt, 0.0)


def _hg_scale_blocks(jobs, qp_ref, kp_ref, fac_ref, nblocks):
    for u in range(0, nblocks, 2):
        r0 = slice(HG_BASE * u, HG_BASE * u + HG_BASE)
        r1 = slice(HG_BASE * (u + 1), HG_BASE * (u + 2))
        for hd in range(HG_HEADS):
            ls = slice(HG_DK * hd, HG_DK * hd + HG_DK)
            q0, k0, q1, k1 = qp_ref[r0, ls], kp_ref[r0, ls], qp_ref[r1, ls], kp_ref[r1, ls]
            for dst_ref, row0, flags in jobs:
                a = (q0 if flags[u] else k0) * fac_ref[row0 + u:row0 + u + 1, ls]
                b = (q1 if flags[u + 1] else k1) * fac_ref[row0 + u + 1:row0 + u + 2, ls]
                dst_ref[HG_BASE * u:HG_BASE * (u + 2), ls] = jnp.concatenate([a, b], axis=0).astype(dst_ref.dtype)


def _hg_main_kernel(h_ref, lb_ref, nw_ref, mf_ref, mb_ref, kf_ref, kb_ref, sb_ref, o_ref,
                    st_ref, e2_ref, qp2_ref, kp2_ref, fac2_ref, s_ref, *, upper):
    L = h_ref.shape[0]
    nb = L // HG_BASE
    nlev = kf_ref.shape[0] + 1
    all_up = (True,) * nb

    @pl.when(pl.program_id(1) == 0)
    def _():
        st_ref[...] = jnp.zeros_like(st_ref)

    lo_v = _lane_lo_mask((L, LANES))
    bmt = _hg_state_mask()
    bones = jnp.where(_head_block_ones(), 1.0, 0.0).astype(BF16)
    osum = [jnp.zeros((L, LANES), F32), jnp.zeros((L, LANES), F32)]
    nbuf = nlev + 1
    for d in range(2):
        m_ref = mf_ref if d == 0 else mb_ref
        k_ref = kf_ref if d == 0 else kb_ref
        e_ref, qp_ref, kp_ref, fac_ref = e2_ref.at[d], qp2_ref.at[d], kp2_ref.at[d], fac2_ref.at[d]
        sbuf = [s_ref.at[d * nbuf + k] for k in range(nbuf)]
        key, lf = _hg_gate(h_ref[:, 512 + 512 * d:1024 + 512 * d].astype(F32), lb_ref[...])
        e_ref[...] = _dot_split(m_ref[...], lf)
        eb = e_ref[pl.ds(0, L), :]
        qp_ref[...] = h_ref[:, 0:512].astype(F32) * jnp.exp2(eb)
        kp_ref[...] = key * jnp.exp2(-eb)
        fac_ref[...] = jnp.exp2(e_ref[pl.ds(L, (nlev + 1) * nb), :])
        jobs = [(sbuf[lv - 1], (1 + lv) * nb, upper[d][lv - 1]) for lv in range(1, nlev)]
        jobs.append((sbuf[nlev - 1], 0, all_up))
        if d == 0:
            jobs.append((sbuf[nlev], nb, (False,) * nb))
        _hg_scale_blocks(jobs, qp_ref, kp_ref, fac_ref, nb)
        H = L // 2
        halves = (slice(0, H), slice(H, L))
        qh, kh = (1, 0) if d == 0 else (0, 1)
        masks = [k_ref[lv] > 0.5 for lv in range(nlev - 1)]

        def tiles(ref, hbs):
            return jnp.stack([ref[halves[hb], HG_DK * hd:HG_DK * hd + HG_DK]
                              for hd in range(HG_HEADS) for hb in hbs], axis=0).astype(BF16)

        def bdot(a, b):
            return lax.dot_general(a, b, (((2,), (2,)), ((0,), (0,))), preferred_element_type=F32)

        diag_all = jnp.where(masks[0][None], bdot(tiles(qp_ref, (0, 1)), tiles(kp_ref, (0, 1))), 0.0)
        for lv in range(2, nlev):
            t = tiles(sbuf[lv - 1], (0, 1))
            diag_all = jnp.where(masks[lv - 1][None], bdot(t, t), diag_all)
        diag_all = diag_all.astype(BF16)
        cross_all = bdot(tiles(sbuf[0], (qh,)), tiles(sbuf[0], (kh,))).astype(BF16)
        for p in range(2):
            v = h_ref[:, 1536 + 128 * p:1664 + 128 * p].astype(F32)
            vlo = jnp.where(lo_v, v, 0.0).astype(BF16)
            vhi = jnp.where(lo_v, 0.0, v).astype(BF16)
            qe_pair = []
            ke_pair = []
            for hh in range(2):
                ls = slice(HG_DK * (2 * p + hh), HG_DK * (2 * p + hh) + HG_DK)
                qe_pair.append(sbuf[nlev - 1][:, ls])
                if d == 0:
                    ke_pair.append(sbuf[nlev][:, ls])
            dg = [[diag_all[2 * (2 * p + hh) + hb] for hb in range(2)] for hh in range(2)]
            cross = [cross_all[2 * p + hh] for hh in range(2)]
            vs = jnp.concatenate([vlo, vhi], axis=0)
            if d == 0:
                o_q = _dot(jnp.concatenate([cross[0], dg[0][1], cross[1], dg[1][1]], axis=1), vs)
                o_k = _dot(jnp.concatenate([dg[0][0], dg[1][0]], axis=1),
                           jnp.concatenate([vlo[halves[0]], vhi[halves[0]]], axis=0))
                intra = jnp.concatenate([o_k, o_q], axis=0)
            else:
                o_q = _dot(jnp.concatenate([dg[0][0], cross[0], dg[1][0], cross[1]], axis=1), vs)
                o_k = _dot(jnp.concatenate([dg[0][1], dg[1][1]], axis=1),
                           jnp.concatenate([vlo[halves[1]], vhi[halves[1]]], axis=0))
                intra = jnp.concatenate([o_q, o_k], axis=0)
            state = st_ref[p] if d == 0 else sb_ref[0, 0, p]
            inter = _dot_nt(jnp.concatenate(qe_pair, axis=1), state.astype(BF16))
            osum[p] = osum[p] + intra + inter
            if d == 0:
                ps = slice(256 * p, 256 * p + 256)
                dec = fac_ref[nb - 1:nb, ps] * fac_ref[2 * nb - 1:2 * nb, ps]
                kvt = _dot_tn(v.astype(BF16), jnp.concatenate(ke_pair, axis=1))
                st_ref[p] = dec * st_ref[p] + jnp.where(bmt, kvt, 0.0)
    for p in range(2):
        o = osum[p]
        ms = _dot_split_rhs_const(o * o, bones) * (1.0 / HG_DV)
        g = h_ref[:, 1792 + 128 * p:1920 + 128 * p].astype(F32)
        y = o * lax.rsqrt(ms + RMS_EPS) * nw_ref[:, 128 * p:128 * p + 128] * jax.nn.sigmoid(g)
        o_ref[:, 128 * p:128 * p + 128] = y.astype(o_ref.dtype)


def _hgrn2(h, lb, nw, consts, B, S):
    L = HG_BLOCK
    nb = S // L
    mf, mb, kf, kb, mpre, upper = consts
    nsec = mf.shape[0]
    nfac = nsec - L
    sb = pl.pallas_call(
        _hg_bwd_state_kernel,
        grid=(B, nb // HG_PRE_BLOCKS),
        in_specs=[pl.BlockSpec((HG_PRE_BLOCKS * L, 512), lambda b, n: ((b * nb + nb) // HG_PRE_BLOCKS - 1 - n, 2)),
                  pl.BlockSpec((HG_PRE_BLOCKS * L, 256), lambda b, n: ((b * nb + nb) // HG_PRE_BLOCKS - 1 - n, 6)),
                  pl.BlockSpec((1, 512), lambda b, n: (0, 0)),
                  pl.BlockSpec((L + 8, L), lambda b, n: (0, 0))],
        out_specs=pl.BlockSpec((1, HG_PRE_BLOCKS, 2, LANES, 2 * HG_DK),
                               lambda b, n: (b, nb // HG_PRE_BLOCKS - 1 - n, 0, 0, 0)),
        out_shape=jax.ShapeDtypeStruct((B, nb, 2, LANES, 2 * HG_DK), F32),
        scratch_shapes=[pltpu.VMEM((2, LANES, 2 * HG_DK), F32)],
        compiler_params=_cparams(("parallel", "arbitrary")),
        name="hg_bwd_state",
    )(h, h, lb, mpre)
    return pl.pallas_call(
        functools.partial(_hg_main_kernel, upper=upper),
        grid=(B, nb),
        in_specs=[pl.BlockSpec((L, HG_W), lambda b, n: (b * nb + n, 0)),
                  pl.BlockSpec((1, 512), lambda b, n: (0, 0)),
                  pl.BlockSpec((1, 256), lambda b, n: (0, 0)),
                  pl.BlockSpec((nsec, L), lambda b, n: (0, 0)),
                  pl.BlockSpec((nsec, L), lambda b, n: (0, 0)),
                  pl.BlockSpec(kf.shape, lambda b, n: (0, 0, 0)),
                  pl.BlockSpec(kb.shape, lambda b, n: (0, 0, 0)),
                  pl.BlockSpec((1, 1, 2, LANES, 2 * HG_DK), lambda b, n: (b, n, 0, 0, 0))],
        out_specs=pl.BlockSpec((L, 256), lambda b, n: (b * nb + n, 0)),
        out_shape=jax.ShapeDtypeStruct((B * S, 256), BF16),
        scratch_shapes=[pltpu.VMEM((2, LANES, 2 * HG_DK), F32),
                        pltpu.VMEM((2, nsec, 512), F32),
                        pltpu.VMEM((2, L, 512), F32), pltpu.VMEM((2, L, 512), F32),
                        pltpu.VMEM((2, nfac, 512), F32),
                        pltpu.VMEM((2 * (kf.shape[0] + 2), L, 512), BF16)],
        compiler_params=_cparams(("parallel", "arbitrary")),
        name="hg_main",
    )(h, lb, nw, mf, mb, kf, kb, sb)


def _outproj_kernel(x_ref, r_ref, s_ref, g_ref, wr_ref, ws_ref, wg_ref, lg_ref, lb_ref, o_ref):
    mix = _dot(r_ref[...], wr_ref[...]) + _dot(s_ref[...], ws_ref[...]) + _dot(g_ref[...], wg_ref[...])
    o_ref[...] = _layer_norm(DEEPNORM_ALPHA * x_ref[...] + mix, lg_ref[...], lb_ref[...])


ROUTER_ROWS = 16


def _outproj_router_kernel(x_ref, r_ref, s_ref, g_ref, wr_ref, ws_ref, wg_ref, lg_ref, lb_ref, wrt_ref, tri_ref,
                           o_ref, ot_ref, route_ref, cnt_ref):
    @pl.when(pl.program_id(0) == 0)
    def _():
        cnt_ref[...] = jnp.zeros_like(cnt_ref)

    mix = _dot(r_ref[...], wr_ref[...]) + _dot(s_ref[...], ws_ref[...]) + _dot(g_ref[...], wg_ref[...])
    y = _layer_norm(DEEPNORM_ALPHA * x_ref[...] + mix, lg_ref[...], lb_ref[...])
    o_ref[...] = y
    ot_ref[...] = _to_token_tiles(y)
    tm = y.shape[0]
    w = wrt_ref[...]
    w_hi = w.astype(BF16)
    w_lo = (w - w_hi.astype(F32)).astype(BF16)
    y_hi = y.astype(BF16)
    y_lo = (y - y_hi.astype(F32)).astype(BF16)
    lt = _dot_nt(jnp.concatenate([w_hi, w_lo], axis=0), y_hi)
    logits = lt[:ROUTER_ROWS] + lt[ROUTER_ROWS:] + _dot_nt(w_hi, y_lo)
    row = lax.broadcasted_iota(jnp.int32, logits.shape, 0).astype(F32)
    neg = jnp.float32(-jnp.inf)
    l1 = jnp.where(row < N_EXPERTS, logits, neg)
    m1 = jnp.max(l1, axis=0, keepdims=True)
    i1 = jnp.min(jnp.where(l1 == m1, row, float(ROUTER_ROWS)), axis=0, keepdims=True)
    l2 = jnp.where(row == i1, neg, l1)
    m2 = jnp.max(l2, axis=0, keepdims=True)
    i2 = jnp.min(jnp.where(l2 == m2, row, float(ROUTER_ROWS)), axis=0, keepdims=True)
    e2 = jnp.exp(m2 - m1)
    den = 1.0 + e2
    picked = jnp.where((row == i1) | (row == i2), 1.0, 0.0)
    before = _dot(picked.astype(BF16), tri_ref[...])
    rank = before + cnt_ref[:, 0:1]
    r1 = jnp.sum(jnp.where(row == i1, rank, 0.0), axis=0, keepdims=True)
    r2 = jnp.sum(jnp.where(row == i2, rank, 0.0), axis=0, keepdims=True)
    cnt_ref[...] = cnt_ref[...] + jnp.sum(picked, axis=1, keepdims=True)
    lrow = lax.broadcasted_iota(jnp.int32, (8, tm), 0)
    route8 = (jnp.where(lrow == 0, i1, 0.0) + jnp.where(lrow == 1, i2, 0.0)
              + jnp.where(lrow == 2, 1.0 / den, 0.0) + jnp.where(lrow == 3, e2 / den, 0.0)
              + jnp.where(lrow == 4, r1, 0.0) + jnp.where(lrow == 5, r2, 0.0))
    route_ref[...] = jnp.concatenate([route8, jnp.zeros((LANES - 8, tm), F32)], axis=0).T


def _out_proj(x2d, ret, swa, hgo, w_r, w_s, w_g, ln_g, ln_b, w_router=None):
    T = x2d.shape[0]
    tm = 1024
    row = lambda i: (i, 0)
    full = lambda i: (0, 0)
    in_specs = [pl.BlockSpec((tm, D_MODEL), row),
                pl.BlockSpec((tm, 256), row), pl.BlockSpec((tm, 512), row), pl.BlockSpec((tm, 256), row),
                pl.BlockSpec((256, D_MODEL), full), pl.BlockSpec((512, D_MODEL), full),
                pl.BlockSpec((256, D_MODEL), full),
                pl.BlockSpec((1, D_MODEL), full), pl.BlockSpec((1, D_MODEL), full)]
    args = [x2d, ret, swa, hgo, w_r, w_s, w_g, ln_g, ln_b]
    if w_router is None:
        return pl.pallas_call(
            _outproj_kernel, grid=(T // tm,), in_specs=in_specs,
            out_specs=pl.BlockSpec((tm, D_MODEL), row),
            out_shape=jax.ShapeDtypeStruct((T, D_MODEL), F32),
            compiler_params=_cparams(("parallel",)), name="out_proj",
        )(*args)
    return pl.pallas_call(
        _outproj_router_kernel, grid=(T // tm,),
        in_specs=in_specs + [pl.BlockSpec((ROUTER_ROWS, D_MODEL), full),
                             pl.BlockSpec((tm, tm), full, pipeline_mode=pl.Buffered(1))],
        out_specs=[pl.BlockSpec((tm, D_MODEL), row), pl.BlockSpec((tm, 8, LANES), lambda i: (i, 0, 0)),
                   pl.BlockSpec((tm, LANES), row), pl.BlockSpec((ROUTER_ROWS, LANES), full)],
        out_shape=[jax.ShapeDtypeStruct((T, D_MODEL), F32), jax.ShapeDtypeStruct((T, 8, LANES), F32),
                   jax.ShapeDtypeStruct((T, LANES), F32), jax.ShapeDtypeStruct((ROUTER_ROWS, LANES), F32)],
        compiler_params=_cparams(("arbitrary",)), name="out_proj_router",
    )(*args, w_router, jnp.asarray(np.triu(np.ones((tm, tm), np.float32), 1), BF16))


FFN_CHUNKS = 1


def _ffn_kernel(x_ref, wg_ref, wu_ref, wd_ref, lg_ref, lb_ref, o_ref):
    x = x_ref[...]
    xb = x.astype(BF16)
    th = FFN_DIM // FFN_CHUNKS
    f = jnp.zeros(x.shape, F32)
    for c in range(FFN_CHUNKS):
        g = _dot(xb, wg_ref[:, c * th:(c + 1) * th])
        u = _dot(xb, wu_ref[:, c * th:(c + 1) * th])
        a = (g * jax.nn.sigmoid(g)) * u
        f = f + _dot(a.astype(BF16), wd_ref[c * th:(c + 1) * th, :])
    o_ref[...] = _layer_norm(DEEPNORM_ALPHA * x + f, lg_ref[...], lb_ref[...])


def _ffn(x2d, wg, wu, wd, ln_g, ln_b):
    T = x2d.shape[0]
    tm = 512
    once = pl.Buffered(1)
    return pl.pallas_call(
        _ffn_kernel, grid=(T // tm,),
        in_specs=[pl.BlockSpec((tm, D_MODEL), lambda i: (i, 0)),
                  pl.BlockSpec((D_MODEL, FFN_DIM), lambda i: (0, 0), pipeline_mode=once),
                  pl.BlockSpec((D_MODEL, FFN_DIM), lambda i: (0, 0), pipeline_mode=once),
                  pl.BlockSpec((FFN_DIM, D_MODEL), lambda i: (0, 0), pipeline_mode=once),
                  pl.BlockSpec((1, D_MODEL), lambda i: (0, 0)),
                  pl.BlockSpec((1, D_MODEL), lambda i: (0, 0))],
        out_specs=pl.BlockSpec((tm, D_MODEL), lambda i: (i, 0)),
        out_shape=jax.ShapeDtypeStruct((T, D_MODEL), F32),
        compiler_params=_cparams(("parallel",)), name="ffn",
    )(x2d, wg, wu, wd, ln_g, ln_b)


MOE_TILE = 512
MOE_TH = 1792
MOE_TOK = 512


def _moe_num_tiles(T):
    return (2 * T) // MOE_TILE + N_EXPERTS + 1


def _moe_tables(route, counts):
    T = route.shape[0]
    tm = MOE_TILE
    cnt = counts[:N_EXPERTS, 0].astype(jnp.int32)
    pc = ((cnt + tm - 1) // tm) * tm
    pend = jnp.cumsum(pc)
    pstart = pend - pc
    e = route[:, 0:2].astype(jnp.int32)
    rank = route[:, 4:6].astype(jnp.int32)
    sel = e[:, :, None] == jnp.arange(N_EXPERTS, dtype=jnp.int32)[None, None, :]
    pos = jnp.sum(jnp.where(sel, pstart[None, None, :], 0), axis=-1) + rank
    n_tiles = _moe_num_tiles(T)
    tile_row = jnp.arange(n_tiles, dtype=jnp.int32) * tm
    texp = jnp.minimum(jnp.sum((tile_row[:, None] >= pend[None, :]).astype(jnp.int32), axis=1), N_EXPERTS - 1)
    n_used = pend[-1] // tm
    last = jnp.sum(jnp.where(jnp.arange(n_tiles) == n_used - 1, texp, 0))
    texp = jnp.where(jnp.arange(n_tiles) < n_used, texp, last)
    return pos.reshape(-1), texp, n_used.reshape(1), pstart + cnt


def _to_token_tiles(x):
    cols = jnp.stack([x[:, LANES * c:LANES * c + LANES] for c in range(D_MODEL // LANES)], axis=0)
    return pltpu.einshape("csl->scl", cols)


def _from_token_tiles(xt):
    y = pltpu.einshape("scl->csl", xt)
    return jnp.concatenate([y[c] for c in range(D_MODEL // LANES)], axis=1)


MOE_DMA_UNROLL = 8


def _moe_dispatch_kernel(pos_ref, pad_ref, nused_ref, xt_hbm, xs_hbm, zero_ref, buf_ref, fsem, ssem, zsem):
    i = pl.program_id(0)
    n = pl.num_programs(0)
    tm = MOE_TOK
    n_tiles = xs_hbm.shape[0] // MOE_TILE
    slot = i % 3

    def zero_copy(row):
        return pltpu.make_async_copy(zero_ref, xs_hbm.at[pl.ds(row, MOE_TILE)], zsem)

    def fetch(step, s):
        return pltpu.make_async_copy(xt_hbm.at[pl.ds(step * tm, tm)], buf_ref.at[s], fsem.at[s])

    def drain_copies(s):
        for _ in range(2):
            pltpu.make_async_copy(buf_ref.at[s], xs_hbm.at[pl.ds(0, tm)], ssem.at[s]).wait()

    @pl.when(i == 0)
    def _():
        fetch(0, 0).start()
        zero_ref[...] = jnp.zeros_like(zero_ref)
        for e in range(N_EXPERTS):
            zero_copy(pad_ref[e]).start()
        for e in range(N_EXPERTS):
            zero_copy(pad_ref[e]).wait()
        for t in range(n_tiles - N_EXPERTS - 1, n_tiles):
            @pl.when(t >= nused_ref[0])
            def _():
                zero_copy(t * MOE_TILE).start()
                zero_copy(t * MOE_TILE).wait()

    @pl.when(i >= 2)
    def _():
        drain_copies((i + 1) % 3)

    @pl.when(i + 1 < n)
    def _():
        fetch(i + 1, (i + 1) % 3).start()

    fetch(i, slot).wait()

    def body(kk, c):
        k0 = kk * MOE_DMA_UNROLL
        t0 = i * tm + k0
        dst = [pos_ref[2 * t0 + u] for u in range(2 * MOE_DMA_UNROLL)]
        for u in range(2 * MOE_DMA_UNROLL):
            pltpu.make_async_copy(buf_ref.at[slot, k0 + u // 2], xs_hbm.at[dst[u]], ssem.at[slot]).start(priority=u % 2)
        return c

    lax.fori_loop(0, tm // MOE_DMA_UNROLL, body, 0)

    @pl.when(i == n - 1)
    def _():
        @pl.when(i >= 1)
        def _():
            drain_copies((i - 1) % 3)
        drain_copies(slot)


def _moe_dispatch(x1t, pos, padstart, n_used):
    T = x1t.shape[0]
    P = _moe_num_tiles(T) * MOE_TILE
    return pl.pallas_call(
        _moe_dispatch_kernel,
        grid_spec=pltpu.PrefetchScalarGridSpec(
            num_scalar_prefetch=3, grid=(T // MOE_TOK,),
            in_specs=[pl.BlockSpec(memory_space=pl.ANY)],
            out_specs=pl.BlockSpec(memory_space=pl.ANY),
            scratch_shapes=[pltpu.VMEM((MOE_TILE, 8, LANES), F32), pltpu.VMEM((3, MOE_TOK, 8, LANES), F32),
                            pltpu.SemaphoreType.DMA((3,)), pltpu.SemaphoreType.DMA((3,)),
                            pltpu.SemaphoreType.DMA]),
        out_shape=jax.ShapeDtypeStruct((P, 8, LANES), F32),
        compiler_params=_cparams(("arbitrary",)), name="moe_dispatch",
    )(pos, padstart, n_used, x1t)


def _moe_group_kernel(texp_ref, nused_ref, xs_ref, wg_ref, wu_ref, wd_ref, y_ref, xb_ref, acc_ref):
    i = pl.program_id(0)
    j = pl.program_id(1)

    @pl.when(i < nused_ref[0])
    def _():
        @pl.when(j == 0)
        def _():
            acc_ref[...] = jnp.zeros_like(acc_ref)

        xb = _from_token_tiles(xs_ref[...]).astype(BF16)
        g = _dot(xb, wg_ref[...])
        u = _dot(xb, wu_ref[...])
        a = (g * jax.nn.sigmoid(g)) * u
        acc_ref[...] += _dot(a.astype(BF16), wd_ref[...])

        @pl.when(j == pl.num_programs(1) - 1)
        def _():
            y_ref[...] = _to_token_tiles(acc_ref[...])

    @pl.when((i >= nused_ref[0]) & (j == 0))
    def _():
        y_ref[...] = jnp.zeros_like(y_ref)


def _moe_group(xs, texp, n_used, wg, wu, wd):
    P = xs.shape[0]
    tm, th = MOE_TILE, MOE_TH
    nh = EXPERT_DIM // th

    def row_map(i, j, texp, nused):
        return (jnp.minimum(i, nused[0] - 1), 0, 0)

    def col(i, j, nused):
        return jnp.where(i < nused[0], j, nh - 1)

    return pl.pallas_call(
        _moe_group_kernel,
        grid_spec=pltpu.PrefetchScalarGridSpec(
            num_scalar_prefetch=2, grid=(P // tm, nh),
            in_specs=[pl.BlockSpec((tm, 8, LANES), row_map),
                      pl.BlockSpec((None, D_MODEL, th), lambda i, j, texp, nused: (texp[i], 0, col(i, j, nused))),
                      pl.BlockSpec((None, D_MODEL, th), lambda i, j, texp, nused: (texp[i], 0, col(i, j, nused))),
                      pl.BlockSpec((None, th, D_MODEL), lambda i, j, texp, nused: (texp[i], col(i, j, nused), 0))],
            out_specs=pl.BlockSpec((tm, 8, LANES), lambda i, j, texp, nused: (i, 0, 0)),
            scratch_shapes=[pltpu.VMEM((tm, D_MODEL), BF16), pltpu.VMEM((tm, D_MODEL), F32)]),
        out_shape=jax.ShapeDtypeStruct((P, 8, LANES), F32),
        compiler_params=_cparams(("arbitrary", "arbitrary")), name="moe_group",
    )(texp, n_used, xs, wg, wu, wd)


def _moe_combine_kernel(pos_ref, x_ref, route_ref, y_hbm, lg_ref, lb_ref, o_ref, buf_ref, sem):
    i = pl.program_id(0)
    tm = x_ref.shape[0]

    def issue(tile, slot):
        def body(kk, c):
            k0 = kk * MOE_DMA_UNROLL
            t0 = tile * tm + k0
            src = [pos_ref[2 * t0 + u] for u in range(2 * MOE_DMA_UNROLL)]
            for u in range(2 * MOE_DMA_UNROLL):
                pltpu.make_async_copy(y_hbm.at[src[u]], buf_ref.at[slot, (u % 2) * tm + k0 + u // 2],
                                      sem.at[slot]).start(priority=u % 2)
            return c
        lax.fori_loop(0, tm // MOE_DMA_UNROLL, body, 0)

    @pl.when(i == 0)
    def _():
        issue(0, 0)

    slot = i % 2
    pltpu.make_async_copy(y_hbm.at[pl.ds(0, 2 * tm)], buf_ref.at[slot], sem.at[slot]).wait()

    @pl.when(i + 1 < pl.num_programs(0))
    def _():
        issue(i + 1, 1 - slot)

    a = _from_token_tiles(buf_ref[slot, pl.ds(0, tm)])
    b = _from_token_tiles(buf_ref[slot, pl.ds(tm, tm)])
    f = route_ref[:, 2:3] * a + route_ref[:, 3:4] * b
    o_ref[...] = _layer_norm(DEEPNORM_ALPHA * x_ref[...] + f, lg_ref[...], lb_ref[...])


def _moe_combine(x1, route, y, pos, ln_g, ln_b):
    T = x1.shape[0]
    tm = MOE_TOK
    return pl.pallas_call(
        _moe_combine_kernel,
        grid_spec=pltpu.PrefetchScalarGridSpec(
            num_scalar_prefetch=1, grid=(T // tm,),
            in_specs=[pl.BlockSpec((tm, D_MODEL), lambda i, pos: (i, 0)),
                      pl.BlockSpec((tm, LANES), lambda i, pos: (i, 0)),
                      pl.BlockSpec(memory_space=pl.ANY),
                      pl.BlockSpec((1, D_MODEL), lambda i, pos: (0, 0)),
                      pl.BlockSpec((1, D_MODEL), lambda i, pos: (0, 0))],
            out_specs=pl.BlockSpec((tm, D_MODEL), lambda i, pos: (i, 0)),
            scratch_shapes=[pltpu.VMEM((2, 2 * tm, 8, LANES), F32), pltpu.SemaphoreType.DMA((2,))]),
        out_shape=jax.ShapeDtypeStruct((T, D_MODEL), F32),
        compiler_params=_cparams(("arbitrary",)), name="moe_combine",
    )(pos, x1, route, y, ln_g, ln_b)


def _moe(x1, x1t, route, counts, wg, wu, wd, ln_g, ln_b):
    pos, texp, n_used, padstart = _moe_tables(route, counts)
    xs = _moe_dispatch(x1t, pos, padstart, n_used)
    y = _moe_group(xs, texp, n_used, wg, wu, wd)
    return _moe_combine(x1, route, y, pos, ln_g, ln_b)


def _t5_bucket(rel):
    nb = REL_BUCKETS // 2
    ret = (rel > 0).astype(np.int32) * nb
    n = np.abs(rel)
    max_exact = nb // 2
    large = max_exact + (np.log(np.maximum(n, 1) / max_exact) / np.log(REL_MAX_DIST / max_exact)
                         * (nb - max_exact)).astype(np.int32)
    large = np.minimum(large, nb - 1)
    return ret + np.where(n < max_exact, n, large)


def _swa_head_order():
    return [h for c in range(4) for h in (c, 4 + c)]


def _prepare(w_in, w_out, ret_decay, swa_sink, rel_bias, hg_lb, hg_norm_w, moe_router, seq_lens):
    order = _swa_head_order()
    w_in_p = jnp.concatenate(
        [w_in[:, :, 1792:3840], w_in[:, :, 0:1024]]
        + [w_in[:, :, 1024 + 64 * h:1088 + 64 * h] for h in order]
        + [w_in[:, :, 1536:1792]], axis=2).astype(BF16)
    w_r = w_out[:, 0:256, :].astype(BF16)
    w_s = jnp.concatenate([w_out[:, 256 + 64 * h:320 + 64 * h, :] for h in order], axis=1).astype(BF16)
    w_g = w_out[:, 768:1024, :].astype(BF16)

    rd = ret_decay.astype(F32)
    rdp = jnp.repeat(rd.reshape(DEPTH, 4, 2), 64, axis=-1)
    rdp = jnp.concatenate([rdp, rdp], axis=1)
    rdh = jnp.broadcast_to(rd.reshape(DEPTH, 8, 1), (DEPTH, 8, LANES))

    W = SWA_BLOCK
    rel = np.arange(3 * W)[None, :] - W - np.arange(W)[:, None]
    onehot = jnp.asarray(_t5_bucket(rel)[..., None] == np.arange(REL_BUCKETS), F32)
    bias = jnp.einsum("qkb,bh->hqk", onehot, rel_bias.astype(F32), precision=lax.Precision.HIGHEST)
    band = jnp.asarray(np.abs(rel) <= WINDOW)
    bias_kq = jnp.transpose(jnp.where(band[None], bias, MASK_VALUE), (0, 2, 1))
    order = _swa_head_order()
    bias_t = jnp.concatenate([bias_kq[h] for h in order], axis=1) * LOG2_E
    sink = swa_sink.astype(F32) * LOG2_E
    sink_t = jnp.concatenate([jnp.broadcast_to(sink[:, h, None, None], (DEPTH, 8, W)) for h in order], axis=2)

    p = jax.nn.softmax(hg_lb.astype(F32), axis=0)
    lower = (jnp.cumsum(p, axis=0) - p[0]).reshape(DEPTH, 1, HG_HEADS * HG_DK)
    nw = hg_norm_w.astype(F32).reshape(DEPTH, 1, HG_HEADS * HG_DV)

    mf, mb, kf, kb, upper = _hg_constants(HG_BLOCK)
    L = HG_BLOCK
    mpre = np.concatenate([np.tril(np.ones((L, L), np.float32), -1), np.ones((8, L), np.float32)], axis=0)
    consts = (jnp.asarray(mf, BF16), jnp.asarray(mb, BF16), jnp.asarray(kf, F32), jnp.asarray(kb, F32),
              jnp.asarray(mpre, BF16), upper)

    half = RET_DK // 2
    inv = ROPE_BASE ** (-np.arange(half, dtype=np.float32) / half)
    rope = {}
    for S in set(seq_lens):
        ang = jnp.arange(S, dtype=F32)[:, None] * jnp.asarray(inv)[None, :]
        c32, s32 = jnp.cos(ang), jnp.sin(ang)
        rope[S] = (jnp.tile(c32, (1, 4)), jnp.tile(jnp.concatenate([-s32, s32], axis=1), (1, 2)))

    router = jnp.pad(jnp.transpose(moe_router.astype(F32), (0, 2, 1)),
                     ((0, 0), (0, ROUTER_ROWS - N_EXPERTS), (0, 0)))
    return dict(w_in=w_in_p, w_r=w_r, w_s=w_s, w_g=w_g, rdp=rdp, rdh=rdh, bias=bias_t, sink=sink_t,
                lower=lower, nw=nw, consts=consts, rope=rope, router=router)


def _trunk(x, prm, ln1_g, ln1_b, ln2_g, ln2_b, ffn_w, moe_w):
    B, S, _ = x.shape
    x2d = x.reshape(B * S, D_MODEL)
    cos_t, sin_t = prm["rope"][S]
    for l in range(DEPTH):
        h = _in_proj(x2d, prm["w_in"][l])
        ret = _retention(h, cos_t, sin_t, prm["rdp"][l], prm["rdh"][l], B, S)
        swa = _swa(h, prm["bias"], prm["sink"][l], B, S)
        hgo = _hgrn2(h, prm["lower"][l], prm["nw"][l], prm["consts"], B, S)
        g1, b1 = ln1_g[l].reshape(1, D_MODEL), ln1_b[l].reshape(1, D_MODEL)
        g2, b2 = ln2_g[l].reshape(1, D_MODEL), ln2_b[l].reshape(1, D_MODEL)
        i = l // 2
        if l % 2 == 0:
            x1 = _out_proj(x2d, ret, swa, hgo, prm["w_r"][l], prm["w_s"][l], prm["w_g"][l], g1, b1)
            wg, wu, wd = ffn_w
            x2d = _ffn(x1, wg[i], wu[i], wd[i], g2, b2)
        else:
            x1, x1t, route, counts = _out_proj(x2d, ret, swa, hgo, prm["w_r"][l], prm["w_s"][l], prm["w_g"][l],
                                               g1, b1, prm["router"][i])
            wg, wu, wd = moe_w
            x2d = _moe(x1, x1t, route, counts, wg[i], wu[i], wd[i], g2, b2)
    return x2d.reshape(B, S, D_MODEL)


def kernel(x_prompt, x_sample, w_in, w_out, ret_decay, swa_sink, rel_bias, hg_lb, hg_norm_w, ln1_g, ln1_b,
           ln2_g, ln2_b, ffn_w_gate, ffn_w_up, ffn_w_down, moe_router, moe_w_gate, moe_w_up, moe_w_down):
    prm = _prepare(w_in, w_out, ret_decay, swa_sink, rel_bias, hg_lb, hg_norm_w, moe_router,
                   (x_prompt.shape[1], x_sample.shape[1]))
    ffn_w = (ffn_w_gate.astype(BF16), ffn_w_up.astype(BF16), ffn_w_down.astype(BF16))
    moe_w = (moe_w_gate.astype(BF16), moe_w_up.astype(BF16), moe_w_down.astype(BF16))
    outs = []
    for x in (x_prompt, x_sample):
        outs.append(_trunk(x, prm, ln1_g, ln1_b, ln2_g, ln2_b, ffn_w, moe_w))
    return tuple(outs)
```
